```python
import jax
import jax.numpy as jnp
from jax import lax
import numpy as np


D_MODEL = 1024
BATCH = 16
SEQ = 2048
DEPTH = 2

GRID_W = 64
CTX_LEN = 256
EPS = 1e-6

ATT_HEADS = 8
ATT_KV_HEADS = 2
ATT_HEAD_DIM = 64
ATT_GROUP = ATT_HEADS // ATT_KV_HEADS
WINDOW = 128
ATT_BLOCK = 128
ROPE_BASE = 10000.0

M_HEADS = 4
M_QK_DIM = 64
M_V_DIM = 128
M_CHUNK = 64
FORGET_BIAS = 3.0

CONV_WIDTH = 512
CONV_K = 3

N_BRANCH = 3
ATT_OUT = ATT_HEADS * ATT_HEAD_DIM
M_OUT = M_HEADS * M_V_DIM

N_EXPERTS = 64
N_GROUPS = 8
TOPK_GROUPS = 4
TOP_K = 8
EXPERT_FF = 256
SHARED_FF = 256
ROUTED_SCALE = 2.5

IN_NAMES = ('q_a', 'k_a', 'v_a', 'q_m', 'k_m', 'v_m', 'o_m', 'g_m', 'b_conv', 'c_conv', 'x_conv', 'branch_gates')
IN_SIZES = (ATT_HEADS * ATT_HEAD_DIM, ATT_KV_HEADS * ATT_HEAD_DIM, ATT_KV_HEADS * ATT_HEAD_DIM,
            M_HEADS * M_QK_DIM, M_HEADS * M_QK_DIM, M_HEADS * M_V_DIM, M_HEADS * M_V_DIM, 4 * M_HEADS,
            CONV_WIDTH, CONV_WIDTH, CONV_WIDTH, N_BRANCH * D_MODEL)
D_IN = sum(IN_SIZES)

kernel_name = 'hybrid_gqa_mlstm_shortconv_moe_dit'


def rmsnorm(x, g):
    xf = x.astype(jnp.float32)
    y = xf * lax.rsqrt(jnp.mean(xf * xf, axis=-1, keepdims=True) + EPS)
    return (y * g.astype(jnp.float32)).astype(x.dtype)


def modulate(x, shift, scale):
    return x * (1 + scale) + shift


def split_proj(p):
    out = {}
    start = 0
    for name, size in zip(IN_NAMES, IN_SIZES):
        out[name] = p[..., start:start + size]
        start += size
    return out


def axial_rope(seq):
    rows = seq // GRID_W
    row = jnp.repeat(jnp.arange(rows, dtype=jnp.float32), GRID_W)
    col = jnp.tile(jnp.arange(GRID_W, dtype=jnp.float32), rows)
    n_pairs = ATT_HEAD_DIM // 4
    inv_freq = ROPE_BASE ** (-jnp.arange(n_pairs, dtype=jnp.float32) / n_pairs)
    ang = jnp.concatenate([row[:, None] * inv_freq, col[:, None] * inv_freq], axis=-1)
    return jnp.cos(ang), jnp.sin(ang)


def apply_rope(x, cos, sin):
    xf = x.astype(jnp.float32).reshape(*x.shape[:-1], -1, 2)
    c = cos[None, :, None, :]
    s = sin[None, :, None, :]
    x0 = xf[..., 0]
    x1 = xf[..., 1]
    out = jnp.stack([x0 * c - x1 * s, x0 * s + x1 * c], axis=-1)
    return out.reshape(x.shape).astype(x.dtype)


def latent_window_attention(q, k, v, kc, vc, sink):
    B, S = q.shape[:2]
    LC = kc.shape[1]
    nb = S // ATT_BLOCK
    scale = ATT_HEAD_DIM ** -0.5
    pad = ((0, 0), (ATT_BLOCK, ATT_BLOCK), (0, 0), (0, 0))
    kp = jnp.pad(k, pad)
    vp = jnp.pad(v, pad)
    sink_l = sink.astype(jnp.float32).reshape(ATT_KV_HEADS, ATT_GROUP)[None, :, :, None, None]

    def block(j):
        q0 = j * ATT_BLOCK
        qj = lax.dynamic_slice_in_dim(q, q0, ATT_BLOCK, axis=1)
        qj = qj.reshape(B, ATT_BLOCK, ATT_KV_HEADS, ATT_GROUP, ATT_HEAD_DIM)
        kj = lax.dynamic_slice_in_dim(kp, q0, 3 * ATT_BLOCK, axis=1)
        vj = lax.dynamic_slice_in_dim(vp, q0, 3 * ATT_BLOCK, axis=1)
        qpos = q0 + jnp.arange(ATT_BLOCK)
        kpos = q0 - ATT_BLOCK + jnp.arange(3 * ATT_BLOCK)
        valid = ((jnp.abs(kpos[None, :] - qpos[:, None]) <= WINDOW)
                 & (kpos >= 0)[None, :] & (kpos < S)[None, :])
        s_band = jnp.einsum('bqhgd,bphd->bhgqp', qj, kj, preferred_element_type=jnp.float32) * scale
        s_band = jnp.where(valid, s_band, -jnp.inf)
        s_ctx = jnp.einsum('bqhgd,bchd->bhgqc', qj, kc, preferred_element_type=jnp.float32) * scale
        s_sink = jnp.broadcast_to(sink_l, s_ctx.shape[:-1] + (1,))
        probs = jax.nn.softmax(jnp.concatenate([s_sink, s_ctx, s_band], axis=-1), axis=-1).astype(v.dtype)
        o = (jnp.einsum('bhgqc,bchd->bqhgd', probs[..., 1:1 + LC], vc)
             + jnp.einsum('bhgqp,bphd->bqhgd', probs[..., 1 + LC:], vj))
        return o.reshape(B, ATT_BLOCK, ATT_OUT)

    out = lax.map(block, jnp.arange(nb))
    return out.transpose(1, 0, 2, 3).reshape(B, S, ATT_OUT)


def context_attention(qc, kc, vc, sink):
    B, L = qc.shape[:2]
    scale = ATT_HEAD_DIM ** -0.5
    qg = qc.reshape(B, L, ATT_KV_HEADS, ATT_GROUP, ATT_HEAD_DIM)
    s = jnp.einsum('bqhgd,bchd->bhgqc', qg, kc, preferred_element_type=jnp.float32) * scale
    sink_l = sink.astype(jnp.float32).reshape(ATT_KV_HEADS, ATT_GROUP)[None, :, :, None, None]
    s_sink = jnp.broadcast_to(sink_l, s.shape[:-1] + (1,))
    probs = jax.nn.softmax(jnp.concatenate([s_sink, s], axis=-1), axis=-1).astype(vc.dtype)
    o = jnp.einsum('bhgqc,bchd->bqhgd', probs[..., 1:], vc)
    return o.reshape(B, L, ATT_OUT)


def mlstm_zero_state(batch):
    return (jnp.zeros((batch, M_HEADS, M_V_DIM, M_QK_DIM), jnp.float32),
            jnp.zeros((batch, M_HEADS, M_QK_DIM), jnp.float32),
            jnp.zeros((batch, M_HEADS), jnp.float32))


def mlstm_scan(q, k, v, ig, lf, state):
    B, L = q.shape[:2]
    nc = L // M_CHUNK
    causal = jnp.tril(jnp.ones((M_CHUNK, M_CHUNK), dtype=bool))

    def to_chunks(a):
        return jnp.moveaxis(a.reshape(B, nc, M_CHUNK, *a.shape[2:]), 1, 0)

    def step(carry, xs):
        C, n, m = carry
        qc, kc, vc, igc, lfc = xs
        b = jnp.cumsum(lfc, axis=1)
        a_inter = b + m[:, None, :]
        dmat = b[:, :, None, :] - b[:, None, :, :] + igc[:, None, :, :]
        dmat = jnp.where(causal[None, :, :, None], dmat, -jnp.inf)
        m_s = jnp.maximum(a_inter, jnp.max(dmat, axis=2))
        w_inter = jnp.exp(a_inter - m_s)
        w_intra = jnp.exp(dmat - m_s[:, :, None, :])
        s_qk = w_intra * jnp.einsum('bshd,brhd->bsrh', qc, kc)
        num = jnp.einsum('bsrh,brhe->bshe', s_qk, vc) + w_inter[..., None] * jnp.einsum('bhed,bshd->bshe', C, qc)
        den = jnp.sum(s_qk, axis=2) + w_inter * jnp.einsum('bhd,bshd->bsh', n, qc)
        h = num / jnp.maximum(jnp.abs(den), jnp.exp(-m_s))[..., None]
        b_last = b[:, -1, :]
        m_new = jnp.maximum(b_last + m, jnp.max(b_last[:, None, :] - b + igc, axis=1))
        decay = jnp.exp(b_last + m - m_new)
        w_r = jnp.exp(b_last[:, None, :] - b + igc - m_new[:, None, :])
        C_new = decay[..., None, None] * C + jnp.einsum('brh,brhe,brhd->bhed', w_r, vc, kc)
        n_new = decay[..., None] * n + jnp.einsum('brh,brhd->bhd', w_r, kc)
        return (C_new, n_new, m_new), h

    state, h = lax.scan(step, state, (to_chunks(q), to_chunks(k), to_chunks(v), to_chunks(ig), to_chunks(lf)))
    return jnp.moveaxis(h, 0, 1).reshape(B, L, M_HEADS, M_V_DIM), state


def mlstm_inputs(p, gate_b):
    B, L = p['q_m'].shape[:2]
    f32 = jnp.float32
    q = p['q_m'].reshape(B, L, M_HEADS, M_QK_DIM).astype(f32)
    k = p['k_m'].reshape(B, L, M_HEADS, M_QK_DIM).astype(f32) * (M_QK_DIM ** -0.5)
    v = p['v_m'].reshape(B, L, M_HEADS, M_V_DIM).astype(f32)
    g = p['g_m'].reshape(B, L, 4, M_HEADS).astype(f32) + gate_b.astype(f32)
    gates_fwd = (g[:, :, 0], jax.nn.log_sigmoid(g[:, :, 1]))
    gates_bwd = (g[:, :, 2], jax.nn.log_sigmoid(g[:, :, 3]))
    return q, k, v, gates_fwd, gates_bwd


def mlstm_bidir(q, k, v, gates_fwd, gates_bwd, state_fwd, state_bwd):
    h_f, st_f = mlstm_scan(q, k, v, gates_fwd[0], gates_fwd[1], state_fwd)
    flip = lambda a: jnp.flip(a, axis=1)
    h_b, st_b = mlstm_scan(flip(q), flip(k), flip(v), flip(gates_bwd[0]), flip(gates_bwd[1]), state_bwd)
    return h_f + flip(h_b), st_f, st_b


def mlstm_output(h, o_pre, norm_g):
    B, L = h.shape[:2]
    hn = h * lax.rsqrt(jnp.mean(h * h, axis=-1, keepdims=True) + EPS)
    hn = hn * norm_g.astype(jnp.float32).reshape(M_HEADS, M_V_DIM)
    out = jax.nn.sigmoid(o_pre.astype(jnp.float32)) * hn.reshape(B, L, M_OUT)
    return out.astype(o_pre.dtype)


def short_conv_mixer(p, conv_w):
    u = p['c_conv'] * p['x_conv']
    w = conv_w.astype(u.dtype)[:, None, :]
    y = lax.conv_general_dilated(u, w, window_strides=(1,), padding=((CONV_K // 2, CONV_K // 2),),
                                 dimension_numbers=('NWC', 'WIO', 'NWC'), feature_group_count=CONV_WIDTH)
    return p['b_conv'] * y


def merge_branches(y_att, y_mls, y_conv, gate_pre, lp):
    g = jax.nn.sigmoid(gate_pre + lp['branch_gate_b'].reshape(-1))
    g = g.reshape(*gate_pre.shape[:-1], N_BRANCH, D_MODEL)
    y = (g[..., 0, :] * (y_att @ lp['w_br_attn'])
         + g[..., 1, :] * (y_mls @ lp['w_br_mlstm'])
         + g[..., 2, :] * (y_conv @ lp['w_br_conv']))
    return y @ lp['w_out']


def token_mixers(h, hc, lp, cos, sin, need_ctx):
    B, S = h.shape[:2]
    L = hc.shape[1]
    p = split_proj(h @ lp['w_in'])
    pc = split_proj(hc @ lp['w_in'])
    q = apply_rope(p['q_a'].reshape(B, S, ATT_HEADS, ATT_HEAD_DIM), cos, sin)
    k = apply_rope(p['k_a'].reshape(B, S, ATT_KV_HEADS, ATT_HEAD_DIM), cos, sin)
    v = p['v_a'].reshape(B, S, ATT_KV_HEADS, ATT_HEAD_DIM)
    kc = pc['k_a'].reshape(B, L, ATT_KV_HEADS, ATT_HEAD_DIM)
    vc = pc['v_a'].reshape(B, L, ATT_KV_HEADS, ATT_HEAD_DIM)
    y_att = latent_window_attention(q, k, v, kc, vc, lp['attn_sink'])
    qm_c, km_c, vm_c, gf_c, gb_c = mlstm_inputs(pc, lp['mlstm_gate_b'])
    zero = mlstm_zero_state(B)
    hm_c, st_f, st_b = mlstm_bidir(qm_c, km_c, vm_c, gf_c, gb_c, zero, zero)
    qm, km, vm, gf, gb = mlstm_inputs(p, lp['mlstm_gate_b'])
    hm, _, _ = mlstm_bidir(qm, km, vm, gf, gb, st_f, st_b)
    y_mls = mlstm_output(hm, p['o_m'], lp['mlstm_norm_g'])
    y_conv = short_conv_mixer(p, lp['conv_w'])
    y = merge_branches(y_att, y_mls, y_conv, p['branch_gates'], lp)
    if not need_ctx:
        return y, None
    qc = pc['q_a'].reshape(B, L, ATT_HEADS, ATT_HEAD_DIM)
    yc_att = context_attention(qc, kc, vc, lp['attn_sink'])
    yc_mls = mlstm_output(hm_c, pc['o_m'], lp['mlstm_norm_g'])
    yc_conv = short_conv_mixer(pc, lp['conv_w'])
    yc = merge_branches(yc_att, yc_mls, yc_conv, pc['branch_gates'], lp)
    return y, yc


def swiglu(t, wg, wu, wd):
    return (jax.nn.silu(t @ wg) * (t @ wu)) @ wd


def moe(h, router_w, router_bias, exp_w_gate, exp_w_up, exp_w_down, sh_w_gate, sh_w_up, sh_w_down):
    shape = h.shape
    t = h.reshape(-1, D_MODEL)
    T = t.shape[0]
    scores = jax.nn.sigmoid((t @ router_w).astype(jnp.float32))
    sel = scores + router_bias.astype(jnp.float32)
    grp = sel.reshape(T, N_GROUPS, N_EXPERTS // N_GROUPS)
    grp_score = jnp.sum(lax.top_k(grp, 2)[0], axis=-1)
    _, gidx = lax.top_k(grp_score, TOPK_GROUPS)
    gmask = jnp.sum(jax.nn.one_hot(gidx, N_GROUPS, dtype=jnp.float32), axis=-2) > 0
    emask = jnp.repeat(gmask, N_EXPERTS // N_GROUPS, axis=-1)
    _, eidx = lax.top_k(jnp.where(emask, sel, -jnp.inf), TOP_K)
    w = jnp.take_along_axis(scores, eidx, axis=-1)
    w = w / jnp.sum(w, axis=-1, keepdims=True) * ROUTED_SCALE
    gates = jnp.sum(jax.nn.one_hot(eidx, N_EXPERTS, dtype=jnp.float32) * w[..., None], axis=-2)
    shared = swiglu(t, sh_w_gate, sh_w_up, sh_w_down)

    def body(acc, xs):
        wg, wu, wd, ge = xs
        return acc + ge[:, None] * swiglu(t, wg, wu, wd), None

    routed, _ = lax.scan(body, jnp.zeros_like(t), (exp_w_gate, exp_w_up, exp_w_down, gates.T.astype(t.dtype)))
    return (shared + routed).reshape(shape)


def setup_inputs(seed: int = 0) -> dict:
    key = jax.random.key(seed)
    keys = list(jax.random.split(key, 32))
    D = D_MODEL

    def nrm(shape, s):
        return jax.random.normal(keys.pop(), shape, jnp.float32) * s

    return {
        'x': nrm((BATCH, SEQ, D), 1.0),
        'c': nrm((BATCH, D), 1.0),
        'ctx': nrm((BATCH, CTX_LEN, D), 1.0),
        'c_ctx': nrm((D,), 1.0),
        'ada_w': nrm((DEPTH, D, 6 * D), 0.5 * D ** -0.5),
        'ada_b': nrm((DEPTH, 6 * D), 0.02),
        'norm1_g': 1.0 + nrm((DEPTH, D), 0.02),
        'norm2_g': 1.0 + nrm((DEPTH, D), 0.02),
        'w_in': nrm((DEPTH, D, D_IN), D ** -0.5),
        'attn_sink': nrm((DEPTH, ATT_HEADS), 0.5),
        'mlstm_gate_b': nrm((DEPTH, 4, M_HEADS), 0.1) + jnp.array([0.0, FORGET_BIAS, 0.0, FORGET_BIAS], jnp.float32)[None, :, None],
        'mlstm_norm_g': 1.0 + nrm((DEPTH, M_OUT), 0.02),
        'conv_w': nrm((DEPTH, CONV_K, CONV_WIDTH), CONV_K ** -0.5),
        'w_br_attn': nrm((DEPTH, ATT_OUT, D), ATT_OUT ** -0.5),
        'w_br_mlstm': nrm((DEPTH, M_OUT, D), M_OUT ** -0.5),
        'w_br_conv': nrm((DEPTH, CONV_WIDTH, D), CONV_WIDTH ** -0.5),
        'branch_gate_b': nrm((DEPTH, N_BRANCH, D), 0.02),
        'w_out': nrm((DEPTH, D, D), D ** -0.5),
        'router_w': nrm((DEPTH, D, N_EXPERTS), D ** -0.5),
        'router_bias': nrm((DEPTH, N_EXPERTS), 0.01),
        'exp_w_gate': nrm((DEPTH, N_EXPERTS, D, EXPERT_FF), D ** -0.5),
        'exp_w_up': nrm((DEPTH, N_EXPERTS, D, EXPERT_FF), D ** -0.5),
        'exp_w_down': nrm((DEPTH, N_EXPERTS, EXPERT_FF, D), EXPERT_FF ** -0.5),
        'sh_w_gate': nrm((DEPTH, D, SHARED_FF), D ** -0.5),
        'sh_w_up': nrm((DEPTH, D, SHARED_FF), D ** -0.5),
        'sh_w_down': nrm((DEPTH, SHARED_FF, D), SHARED_FF ** -0.5),
        'final_g': 1.0 + nrm((D,), 0.02),
    }


def reference(x, c, ctx, c_ctx, ada_w, ada_b, norm1_g, norm2_g, w_in, attn_sink, mlstm_gate_b,
              mlstm_norm_g, conv_w, w_br_attn, w_br_mlstm, w_br_conv, branch_gate_b, w_out,
              router_w, router_bias, exp_w_gate, exp_w_up, exp_w_down, sh_w_gate, sh_w_up,
              sh_w_down, final_g):
    seq = x.shape[1]
    cos, sin = axial_rope(seq)
    xc = ctx
    s_lat = jax.nn.silu(c)
    s_ctx = jax.nn.silu(c_ctx)
    for l in range(DEPTH):
        need_ctx = l < DEPTH - 1
        mod = (s_lat @ ada_w[l] + ada_b[l])[:, None, :]
        modc = s_ctx @ ada_w[l] + ada_b[l]
        sh1, sc1, g1, sh2, sc2, g2 = jnp.split(mod, 6, axis=-1)
        csh1, csc1, cg1, csh2, csc2, cg2 = jnp.split(modc, 6, axis=-1)
        lp = {
            'w_in': w_in[l], 'attn_sink': attn_sink[l], 'mlstm_gate_b': mlstm_gate_b[l],
            'mlstm_norm_g': mlstm_norm_g[l], 'conv_w': conv_w[l], 'w_br_attn': w_br_attn[l],
            'w_br_mlstm': w_br_mlstm[l], 'w_br_conv': w_br_conv[l], 'branch_gate_b': branch_gate_b[l],
            'w_out': w_out[l],
        }
        h = modulate(rmsnorm(x, norm1_g[l]), sh1, sc1)
        hc = modulate(rmsnorm(xc, norm1_g[l]), csh1, csc1)
        y, yc = token_mixers(h, hc, lp, cos, sin, need_ctx)
        x = x + g1 * y
        h = modulate(rmsnorm(x, norm2_g[l]), sh2, sc2)
        x = x + g2 * moe(h, router_w[l], router_bias[l], exp_w_gate[l], exp_w_up[l], exp_w_down[l],
                         sh_w_gate[l], sh_w_up[l], sh_w_down[l])
        if need_ctx:
            xc = xc + cg1 * yc
            hc = modulate(rmsnorm(xc, norm2_g[l]), csh2, csc2)
            xc = xc + cg2 * moe(hc, router_w[l], router_bias[l], exp_w_gate[l], exp_w_up[l], exp_w_down[l],
                                sh_w_gate[l], sh_w_up[l], sh_w_down[l])
    return rmsnorm(x, final_g)
```

```python
import functools

import numpy as np
import jax
import jax.numpy as jnp
from jax import lax
from jax.experimental import pallas as pl
from jax.experimental.pallas import tpu as pltpu

F32 = jnp.float32
BF16 = jnp.bfloat16

D_MODEL = 1024
GRID_W = 64
EPS = 1e-6
ATT_HEADS = 8
ATT_KV_HEADS = 2
ATT_HEAD_DIM = 64
ATT_GROUP = ATT_HEADS // ATT_KV_HEADS
ATT_BLOCK = 128
ATT_OUT = ATT_HEADS * ATT_HEAD_DIM
ROPE_BASE = 10000.0
M_HEADS = 4
M_QK_DIM = 64
M_V_DIM = 128
M_CHUNK = 64
M_OUT = M_HEADS * M_V_DIM
CONV_WIDTH = 512
N_BRANCH = 3
N_EXPERTS = 64
N_GROUPS = 8
GROUP_SIZE = N_EXPERTS // N_GROUPS
TOPK_GROUPS = 4
TOP_K = 8
EXPERT_FF = 256
SHARED_FF = 256
ROUTED_SCALE = 2.5

LANES = 128
BF16_SUBLANES = 16
VMEM_LIMIT = 56 * 1024 * 1024

_SEGS = (('q', 512), ('k', 128), ('v', 128), ('qm', 256), ('km', 256), ('vm', 512), ('om', 512),
         ('bc', 512), ('cc', 512), ('xc', 512), ('gt', 3072), ('gm', 128))
_OFF = {}
_o = 0
for _n, _s in _SEGS:
    _OFF[_n] = (_o, _o + _s)
    _o += _s
N_PROJ = _o
D_IN = 6928


def _proj_column_index():
    idx = []
    half = ATT_HEAD_DIM // 2
    for hh in range(ATT_GROUP):
        for g in range(ATT_KV_HEADS):
            head = g * ATT_GROUP + hh
            for par in range(2):
                idx += [head * ATT_HEAD_DIM + 2 * i + par for i in range(half)]
    for g in range(ATT_KV_HEADS):
        for par in range(2):
            idx += [512 + g * ATT_HEAD_DIM + 2 * i + par for i in range(half)]
    idx += list(range(640, 768))
    idx += list(range(768, 2304))
    idx += list(range(2320, 3856))
    idx += list(range(3856, 6928))
    idx += list(range(2304, 2320)) + [D_IN] * (LANES - 16)
    assert len(idx) == N_PROJ
    return np.asarray(idx, np.int32)


def _attn_row_index():
    idx = []
    for hh in range(ATT_GROUP):
        for g in range(ATT_KV_HEADS):
            head = g * ATT_GROUP + hh
            idx += [head * ATT_HEAD_DIM + d for d in range(ATT_HEAD_DIM)]
    return np.asarray(idx, np.int32)


def _rope_tables(seq):
    rows = seq // GRID_W
    row = jnp.repeat(jnp.arange(rows, dtype=F32), GRID_W)
    col = jnp.tile(jnp.arange(GRID_W, dtype=F32), rows)
    n_pairs = ATT_HEAD_DIM // 4
    inv_freq = ROPE_BASE ** (-jnp.arange(n_pairs, dtype=F32) / n_pairs)
    ang = jnp.concatenate([row[:, None] * inv_freq, col[:, None] * inv_freq], axis=-1)
    c, s = jnp.cos(ang), jnp.sin(ang)
    cos_t = jnp.concatenate([c, c, c, c], axis=-1)
    sin_t = jnp.concatenate([-s, s, -s, s], axis=-1)
    return cos_t, sin_t


def _dot(a, b):
    return jnp.dot(a, b, preferred_element_type=F32)


def _dot_nt(a, b):
    return lax.dot_general(a, b, (((1,), (1,)), ((), ())), preferred_element_type=F32)


def _dot_tn(a, b):
    return lax.dot_general(a, b, (((0,), (0,)), ((), ())), preferred_element_type=F32)


def _sigmoid(x):
    return 1.0 / (1.0 + jnp.exp(-x))


def _silu(x):
    return x * _sigmoid(x)


def _log_sigmoid(x):
    return jnp.minimum(x, 0.0) - jnp.log(1.0 + jnp.exp(-jnp.abs(x)))


def _rms_mod(x, g, shift, scale):
    y = x * lax.rsqrt(jnp.mean(x * x, axis=-1, keepdims=True) + EPS) * g
    return y * (1.0 + scale) + shift


def _params(sem):
    return pltpu.CompilerParams(dimension_semantics=sem, vmem_limit_bytes=VMEM_LIMIT)


def _ada_kernel(c_ref, w_ref, b_ref, o_ref):
    s = _silu(c_ref[...])
    o_ref[0] = jnp.dot(s, w_ref[0], preferred_element_type=F32,
                       precision=lax.Precision.HIGHEST) + b_ref[0]


def _ada(cc, ada_w, ada_b):
    depth, d, n = ada_w.shape
    rows = cc.shape[0]
    tn = 1536
    return pl.pallas_call(
        _ada_kernel,
        grid=(depth, n // tn),
        in_specs=[pl.BlockSpec((rows, d), lambda l, j: (0, 0)),
                  pl.BlockSpec((1, d, tn), lambda l, j: (l, 0, j)),
                  pl.BlockSpec((1, 1, tn), lambda l, j: (l, 0, j))],
        out_specs=pl.BlockSpec((1, rows, tn), lambda l, j: (l, 0, j)),
        out_shape=jax.ShapeDtypeStruct((depth, rows, n), F32),
        compiler_params=_params(("parallel", "parallel")),
        name="ada_mod",
    )(cc, ada_w, ada_b.reshape(depth, 1, n))


def _swap_halves(x):
    lane = lax.broadcasted_iota(jnp.int32, x.shape, 1)
    first = (lane % ATT_HEAD_DIM) < (ATT_HEAD_DIM // 2)
    return jnp.where(first, pltpu.roll(x, LANES - 32, axis=1), pltpu.roll(x, 32, axis=1))


def _in_kernel(x_ref, mod_ref, g_ref, w_ref, cos_ref, sin_ref,
               qs_ref, k_ref, v_ref, qm_ref, km_ref, vm_ref, om_ref, gm_ref, bc_ref, u_ref, gt_ref):
    tm = x_ref.shape[1]
    h = _rms_mod(x_ref[0], g_ref[...], mod_ref[0, 0:1, :], mod_ref[0, 1:2, :]).astype(BF16)

    def proj(name):
        lo, hi = _OFF[name]
        return _dot(h, w_ref[:, lo:hi])

    cos_t = cos_ref[...]
    sin_t = sin_ref[...]

    def rope(t):
        return t * cos_t + _swap_halves(t) * sin_t

    q = proj('q')
    scale = ATT_HEAD_DIM ** -0.5
    for hh in range(ATT_GROUP):
        r = (rope(q[:, hh * LANES:(hh + 1) * LANES]) * scale).astype(BF16)
        for qb in range(tm // ATT_BLOCK):
            qs_ref[0, qb, hh * ATT_BLOCK:(hh + 1) * ATT_BLOCK, :] = r[qb * ATT_BLOCK:(qb + 1) * ATT_BLOCK, :]
    k_ref[0] = rope(proj('k')).astype(BF16)
    v_ref[0] = proj('v').astype(BF16)
    qm_ref[0] = proj('qm').astype(BF16)
    km_ref[0] = proj('km').astype(BF16)
    vm_ref[0] = proj('vm').astype(BF16)
    om_ref[0] = proj('om').astype(BF16)
    gm_ref[0] = proj('gm')
    bc_ref[0] = proj('bc').astype(BF16)
    u_ref[0] = (proj('cc') * proj('xc')).astype(BF16)
    gt_ref[0] = proj('gt').astype(BF16)


def _in_proj(x, mod, mod_row, norm_g, w_p, cos_t, sin_t, tm):
    B, S, D = x.shape
    nb = S // ATT_BLOCK
    tok = lambda n, dt: jax.ShapeDtypeStruct((B, S, n), dt)
    tspec = lambda n: pl.BlockSpec((1, tm, n), lambda b, i: (b, i, 0))
    out_shape = (jax.ShapeDtypeStruct((B, nb, ATT_GROUP * ATT_BLOCK, LANES), BF16),
                 tok(128, BF16), tok(128, BF16), tok(256, BF16), tok(256, BF16), tok(512, BF16),
                 tok(512, BF16), tok(128, F32), tok(512, BF16), tok(512, BF16), tok(3072, BF16))
    out_specs = (pl.BlockSpec((1, tm // ATT_BLOCK, ATT_GROUP * ATT_BLOCK, LANES), lambda b, i: (b, i, 0, 0)),
                 tspec(128), tspec(128), tspec(256), tspec(256), tspec(512), tspec(512), tspec(128),
                 tspec(512), tspec(512), tspec(3072))
    return pl.pallas_call(
        _in_kernel,
        grid=(B, S // tm),
        in_specs=[pl.BlockSpec((1, tm, D), lambda b, i: (b, i, 0)),
                  pl.BlockSpec((1, 6, D), lambda b, i: (mod_row(b), 0, 0)),
                  pl.BlockSpec((1, D), lambda b, i: (0, 0)),
                  pl.BlockSpec((D, N_PROJ), lambda b, i: (0, 0), pipeline_mode=pl.Buffered(1)),
                  pl.BlockSpec((tm, LANES), lambda b, i: (i, 0)),
                  pl.BlockSpec((tm, LANES), lambda b, i: (i, 0))],
        out_specs=out_specs,
        out_shape=out_shape,
        compiler_params=_params(("parallel", "parallel")),
        name="in_proj",
    )(x, mod, norm_g.reshape(1, D), w_p, cos_t, sin_t)


def _attn_kernel(sink_ref, qs_ref, kc_ref, vc_ref, *rest, band):
    if band:
        kp_ref, kcur_ref, kn_ref, vp_ref, vcur_ref, vn_ref, o_ref = rest
    else:
        (o_ref,) = rest
    j = pl.program_id(1)
    nblk = pl.num_programs(1)
    q = qs_ref[0, 0]
    rows = q.shape[0]
    if band:
        kcat = jnp.concatenate([kc_ref[0], kp_ref[0], kcur_ref[0], kn_ref[0]], axis=0)
        vcat = jnp.concatenate([vc_ref[0], vp_ref[0], vcur_ref[0], vn_ref[0]], axis=0)
    else:
        kcat = kc_ref[0]
        vcat = vc_ref[0]
    nkeys = kcat.shape[0]
    lc = kc_ref.shape[1]
    lane = lax.broadcasted_iota(jnp.int32, (1, LANES), 1)
    t = lax.broadcasted_iota(jnp.int32, (rows, 1), 0) % ATT_BLOCK
    hh = lax.broadcasted_iota(jnp.int32, (rows, 1), 0) // ATT_BLOCK
    if band:
        c = lax.broadcasted_iota(jnp.int32, (1, nkeys), 1)
        t_prev = t + jnp.where(j > 0, 0, 2 * ATT_BLOCK)
        t_next = t - jnp.where(j < nblk - 1, 0, 2 * ATT_BLOCK)
        i_prev = c - lc
        i_next = c - (lc + 2 * ATT_BLOCK)
        valid = ((c < lc)
                 | ((c >= lc) & (c < lc + ATT_BLOCK) & (i_prev >= t_prev))
                 | ((c >= lc + ATT_BLOCK) & (c < lc + 2 * ATT_BLOCK))
                 | ((c >= lc + 2 * ATT_BLOCK) & (i_next <= t_next)))
    out = jnp.zeros((rows, LANES), F32)
    for g in range(ATT_KV_HEADS):
        lm = (lane < ATT_HEAD_DIM) if g == 0 else (lane >= ATT_HEAD_DIM)
        kz = jnp.where(lm, kcat, jnp.zeros_like(kcat))
        vz = jnp.where(lm, vcat, jnp.zeros_like(vcat))
        s = _dot_nt(q, kz)
        if band:
            s = jnp.where(valid, s, -jnp.inf)
        sink = jnp.zeros((rows, 1), F32)
        for a in range(ATT_GROUP):
            sink = jnp.where(hh == a, sink_ref[g * ATT_GROUP + a], sink)
        m = jnp.maximum(jnp.max(s, axis=-1, keepdims=True), sink)
        p = jnp.exp(s - m)
        l = jnp.sum(p, axis=-1, keepdims=True) + jnp.exp(sink - m)
        out = out + _dot(p.astype(BF16), vz) / l
    o_ref[0, 0] = out.astype(BF16)


def _attention(qs, k, v, kc, vc, sink, band):
    B, nb = qs.shape[:2]
    lc = kc.shape[1]
    last = nb - 1
    qspec = pl.BlockSpec((1, 1, ATT_GROUP * ATT_BLOCK, LANES), lambda b, j: (b, j, 0, 0))
    cspec = pl.BlockSpec((1, lc, LANES), lambda b, j: (b, 0, 0))
    in_specs = [pl.BlockSpec(memory_space=pltpu.SMEM), qspec, cspec, cspec]
    args = [sink.astype(F32), qs, kc, vc]
    if band:
        prev = pl.BlockSpec((1, ATT_BLOCK, LANES), lambda b, j: (b, jnp.maximum(j - 1, 0), 0))
        cur = pl.BlockSpec((1, ATT_BLOCK, LANES), lambda b, j: (b, j, 0))
        nxt = pl.BlockSpec((1, ATT_BLOCK, LANES), lambda b, j: (b, jnp.minimum(j + 1, last), 0))
        in_specs += [prev, cur, nxt, prev, cur, nxt]
        args += [k, k, k, v, v, v]
    return pl.pallas_call(
        functools.partial(_attn_kernel, band=band),
        grid=(B, nb),
        in_specs=in_specs,
        out_specs=qspec,
        out_shape=jax.ShapeDtypeStruct(qs.shape, BF16),
        compiler_params=_params(("parallel", "parallel")),
        name="attention_band" if band else "attention_ctx",
    )(*args)


def _mlstm_chunk(q, k, v, ig, lf, C, n, m, causal):
    T = q.shape[0]
    si = lax.broadcasted_iota(jnp.int32, (T, T), 0)
    ri = lax.broadcasted_iota(jnp.int32, (T, T), 1)
    eye = si == ri
    tri = (ri <= si) if causal else (ri >= si)
    to_row = lambda col: jnp.sum(jnp.where(eye, col, 0.0), axis=0, keepdims=True)
    lf_row = to_row(lf)
    ig_row = to_row(ig)
    b_col = jnp.sum(jnp.where(tri, lf_row, 0.0), axis=1, keepdims=True)
    b_row = to_row(b_col)
    dmat = jnp.where(tri, b_col - b_row + ig_row, -jnp.inf)
    a_int = b_col + m
    m_s = jnp.maximum(a_int, jnp.max(dmat, axis=1, keepdims=True))
    w_int = jnp.exp(a_int - m_s)
    w_intra = jnp.exp(dmat - m_s)
    kscale = M_QK_DIM ** -0.5
    s_qk = w_intra * (_dot_nt(q, k) * kscale)
    num = _dot(s_qk.astype(BF16), v) + w_int * _dot(q, C.astype(BF16))
    qn = jnp.sum(q.astype(F32) * n, axis=1, keepdims=True)
    den = jnp.sum(s_qk, axis=1, keepdims=True) + w_int * qn
    h = num / jnp.maximum(jnp.abs(den), jnp.exp(-m_s))
    b_last = jnp.sum(lf_row, axis=1, keepdims=True)
    r_col = b_last - b_col + ig
    m_new = jnp.maximum(b_last + m, jnp.max(r_col, axis=0, keepdims=True))
    decay = jnp.exp(b_last + m - m_new)
    kw = k.astype(F32) * (jnp.exp(r_col - m_new) * kscale)
    C_new = decay * C + _dot_tn(kw.astype(BF16), v)
    n_new = decay * n + jnp.sum(kw, axis=0, keepdims=True)
    return h, C_new, n_new, m_new


def _mlstm_kernel(gb_ref, qf_ref, kf_ref, vf_ref, gf_ref, qb_ref, kb_ref, vb_ref, gbk_ref,
                  c0_ref, n0_ref, m0_ref, hf_ref, hb_ref, cf_ref, nf_ref, mf_ref,
                  c_s, n_s, m_s):
    ci = pl.program_id(1)

    @pl.when(ci == 0)
    def _():
        c_s[...] = c0_ref[0]
        n_s[...] = n0_ref[0]
        m_s[...] = m0_ref[0]

    for d, (q_ref, k_ref, v_ref, g_ref, h_ref) in enumerate(
            ((qf_ref, kf_ref, vf_ref, gf_ref, hf_ref), (qb_ref, kb_ref, vb_ref, gbk_ref, hb_ref))):
        g = g_ref[0]
        for hd in range(M_HEADS):
            s = d * M_HEADS + hd
            ci_col = (2 * d) * M_HEADS + hd
            cf_col = (2 * d + 1) * M_HEADS + hd
            ig = g[:, ci_col:ci_col + 1] + gb_ref[ci_col]
            lf = _log_sigmoid(g[:, cf_col:cf_col + 1] + gb_ref[cf_col])
            q = q_ref[0, :, hd * M_QK_DIM:(hd + 1) * M_QK_DIM]
            k = k_ref[0, :, hd * M_QK_DIM:(hd + 1) * M_QK_DIM]
            v = v_ref[0, :, hd * M_V_DIM:(hd + 1) * M_V_DIM]
            h, c_new, n_new, m_new = _mlstm_chunk(q, k, v, ig, lf, c_s[s], n_s[s], m_s[s][:, 0:1],
                                                  causal=(d == 0))
            h_ref[0, :, hd * M_V_DIM:(hd + 1) * M_V_DIM] = h
            c_s[s] = c_new
            n_s[s] = n_new
            m_s[s] = jnp.broadcast_to(m_new, (1, LANES))

    @pl.when(ci == pl.num_programs(1) - 1)
    def _():
        cf_ref[0] = c_s[...]
        nf_ref[0] = n_s[...]
        mf_ref[0] = m_s[...]


def _mlstm(qm, km, vm, gm, gate_b, state):
    B, S, _ = qm.shape
    T = M_CHUNK
    nc = S // T
    ns = 2 * M_HEADS
    fwd = lambda n: pl.BlockSpec((1, T, n), lambda b, c: (b, c, 0))
    bwd = lambda n: pl.BlockSpec((1, T, n), lambda b, c: (b, nc - 1 - c, 0))
    st_specs = [pl.BlockSpec((1, ns, M_QK_DIM, M_V_DIM), lambda b, c: (b, 0, 0, 0)),
                pl.BlockSpec((1, ns, 1, M_QK_DIM), lambda b, c: (b, 0, 0, 0)),
                pl.BlockSpec((1, ns, 1, LANES), lambda b, c: (b, 0, 0, 0))]
    st_shapes = [jax.ShapeDtypeStruct((B, ns, M_QK_DIM, M_V_DIM), F32),
                 jax.ShapeDtypeStruct((B, ns, 1, M_QK_DIM), F32),
                 jax.ShapeDtypeStruct((B, ns, 1, LANES), F32)]
    outs = pl.pallas_call(
        _mlstm_kernel,
        grid=(B, nc),
        in_specs=[pl.BlockSpec(memory_space=pltpu.SMEM),
                  fwd(256), fwd(256), fwd(512), fwd(LANES),
                  bwd(256), bwd(256), bwd(512), bwd(LANES)] + st_specs,
        out_specs=[fwd(M_OUT), bwd(M_OUT)] + st_specs,
        out_shape=[jax.ShapeDtypeStruct((B, S, M_OUT), F32)] * 2 + st_shapes,
        scratch_shapes=[pltpu.VMEM((ns, M_QK_DIM, M_V_DIM), F32),
                        pltpu.VMEM((ns, 1, M_QK_DIM), F32),
                        pltpu.VMEM((ns, 1, LANES), F32)],
        compiler_params=_params(("parallel", "arbitrary")),
        name="mlstm_scan",
    )(gate_b.reshape(-1).astype(F32), qm, km, vm, gm, qm, km, vm, gm, *state)
    return outs[0], outs[1], tuple(outs[2:])


def _route(scores, sel):
    tm = scores.shape[1]
    gi8 = lax.broadcasted_iota(jnp.int32, (GROUP_SIZE, tm), 0)

    def stack_rows(rows):
        out = jnp.broadcast_to(rows[0], (len(rows), tm))
        for r, v in enumerate(rows[1:], start=1):
            out = jnp.where(gi8 == r, v, out)
        return out

    gs = []
    for g in range(N_GROUPS):
        blk = sel[g * GROUP_SIZE:(g + 1) * GROUP_SIZE, :]
        m1 = jnp.max(blk, axis=0, keepdims=True)
        first = jnp.min(jnp.where(blk == m1, gi8, GROUP_SIZE), axis=0, keepdims=True)
        m2 = jnp.max(jnp.where(gi8 == first, -jnp.inf, blk), axis=0, keepdims=True)
        gs.append(m1 + m2)
    gsc = stack_rows(gs)
    gsel = jnp.zeros((N_GROUPS, tm), F32)
    for _ in range(TOPK_GROUPS):
        mx = jnp.max(gsc, axis=0, keepdims=True)
        first = jnp.min(jnp.where(gsc == mx, gi8, N_GROUPS), axis=0, keepdims=True)
        pick = gi8 == first
        gsel = jnp.where(pick, 1.0, gsel)
        gsc = jnp.where(pick, -jnp.inf, gsc)
    cur = jnp.concatenate(
        [jnp.where(gsel[g:g + 1, :] > 0.0, sel[g * GROUP_SIZE:(g + 1) * GROUP_SIZE, :], -jnp.inf)
         for g in range(N_GROUPS)], axis=0)
    ei = lax.broadcasted_iota(jnp.int32, (N_EXPERTS, tm), 0)
    idx, wts = [], []
    for _ in range(TOP_K):
        mx = jnp.max(cur, axis=0, keepdims=True)
        first = jnp.min(jnp.where(cur == mx, ei, N_EXPERTS), axis=0, keepdims=True)
        pick = ei == first
        idx.append(first)
        wts.append(jnp.sum(jnp.where(pick, scores, 0.0), axis=0, keepdims=True))
        cur = jnp.where(pick, -jnp.inf, cur)
    tot = wts[0]
    for w in wts[1:]:
        tot = tot + w
    wts = [w / tot * ROUTED_SCALE for w in wts]
    return stack_rows(idx), stack_rows(wts)


def _merge_kernel(x_ref, mod_ref, oat_ref, hf_ref, hb_ref, om_ref, ng_ref, u_ref, up_ref, un_ref,
                  bc_ref, cw_ref, gt_ref, bgb_ref, wa_ref, wm_ref, wc_ref, wo_ref, n2_ref,
                  rw_ref, rb_ref, sgu_ref, sd_ref,
                  base_ref, h2_ref, idx_ref, wt_ref, gd_ref):
    i = pl.program_id(1)
    tm = x_ref.shape[1]
    x = x_ref[0]
    g1 = mod_ref[0, 2:3, :]
    sh2 = mod_ref[0, 3:4, :]
    sc2 = mod_ref[0, 4:5, :]
    g2 = mod_ref[0, 5:6, :]

    ya = jnp.concatenate(
        [jnp.concatenate([oat_ref[0, qb, hh * ATT_BLOCK:(hh + 1) * ATT_BLOCK, :] for hh in range(ATT_GROUP)], axis=1)
         for qb in range(tm // ATT_BLOCK)], axis=0)

    hsum = hf_ref[0] + hb_ref[0]
    parts = []
    for hd in range(M_HEADS):
        hh_ = hsum[:, hd * M_V_DIM:(hd + 1) * M_V_DIM]
        parts.append(hh_ * lax.rsqrt(jnp.mean(hh_ * hh_, axis=-1, keepdims=True) + EPS))
    hn = jnp.concatenate(parts, axis=1) * ng_ref[...]
    ym = (_sigmoid(om_ref[0].astype(F32)) * hn).astype(BF16)

    u = u_ref[0].astype(F32)
    row = lax.broadcasted_iota(jnp.int32, (tm, 1), 0)
    has_prev = (i > 0).astype(F32)
    has_next = (i < pl.num_programs(1) - 1).astype(F32)
    prev_row = up_ref[0, BF16_SUBLANES - 1:BF16_SUBLANES, :].astype(F32) * has_prev
    next_row = un_ref[0, 0:1, :].astype(F32) * has_next
    u_m1 = jnp.where(row == 0, prev_row, pltpu.roll(u, 1, axis=0))
    u_p1 = jnp.where(row == tm - 1, next_row, pltpu.roll(u, tm - 1, axis=0))
    conv = cw_ref[0:1, :] * u_m1 + cw_ref[1:2, :] * u + cw_ref[2:3, :] * u_p1
    yc = (bc_ref[0].astype(F32) * conv).astype(BF16)

    gg = _sigmoid(gt_ref[0].astype(F32) + bgb_ref[...])
    ymix = (gg[:, 0:D_MODEL] * _dot(ya, wa_ref[...])
            + gg[:, D_MODEL:2 * D_MODEL] * _dot(ym, wm_ref[...])
            + gg[:, 2 * D_MODEL:3 * D_MODEL] * _dot(yc, wc_ref[...]))
    y = _dot(ymix.astype(BF16), wo_ref[...])
    xm = x + g1 * y

    h2f = _rms_mod(xm, n2_ref[...], sh2, sc2)
    h2 = h2f.astype(BF16)
    h2_ref[0] = h2

    logits_t = lax.dot_general(rw_ref[...], h2f, (((1,), (1,)), ((), ())),
                               preferred_element_type=F32, precision=lax.Precision.HIGHEST)
    scores = _sigmoid(logits_t)
    idx, wts = _route(scores, scores + rb_ref[...])
    idx_ref[0] = idx
    wt_ref[0] = wts
    ei = lax.broadcasted_iota(jnp.int32, (N_EXPERTS, tm), 0)
    gd = jnp.zeros((N_EXPERTS, tm), F32)
    for kk in range(TOP_K):
        gd = gd + jnp.where(ei == idx[kk:kk + 1, :], wts[kk:kk + 1, :], 0.0)
    gd_ref[0] = gd

    a = _dot(h2, sgu_ref[...])
    act = (_silu(a[:, 0:SHARED_FF]) * a[:, SHARED_FF:2 * SHARED_FF]).astype(BF16)
    base_ref[0] = xm + g2 * _dot(act, sd_ref[...])


def _merge(x, mod, mod_row, oat, hf, hb, om, u, bc, gt, lw, tm):
    B, S, D = x.shape
    nt = S // tm
    hal = BF16_SUBLANES
    last_h = S // hal - 1
    tspec = lambda n: pl.BlockSpec((1, tm, n), lambda b, i: (b, i, 0))
    full = lambda a: pl.BlockSpec(a.shape, lambda b, i: (0,) * a.ndim)
    weights = [lw['mlstm_norm_g'], lw['conv_w'], lw['branch_gate_b'], lw['w_br_attn'], lw['w_br_mlstm'],
               lw['w_br_conv'], lw['w_out'], lw['norm2_g'], lw['router_wt'], lw['router_bias'],
               lw['sh_gu'], lw['sh_d']]
    in_specs = [tspec(D),
                pl.BlockSpec((1, 6, D), lambda b, i: (mod_row(b), 0, 0)),
                pl.BlockSpec((1, tm // ATT_BLOCK, ATT_GROUP * ATT_BLOCK, LANES), lambda b, i: (b, i, 0, 0)),
                tspec(M_OUT), tspec(M_OUT), tspec(M_OUT), full(weights[0]),
                tspec(CONV_WIDTH),
                pl.BlockSpec((1, hal, CONV_WIDTH), lambda b, i: (b, jnp.maximum(i * (tm // hal) - 1, 0), 0)),
                pl.BlockSpec((1, hal, CONV_WIDTH), lambda b, i: (b, jnp.minimum((i + 1) * (tm // hal), last_h), 0)),
                tspec(CONV_WIDTH), full(weights[1]), tspec(N_BRANCH * D), full(weights[2])]
    in_specs += [full(w) for w in weights[3:]]
    tr = lambda n, dt: (jax.ShapeDtypeStruct((B, n, S), dt), pl.BlockSpec((1, n, tm), lambda b, i: (b, 0, i)))
    outs = [(jax.ShapeDtypeStruct((B, S, D), F32), tspec(D)),
            (jax.ShapeDtypeStruct((B, S, D), BF16), tspec(D)),
            tr(TOP_K, jnp.int32), tr(TOP_K, F32), tr(N_EXPERTS, F32)]
    return pl.pallas_call(
        _merge_kernel,
        grid=(B, nt),
        in_specs=in_specs,
        out_specs=[o[1] for o in outs],
        out_shape=[o[0] for o in outs],
        compiler_params=_params(("parallel", "parallel")),
        name="merge_route",
    )(x, mod, oat, hf, hb, om, weights[0], u, u, u, bc, weights[1], gt, weights[2], *weights[3:])


def _moe_kernel(h_ref, g_ref, base_ref, mod_ref, wgu_ref, wd_ref, o_ref, acc_ref):
    e = pl.program_id(2)

    @pl.when(e == 0)
    def _():
        acc_ref[...] = jnp.zeros_like(acc_ref)

    a = _dot(h_ref[0], wgu_ref[0])
    lane = lax.broadcasted_iota(jnp.int32, g_ref.shape[1:], 1)
    ge = jnp.sum(jnp.where(lane == e, g_ref[0], 0.0), axis=1, keepdims=True)
    act = (_silu(a[:, 0:EXPERT_FF]) * a[:, EXPERT_FF:2 * EXPERT_FF]) * ge
    acc_ref[...] += _dot(act.astype(BF16), wd_ref[0])

    @pl.when(e == pl.num_programs(2) - 1)
    def _():
        o_ref[0] = base_ref[0] + mod_ref[0, 5:6, :] * acc_ref[...]


def _moe_dense(h2, gates, base, mod, mod_row, wgu, wd, tm):
    B, S, D = h2.shape
    tspec = lambda n: pl.BlockSpec((1, tm, n), lambda b, i, e: (b, i, 0))
    return pl.pallas_call(
        _moe_kernel,
        grid=(B, S // tm, N_EXPERTS),
        in_specs=[tspec(D), tspec(N_EXPERTS), tspec(D),
                  pl.BlockSpec((1, 6, D), lambda b, i, e: (mod_row(b), 0, 0)),
                  pl.BlockSpec((1, D, 2 * EXPERT_FF), lambda b, i, e: (e, 0, 0)),
                  pl.BlockSpec((1, EXPERT_FF, D), lambda b, i, e: (e, 0, 0))],
        out_specs=tspec(D),
        out_shape=jax.ShapeDtypeStruct((B, S, D), F32),
        scratch_shapes=[pltpu.VMEM((tm, D), F32)],
        compiler_params=_params(("parallel", "parallel", "arbitrary")),
        name="moe_experts",
    )(h2, gates, base, mod, wgu, wd)


def _final_kernel(x_ref, g_ref, o_ref):
    x = x_ref[0]
    o_ref[0] = x * lax.rsqrt(jnp.mean(x * x, axis=-1, keepdims=True) + EPS) * g_ref[...]


def _final_norm(x, g, tm):
    B, S, D = x.shape
    spec = pl.BlockSpec((1, tm, D), lambda b, i: (b, i, 0))
    return pl.pallas_call(
        _final_kernel,
        grid=(B, S // tm),
        in_specs=[spec, pl.BlockSpec((1, D), lambda b, i: (0, 0))],
        out_specs=spec,
        out_shape=jax.ShapeDtypeStruct((B, S, D), F32),
        compiler_params=_params(("parallel", "parallel")),
        name="final_norm",
    )(x, g.reshape(1, D))


def _zero_state(batch):
    ns = 2 * M_HEADS
    return (jnp.zeros((batch, ns, M_QK_DIM, M_V_DIM), F32),
            jnp.zeros((batch, ns, 1, M_QK_DIM), F32),
            jnp.zeros((batch, ns, 1, LANES), F32))


def kernel(x, c, ctx, c_ctx, ada_w, ada_b, norm1_g, norm2_g, w_in, attn_sink, mlstm_gate_b, mlstm_norm_g, conv_w, w_br_attn, w_br_mlstm, w_br_conv, branch_gate_b, w_out, router_w, router_bias, exp_w_gate, exp_w_up, exp_w_down, sh_w_gate, sh_w_up, sh_w_down, final_g):
    B, S, D = x.shape
    L = ctx.shape[1]
    depth = ada_w.shape[0]
    ctx_row = B

    pad_rows = (-(B + 1)) % 8
    cc = jnp.concatenate([c, c_ctx[None, :], jnp.zeros((pad_rows, D), F32)], axis=0)
    mod_all = _ada(cc, ada_w, ada_b).reshape(depth, B + 1 + pad_rows, 6, D)

    cos_t, sin_t = _rope_tables(S)
    cos_c = jnp.ones((L, LANES), F32)
    sin_c = jnp.zeros((L, LANES), F32)
    col_idx = _proj_column_index()
    att_idx = _attn_row_index()
    lat_row = lambda b: b
    ctx_mod = lambda b: ctx_row

    xc = ctx
    for l in range(depth):
        need_ctx = l < depth - 1
        mod = mod_all[l]
        w_ext = jnp.concatenate([w_in[l], jnp.zeros((D, 1), F32)], axis=1)
        w_p = jnp.take(w_ext, col_idx, axis=1).astype(BF16)
        lw = {
            'mlstm_norm_g': mlstm_norm_g[l].reshape(1, M_OUT),
            'conv_w': conv_w[l],
            'branch_gate_b': branch_gate_b[l].reshape(1, N_BRANCH * D),
            'w_br_attn': jnp.take(w_br_attn[l], att_idx, axis=0).astype(BF16),
            'w_br_mlstm': w_br_mlstm[l].astype(BF16),
            'w_br_conv': w_br_conv[l].astype(BF16),
            'w_out': w_out[l].astype(BF16),
            'norm2_g': norm2_g[l].reshape(1, D),
            'router_wt': router_w[l].T,
            'router_bias': router_bias[l].reshape(N_EXPERTS, 1),
            'sh_gu': jnp.concatenate([sh_w_gate[l], sh_w_up[l]], axis=1).astype(BF16),
            'sh_d': sh_w_down[l].astype(BF16),
        }
        wgu = jnp.concatenate([exp_w_gate[l], exp_w_up[l]], axis=2).astype(BF16)
        wd = exp_w_down[l].astype(BF16)

        pc = _in_proj(xc, mod, ctx_mod, norm1_g[l], w_p, cos_c, sin_c, tm=256)
        p = _in_proj(x, mod, lat_row, norm1_g[l], w_p, cos_t, sin_t, tm=256)
        qs_c, k_c, v_c, qm_c, km_c, vm_c, om_c, gm_c, bc_c, u_c, gt_c = pc
        qs, k, v, qm, km, vm, om, gm, bc, u, gt = p

        oat = _attention(qs, k, v, k_c, v_c, attn_sink[l], band=True)
        hf_c, hb_c, st = _mlstm(qm_c, km_c, vm_c, gm_c, mlstm_gate_b[l], _zero_state(B))
        hf, hb, _ = _mlstm(qm, km, vm, gm, mlstm_gate_b[l], st)

        base, h2, _, _, gd = _merge(x, mod, lat_row, oat, hf, hb, om, u, bc, gt, lw, tm=256)
        gates = jnp.swapaxes(gd, 1, 2)
        x_new = _moe_dense(h2, gates, base, mod, lat_row, wgu, wd, tm=min(1024, S))

        if need_ctx:
            oat_c = _attention(qs_c, None, None, k_c, v_c, attn_sink[l], band=False)
            base_c, h2_c, _, _, gd_c = _merge(xc, mod, ctx_mod, oat_c, hf_c, hb_c, om_c, u_c, bc_c, gt_c, lw, tm=256)
            xc = _moe_dense(h2_c, jnp.swapaxes(gd_c, 1, 2), base_c, mod, ctx_mod, wgu, wd, tm=256)
        x = x_new
    return _final_norm(x, final_g, tm=512)
```

```python
import functools

import numpy as np
import jax
import jax.numpy as jnp
from jax import lax
from jax.experimental import pallas as pl
from jax.experimental.pallas import tpu as pltpu

F32 = jnp.float32
BF16 = jnp.bfloat16

D_MODEL = 1024
GRID_W = 64
EPS = 1e-6
ATT_HEADS = 8
ATT_KV_HEADS = 2
ATT_HEAD_DIM = 64
ATT_GROUP = ATT_HEADS // ATT_KV_HEADS
ATT_BLOCK = 128
ATT_OUT = ATT_HEADS * ATT_HEAD_DIM
ROPE_BASE = 10000.0
M_HEADS = 4
M_QK_DIM = 64
M_V_DIM = 128
M_CHUNK = 64
M_OUT = M_HEADS * M_V_DIM
CONV_WIDTH = 512
N_BRANCH = 3
N_EXPERTS = 64
N_GROUPS = 8
GROUP_SIZE = N_EXPERTS // N_GROUPS
TOPK_GROUPS = 4
TOP_K = 8
EXPERT_FF = 256
SHARED_FF = 256
ROUTED_SCALE = 2.5

LANES = 128
BF16_SUBLANES = 16
VMEM_LIMIT = 56 * 1024 * 1024

_SEGS = (('q', 512), ('k', 128), ('v', 128), ('qm', 256), ('km', 256), ('vm', 512), ('om', 512),
         ('bc', 512), ('cc', 512), ('xc', 512), ('gt', 3072), ('gm', 128))
_OFF = {}
_o = 0
for _n, _s in _SEGS:
    _OFF[_n] = (_o, _o + _s)
    _o += _s
N_PROJ = _o
D_IN = 6928


def _proj_column_index():
    idx = []
    half = ATT_HEAD_DIM // 2
    for hh in range(ATT_GROUP):
        for g in range(ATT_KV_HEADS):
            head = g * ATT_GROUP + hh
            for par in range(2):
                idx += [head * ATT_HEAD_DIM + 2 * i + par for i in range(half)]
    for g in range(ATT_KV_HEADS):
        for par in range(2):
            idx += [512 + g * ATT_HEAD_DIM + 2 * i + par for i in range(half)]
    idx += list(range(640, 768))
    idx += list(range(768, 2304))
    idx += list(range(2320, 3856))
    idx += list(range(3856, 6928))
    idx += list(range(2304, 2320)) + [D_IN] * (LANES - 16)
    assert len(idx) == N_PROJ
    return np.asarray(idx, np.int32)


def _attn_row_index():
    idx = []
    for hh in range(ATT_GROUP):
        for g in range(ATT_KV_HEADS):
            head = g * ATT_GROUP + hh
            idx += [head * ATT_HEAD_DIM + d for d in range(ATT_HEAD_DIM)]
    return np.asarray(idx, np.int32)


def _rope_tables(seq):
    rows = seq // GRID_W
    row = jnp.repeat(jnp.arange(rows, dtype=F32), GRID_W)
    col = jnp.tile(jnp.arange(GRID_W, dtype=F32), rows)
    n_pairs = ATT_HEAD_DIM // 4
    inv_freq = ROPE_BASE ** (-jnp.arange(n_pairs, dtype=F32) / n_pairs)
    ang = jnp.concatenate([row[:, None] * inv_freq, col[:, None] * inv_freq], axis=-1)
    c, s = jnp.cos(ang), jnp.sin(ang)
    cos_t = jnp.concatenate([c, c, c, c], axis=-1)
    sin_t = jnp.concatenate([-s, s, -s, s], axis=-1)
    return cos_t, sin_t


def _dot(a, b):
    return jnp.dot(a, b, preferred_element_type=F32)


def _dot_nt(a, b):
    return lax.dot_general(a, b, (((1,), (1,)), ((), ())), preferred_element_type=F32)


def _dot_tn(a, b):
    return lax.dot_general(a, b, (((0,), (0,)), ((), ())), preferred_element_type=F32)


def _sigmoid(x):
    return 1.0 / (1.0 + jnp.exp(-x))


def _silu(x):
    return x * _sigmoid(x)


def _log_sigmoid(x):
    return jnp.minimum(x, 0.0) - jnp.log(1.0 + jnp.exp(-jnp.abs(x)))


def _rms_mod(x, g, shift, scale):
    y = x * lax.rsqrt(jnp.mean(x * x, axis=-1, keepdims=True) + EPS) * g
    return y * (1.0 + scale) + shift


def _params(sem):
    return pltpu.CompilerParams(dimension_semantics=sem, vmem_limit_bytes=VMEM_LIMIT)


def _ada_kernel(c_ref, w_ref, b_ref, o_ref):
    s = _silu(c_ref[...])
    o_ref[0] = jnp.dot(s, w_ref[0], preferred_element_type=F32,
                       precision=lax.Precision.HIGHEST) + b_ref[0]


def _ada(cc, ada_w, ada_b):
    depth, d, n = ada_w.shape
    rows = cc.shape[0]
    tn = 1536
    return pl.pallas_call(
        _ada_kernel,
        grid=(depth, n // tn),
        in_specs=[pl.BlockSpec((rows, d), lambda l, j: (0, 0)),
                  pl.BlockSpec((1, d, tn), lambda l, j: (l, 0, j)),
                  pl.BlockSpec((1, 1, tn), lambda l, j: (l, 0, j))],
        out_specs=pl.BlockSpec((1, rows, tn), lambda l, j: (l, 0, j)),
        out_shape=jax.ShapeDtypeStruct((depth, rows, n), F32),
        compiler_params=_params(("parallel", "parallel")),
        name="ada_mod",
    )(cc, ada_w, ada_b.reshape(depth, 1, n))


def _swap_halves(x):
    lane = lax.broadcasted_iota(jnp.int32, x.shape, 1)
    first = (lane % ATT_HEAD_DIM) < (ATT_HEAD_DIM // 2)
    return jnp.where(first, pltpu.roll(x, LANES - 32, axis=1), pltpu.roll(x, 32, axis=1))


def _in_kernel(x_ref, mod_ref, g_ref, w_ref, cos_ref, sin_ref,
               qs_ref, k_ref, v_ref, qm_ref, km_ref, vm_ref, om_ref, gm_ref, bc_ref, u_ref, gt_ref):
    tm = x_ref.shape[1]
    h = _rms_mod(x_ref[0], g_ref[...], mod_ref[0, 0:1, :], mod_ref[0, 1:2, :]).astype(BF16)

    def proj(name):
        lo, hi = _OFF[name]
        return _dot(h, w_ref[:, lo:hi])

    cos_t = cos_ref[...]
    sin_t = sin_ref[...]

    def rope(t):
        return t * cos_t + _swap_halves(t) * sin_t

    q = proj('q')
    scale = ATT_HEAD_DIM ** -0.5
    for hh in range(ATT_GROUP):
        r = (rope(q[:, hh * LANES:(hh + 1) * LANES]) * scale).astype(BF16)
        for qb in range(tm // ATT_BLOCK):
            qs_ref[0, qb, hh * ATT_BLOCK:(hh + 1) * ATT_BLOCK, :] = r[qb * ATT_BLOCK:(qb + 1) * ATT_BLOCK, :]
    k_ref[0] = rope(proj('k')).astype(BF16)
    v_ref[0] = proj('v').astype(BF16)
    qm_ref[0] = proj('qm').astype(BF16)
    km_ref[0] = proj('km').astype(BF16)
    vm_ref[0] = proj('vm').astype(BF16)
    om_ref[0] = proj('om').astype(BF16)
    gm_ref[0] = proj('gm')
    bc_ref[0] = proj('bc').astype(BF16)
    u_ref[0] = (proj('cc') * proj('xc')).astype(BF16)
    gt_ref[0] = proj('gt').astype(BF16)


def _in_proj(x, mod, mod_row, norm_g, w_p, cos_t, sin_t, tm):
    B, S, D = x.shape
    nb = S // ATT_BLOCK
    tok = lambda n, dt: jax.ShapeDtypeStruct((B, S, n), dt)
    tspec = lambda n: pl.BlockSpec((1, tm, n), lambda b, i: (b, i, 0))
    out_shape = (jax.ShapeDtypeStruct((B, nb, ATT_GROUP * ATT_BLOCK, LANES), BF16),
                 tok(128, BF16), tok(128, BF16), tok(256, BF16), tok(256, BF16), tok(512, BF16),
                 tok(512, BF16), tok(128, F32), tok(512, BF16), tok(512, BF16), tok(3072, BF16))
    out_specs = (pl.BlockSpec((1, tm // ATT_BLOCK, ATT_GROUP * ATT_BLOCK, LANES), lambda b, i: (b, i, 0, 0)),
                 tspec(128), tspec(128), tspec(256), tspec(256), tspec(512), tspec(512), tspec(128),
                 tspec(512), tspec(512), tspec(3072))
    return pl.pallas_call(
        _in_kernel,
        grid=(B, S // tm),
        in_specs=[pl.BlockSpec((1, tm, D), lambda b, i: (b, i, 0)),
                  pl.BlockSpec((1, 6, D), lambda b, i: (mod_row(b), 0, 0)),
                  pl.BlockSpec((1, D), lambda b, i: (0, 0)),
                  pl.BlockSpec((D, N_PROJ), lambda b, i: (0, 0), pipeline_mode=pl.Buffered(1)),
                  pl.BlockSpec((tm, LANES), lambda b, i: (i, 0)),
                  pl.BlockSpec((tm, LANES), lambda b, i: (i, 0))],
        out_specs=out_specs,
        out_shape=out_shape,
        compiler_params=_params(("parallel", "parallel")),
        name="in_proj",
    )(x, mod, norm_g.reshape(1, D), w_p, cos_t, sin_t)


def _attn_kernel(sink_ref, qs_ref, kc_ref, vc_ref, *rest, band):
    if band:
        kp_ref, kcur_ref, kn_ref, vp_ref, vcur_ref, vn_ref, o_ref = rest
    else:
        (o_ref,) = rest
    j = pl.program_id(1)
    nblk = pl.num_programs(1)
    q = qs_ref[0, 0]
    rows = q.shape[0]
    if band:
        kcat = jnp.concatenate([kc_ref[0], kp_ref[0], kcur_ref[0], kn_ref[0]], axis=0)
        vcat = jnp.concatenate([vc_ref[0], vp_ref[0], vcur_ref[0], vn_ref[0]], axis=0)
    else:
        kcat = kc_ref[0]
        vcat = vc_ref[0]
    nkeys = kcat.shape[0]
    lc = kc_ref.shape[1]
    lane = lax.broadcasted_iota(jnp.int32, (1, LANES), 1)
    t = lax.broadcasted_iota(jnp.int32, (rows, 1), 0) % ATT_BLOCK
    hh = lax.broadcasted_iota(jnp.int32, (rows, 1), 0) // ATT_BLOCK
    if band:
        c = lax.broadcasted_iota(jnp.int32, (1, nkeys), 1)
        t_prev = t + jnp.where(j > 0, 0, 2 * ATT_BLOCK)
        t_next = t - jnp.where(j < nblk - 1, 0, 2 * ATT_BLOCK)
        i_prev = c - lc
        i_next = c - (lc + 2 * ATT_BLOCK)
        valid = ((c < lc)
                 | ((c >= lc) & (c < lc + ATT_BLOCK) & (i_prev >= t_prev))
                 | ((c >= lc + ATT_BLOCK) & (c < lc + 2 * ATT_BLOCK))
                 | ((c >= lc + 2 * ATT_BLOCK) & (i_next <= t_next)))
    out = jnp.zeros((rows, LANES), F32)
    for g in range(ATT_KV_HEADS):
        lm = (lane < ATT_HEAD_DIM) if g == 0 else (lane >= ATT_HEAD_DIM)
        kz = jnp.where(lm, kcat, jnp.zeros_like(kcat))
        vz = jnp.where(lm, vcat, jnp.zeros_like(vcat))
        s = _dot_nt(q, kz)
        if band:
            s = jnp.where(valid, s, -jnp.inf)
        sink = jnp.zeros((rows, 1), F32)
        for a in range(ATT_GROUP):
            sink = jnp.where(hh == a, sink_ref[g * ATT_GROUP + a], sink)
        m = jnp.maximum(jnp.max(s, axis=-1, keepdims=True), sink)
        p = jnp.exp(s - m)
        l = jnp.sum(p, axis=-1, keepdims=True) + jnp.exp(sink - m)
        out = out + _dot(p.astype(BF16), vz) / l
    o_ref[0, 0] = out.astype(BF16)


def _attention(qs, k, v, kc, vc, sink, band):
    B, nb = qs.shape[:2]
    lc = kc.shape[1]
    last = nb - 1
    qspec = pl.BlockSpec((1, 1, ATT_GROUP * ATT_BLOCK, LANES), lambda b, j: (b, j, 0, 0))
    cspec = pl.BlockSpec((1, lc, LANES), lambda b, j: (b, 0, 0))
    in_specs = [pl.BlockSpec(memory_space=pltpu.SMEM), qspec, cspec, cspec]
    args = [sink.astype(F32), qs, kc, vc]
    if band:
        prev = pl.BlockSpec((1, ATT_BLOCK, LANES), lambda b, j: (b, jnp.maximum(j - 1, 0), 0))
        cur = pl.BlockSpec((1, ATT_BLOCK, LANES), lambda b, j: (b, j, 0))
        nxt = pl.BlockSpec((1, ATT_BLOCK, LANES), lambda b, j: (b, jnp.minimum(j + 1, last), 0))
        in_specs += [prev, cur, nxt, prev, cur, nxt]
        args += [k, k, k, v, v, v]
    return pl.pallas_call(
        functools.partial(_attn_kernel, band=band),
        grid=(B, nb),
        in_specs=in_specs,
        out_specs=qspec,
        out_shape=jax.ShapeDtypeStruct(qs.shape, BF16),
        compiler_params=_params(("parallel", "parallel")),
        name="attention_band" if band else "attention_ctx",
    )(*args)


def _mlstm_step(dirs, T):
    kscale = M_QK_DIM ** -0.5
    si = lax.broadcasted_iota(jnp.int32, (T, T), 0)
    ri = lax.broadcasted_iota(jnp.int32, (T, T), 1)
    lane_qk = lax.broadcasted_iota(jnp.int32, (1, M_HEADS * M_QK_DIM), 1) // M_QK_DIM
    lane_m = lax.broadcasted_iota(jnp.int32, (1, LANES), 1)
    row_c = lax.broadcasted_iota(jnp.int32, (M_HEADS * M_QK_DIM, 1), 0) // M_QK_DIM
    combos = [(d, hd) for d in range(2) for hd in range(M_HEADS)]

    tri, bcol, gt, bt, blast = [], [], [], [], []
    for d, (q, k, v, g, C, n, m) in enumerate(dirs):
        t = (ri <= si) if d == 0 else (ri >= si)
        tri.append(t)
        lf = _log_sigmoid(g)
        bc = jnp.dot(t.astype(F32), lf, preferred_element_type=F32, precision=lax.Precision.HIGHEST)
        bcol.append(bc)
        gt.append(g.T)
        bt.append(bc.T)
        blast.append(bc[T - 1:T, :] if d == 0 else bc[0:1, :])

    def lanes(d, hd):
        return (2 * d) * M_HEADS + hd, (2 * d + 1) * M_HEADS + hd

    b_col = {c: bcol[c[0]][:, lanes(*c)[1]:lanes(*c)[1] + 1] for c in combos}
    ig_col = {c: dirs[c[0]][3][:, lanes(*c)[0]:lanes(*c)[0] + 1] for c in combos}
    alpha = {c: gt[c[0]][lanes(*c)[0]:lanes(*c)[0] + 1, :] - bt[c[0]][lanes(*c)[1]:lanes(*c)[1] + 1, :]
             for c in combos}
    m_old = {c: dirs[c[0]][6][:, c[1]:c[1] + 1] for c in combos}
    b_last = {c: blast[c[0]][:, lanes(*c)[1]:lanes(*c)[1] + 1] for c in combos}
    hmask = {hd: lane_qk == hd for hd in range(M_HEADS)}

    a_mat = {c: jnp.where(tri[c[0]], alpha[c], -jnp.inf) for c in combos}
    a_max = {c: jnp.max(a_mat[c], axis=1, keepdims=True) for c in combos}
    a_int = {c: b_col[c] + m_old[c] for c in combos}
    m_s = {c: jnp.maximum(a_int[c], b_col[c] + a_max[c]) for c in combos}
    w_int = {c: jnp.exp(a_int[c] - m_s[c]) for c in combos}
    w_mat = {c: jnp.exp(a_mat[c] + (b_col[c] - m_s[c])) for c in combos}
    qmask = {c: jnp.where(hmask[c[1]], dirs[c[0]][0], jnp.zeros_like(dirs[c[0]][0])) for c in combos}
    s_qk = {c: w_mat[c] * (_dot_nt(qmask[c], dirs[c[0]][1]) * kscale) for c in combos}
    vh = {c: dirs[c[0]][2][:, c[1] * M_V_DIM:(c[1] + 1) * M_V_DIM] for c in combos}
    c_bf = [dirs[d][4].astype(BF16) for d in range(2)]
    num = {c: _dot(s_qk[c].astype(BF16), vh[c]) + w_int[c] * _dot(qmask[c], c_bf[c[0]]) for c in combos}
    qn_all = [dirs[d][0].astype(F32) * dirs[d][5] for d in range(2)]
    qn = {c: jnp.sum(jnp.where(hmask[c[1]], qn_all[c[0]], 0.0), axis=1, keepdims=True) for c in combos}
    den = {c: jnp.sum(s_qk[c], axis=1, keepdims=True) + w_int[c] * qn[c] for c in combos}
    h = {c: num[c] / jnp.maximum(jnp.abs(den[c]), jnp.exp(-m_s[c])) for c in combos}

    r_col = {c: b_last[c] - b_col[c] + ig_col[c] for c in combos}
    m_new = {c: jnp.maximum(b_last[c] + m_old[c], jnp.max(r_col[c], axis=0, keepdims=True)) for c in combos}
    decay = {c: jnp.exp(b_last[c] + m_old[c] - m_new[c]) for c in combos}
    w_r = {c: jnp.exp(r_col[c] - m_new[c]) for c in combos}

    outs = []
    for d, (q, k, v, g, C, n, m) in enumerate(dirs):
        w_lanes = jnp.zeros((T, M_HEADS * M_QK_DIM), F32)
        dec_lanes = jnp.zeros((1, M_HEADS * M_QK_DIM), F32)
        dec_rows = jnp.zeros((M_HEADS * M_QK_DIM, 1), F32)
        m_row = jnp.zeros((1, LANES), F32)
        for hd in range(M_HEADS):
            w_lanes = jnp.where(hmask[hd], w_r[(d, hd)], w_lanes)
            dec_lanes = jnp.where(hmask[hd], decay[(d, hd)], dec_lanes)
            dec_rows = jnp.where(row_c == hd, decay[(d, hd)], dec_rows)
            m_row = jnp.where(lane_m == hd, m_new[(d, hd)], m_row)
        kw = k.astype(F32) * (w_lanes * kscale)
        kwt = kw.T.astype(BF16)
        upd = jnp.concatenate(
            [_dot(kwt[hd * M_QK_DIM:(hd + 1) * M_QK_DIM, :], vh[(d, hd)]) for hd in range(M_HEADS)], axis=0)
        c_new = dec_rows * C + upd
        n_new = dec_lanes * n + jnp.sum(kw, axis=0, keepdims=True)
        h_all = jnp.concatenate([h[(d, hd)] for hd in range(M_HEADS)], axis=1)
        outs.append((h_all, c_new, n_new, m_row))
    return outs


def _mlstm_kernel(gb_ref, qf_ref, kf_ref, vf_ref, gf_ref, qb_ref, kb_ref, vb_ref, gbk_ref,
                  c0_ref, n0_ref, m0_ref, hf_ref, hb_ref, cf_ref, nf_ref, mf_ref,
                  c_s, n_s, m_s):
    ci = pl.program_id(1)
    T = qf_ref.shape[1]

    @pl.when(ci == 0)
    def _():
        c_s[...] = c0_ref[0]
        n_s[...] = n0_ref[0]
        m_s[...] = m0_ref[0]

    gb = gb_ref[...]
    dirs = [(q_ref[0], k_ref[0], v_ref[0], g_ref[0] + gb, c_s[d], n_s[d], m_s[d])
            for d, (q_ref, k_ref, v_ref, g_ref) in enumerate(
                ((qf_ref, kf_ref, vf_ref, gf_ref), (qb_ref, kb_ref, vb_ref, gbk_ref)))]
    outs = _mlstm_step(dirs, T)
    for d, h_ref in enumerate((hf_ref, hb_ref)):
        h_all, c_new, n_new, m_row = outs[d]
        h_ref[0] = h_all
        c_s[d] = c_new
        n_s[d] = n_new
        m_s[d] = m_row

    @pl.when(ci == pl.num_programs(1) - 1)
    def _():
        cf_ref[0] = c_s[...]
        nf_ref[0] = n_s[...]
        mf_ref[0] = m_s[...]


MLSTM_TILE = 128


def _mlstm(qm, km, vm, gm, gate_b, state):
    B, S, _ = qm.shape
    T = MLSTM_TILE
    nc = S // T
    nq = M_HEADS * M_QK_DIM
    fwd = lambda n: pl.BlockSpec((1, T, n), lambda b, c: (b, c, 0))
    bwd = lambda n: pl.BlockSpec((1, T, n), lambda b, c: (b, nc - 1 - c, 0))
    st_specs = [pl.BlockSpec((1, 2, nq, M_V_DIM), lambda b, c: (b, 0, 0, 0)),
                pl.BlockSpec((1, 2, 1, nq), lambda b, c: (b, 0, 0, 0)),
                pl.BlockSpec((1, 2, 1, LANES), lambda b, c: (b, 0, 0, 0))]
    st_shapes = [jax.ShapeDtypeStruct((B, 2, nq, M_V_DIM), F32),
                 jax.ShapeDtypeStruct((B, 2, 1, nq), F32),
                 jax.ShapeDtypeStruct((B, 2, 1, LANES), F32)]
    gb_row = jnp.pad(gate_b.reshape(1, -1).astype(F32), ((0, 0), (0, LANES - gate_b.size)))
    outs = pl.pallas_call(
        _mlstm_kernel,
        grid=(B, nc),
        in_specs=[pl.BlockSpec((1, LANES), lambda b, c: (0, 0)),
                  fwd(256), fwd(256), fwd(512), fwd(LANES),
                  bwd(256), bwd(256), bwd(512), bwd(LANES)] + st_specs,
        out_specs=[fwd(M_OUT), bwd(M_OUT)] + st_specs,
        out_shape=[jax.ShapeDtypeStruct((B, S, M_OUT), F32)] * 2 + st_shapes,
        scratch_shapes=[pltpu.VMEM((2, nq, M_V_DIM), F32),
                        pltpu.VMEM((2, 1, nq), F32),
                        pltpu.VMEM((2, 1, LANES), F32)],
        compiler_params=_params(("parallel", "arbitrary")),
        name="mlstm_scan",
    )(gb_row, qm, km, vm, gm, qm, km, vm, gm, *state)
    return outs[0], outs[1], tuple(outs[2:])


def _route(scores, sel):
    tm = scores.shape[1]
    gi8 = lax.broadcasted_iota(jnp.int32, (GROUP_SIZE, tm), 0)

    def stack_rows(rows):
        out = jnp.broadcast_to(rows[0], (len(rows), tm))
        for r, v in enumerate(rows[1:], start=1):
            out = jnp.where(gi8 == r, v, out)
        return out

    gs = []
    for g in range(N_GROUPS):
        blk = sel[g * GROUP_SIZE:(g + 1) * GROUP_SIZE, :]
        m1 = jnp.max(blk, axis=0, keepdims=True)
        first = jnp.min(jnp.where(blk == m1, gi8, GROUP_SIZE), axis=0, keepdims=True)
        m2 = jnp.max(jnp.where(gi8 == first, -jnp.inf, blk), axis=0, keepdims=True)
        gs.append(m1 + m2)
    gsc = stack_rows(gs)
    gsel = jnp.zeros((N_GROUPS, tm), F32)
    for _ in range(TOPK_GROUPS):
        mx = jnp.max(gsc, axis=0, keepdims=True)
        first = jnp.min(jnp.where(gsc == mx, gi8, N_GROUPS), axis=0, keepdims=True)
        pick = gi8 == first
        gsel = jnp.where(pick, 1.0, gsel)
        gsc = jnp.where(pick, -jnp.inf, gsc)
    cur = jnp.concatenate(
        [jnp.where(gsel[g:g + 1, :] > 0.0, sel[g * GROUP_SIZE:(g + 1) * GROUP_SIZE, :], -jnp.inf)
         for g in range(N_GROUPS)], axis=0)
    ei = lax.broadcasted_iota(jnp.int32, (N_EXPERTS, tm), 0)
    idx, wts = [], []
    for _ in range(TOP_K):
        mx = jnp.max(cur, axis=0, keepdims=True)
        first = jnp.min(jnp.where(cur == mx, ei, N_EXPERTS), axis=0, keepdims=True)
        pick = ei == first
        idx.append(first)
        wts.append(jnp.sum(jnp.where(pick, scores, 0.0), axis=0, keepdims=True))
        cur = jnp.where(pick, -jnp.inf, cur)
    tot = wts[0]
    for w in wts[1:]:
        tot = tot + w
    wts = [w / tot * ROUTED_SCALE for w in wts]
    return stack_rows(idx), stack_rows(wts)


def _merge_kernel(x_ref, mod_ref, oat_ref, hf_ref, hb_ref, om_ref, ng_ref, u_ref, up_ref, un_ref,
                  bc_ref, cw_ref, gt_ref, bgb_ref, wa_ref, wm_ref, wc_ref, wo_ref, n2_ref,
                  rw_ref, rb_ref, sgu_ref, sd_ref,
                  base_ref, h2_ref, idx_ref, wt_ref, gd_ref):
    i = pl.program_id(1)
    tm = x_ref.shape[1]
    x = x_ref[0]
    g1 = mod_ref[0, 2:3, :]
    sh2 = mod_ref[0, 3:4, :]
    sc2 = mod_ref[0, 4:5, :]
    g2 = mod_ref[0, 5:6, :]

    ya = jnp.concatenate(
        [jnp.concatenate([oat_ref[0, qb, hh * ATT_BLOCK:(hh + 1) * ATT_BLOCK, :] for hh in range(ATT_GROUP)], axis=1)
         for qb in range(tm // ATT_BLOCK)], axis=0)

    hsum = hf_ref[0] + hb_ref[0]
    parts = []
    for hd in range(M_HEADS):
        hh_ = hsum[:, hd * M_V_DIM:(hd + 1) * M_V_DIM]
        parts.append(hh_ * lax.rsqrt(jnp.mean(hh_ * hh_, axis=-1, keepdims=True) + EPS))
    hn = jnp.concatenate(parts, axis=1) * ng_ref[...]
    ym = (_sigmoid(om_ref[0].astype(F32)) * hn).astype(BF16)

    u = u_ref[0].astype(F32)
    row = lax.broadcasted_iota(jnp.int32, (tm, 1), 0)
    has_prev = (i > 0).astype(F32)
    has_next = (i < pl.num_programs(1) - 1).astype(F32)
    prev_row = up_ref[0, BF16_SUBLANES - 1:BF16_SUBLANES, :].astype(F32) * has_prev
    next_row = un_ref[0, 0:1, :].astype(F32) * has_next
    u_m1 = jnp.where(row == 0, prev_row, pltpu.roll(u, 1, axis=0))
    u_p1 = jnp.where(row == tm - 1, next_row, pltpu.roll(u, tm - 1, axis=0))
    conv = cw_ref[0:1, :] * u_m1 + cw_ref[1:2, :] * u + cw_ref[2:3, :] * u_p1
    yc = (bc_ref[0].astype(F32) * conv).astype(BF16)

    gg = _sigmoid(gt_ref[0].astype(F32) + bgb_ref[...])
    ymix = (gg[:, 0:D_MODEL] * _dot(ya, wa_ref[...])
            + gg[:, D_MODEL:2 * D_MODEL] * _dot(ym, wm_ref[...])
            + gg[:, 2 * D_MODEL:3 * D_MODEL] * _dot(yc, wc_ref[...]))
    y = _dot(ymix.astype(BF16), wo_ref[...])
    xm = x + g1 * y

    h2f = _rms_mod(xm, n2_ref[...], sh2, sc2)
    h2 = h2f.astype(BF16)
    h2_ref[0] = h2

    logits_t = lax.dot_general(rw_ref[...], h2f, (((1,), (1,)), ((), ())),
                               preferred_element_type=F32, precision=lax.Precision.HIGHEST)
    scores = _sigmoid(logits_t)
    idx, wts = _route(scores, scores + rb_ref[...])
    idx_ref[0] = idx
    wt_ref[0] = wts
    ei = lax.broadcasted_iota(jnp.int32, (N_EXPERTS, tm), 0)
    gd = jnp.zeros((N_EXPERTS, tm), F32)
    for kk in range(TOP_K):
        gd = gd + jnp.where(ei == idx[kk:kk + 1, :], wts[kk:kk + 1, :], 0.0)
    gd_ref[0] = gd

    a = _dot(h2, sgu_ref[...])
    act = (_silu(a[:, 0:SHARED_FF]) * a[:, SHARED_FF:2 * SHARED_FF]).astype(BF16)
    base_ref[0] = xm + g2 * _dot(act, sd_ref[...])


def _merge(x, mod, mod_row, oat, hf, hb, om, u, bc, gt, lw, tm):
    B, S, D = x.shape
    nt = S // tm
    hal = BF16_SUBLANES
    last_h = S // hal - 1
    tspec = lambda n: pl.BlockSpec((1, tm, n), lambda b, i: (b, i, 0))
    full = lambda a: pl.BlockSpec(a.shape, lambda b, i: (0,) * a.ndim)
    weights = [lw['mlstm_norm_g'], lw['conv_w'], lw['branch_gate_b'], lw['w_br_attn'], lw['w_br_mlstm'],
               lw['w_br_conv'], lw['w_out'], lw['norm2_g'], lw['router_wt'], lw['router_bias'],
               lw['sh_gu'], lw['sh_d']]
    in_specs = [tspec(D),
                pl.BlockSpec((1, 6, D), lambda b, i: (mod_row(b), 0, 0)),
                pl.BlockSpec((1, tm // ATT_BLOCK, ATT_GROUP * ATT_BLOCK, LANES), lambda b, i: (b, i, 0, 0)),
                tspec(M_OUT), tspec(M_OUT), tspec(M_OUT), full(weights[0]),
                tspec(CONV_WIDTH),
                pl.BlockSpec((1, hal, CONV_WIDTH), lambda b, i: (b, jnp.maximum(i * (tm // hal) - 1, 0), 0)),
                pl.BlockSpec((1, hal, CONV_WIDTH), lambda b, i: (b, jnp.minimum((i + 1) * (tm // hal), last_h), 0)),
                tspec(CONV_WIDTH), full(weights[1]), tspec(N_BRANCH * D), full(weights[2])]
    in_specs += [full(w) for w in weights[3:]]
    tr = lambda n, dt: (jax.ShapeDtypeStruct((B, n, S), dt), pl.BlockSpec((1, n, tm), lambda b, i: (b, 0, i)))
    outs = [(jax.ShapeDtypeStruct((B, S, D), F32), tspec(D)),
            (jax.ShapeDtypeStruct((B, S, D), BF16), tspec(D)),
            tr(TOP_K, jnp.int32), tr(TOP_K, F32), tr(N_EXPERTS, F32)]
    return pl.pallas_call(
        _merge_kernel,
        grid=(B, nt),
        in_specs=in_specs,
        out_specs=[o[1] for o in outs],
        out_shape=[o[0] for o in outs],
        compiler_params=_params(("parallel", "parallel")),
        name="merge_route",
    )(x, mod, oat, hf, hb, om, weights[0], u, u, u, bc, weights[1], gt, weights[2], *weights[3:])


def _moe_kernel(h_ref, g_ref, base_ref, mod_ref, wgu_ref, wd_ref, o_ref, acc_ref):
    e = pl.program_id(2)

    @pl.when(e == 0)
    def _():
        acc_ref[...] = jnp.zeros_like(acc_ref)

    a = _dot(h_ref[0], wgu_ref[0])
    lane = lax.broadcasted_iota(jnp.int32, g_ref.shape[1:], 1)
    ge = jnp.sum(jnp.where(lane == e, g_ref[0], 0.0), axis=1, keepdims=True)
    act = (_silu(a[:, 0:EXPERT_FF]) * a[:, EXPERT_FF:2 * EXPERT_FF]) * ge
    acc_ref[...] += _dot(act.astype(BF16), wd_ref[0])

    @pl.when(e == pl.num_programs(2) - 1)
    def _():
        o_ref[0] = base_ref[0] + mod_ref[0, 5:6, :] * acc_ref[...]


def _moe_dense(h2, gates, base, mod, mod_row, wgu, wd, tm):
    B, S, D = h2.shape
    tspec = lambda n: pl.BlockSpec((1, tm, n), lambda b, i, e: (b, i, 0))
    return pl.pallas_call(
        _moe_kernel,
        grid=(B, S // tm, N_EXPERTS),
        in_specs=[tspec(D), tspec(N_EXPERTS), tspec(D),
                  pl.BlockSpec((1, 6, D), lambda b, i, e: (mod_row(b), 0, 0)),
                  pl.BlockSpec((1, D, 2 * EXPERT_FF), lambda b, i, e: (e, 0, 0)),
                  pl.BlockSpec((1, EXPERT_FF, D), lambda b, i, e: (e, 0, 0))],
        out_specs=tspec(D),
        out_shape=jax.ShapeDtypeStruct((B, S, D), F32),
        scratch_shapes=[pltpu.VMEM((tm, D), F32)],
        compiler_params=_params(("parallel", "parallel", "arbitrary")),
        name="moe_experts",
    )(h2, gates, base, mod, wgu, wd)


def _final_kernel(x_ref, g_ref, o_ref):
    x = x_ref[0]
    o_ref[0] = x * lax.rsqrt(jnp.mean(x * x, axis=-1, keepdims=True) + EPS) * g_ref[...]


def _final_norm(x, g, tm):
    B, S, D = x.shape
    spec = pl.BlockSpec((1, tm, D), lambda b, i: (b, i, 0))
    return pl.pallas_call(
        _final_kernel,
        grid=(B, S // tm),
        in_specs=[spec, pl.BlockSpec((1, D), lambda b, i: (0, 0))],
        out_specs=spec,
        out_shape=jax.ShapeDtypeStruct((B, S, D), F32),
        compiler_params=_params(("parallel", "parallel")),
        name="final_norm",
    )(x, g.reshape(1, D))


def _zero_state(batch):
    nq = M_HEADS * M_QK_DIM
    return (jnp.zeros((batch, 2, nq, M_V_DIM), F32),
            jnp.zeros((batch, 2, 1, nq), F32),
            jnp.zeros((batch, 2, 1, LANES), F32))


def kernel(x, c, ctx, c_ctx, ada_w, ada_b, norm1_g, norm2_g, w_in, attn_sink, mlstm_gate_b, mlstm_norm_g, conv_w, w_br_attn, w_br_mlstm, w_br_conv, branch_gate_b, w_out, router_w, router_bias, exp_w_gate, exp_w_up, exp_w_down, sh_w_gate, sh_w_up, sh_w_down, final_g):
    B, S, D = x.shape
    L = ctx.shape[1]
    depth = ada_w.shape[0]
    ctx_row = B

    pad_rows = (-(B + 1)) % 8
    cc = jnp.concatenate([c, c_ctx[None, :], jnp.zeros((pad_rows, D), F32)], axis=0)
    mod_all = _ada(cc, ada_w, ada_b).reshape(depth, B + 1 + pad_rows, 6, D)

    cos_t, sin_t = _rope_tables(S)
    cos_c = jnp.ones((L, LANES), F32)
    sin_c = jnp.zeros((L, LANES), F32)
    col_idx = _proj_column_index()
    att_idx = _attn_row_index()
    lat_row = lambda b: b
    ctx_mod = lambda b: ctx_row

    xc = ctx
    for l in range(depth):
        need_ctx = l < depth - 1
        mod = mod_all[l]
        w_ext = jnp.concatenate([w_in[l], jnp.zeros((D, 1), F32)], axis=1)
        w_p = jnp.take(w_ext, col_idx, axis=1).astype(BF16)
        lw = {
            'mlstm_norm_g': mlstm_norm_g[l].reshape(1, M_OUT),
            'conv_w': conv_w[l],
            'branch_gate_b': branch_gate_b[l].reshape(1, N_BRANCH * D),
            'w_br_attn': jnp.take(w_br_attn[l], att_idx, axis=0).astype(BF16),
            'w_br_mlstm': w_br_mlstm[l].astype(BF16),
            'w_br_conv': w_br_conv[l].astype(BF16),
            'w_out': w_out[l].astype(BF16),
            'norm2_g': norm2_g[l].reshape(1, D),
            'router_wt': router_w[l].T,
            'router_bias': router_bias[l].reshape(N_EXPERTS, 1),
            'sh_gu': jnp.concatenate([sh_w_gate[l], sh_w_up[l]], axis=1).astype(BF16),
            'sh_d': sh_w_down[l].astype(BF16),
        }
        wgu = jnp.concatenate([exp_w_gate[l], exp_w_up[l]], axis=2).astype(BF16)
        wd = exp_w_down[l].astype(BF16)

        pc = _in_proj(xc, mod, ctx_mod, norm1_g[l], w_p, cos_c, sin_c, tm=256)
        p = _in_proj(x, mod, lat_row, norm1_g[l], w_p, cos_t, sin_t, tm=256)
        qs_c, k_c, v_c, qm_c, km_c, vm_c, om_c, gm_c, bc_c, u_c, gt_c = pc
        qs, k, v, qm, km, vm, om, gm, bc, u, gt = p

        oat = _attention(qs, k, v, k_c, v_c, attn_sink[l], band=True)
        hf_c, hb_c, st = _mlstm(qm_c, km_c, vm_c, gm_c, mlstm_gate_b[l], _zero_state(B))
        hf, hb, _ = _mlstm(qm, km, vm, gm, mlstm_gate_b[l], st)

        base, h2, _, _, gd = _merge(x, mod, lat_row, oat, hf, hb, om, u, bc, gt, lw, tm=256)
        gates = jnp.swapaxes(gd, 1, 2)
        x_new = _moe_dense(h2, gates, base, mod, lat_row, wgu, wd, tm=min(1024, S))

        if need_ctx:
            oat_c = _attention(qs_c, None, None, k_c, v_c, attn_sink[l], band=False)
            base_c, h2_c, _, _, gd_c = _merge(xc, mod, ctx_mod, oat_c, hf_c, hb_c, om_c, u_c, bc_c, gt_c, lw, tm=256)
            xc = _moe_dense(h2_c, jnp.swapaxes(gd_c, 1, 2), base_c, mod, ctx_mod, wgu, wd, tm=256)
        x = x_new
    return _final_norm(x, final_g, tm=512)
```

```python
import functools

import numpy as np
import jax
import jax.numpy as jnp
from jax import lax
from jax.experimental import pallas as pl
from jax.experimental.pallas import tpu as pltpu

F32 = jnp.float32
BF16 = jnp.bfloat16

D_MODEL = 1024
GRID_W = 64
EPS = 1e-6
ATT_HEADS = 8
ATT_KV_HEADS = 2
ATT_HEAD_DIM = 64
ATT_GROUP = ATT_HEADS // ATT_KV_HEADS
ATT_BLOCK = 128
ATT_OUT = ATT_HEADS * ATT_HEAD_DIM
ROPE_BASE = 10000.0
M_HEADS = 4
M_QK_DIM = 64
M_V_DIM = 128
M_CHUNK = 64
M_OUT = M_HEADS * M_V_DIM
CONV_WIDTH = 512
N_BRANCH = 3
N_EXPERTS = 64
N_GROUPS = 8
GROUP_SIZE = N_EXPERTS // N_GROUPS
TOPK_GROUPS = 4
TOP_K = 8
EXPERT_FF = 256
SHARED_FF = 256
ROUTED_SCALE = 2.5

LANES = 128
BF16_SUBLANES = 16
VMEM_LIMIT = 56 * 1024 * 1024

_SEGS = (('q', 512), ('k', 128), ('v', 128), ('qm', 256), ('km', 256), ('vm', 512), ('om', 512),
         ('bc', 512), ('cc', 512), ('xc', 512), ('gt', 3072), ('gm', 128))
_OFF = {}
_o = 0
for _n, _s in _SEGS:
    _OFF[_n] = (_o, _o + _s)
    _o += _s
N_PROJ = _o
D_IN = 6928


def _proj_column_index():
    idx = []
    half = ATT_HEAD_DIM // 2
    for hh in range(ATT_GROUP):
        for g in range(ATT_KV_HEADS):
            head = g * ATT_GROUP + hh
            for par in range(2):
                idx += [head * ATT_HEAD_DIM + 2 * i + par for i in range(half)]
    for g in range(ATT_KV_HEADS):
        for par in range(2):
            idx += [512 + g * ATT_HEAD_DIM + 2 * i + par for i in range(half)]
    idx += list(range(640, 768))
    idx += list(range(768, 2304))
    idx += list(range(2320, 3856))
    idx += list(range(3856, 6928))
    idx += list(range(2304, 2320)) + [D_IN] * (LANES - 16)
    assert len(idx) == N_PROJ
    return np.asarray(idx, np.int32)


def _attn_row_index():
    idx = []
    for hh in range(ATT_GROUP):
        for g in range(ATT_KV_HEADS):
            head = g * ATT_GROUP + hh
            idx += [head * ATT_HEAD_DIM + d for d in range(ATT_HEAD_DIM)]
    return np.asarray(idx, np.int32)


def _rope_tables(seq):
    rows = seq // GRID_W
    row = jnp.repeat(jnp.arange(rows, dtype=F32), GRID_W)
    col = jnp.tile(jnp.arange(GRID_W, dtype=F32), rows)
    n_pairs = ATT_HEAD_DIM // 4
    inv_freq = ROPE_BASE ** (-jnp.arange(n_pairs, dtype=F32) / n_pairs)
    ang = jnp.concatenate([row[:, None] * inv_freq, col[:, None] * inv_freq], axis=-1)
    c, s = jnp.cos(ang), jnp.sin(ang)
    cos_t = jnp.concatenate([c, c, c, c], axis=-1)
    sin_t = jnp.concatenate([-s, s, -s, s], axis=-1)
    return cos_t, sin_t


def _dot(a, b):
    return jnp.dot(a, b, preferred_element_type=F32)


def _dot_nt(a, b):
    return lax.dot_general(a, b, (((1,), (1,)), ((), ())), preferred_element_type=F32)


def _dot_tn(a, b):
    return lax.dot_general(a, b, (((0,), (0,)), ((), ())), preferred_element_type=F32)


def _sigmoid(x):
    return 1.0 / (1.0 + jnp.exp(-x))


def _silu(x):
    return x * _sigmoid(x)


def _log_sigmoid(x):
    return jnp.minimum(x, 0.0) - jnp.log(1.0 + jnp.exp(-jnp.abs(x)))


def _rms_mod(x, g, shift, scale):
    y = x * lax.rsqrt(jnp.mean(x * x, axis=-1, keepdims=True) + EPS) * g
    return y * (1.0 + scale) + shift


def _params(sem):
    return pltpu.CompilerParams(dimension_semantics=sem, vmem_limit_bytes=VMEM_LIMIT)


def _ada_kernel(c_ref, w_ref, b_ref, o_ref):
    s = _silu(c_ref[...])
    o_ref[0] = jnp.dot(s, w_ref[0], preferred_element_type=F32,
                       precision=lax.Precision.HIGHEST) + b_ref[0]


def _ada(cc, ada_w, ada_b):
    depth, d, n = ada_w.shape
    rows = cc.shape[0]
    tn = 1536
    return pl.pallas_call(
        _ada_kernel,
        grid=(depth, n // tn),
        in_specs=[pl.BlockSpec((rows, d), lambda l, j: (0, 0)),
                  pl.BlockSpec((1, d, tn), lambda l, j: (l, 0, j)),
                  pl.BlockSpec((1, 1, tn), lambda l, j: (l, 0, j))],
        out_specs=pl.BlockSpec((1, rows, tn), lambda l, j: (l, 0, j)),
        out_shape=jax.ShapeDtypeStruct((depth, rows, n), F32),
        compiler_params=_params(("parallel", "parallel")),
        name="ada_mod",
    )(cc, ada_w, ada_b.reshape(depth, 1, n))


def _swap_halves(x):
    lane = lax.broadcasted_iota(jnp.int32, x.shape, 1)
    first = (lane % ATT_HEAD_DIM) < (ATT_HEAD_DIM // 2)
    return jnp.where(first, pltpu.roll(x, LANES - 32, axis=1), pltpu.roll(x, 32, axis=1))


def _in_kernel(x_ref, mod_ref, g_ref, w_ref, cos_ref, sin_ref,
               qs_ref, k_ref, v_ref, qm_ref, km_ref, vm_ref, om_ref, gm_ref, bc_ref, u_ref, gt_ref):
    tm = x_ref.shape[1]
    h = _rms_mod(x_ref[0], g_ref[...], mod_ref[0, 0:1, :], mod_ref[0, 1:2, :]).astype(BF16)

    def proj(name):
        lo, hi = _OFF[name]
        return _dot(h, w_ref[:, lo:hi])

    cos_t = cos_ref[...]
    sin_t = sin_ref[...]

    def rope(t):
        return t * cos_t + _swap_halves(t) * sin_t

    q = proj('q')
    scale = ATT_HEAD_DIM ** -0.5
    for hh in range(ATT_GROUP):
        r = (rope(q[:, hh * LANES:(hh + 1) * LANES]) * scale).astype(BF16)
        for qb in range(tm // ATT_BLOCK):
            qs_ref[0, qb, hh * ATT_BLOCK:(hh + 1) * ATT_BLOCK, :] = r[qb * ATT_BLOCK:(qb + 1) * ATT_BLOCK, :]
    k_ref[0] = rope(proj('k')).astype(BF16)
    v_ref[0] = proj('v').astype(BF16)
    qm_ref[0] = proj('qm').astype(BF16)
    km_ref[0] = proj('km').astype(BF16)
    vm_ref[0] = proj('vm').astype(BF16)
    om_ref[0] = proj('om').astype(BF16)
    gm_ref[0] = proj('gm')
    bc_ref[0] = proj('bc').astype(BF16)
    u_ref[0] = (proj('cc') * proj('xc')).astype(BF16)
    gt_ref[0] = proj('gt').astype(BF16)


def _in_proj(x, mod, mod_row, norm_g, w_p, cos_t, sin_t, tm):
    B, S, D = x.shape
    nb = S // ATT_BLOCK
    tok = lambda n, dt: jax.ShapeDtypeStruct((B, S, n), dt)
    tspec = lambda n: pl.BlockSpec((1, tm, n), lambda b, i: (b, i, 0))
    out_shape = (jax.ShapeDtypeStruct((B, nb, ATT_GROUP * ATT_BLOCK, LANES), BF16),
                 tok(128, BF16), tok(128, BF16), tok(256, BF16), tok(256, BF16), tok(512, BF16),
                 tok(512, BF16), tok(128, F32), tok(512, BF16), tok(512, BF16), tok(3072, BF16))
    out_specs = (pl.BlockSpec((1, tm // ATT_BLOCK, ATT_GROUP * ATT_BLOCK, LANES), lambda b, i: (b, i, 0, 0)),
                 tspec(128), tspec(128), tspec(256), tspec(256), tspec(512), tspec(512), tspec(128),
                 tspec(512), tspec(512), tspec(3072))
    return pl.pallas_call(
        _in_kernel,
        grid=(B, S // tm),
        in_specs=[pl.BlockSpec((1, tm, D), lambda b, i: (b, i, 0)),
                  pl.BlockSpec((1, 6, D), lambda b, i: (mod_row(b), 0, 0)),
                  pl.BlockSpec((1, D), lambda b, i: (0, 0)),
                  pl.BlockSpec((D, N_PROJ), lambda b, i: (0, 0), pipeline_mode=pl.Buffered(1)),
                  pl.BlockSpec((tm, LANES), lambda b, i: (i, 0)),
                  pl.BlockSpec((tm, LANES), lambda b, i: (i, 0))],
        out_specs=out_specs,
        out_shape=out_shape,
        compiler_params=_params(("parallel", "parallel")),
        name="in_proj",
    )(x, mod, norm_g.reshape(1, D), w_p, cos_t, sin_t)


def _attn_kernel(sink_ref, qs_ref, kc_ref, vc_ref, *rest, band):
    if band:
        kp_ref, kcur_ref, kn_ref, vp_ref, vcur_ref, vn_ref, o_ref = rest
    else:
        (o_ref,) = rest
    j = pl.program_id(1)
    nblk = pl.num_programs(1)
    q = qs_ref[0, 0]
    rows = q.shape[0]
    if band:
        kcat = jnp.concatenate([kc_ref[0], kp_ref[0], kcur_ref[0], kn_ref[0]], axis=0)
        vcat = jnp.concatenate([vc_ref[0], vp_ref[0], vcur_ref[0], vn_ref[0]], axis=0)
    else:
        kcat = kc_ref[0]
        vcat = vc_ref[0]
    nkeys = kcat.shape[0]
    lc = kc_ref.shape[1]
    lane = lax.broadcasted_iota(jnp.int32, (1, LANES), 1)
    t = lax.broadcasted_iota(jnp.int32, (rows, 1), 0) % ATT_BLOCK
    hh = lax.broadcasted_iota(jnp.int32, (rows, 1), 0) // ATT_BLOCK
    if band:
        c = lax.broadcasted_iota(jnp.int32, (1, nkeys), 1)
        t_prev = t + jnp.where(j > 0, 0, 2 * ATT_BLOCK)
        t_next = t - jnp.where(j < nblk - 1, 0, 2 * ATT_BLOCK)
        i_prev = c - lc
        i_next = c - (lc + 2 * ATT_BLOCK)
        valid = ((c < lc)
                 | ((c >= lc) & (c < lc + ATT_BLOCK) & (i_prev >= t_prev))
                 | ((c >= lc + ATT_BLOCK) & (c < lc + 2 * ATT_BLOCK))
                 | ((c >= lc + 2 * ATT_BLOCK) & (i_next <= t_next)))
    out = jnp.zeros((rows, LANES), F32)
    for g in range(ATT_KV_HEADS):
        lm = (lane < ATT_HEAD_DIM) if g == 0 else (lane >= ATT_HEAD_DIM)
        kz = jnp.where(lm, kcat, jnp.zeros_like(kcat))
        vz = jnp.where(lm, vcat, jnp.zeros_like(vcat))
        s = _dot_nt(q, kz)
        if band:
            s = jnp.where(valid, s, -jnp.inf)
        sink = jnp.zeros((rows, 1), F32)
        for a in range(ATT_GROUP):
            sink = jnp.where(hh == a, sink_ref[g * ATT_GROUP + a], sink)
        m = jnp.maximum(jnp.max(s, axis=-1, keepdims=True), sink)
        p = jnp.exp(s - m)
        l = jnp.sum(p, axis=-1, keepdims=True) + jnp.exp(sink - m)
        out = out + _dot(p.astype(BF16), vz) / l
    o_ref[0, 0] = out.astype(BF16)


def _attention(qs, k, v, kc, vc, sink, band):
    B, nb = qs.shape[:2]
    lc = kc.shape[1]
    last = nb - 1
    qspec = pl.BlockSpec((1, 1, ATT_GROUP * ATT_BLOCK, LANES), lambda b, j: (b, j, 0, 0))
    cspec = pl.BlockSpec((1, lc, LANES), lambda b, j: (b, 0, 0))
    in_specs = [pl.BlockSpec(memory_space=pltpu.SMEM), qspec, cspec, cspec]
    args = [sink.astype(F32), qs, kc, vc]
    if band:
        prev = pl.BlockSpec((1, ATT_BLOCK, LANES), lambda b, j: (b, jnp.maximum(j - 1, 0), 0))
        cur = pl.BlockSpec((1, ATT_BLOCK, LANES), lambda b, j: (b, j, 0))
        nxt = pl.BlockSpec((1, ATT_BLOCK, LANES), lambda b, j: (b, jnp.minimum(j + 1, last), 0))
        in_specs += [prev, cur, nxt, prev, cur, nxt]
        args += [k, k, k, v, v, v]
    return pl.pallas_call(
        functools.partial(_attn_kernel, band=band),
        grid=(B, nb),
        in_specs=in_specs,
        out_specs=qspec,
        out_shape=jax.ShapeDtypeStruct(qs.shape, BF16),
        compiler_params=_params(("parallel", "parallel")),
        name="attention_band" if band else "attention_ctx",
    )(*args)


def _mlstm_step(dirs, T):
    kscale = M_QK_DIM ** -0.5
    si = lax.broadcasted_iota(jnp.int32, (T, T), 0)
    ri = lax.broadcasted_iota(jnp.int32, (T, T), 1)
    lane_qk = lax.broadcasted_iota(jnp.int32, (1, M_HEADS * M_QK_DIM), 1) // M_QK_DIM
    lane_m = lax.broadcasted_iota(jnp.int32, (1, LANES), 1)
    row_c = lax.broadcasted_iota(jnp.int32, (M_HEADS * M_QK_DIM, 1), 0) // M_QK_DIM
    combos = [(d, hd) for d in range(2) for hd in range(M_HEADS)]

    tri, bcol, gt, bt, blast = [], [], [], [], []
    for d, (q, k, v, g, C, n, m) in enumerate(dirs):
        t = (ri <= si) if d == 0 else (ri >= si)
        tri.append(t)
        lf = _log_sigmoid(g)
        bc = jnp.dot(t.astype(F32), lf, preferred_element_type=F32, precision=lax.Precision.HIGHEST)
        bcol.append(bc)
        gt.append(g.T)
        bt.append(bc.T)
        blast.append(bc[T - 1:T, :] if d == 0 else bc[0:1, :])

    def lanes(d, hd):
        return (2 * d) * M_HEADS + hd, (2 * d + 1) * M_HEADS + hd

    b_col = {c: bcol[c[0]][:, lanes(*c)[1]:lanes(*c)[1] + 1] for c in combos}
    ig_col = {c: dirs[c[0]][3][:, lanes(*c)[0]:lanes(*c)[0] + 1] for c in combos}
    alpha = {c: gt[c[0]][lanes(*c)[0]:lanes(*c)[0] + 1, :] - bt[c[0]][lanes(*c)[1]:lanes(*c)[1] + 1, :]
             for c in combos}
    m_old = {c: dirs[c[0]][6][:, c[1]:c[1] + 1] for c in combos}
    b_last = {c: blast[c[0]][:, lanes(*c)[1]:lanes(*c)[1] + 1] for c in combos}
    hmask = {hd: lane_qk == hd for hd in range(M_HEADS)}

    a_mat = {c: jnp.where(tri[c[0]], alpha[c], -jnp.inf) for c in combos}
    a_max = {c: jnp.max(a_mat[c], axis=1, keepdims=True) for c in combos}
    a_int = {c: b_col[c] + m_old[c] for c in combos}
    m_s = {c: jnp.maximum(a_int[c], b_col[c] + a_max[c]) for c in combos}
    w_int = {c: jnp.exp(a_int[c] - m_s[c]) for c in combos}
    w_mat = {c: jnp.exp(a_mat[c] + (b_col[c] - m_s[c])) for c in combos}
    qmask = {c: jnp.where(hmask[c[1]], dirs[c[0]][0], jnp.zeros_like(dirs[c[0]][0])) for c in combos}
    s_qk = {c: w_mat[c] * (_dot_nt(qmask[c], dirs[c[0]][1]) * kscale) for c in combos}
    vh = {c: dirs[c[0]][2][:, c[1] * M_V_DIM:(c[1] + 1) * M_V_DIM] for c in combos}
    c_bf = [dirs[d][4].astype(BF16) for d in range(2)]
    num = {c: _dot(s_qk[c].astype(BF16), vh[c]) + w_int[c] * _dot(qmask[c], c_bf[c[0]]) for c in combos}
    qn_all = [dirs[d][0].astype(F32) * dirs[d][5] for d in range(2)]
    qn = {c: jnp.sum(jnp.where(hmask[c[1]], qn_all[c[0]], 0.0), axis=1, keepdims=True) for c in combos}
    den = {c: jnp.sum(s_qk[c], axis=1, keepdims=True) + w_int[c] * qn[c] for c in combos}
    h = {c: num[c] / jnp.maximum(jnp.abs(den[c]), jnp.exp(-m_s[c])) for c in combos}

    r_col = {c: b_last[c] - b_col[c] + ig_col[c] for c in combos}
    m_new = {c: jnp.maximum(b_last[c] + m_old[c], jnp.max(r_col[c], axis=0, keepdims=True)) for c in combos}
    decay = {c: jnp.exp(b_last[c] + m_old[c] - m_new[c]) for c in combos}
    w_r = {c: jnp.exp(r_col[c] - m_new[c]) for c in combos}

    outs = []
    for d, (q, k, v, g, C, n, m) in enumerate(dirs):
        w_lanes = jnp.zeros((T, M_HEADS * M_QK_DIM), F32)
        dec_lanes = jnp.zeros((1, M_HEADS * M_QK_DIM), F32)
        dec_rows = jnp.zeros((M_HEADS * M_QK_DIM, 1), F32)
        m_row = jnp.zeros((1, LANES), F32)
        for hd in range(M_HEADS):
            w_lanes = jnp.where(hmask[hd], w_r[(d, hd)], w_lanes)
            dec_lanes = jnp.where(hmask[hd], decay[(d, hd)], dec_lanes)
            dec_rows = jnp.where(row_c == hd, decay[(d, hd)], dec_rows)
            m_row = jnp.where(lane_m == hd, m_new[(d, hd)], m_row)
        kw = k.astype(F32) * (w_lanes * kscale)
        kwt = kw.T.astype(BF16)
        upd = jnp.concatenate(
            [_dot(kwt[hd * M_QK_DIM:(hd + 1) * M_QK_DIM, :], vh[(d, hd)]) for hd in range(M_HEADS)], axis=0)
        c_new = dec_rows * C + upd
        n_new = dec_lanes * n + jnp.sum(kw, axis=0, keepdims=True)
        h_all = jnp.concatenate([h[(d, hd)] for hd in range(M_HEADS)], axis=1)
        outs.append((h_all, c_new, n_new, m_row))
    return outs


def _mlstm_kernel(gb_ref, qf_ref, kf_ref, vf_ref, gf_ref, qb_ref, kb_ref, vb_ref, gbk_ref,
                  c0_ref, n0_ref, m0_ref, hf_ref, hb_ref, cf_ref, nf_ref, mf_ref,
                  c_s, n_s, m_s):
    ci = pl.program_id(1)
    T = qf_ref.shape[1]

    @pl.when(ci == 0)
    def _():
        c_s[...] = c0_ref[0]
        n_s[...] = n0_ref[0]
        m_s[...] = m0_ref[0]

    gb = gb_ref[...]
    dirs = [(q_ref[0], k_ref[0], v_ref[0], g_ref[0] + gb, c_s[d], n_s[d], m_s[d])
            for d, (q_ref, k_ref, v_ref, g_ref) in enumerate(
                ((qf_ref, kf_ref, vf_ref, gf_ref), (qb_ref, kb_ref, vb_ref, gbk_ref)))]
    outs = _mlstm_step(dirs, T)
    for d, h_ref in enumerate((hf_ref, hb_ref)):
        h_all, c_new, n_new, m_row = outs[d]
        h_ref[0] = h_all
        c_s[d] = c_new
        n_s[d] = n_new
        m_s[d] = m_row

    @pl.when(ci == pl.num_programs(1) - 1)
    def _():
        cf_ref[0] = c_s[...]
        nf_ref[0] = n_s[...]
        mf_ref[0] = m_s[...]


MLSTM_TILE = 128


def _mlstm(qm, km, vm, gm, gate_b, state):
    B, S, _ = qm.shape
    T = MLSTM_TILE
    nc = S // T
    nq = M_HEADS * M_QK_DIM
    fwd = lambda n: pl.BlockSpec((1, T, n), lambda b, c: (b, c, 0))
    bwd = lambda n: pl.BlockSpec((1, T, n), lambda b, c: (b, nc - 1 - c, 0))
    st_specs = [pl.BlockSpec((1, 2, nq, M_V_DIM), lambda b, c: (b, 0, 0, 0)),
                pl.BlockSpec((1, 2, 1, nq), lambda b, c: (b, 0, 0, 0)),
                pl.BlockSpec((1, 2, 1, LANES), lambda b, c: (b, 0, 0, 0))]
    st_shapes = [jax.ShapeDtypeStruct((B, 2, nq, M_V_DIM), F32),
                 jax.ShapeDtypeStruct((B, 2, 1, nq), F32),
                 jax.ShapeDtypeStruct((B, 2, 1, LANES), F32)]
    gb_row = jnp.pad(gate_b.reshape(1, -1).astype(F32), ((0, 0), (0, LANES - gate_b.size)))
    outs = pl.pallas_call(
        _mlstm_kernel,
        grid=(B, nc),
        in_specs=[pl.BlockSpec((1, LANES), lambda b, c: (0, 0)),
                  fwd(256), fwd(256), fwd(512), fwd(LANES),
                  bwd(256), bwd(256), bwd(512), bwd(LANES)] + st_specs,
        out_specs=[fwd(M_OUT), bwd(M_OUT)] + st_specs,
        out_shape=[jax.ShapeDtypeStruct((B, S, M_OUT), F32)] * 2 + st_shapes,
        scratch_shapes=[pltpu.VMEM((2, nq, M_V_DIM), F32),
                        pltpu.VMEM((2, 1, nq), F32),
                        pltpu.VMEM((2, 1, LANES), F32)],
        compiler_params=_params(("parallel", "arbitrary")),
        name="mlstm_scan",
    )(gb_row, qm, km, vm, gm, qm, km, vm, gm, *state)
    return outs[0], outs[1], tuple(outs[2:])


def _route(scores, sel):
    tm = scores.shape[1]
    gi8 = lax.broadcasted_iota(jnp.int32, (GROUP_SIZE, tm), 0)

    def stack_rows(rows):
        out = jnp.broadcast_to(rows[0], (len(rows), tm))
        for r, v in enumerate(rows[1:], start=1):
            out = jnp.where(gi8 == r, v, out)
        return out

    gs = []
    for g in range(N_GROUPS):
        blk = sel[g * GROUP_SIZE:(g + 1) * GROUP_SIZE, :]
        m1 = jnp.max(blk, axis=0, keepdims=True)
        first = jnp.min(jnp.where(blk == m1, gi8, GROUP_SIZE), axis=0, keepdims=True)
        m2 = jnp.max(jnp.where(gi8 == first, -jnp.inf, blk), axis=0, keepdims=True)
        gs.append(m1 + m2)
    gsc = stack_rows(gs)
    gsel = jnp.zeros((N_GROUPS, tm), F32)
    for _ in range(TOPK_GROUPS):
        mx = jnp.max(gsc, axis=0, keepdims=True)
        first = jnp.min(jnp.where(gsc == mx, gi8, N_GROUPS), axis=0, keepdims=True)
        pick = gi8 == first
        gsel = jnp.where(pick, 1.0, gsel)
        gsc = jnp.where(pick, -jnp.inf, gsc)
    cur = jnp.concatenate(
        [jnp.where(gsel[g:g + 1, :] > 0.0, sel[g * GROUP_SIZE:(g + 1) * GROUP_SIZE, :], -jnp.inf)
         for g in range(N_GROUPS)], axis=0)
    ei = lax.broadcasted_iota(jnp.int32, (N_EXPERTS, tm), 0)
    idx, wts = [], []
    for _ in range(TOP_K):
        mx = jnp.max(cur, axis=0, keepdims=True)
        first = jnp.min(jnp.where(cur == mx, ei, N_EXPERTS), axis=0, keepdims=True)
        pick = ei == first
        idx.append(first)
        wts.append(jnp.sum(jnp.where(pick, scores, 0.0), axis=0, keepdims=True))
        cur = jnp.where(pick, -jnp.inf, cur)
    tot = wts[0]
    for w in wts[1:]:
        tot = tot + w
    wts = [w / tot * ROUTED_SCALE for w in wts]
    return stack_rows(idx), stack_rows(wts)


def _merge_kernel(x_ref, mod_ref, oat_ref, hf_ref, hb_ref, om_ref, ng_ref, u_ref, up_ref, un_ref,
                  bc_ref, cw_ref, gt_ref, bgb_ref, wa_ref, wm_ref, wc_ref, wo_ref, n2_ref,
                  rw_ref, rb_ref, sgu_ref, sd_ref,
                  base_ref, h2_ref, idx_ref, wt_ref):
    i = pl.program_id(1)
    tm = x_ref.shape[1]
    x = x_ref[0]
    g1 = mod_ref[0, 2:3, :]
    sh2 = mod_ref[0, 3:4, :]
    sc2 = mod_ref[0, 4:5, :]
    g2 = mod_ref[0, 5:6, :]

    ya = jnp.concatenate(
        [jnp.concatenate([oat_ref[0, qb, hh * ATT_BLOCK:(hh + 1) * ATT_BLOCK, :] for hh in range(ATT_GROUP)], axis=1)
         for qb in range(tm // ATT_BLOCK)], axis=0)

    hsum = hf_ref[0] + hb_ref[0]
    parts = []
    for hd in range(M_HEADS):
        hh_ = hsum[:, hd * M_V_DIM:(hd + 1) * M_V_DIM]
        parts.append(hh_ * lax.rsqrt(jnp.mean(hh_ * hh_, axis=-1, keepdims=True) + EPS))
    hn = jnp.concatenate(parts, axis=1) * ng_ref[...]
    ym = (_sigmoid(om_ref[0].astype(F32)) * hn).astype(BF16)

    u = u_ref[0].astype(F32)
    row = lax.broadcasted_iota(jnp.int32, (tm, 1), 0)
    has_prev = (i > 0).astype(F32)
    has_next = (i < pl.num_programs(1) - 1).astype(F32)
    prev_row = up_ref[0, BF16_SUBLANES - 1:BF16_SUBLANES, :].astype(F32) * has_prev
    next_row = un_ref[0, 0:1, :].astype(F32) * has_next
    u_m1 = jnp.where(row == 0, prev_row, pltpu.roll(u, 1, axis=0))
    u_p1 = jnp.where(row == tm - 1, next_row, pltpu.roll(u, tm - 1, axis=0))
    conv = cw_ref[0:1, :] * u_m1 + cw_ref[1:2, :] * u + cw_ref[2:3, :] * u_p1
    yc = (bc_ref[0].astype(F32) * conv).astype(BF16)

    gg = _sigmoid(gt_ref[0].astype(F32) + bgb_ref[...])
    ymix = (gg[:, 0:D_MODEL] * _dot(ya, wa_ref[...])
            + gg[:, D_MODEL:2 * D_MODEL] * _dot(ym, wm_ref[...])
            + gg[:, 2 * D_MODEL:3 * D_MODEL] * _dot(yc, wc_ref[...]))
    y = _dot(ymix.astype(BF16), wo_ref[...])
    xm = x + g1 * y

    h2f = _rms_mod(xm, n2_ref[...], sh2, sc2)
    h2 = h2f.astype(BF16)
    h2_ref[0] = h2

    logits_t = lax.dot_general(rw_ref[...], h2f, (((1,), (1,)), ((), ())),
                               preferred_element_type=F32, precision=lax.Precision.HIGHEST)
    scores = _sigmoid(logits_t)
    idx, wts = _route(scores, scores + rb_ref[...])
    idx_ref[0] = idx
    wt_ref[0] = wts

    a = _dot(h2, sgu_ref[...])
    act = (_silu(a[:, 0:SHARED_FF]) * a[:, SHARED_FF:2 * SHARED_FF]).astype(BF16)
    base_ref[0] = xm + g2 * _dot(act, sd_ref[...])


def _merge(x, mod, mod_row, oat, hf, hb, om, u, bc, gt, lw, tm):
    B, S, D = x.shape
    nt = S // tm
    hal = BF16_SUBLANES
    last_h = S // hal - 1
    tspec = lambda n: pl.BlockSpec((1, tm, n), lambda b, i: (b, i, 0))
    full = lambda a: pl.BlockSpec(a.shape, lambda b, i: (0,) * a.ndim)
    weights = [lw['mlstm_norm_g'], lw['conv_w'], lw['branch_gate_b'], lw['w_br_attn'], lw['w_br_mlstm'],
               lw['w_br_conv'], lw['w_out'], lw['norm2_g'], lw['router_wt'], lw['router_bias'],
               lw['sh_gu'], lw['sh_d']]
    in_specs = [tspec(D),
                pl.BlockSpec((1, 6, D), lambda b, i: (mod_row(b), 0, 0)),
                pl.BlockSpec((1, tm // ATT_BLOCK, ATT_GROUP * ATT_BLOCK, LANES), lambda b, i: (b, i, 0, 0)),
                tspec(M_OUT), tspec(M_OUT), tspec(M_OUT), full(weights[0]),
                tspec(CONV_WIDTH),
                pl.BlockSpec((1, hal, CONV_WIDTH), lambda b, i: (b, jnp.maximum(i * (tm // hal) - 1, 0), 0)),
                pl.BlockSpec((1, hal, CONV_WIDTH), lambda b, i: (b, jnp.minimum((i + 1) * (tm // hal), last_h), 0)),
                tspec(CONV_WIDTH), full(weights[1]), tspec(N_BRANCH * D), full(weights[2])]
    in_specs += [full(w) for w in weights[3:]]
    tr = lambda n, dt: (jax.ShapeDtypeStruct((B, n, S), dt), pl.BlockSpec((1, n, tm), lambda b, i: (b, 0, i)))
    outs = [(jax.ShapeDtypeStruct((B, S, D), F32), tspec(D)),
            (jax.ShapeDtypeStruct((B, S, D), BF16), tspec(D)),
            tr(TOP_K, jnp.int32), tr(TOP_K, F32)]
    return pl.pallas_call(
        _merge_kernel,
        grid=(B, nt),
        in_specs=in_specs,
        out_specs=[o[1] for o in outs],
        out_shape=[o[0] for o in outs],
        compiler_params=_params(("parallel", "parallel")),
        name="merge_route",
    )(x, mod, oat, hf, hb, om, weights[0], u, u, u, bc, weights[1], gt, weights[2], *weights[3:])


MOE_TILE = 256
CHUNK = BF16_SUBLANES
GROUP_CHUNKS = 16
GROUP_ROWS = GROUP_CHUNKS * CHUNK
TILE_ROWS = MOE_TILE * TOP_K + N_EXPERTS * CHUNK
TILE_CHUNKS = TILE_ROWS // CHUNK
TILE_GROUPS = TILE_ROWS // GROUP_ROWS


def _dispatch_kernel(h_ref, idx_ref, wt_ref, xg_ref, pw_ref, cnt_ref):
    tm = h_ref.shape[1]
    idx = idx_ref[0]
    wts = wt_ref[0]
    ei = lax.broadcasted_iota(jnp.int32, (N_EXPERTS, tm), 0)
    pick = jnp.zeros((N_EXPERTS, tm), F32)
    for kk in range(TOP_K):
        pick = jnp.where(ei == idx[kk:kk + 1, :], 1.0, pick)
    t0 = lax.broadcasted_iota(jnp.int32, (tm, tm), 0)
    t1 = lax.broadcasted_iota(jnp.int32, (tm, tm), 1)
    rank = _dot(pick.astype(BF16), jnp.where(t0 < t1, 1.0, 0.0).astype(BF16))
    n_e = jnp.sum(pick, axis=1, keepdims=True)
    n_pad = jnp.floor((n_e + (CHUNK - 1)) * (1.0 / CHUNK)) * CHUNK
    e0 = lax.broadcasted_iota(jnp.int32, (N_EXPERTS, N_EXPERTS), 0)
    e1 = lax.broadcasted_iota(jnp.int32, (N_EXPERTS, N_EXPERTS), 1)
    seg = _dot(jnp.where(e1 < e0, 1.0, 0.0).astype(BF16),
               jnp.broadcast_to(n_pad, (N_EXPERTS, tm)).astype(BF16))
    posmat = seg + rank
    riota = lax.broadcasted_iota(jnp.int32, (TILE_ROWS, tm), 0)
    pw = jnp.full((TILE_ROWS, tm), -1.0, F32)
    for kk in range(TOP_K):
        pos = jnp.sum(jnp.where(ei == idx[kk:kk + 1, :], posmat, 0.0), axis=0, keepdims=True).astype(jnp.int32)
        pw = jnp.where(riota == pos, wts[kk:kk + 1, :], pw)
    onehot = jnp.where(pw >= 0.0, 1.0, 0.0).astype(BF16)
    xg_ref[...] = _dot(onehot, h_ref[0]).astype(BF16)
    pw_ref[...] = jnp.maximum(pw, 0.0).astype(BF16)
    cnt_ref[0] = jnp.broadcast_to(n_e, (N_EXPERTS, LANES)).astype(jnp.int32)


def _dispatch(h2, idx, wts):
    B, S, D = h2.shape
    tm = MOE_TILE
    nt = S // tm
    return pl.pallas_call(
        _dispatch_kernel,
        grid=(B, nt),
        in_specs=[pl.BlockSpec((1, tm, D), lambda b, i: (b, i, 0)),
                  pl.BlockSpec((1, TOP_K, tm), lambda b, i: (b, 0, i)),
                  pl.BlockSpec((1, TOP_K, tm), lambda b, i: (b, 0, i))],
        out_specs=[pl.BlockSpec((TILE_ROWS, D), lambda b, i: (b * nt + i, 0)),
                   pl.BlockSpec((TILE_ROWS, tm), lambda b, i: (b * nt + i, 0)),
                   pl.BlockSpec((1, N_EXPERTS, LANES), lambda b, i: (b * nt + i, 0, 0))],
        out_shape=[jax.ShapeDtypeStruct((B * nt * TILE_ROWS, D), BF16),
                   jax.ShapeDtypeStruct((B * nt * TILE_ROWS, tm), BF16),
                   jax.ShapeDtypeStruct((B * nt, N_EXPERTS, LANES), jnp.int32)],
        compiler_params=_params(("parallel", "parallel")),
        name="moe_dispatch",
    )(h2, idx, wts)


def _moe_tables(cnt, g_max):
    nt = cnt.shape[0]
    cc = (cnt + (CHUNK - 1)) // CHUNK
    segblk = jnp.cumsum(cc, axis=1) - cc
    tile_chunks = jnp.sum(cc, axis=1)
    prior = jnp.cumsum(cc, axis=0) - cc
    ge_cnt = (jnp.sum(cc, axis=0) + (GROUP_CHUNKS - 1)) // GROUP_CHUNKS
    gbase = jnp.cumsum(ge_cnt) - ge_cnt
    n_groups = jnp.sum(ge_cnt)
    c = jnp.arange(TILE_CHUNKS, dtype=jnp.int32)
    e_of = jnp.sum(((segblk + cc)[:, None, :] <= c[None, :, None]).astype(jnp.int32), axis=-1)
    e_of = jnp.minimum(e_of, N_EXPERTS - 1)
    take = lambda a: jnp.take_along_axis(a, e_of, axis=1)
    pos = gbase[e_of] * GROUP_CHUNKS + take(prior) + (c[None, :] - take(segblk))
    valid = c[None, :] < tile_chunks[:, None]
    pos = jnp.where(valid, pos, pos[:, 0:1]).astype(jnp.int32)
    src = jnp.arange(nt, dtype=jnp.int32)[:, None] * TILE_CHUNKS + c[None, :]
    n_slots = g_max * GROUP_CHUNKS
    tbl_f = jnp.zeros((n_slots,), jnp.int32).at[jnp.where(valid, pos, n_slots).reshape(-1)].set(
        src.reshape(-1), mode='drop')
    g = jnp.arange(g_max, dtype=jnp.int32)
    grp_e = jnp.minimum(jnp.sum(((gbase + ge_cnt)[None, :] <= g[:, None]).astype(jnp.int32), axis=1),
                        N_EXPERTS - 1).astype(jnp.int32)
    return tbl_f, grp_e, n_groups.reshape(1).astype(jnp.int32), pos.reshape(-1), tile_chunks.astype(jnp.int32)


def _ffn_kernel(tbl_ref, ge_ref, na_ref, *refs):
    x_refs = refs[:GROUP_CHUNKS]
    wgu_ref, wd_ref, y_ref = refs[GROUP_CHUNKS:]
    g = pl.program_id(0)

    @pl.when(g < na_ref[0])
    def _():
        x = jnp.concatenate([r[...] for r in x_refs], axis=0)
        a = _dot(x, wgu_ref[0])
        act = (_silu(a[:, 0:EXPERT_FF]) * a[:, EXPERT_FF:2 * EXPERT_FF]).astype(BF16)
        y_ref[...] = _dot(act, wd_ref[0]).astype(BF16)

    @pl.when(g >= na_ref[0])
    def _():
        y_ref[...] = jnp.zeros_like(y_ref)


def _ffn_grouped(xg, tbl_f, grp_e, n_groups, wgu, wd, g_max):
    D = xg.shape[1]
    chunk_spec = lambda j: pl.BlockSpec((CHUNK, D), lambda g, tbl, ge, na: (tbl[g * GROUP_CHUNKS + j], 0))
    grid_spec = pltpu.PrefetchScalarGridSpec(
        num_scalar_prefetch=3,
        grid=(g_max,),
        in_specs=[chunk_spec(j) for j in range(GROUP_CHUNKS)]
        + [pl.BlockSpec((1, D, 2 * EXPERT_FF), lambda g, tbl, ge, na: (ge[g], 0, 0)),
           pl.BlockSpec((1, EXPERT_FF, D), lambda g, tbl, ge, na: (ge[g], 0, 0))],
        out_specs=pl.BlockSpec((GROUP_ROWS, D), lambda g, tbl, ge, na: (g, 0)),
    )
    return pl.pallas_call(
        _ffn_kernel,
        grid_spec=grid_spec,
        out_shape=jax.ShapeDtypeStruct((g_max * GROUP_ROWS, D), BF16),
        compiler_params=_params(("arbitrary",)),
        name="moe_ffn",
    )(tbl_f, grp_e, n_groups, *([xg] * GROUP_CHUNKS), wgu, wd)


def _combine_kernel(tbl_ref, nch_ref, *refs):
    y_refs = refs[:GROUP_CHUNKS]
    pw_ref, base_ref, mod_ref, o_ref, acc_ref = refs[GROUP_CHUNKS:]
    i = pl.program_id(0)
    s = pl.program_id(1)

    @pl.when(s == 0)
    def _():
        acc_ref[...] = jnp.zeros_like(acc_ref)

    @pl.when(s * GROUP_CHUNKS < nch_ref[i])
    def _():
        y = jnp.concatenate([r[...] for r in y_refs], axis=0)
        acc_ref[...] += _dot_tn(pw_ref[...], y)

    @pl.when(s == pl.num_programs(1) - 1)
    def _():
        o_ref[...] = base_ref[...] + mod_ref[0, 5:6, :] * acc_ref[...]


def _combine(ys, pos, tile_chunks, pw, base, mod, mod_row):
    B, S, D = base.shape
    tm = MOE_TILE
    nt = S // tm
    chunk_spec = lambda j: pl.BlockSpec(
        (CHUNK, D), lambda i, s, tbl, nch: (tbl[i * TILE_CHUNKS + s * GROUP_CHUNKS + j], 0))
    grid_spec = pltpu.PrefetchScalarGridSpec(
        num_scalar_prefetch=2,
        grid=(B * nt, TILE_GROUPS),
        in_specs=[chunk_spec(j) for j in range(GROUP_CHUNKS)]
        + [pl.BlockSpec((GROUP_ROWS, tm), lambda i, s, tbl, nch: (i * TILE_GROUPS + s, 0)),
           pl.BlockSpec((tm, D), lambda i, s, tbl, nch: (i, 0)),
           pl.BlockSpec((1, 6, D), lambda i, s, tbl, nch: (mod_row(i // nt), 0, 0))],
        out_specs=pl.BlockSpec((tm, D), lambda i, s, tbl, nch: (i, 0)),
        scratch_shapes=[pltpu.VMEM((tm, D), F32)],
    )
    out = pl.pallas_call(
        _combine_kernel,
        grid_spec=grid_spec,
        out_shape=jax.ShapeDtypeStruct((B * S, D), F32),
        compiler_params=_params(("parallel", "arbitrary")),
        name="moe_combine",
    )(pos, tile_chunks, *([ys] * GROUP_CHUNKS), pw, base.reshape(B * S, D), mod)
    return out.reshape(B, S, D)


def _moe_sparse(h2, idx, wts, base, mod, mod_row, wgu, wd):
    B, S, D = h2.shape
    n_tiles = B * (S // MOE_TILE)
    g_max = (n_tiles * TILE_CHUNKS + N_EXPERTS * (GROUP_CHUNKS - 1) + GROUP_CHUNKS - 1) // GROUP_CHUNKS
    xg, pw, cnt = _dispatch(h2, idx, wts)
    tbl_f, grp_e, n_groups, pos, tile_chunks = _moe_tables(cnt[:, :, 0], g_max)
    ys = _ffn_grouped(xg, tbl_f, grp_e, n_groups, wgu, wd, g_max)
    return _combine(ys, pos, tile_chunks, pw, base, mod, mod_row)


def _final_kernel(x_ref, g_ref, o_ref):
    x = x_ref[0]
    o_ref[0] = x * lax.rsqrt(jnp.mean(x * x, axis=-1, keepdims=True) + EPS) * g_ref[...]


def _final_norm(x, g, tm):
    B, S, D = x.shape
    spec = pl.BlockSpec((1, tm, D), lambda b, i: (b, i, 0))
    return pl.pallas_call(
        _final_kernel,
        grid=(B, S // tm),
        in_specs=[spec, pl.BlockSpec((1, D), lambda b, i: (0, 0))],
        out_specs=spec,
        out_shape=jax.ShapeDtypeStruct((B, S, D), F32),
        compiler_params=_params(("parallel", "parallel")),
        name="final_norm",
    )(x, g.reshape(1, D))


def _zero_state(batch):
    nq = M_HEADS * M_QK_DIM
    return (jnp.zeros((batch, 2, nq, M_V_DIM), F32),
            jnp.zeros((batch, 2, 1, nq), F32),
            jnp.zeros((batch, 2, 1, LANES), F32))


def kernel(x, c, ctx, c_ctx, ada_w, ada_b, norm1_g, norm2_g, w_in, attn_sink, mlstm_gate_b, mlstm_norm_g, conv_w, w_br_attn, w_br_mlstm, w_br_conv, branch_gate_b, w_out, router_w, router_bias, exp_w_gate, exp_w_up, exp_w_down, sh_w_gate, sh_w_up, sh_w_down, final_g):
    B, S, D = x.shape
    L = ctx.shape[1]
    depth = ada_w.shape[0]
    ctx_row = B

    pad_rows = (-(B + 1)) % 8
    cc = jnp.concatenate([c, c_ctx[None, :], jnp.zeros((pad_rows, D), F32)], axis=0)
    mod_all = _ada(cc, ada_w, ada_b).reshape(depth, B + 1 + pad_rows, 6, D)

    cos_t, sin_t = _rope_tables(S)
    cos_c = jnp.ones((L, LANES), F32)
    sin_c = jnp.zeros((L, LANES), F32)
    col_idx = _proj_column_index()
    att_idx = _attn_row_index()
    lat_row = lambda b: b
    ctx_mod = lambda b: ctx_row

    xc = ctx
    for l in range(depth):
        need_ctx = l < depth - 1
        mod = mod_all[l]
        w_ext = jnp.concatenate([w_in[l], jnp.zeros((D, 1), F32)], axis=1)
        w_p = jnp.take(w_ext, col_idx, axis=1).astype(BF16)
        lw = {
            'mlstm_norm_g': mlstm_norm_g[l].reshape(1, M_OUT),
            'conv_w': conv_w[l],
            'branch_gate_b': branch_gate_b[l].reshape(1, N_BRANCH * D),
            'w_br_attn': jnp.take(w_br_attn[l], att_idx, axis=0).astype(BF16),
            'w_br_mlstm': w_br_mlstm[l].astype(BF16),
            'w_br_conv': w_br_conv[l].astype(BF16),
            'w_out': w_out[l].astype(BF16),
            'norm2_g': norm2_g[l].reshape(1, D),
            'router_wt': router_w[l].T,
            'router_bias': router_bias[l].reshape(N_EXPERTS, 1),
            'sh_gu': jnp.concatenate([sh_w_gate[l], sh_w_up[l]], axis=1).astype(BF16),
            'sh_d': sh_w_down[l].astype(BF16),
        }
        wgu = jnp.concatenate([exp_w_gate[l], exp_w_up[l]], axis=2).astype(BF16)
        wd = exp_w_down[l].astype(BF16)

        pc = _in_proj(xc, mod, ctx_mod, norm1_g[l], w_p, cos_c, sin_c, tm=256)
        p = _in_proj(x, mod, lat_row, norm1_g[l], w_p, cos_t, sin_t, tm=256)
        qs_c, k_c, v_c, qm_c, km_c, vm_c, om_c, gm_c, bc_c, u_c, gt_c = pc
        qs, k, v, qm, km, vm, om, gm, bc, u, gt = p

        oat = _attention(qs, k, v, k_c, v_c, attn_sink[l], band=True)
        hf_c, hb_c, st = _mlstm(qm_c, km_c, vm_c, gm_c, mlstm_gate_b[l], _zero_state(B))
        hf, hb, _ = _mlstm(qm, km, vm, gm, mlstm_gate_b[l], st)

        base, h2, idx, wts = _merge(x, mod, lat_row, oat, hf, hb, om, u, bc, gt, lw, tm=256)
        x_new = _moe_sparse(h2, idx, wts, base, mod, lat_row, wgu, wd)

        if need_ctx:
            oat_c = _attention(qs_c, None, None, k_c, v_c, attn_sink[l], band=False)
            base_c, h2_c, idx_c, wts_c = _merge(xc, mod, ctx_mod, oat_c, hf_c, hb_c, om_c, u_c, bc_c, gt_c, lw, tm=256)
            xc = _moe_sparse(h2_c, idx_c, wts_c, base_c, mod, ctx_mod, wgu, wd)
        x = x_new
    return _final_norm(x, final_g, tm=512)
```

```python
import functools

import numpy as np
import jax
import jax.numpy as jnp
from jax import lax
from jax.experimental import pallas as pl
from jax.experimental.pallas import tpu as pltpu

F32 = jnp.float32
BF16 = jnp.bfloat16

D_MODEL = 1024
GRID_W = 64
EPS = 1e-6
ATT_HEADS = 8
ATT_KV_HEADS = 2
ATT_HEAD_DIM = 64
ATT_GROUP = ATT_HEADS // ATT_KV_HEADS
ATT_BLOCK = 128
ATT_OUT = ATT_HEADS * ATT_HEAD_DIM
ROPE_BASE = 10000.0
M_HEADS = 4
M_QK_DIM = 64
M_V_DIM = 128
M_CHUNK = 64
M_OUT = M_HEADS * M_V_DIM
CONV_WIDTH = 512
N_BRANCH = 3
N_EXPERTS = 64
N_GROUPS = 8
GROUP_SIZE = N_EXPERTS // N_GROUPS
TOPK_GROUPS = 4
TOP_K = 8
EXPERT_FF = 256
SHARED_FF = 256
ROUTED_SCALE = 2.5

LANES = 128
BF16_SUBLANES = 16
VMEM_LIMIT = 56 * 1024 * 1024

_SEGS = (('q', 512), ('k', 128), ('v', 128), ('qm', 256), ('km', 256), ('vm', 512), ('om', 512),
         ('bc', 512), ('cc', 512), ('xc', 512), ('gt', 3072), ('gm', 128))
_OFF = {}
_o = 0
for _n, _s in _SEGS:
    _OFF[_n] = (_o, _o + _s)
    _o += _s
N_PROJ = _o
D_IN = 6928


def _proj_column_index():
    idx = []
    half = ATT_HEAD_DIM // 2
    for hh in range(ATT_GROUP):
        for g in range(ATT_KV_HEADS):
            head = g * ATT_GROUP + hh
            for par in range(2):
                idx += [head * ATT_HEAD_DIM + 2 * i + par for i in range(half)]
    for g in range(ATT_KV_HEADS):
        for par in range(2):
            idx += [512 + g * ATT_HEAD_DIM + 2 * i + par for i in range(half)]
    idx += list(range(640, 768))
    idx += list(range(768, 2304))
    idx += list(range(2320, 3856))
    idx += list(range(3856, 6928))
    idx += list(range(2304, 2320)) + [D_IN] * (LANES - 16)
    assert len(idx) == N_PROJ
    return np.asarray(idx, np.int32)


def _attn_row_index():
    idx = []
    for hh in range(ATT_GROUP):
        for g in range(ATT_KV_HEADS):
            head = g * ATT_GROUP + hh
            idx += [head * ATT_HEAD_DIM + d for d in range(ATT_HEAD_DIM)]
    return np.asarray(idx, np.int32)


def _rope_tables(seq):
    rows = seq // GRID_W
    row = jnp.repeat(jnp.arange(rows, dtype=F32), GRID_W)
    col = jnp.tile(jnp.arange(GRID_W, dtype=F32), rows)
    n_pairs = ATT_HEAD_DIM // 4
    inv_freq = ROPE_BASE ** (-jnp.arange(n_pairs, dtype=F32) / n_pairs)
    ang = jnp.concatenate([row[:, None] * inv_freq, col[:, None] * inv_freq], axis=-1)
    c, s = jnp.cos(ang), jnp.sin(ang)
    cos_t = jnp.concatenate([c, c, c, c], axis=-1)
    sin_t = jnp.concatenate([-s, s, -s, s], axis=-1)
    return cos_t, sin_t


def _dot(a, b):
    return jnp.dot(a, b, preferred_element_type=F32)


def _dot_nt(a, b):
    return lax.dot_general(a, b, (((1,), (1,)), ((), ())), preferred_element_type=F32)


def _dot_tn(a, b):
    return lax.dot_general(a, b, (((0,), (0,)), ((), ())), preferred_element_type=F32)


def _sigmoid(x):
    return 1.0 / (1.0 + jnp.exp(-x))


def _silu(x):
    return x * _sigmoid(x)


def _log_sigmoid(x):
    return jnp.minimum(x, 0.0) - jnp.log(1.0 + jnp.exp(-jnp.abs(x)))


def _rms_mod(x, g, shift, scale):
    y = x * lax.rsqrt(jnp.mean(x * x, axis=-1, keepdims=True) + EPS) * g
    return y * (1.0 + scale) + shift


def _params(sem):
    return pltpu.CompilerParams(dimension_semantics=sem, vmem_limit_bytes=VMEM_LIMIT)


def _ada_kernel(c_ref, w_ref, b_ref, o_ref):
    s = _silu(c_ref[...])
    o_ref[0] = jnp.dot(s, w_ref[0], preferred_element_type=F32,
                       precision=lax.Precision.HIGHEST) + b_ref[0]


def _ada(cc, ada_w, ada_b):
    depth, d, n = ada_w.shape
    rows = cc.shape[0]
    tn = 1536
    return pl.pallas_call(
        _ada_kernel,
        grid=(depth, n // tn),
        in_specs=[pl.BlockSpec((rows, d), lambda l, j: (0, 0)),
                  pl.BlockSpec((1, d, tn), lambda l, j: (l, 0, j)),
                  pl.BlockSpec((1, 1, tn), lambda l, j: (l, 0, j))],
        out_specs=pl.BlockSpec((1, rows, tn), lambda l, j: (l, 0, j)),
        out_shape=jax.ShapeDtypeStruct((depth, rows, n), F32),
        compiler_params=_params(("parallel", "parallel")),
        name="ada_mod",
    )(cc, ada_w, ada_b.reshape(depth, 1, n))


def _swap_halves(x):
    lane = lax.broadcasted_iota(jnp.int32, x.shape, 1)
    first = (lane % ATT_HEAD_DIM) < (ATT_HEAD_DIM // 2)
    return jnp.where(first, pltpu.roll(x, LANES - 32, axis=1), pltpu.roll(x, 32, axis=1))


def _in_kernel(x_ref, mod_ref, g_ref, w_ref, cos_ref, sin_ref,
               qs_ref, k_ref, v_ref, qm_ref, km_ref, vm_ref, om_ref, gm_ref, bc_ref, u_ref, gt_ref):
    tm = x_ref.shape[1]
    h = _rms_mod(x_ref[0], g_ref[...], mod_ref[0, 0:1, :], mod_ref[0, 1:2, :]).astype(BF16)

    def proj(name):
        lo, hi = _OFF[name]
        return _dot(h, w_ref[:, lo:hi])

    cos_t = cos_ref[...]
    sin_t = sin_ref[...]

    def rope(t):
        return t * cos_t + _swap_halves(t) * sin_t

    q = proj('q')
    scale = ATT_HEAD_DIM ** -0.5
    for hh in range(ATT_GROUP):
        r = (rope(q[:, hh * LANES:(hh + 1) * LANES]) * scale).astype(BF16)
        for qb in range(tm // ATT_BLOCK):
            qs_ref[0, qb, hh * ATT_BLOCK:(hh + 1) * ATT_BLOCK, :] = r[qb * ATT_BLOCK:(qb + 1) * ATT_BLOCK, :]
    k_ref[0] = rope(proj('k')).astype(BF16)
    v_ref[0] = proj('v').astype(BF16)
    qm_ref[0] = proj('qm').astype(BF16)
    km_ref[0] = proj('km').astype(BF16)
    vm_ref[0] = proj('vm').astype(BF16)
    om_ref[0] = proj('om').astype(BF16)
    gm_ref[0] = proj('gm')
    bc_ref[0] = proj('bc').astype(BF16)
    u_ref[0] = (proj('cc') * proj('xc')).astype(BF16)
    gt_ref[0] = proj('gt').astype(BF16)


def _in_proj(x, mod, mod_row, norm_g, w_p, cos_t, sin_t, tm):
    B, S, D = x.shape
    nb = S // ATT_BLOCK
    tok = lambda n, dt: jax.ShapeDtypeStruct((B, S, n), dt)
    tspec = lambda n: pl.BlockSpec((1, tm, n), lambda b, i: (b, i, 0))
    out_shape = (jax.ShapeDtypeStruct((B, nb, ATT_GROUP * ATT_BLOCK, LANES), BF16),
                 tok(128, BF16), tok(128, BF16), tok(256, BF16), tok(256, BF16), tok(512, BF16),
                 tok(512, BF16), tok(128, F32), tok(512, BF16), tok(512, BF16), tok(3072, BF16))
    out_specs = (pl.BlockSpec((1, tm // ATT_BLOCK, ATT_GROUP * ATT_BLOCK, LANES), lambda b, i: (b, i, 0, 0)),
                 tspec(128), tspec(128), tspec(256), tspec(256), tspec(512), tspec(512), tspec(128),
                 tspec(512), tspec(512), tspec(3072))
    return pl.pallas_call(
        _in_kernel,
        grid=(B, S // tm),
        in_specs=[pl.BlockSpec((1, tm, D), lambda b, i: (b, i, 0)),
                  pl.BlockSpec((1, 6, D), lambda b, i: (mod_row(b), 0, 0)),
                  pl.BlockSpec((1, D), lambda b, i: (0, 0)),
                  pl.BlockSpec((D, N_PROJ), lambda b, i: (0, 0), pipeline_mode=pl.Buffered(1)),
                  pl.BlockSpec((tm, LANES), lambda b, i: (i, 0)),
                  pl.BlockSpec((tm, LANES), lambda b, i: (i, 0))],
        out_specs=out_specs,
        out_shape=out_shape,
        compiler_params=_params(("parallel", "parallel")),
        name="in_proj",
    )(x, mod, norm_g.reshape(1, D), w_p, cos_t, sin_t)


def _attn_kernel(sink_ref, qs_ref, kc_ref, vc_ref, *rest, band):
    if band:
        kp_ref, kcur_ref, kn_ref, vp_ref, vcur_ref, vn_ref, o_ref = rest
    else:
        (o_ref,) = rest
    j = pl.program_id(1)
    nblk = pl.num_programs(1)
    q = qs_ref[0, 0]
    rows = q.shape[0]
    if band:
        kcat = jnp.concatenate([kc_ref[0], kp_ref[0], kcur_ref[0], kn_ref[0]], axis=0)
        vcat = jnp.concatenate([vc_ref[0], vp_ref[0], vcur_ref[0], vn_ref[0]], axis=0)
    else:
        kcat = kc_ref[0]
        vcat = vc_ref[0]
    nkeys = kcat.shape[0]
    lc = kc_ref.shape[1]
    lane = lax.broadcasted_iota(jnp.int32, (1, LANES), 1)
    t = lax.broadcasted_iota(jnp.int32, (rows, 1), 0) % ATT_BLOCK
    hh = lax.broadcasted_iota(jnp.int32, (rows, 1), 0) // ATT_BLOCK
    if band:
        c = lax.broadcasted_iota(jnp.int32, (1, nkeys), 1)
        t_prev = t + jnp.where(j > 0, 0, 2 * ATT_BLOCK)
        t_next = t - jnp.where(j < nblk - 1, 0, 2 * ATT_BLOCK)
        i_prev = c - lc
        i_next = c - (lc + 2 * ATT_BLOCK)
        valid = ((c < lc)
                 | ((c >= lc) & (c < lc + ATT_BLOCK) & (i_prev >= t_prev))
                 | ((c >= lc + ATT_BLOCK) & (c < lc + 2 * ATT_BLOCK))
                 | ((c >= lc + 2 * ATT_BLOCK) & (i_next <= t_next)))
    out = jnp.zeros((rows, LANES), F32)
    for g in range(ATT_KV_HEADS):
        lm = (lane < ATT_HEAD_DIM) if g == 0 else (lane >= ATT_HEAD_DIM)
        kz = jnp.where(lm, kcat, jnp.zeros_like(kcat))
        vz = jnp.where(lm, vcat, jnp.zeros_like(vcat))
        s = _dot_nt(q, kz)
        if band:
            s = jnp.where(valid, s, -jnp.inf)
        sink = jnp.zeros((rows, 1), F32)
        for a in range(ATT_GROUP):
            sink = jnp.where(hh == a, sink_ref[g * ATT_GROUP + a], sink)
        m = jnp.maximum(jnp.max(s, axis=-1, keepdims=True), sink)
        p = jnp.exp(s - m)
        l = jnp.sum(p, axis=-1, keepdims=True) + jnp.exp(sink - m)
        out = out + _dot(p.astype(BF16), vz) / l
    o_ref[0, 0] = out.astype(BF16)


def _attention(qs, k, v, kc, vc, sink, band):
    B, nb = qs.shape[:2]
    lc = kc.shape[1]
    last = nb - 1
    qspec = pl.BlockSpec((1, 1, ATT_GROUP * ATT_BLOCK, LANES), lambda b, j: (b, j, 0, 0))
    cspec = pl.BlockSpec((1, lc, LANES), lambda b, j: (b, 0, 0))
    in_specs = [pl.BlockSpec(memory_space=pltpu.SMEM), qspec, cspec, cspec]
    args = [sink.astype(F32), qs, kc, vc]
    if band:
        prev = pl.BlockSpec((1, ATT_BLOCK, LANES), lambda b, j: (b, jnp.maximum(j - 1, 0), 0))
        cur = pl.BlockSpec((1, ATT_BLOCK, LANES), lambda b, j: (b, j, 0))
        nxt = pl.BlockSpec((1, ATT_BLOCK, LANES), lambda b, j: (b, jnp.minimum(j + 1, last), 0))
        in_specs += [prev, cur, nxt, prev, cur, nxt]
        args += [k, k, k, v, v, v]
    return pl.pallas_call(
        functools.partial(_attn_kernel, band=band),
        grid=(B, nb),
        in_specs=in_specs,
        out_specs=qspec,
        out_shape=jax.ShapeDtypeStruct(qs.shape, BF16),
        compiler_params=_params(("parallel", "parallel")),
        name="attention_band" if band else "attention_ctx",
    )(*args)


def _mlstm_step(dirs, T):
    kscale = M_QK_DIM ** -0.5
    si = lax.broadcasted_iota(jnp.int32, (T, T), 0)
    ri = lax.broadcasted_iota(jnp.int32, (T, T), 1)
    lane_qk = lax.broadcasted_iota(jnp.int32, (1, M_HEADS * M_QK_DIM), 1) // M_QK_DIM
    lane_m = lax.broadcasted_iota(jnp.int32, (1, LANES), 1)
    row_c = lax.broadcasted_iota(jnp.int32, (M_HEADS * M_QK_DIM, 1), 0) // M_QK_DIM
    combos = [(d, hd) for d in range(2) for hd in range(M_HEADS)]

    tri, bcol, gt, bt, blast = [], [], [], [], []
    for d, (q, k, v, g, C, n, m) in enumerate(dirs):
        t = (ri <= si) if d == 0 else (ri >= si)
        tri.append(t)
        lf = _log_sigmoid(g)
        bc = jnp.dot(t.astype(F32), lf, preferred_element_type=F32, precision=lax.Precision.HIGHEST)
        bcol.append(bc)
        gt.append(g.T)
        bt.append(bc.T)
        blast.append(bc[T - 1:T, :] if d == 0 else bc[0:1, :])

    def lanes(d, hd):
        return (2 * d) * M_HEADS + hd, (2 * d + 1) * M_HEADS + hd

    b_col = {c: bcol[c[0]][:, lanes(*c)[1]:lanes(*c)[1] + 1] for c in combos}
    ig_col = {c: dirs[c[0]][3][:, lanes(*c)[0]:lanes(*c)[0] + 1] for c in combos}
    alpha = {c: gt[c[0]][lanes(*c)[0]:lanes(*c)[0] + 1, :] - bt[c[0]][lanes(*c)[1]:lanes(*c)[1] + 1, :]
             for c in combos}
    m_old = {c: dirs[c[0]][6][:, c[1]:c[1] + 1] for c in combos}
    b_last = {c: blast[c[0]][:, lanes(*c)[1]:lanes(*c)[1] + 1] for c in combos}
    hmask = {hd: lane_qk == hd for hd in range(M_HEADS)}

    a_mat = {c: jnp.where(tri[c[0]], alpha[c], -jnp.inf) for c in combos}
    a_max = {c: jnp.max(a_mat[c], axis=1, keepdims=True) for c in combos}
    a_int = {c: b_col[c] + m_old[c] for c in combos}
    m_s = {c: jnp.maximum(a_int[c], b_col[c] + a_max[c]) for c in combos}
    w_int = {c: jnp.exp(a_int[c] - m_s[c]) for c in combos}
    w_mat = {c: jnp.exp(a_mat[c] + (b_col[c] - m_s[c])) for c in combos}
    qmask = {c: jnp.where(hmask[c[1]], dirs[c[0]][0], jnp.zeros_like(dirs[c[0]][0])) for c in combos}
    s_qk = {c: w_mat[c] * (_dot_nt(qmask[c], dirs[c[0]][1]) * kscale) for c in combos}
    vh = {c: dirs[c[0]][2][:, c[1] * M_V_DIM:(c[1] + 1) * M_V_DIM] for c in combos}
    c_bf = [dirs[d][4].astype(BF16) for d in range(2)]
    num = {c: _dot(s_qk[c].astype(BF16), vh[c]) + w_int[c] * _dot(qmask[c], c_bf[c[0]]) for c in combos}
    qn_all = [dirs[d][0].astype(F32) * dirs[d][5] for d in range(2)]
    qn = {c: jnp.sum(jnp.where(hmask[c[1]], qn_all[c[0]], 0.0), axis=1, keepdims=True) for c in combos}
    den = {c: jnp.sum(s_qk[c], axis=1, keepdims=True) + w_int[c] * qn[c] for c in combos}
    h = {c: num[c] / jnp.maximum(jnp.abs(den[c]), jnp.exp(-m_s[c])) for c in combos}

    r_col = {c: b_last[c] - b_col[c] + ig_col[c] for c in combos}
    m_new = {c: jnp.maximum(b_last[c] + m_old[c], jnp.max(r_col[c], axis=0, keepdims=True)) for c in combos}
    decay = {c: jnp.exp(b_last[c] + m_old[c] - m_new[c]) for c in combos}
    w_r = {c: jnp.exp(r_col[c] - m_new[c]) for c in combos}

    outs = []
    for d, (q, k, v, g, C, n, m) in enumerate(dirs):
        w_lanes = jnp.zeros((T, M_HEADS * M_QK_DIM), F32)
        dec_lanes = jnp.zeros((1, M_HEADS * M_QK_DIM), F32)
        dec_rows = jnp.zeros((M_HEADS * M_QK_DIM, 1), F32)
        m_row = jnp.zeros((1, LANES), F32)
        for hd in range(M_HEADS):
            w_lanes = jnp.where(hmask[hd], w_r[(d, hd)], w_lanes)
            dec_lanes = jnp.where(hmask[hd], decay[(d, hd)], dec_lanes)
            dec_rows = jnp.where(row_c == hd, decay[(d, hd)], dec_rows)
            m_row = jnp.where(lane_m == hd, m_new[(d, hd)], m_row)
        kw = k.astype(F32) * (w_lanes * kscale)
        kwt = kw.T.astype(BF16)
        upd = jnp.concatenate(
            [_dot(kwt[hd * M_QK_DIM:(hd + 1) * M_QK_DIM, :], vh[(d, hd)]) for hd in range(M_HEADS)], axis=0)
        c_new = dec_rows * C + upd
        n_new = dec_lanes * n + jnp.sum(kw, axis=0, keepdims=True)
        h_all = jnp.concatenate([h[(d, hd)] for hd in range(M_HEADS)], axis=1)
        outs.append((h_all, c_new, n_new, m_row))
    return outs


def _mlstm_kernel(gb_ref, qf_ref, kf_ref, vf_ref, gf_ref, qb_ref, kb_ref, vb_ref, gbk_ref,
                  c0_ref, n0_ref, m0_ref, hf_ref, hb_ref, cf_ref, nf_ref, mf_ref,
                  c_s, n_s, m_s):
    ci = pl.program_id(1)
    T = qf_ref.shape[1]

    @pl.when(ci == 0)
    def _():
        c_s[...] = c0_ref[0]
        n_s[...] = n0_ref[0]
        m_s[...] = m0_ref[0]

    gb = gb_ref[...]
    dirs = [(q_ref[0], k_ref[0], v_ref[0], g_ref[0] + gb, c_s[d], n_s[d], m_s[d])
            for d, (q_ref, k_ref, v_ref, g_ref) in enumerate(
                ((qf_ref, kf_ref, vf_ref, gf_ref), (qb_ref, kb_ref, vb_ref, gbk_ref)))]
    outs = _mlstm_step(dirs, T)
    for d, h_ref in enumerate((hf_ref, hb_ref)):
        h_all, c_new, n_new, m_row = outs[d]
        h_ref[0] = h_all
        c_s[d] = c_new
        n_s[d] = n_new
        m_s[d] = m_row

    @pl.when(ci == pl.num_programs(1) - 1)
    def _():
        cf_ref[0] = c_s[...]
        nf_ref[0] = n_s[...]
        mf_ref[0] = m_s[...]


MLSTM_TILE = 128


def _mlstm(qm, km, vm, gm, gate_b, state):
    B, S, _ = qm.shape
    T = MLSTM_TILE
    nc = S // T
    nq = M_HEADS * M_QK_DIM
    fwd = lambda n: pl.BlockSpec((1, T, n), lambda b, c: (b, c, 0))
    bwd = lambda n: pl.BlockSpec((1, T, n), lambda b, c: (b, nc - 1 - c, 0))
    st_specs = [pl.BlockSpec((1, 2, nq, M_V_DIM), lambda b, c: (b, 0, 0, 0)),
                pl.BlockSpec((1, 2, 1, nq), lambda b, c: (b, 0, 0, 0)),
                pl.BlockSpec((1, 2, 1, LANES), lambda b, c: (b, 0, 0, 0))]
    st_shapes = [jax.ShapeDtypeStruct((B, 2, nq, M_V_DIM), F32),
                 jax.ShapeDtypeStruct((B, 2, 1, nq), F32),
                 jax.ShapeDtypeStruct((B, 2, 1, LANES), F32)]
    gb_row = jnp.pad(gate_b.reshape(1, -1).astype(F32), ((0, 0), (0, LANES - gate_b.size)))
    outs = pl.pallas_call(
        _mlstm_kernel,
        grid=(B, nc),
        in_specs=[pl.BlockSpec((1, LANES), lambda b, c: (0, 0)),
                  fwd(256), fwd(256), fwd(512), fwd(LANES),
                  bwd(256), bwd(256), bwd(512), bwd(LANES)] + st_specs,
        out_specs=[fwd(M_OUT), bwd(M_OUT)] + st_specs,
        out_shape=[jax.ShapeDtypeStruct((B, S, M_OUT), F32)] * 2 + st_shapes,
        scratch_shapes=[pltpu.VMEM((2, nq, M_V_DIM), F32),
                        pltpu.VMEM((2, 1, nq), F32),
                        pltpu.VMEM((2, 1, LANES), F32)],
        compiler_params=_params(("parallel", "arbitrary")),
        name="mlstm_scan",
    )(gb_row, qm, km, vm, gm, qm, km, vm, gm, *state)
    return outs[0], outs[1], tuple(outs[2:])


def _route(scores, sel):
    tm = scores.shape[1]
    gi8 = lax.broadcasted_iota(jnp.int32, (GROUP_SIZE, tm), 0)

    def stack_rows(rows):
        out = jnp.broadcast_to(rows[0], (len(rows), tm))
        for r, v in enumerate(rows[1:], start=1):
            out = jnp.where(gi8 == r, v, out)
        return out

    gs = []
    for g in range(N_GROUPS):
        blk = sel[g * GROUP_SIZE:(g + 1) * GROUP_SIZE, :]
        m1 = jnp.max(blk, axis=0, keepdims=True)
        first = jnp.min(jnp.where(blk == m1, gi8, GROUP_SIZE), axis=0, keepdims=True)
        m2 = jnp.max(jnp.where(gi8 == first, -jnp.inf, blk), axis=0, keepdims=True)
        gs.append(m1 + m2)
    gsc = stack_rows(gs)
    gsel = jnp.zeros((N_GROUPS, tm), F32)
    for _ in range(TOPK_GROUPS):
        mx = jnp.max(gsc, axis=0, keepdims=True)
        first = jnp.min(jnp.where(gsc == mx, gi8, N_GROUPS), axis=0, keepdims=True)
        pick = gi8 == first
        gsel = jnp.where(pick, 1.0, gsel)
        gsc = jnp.where(pick, -jnp.inf, gsc)
    cur = jnp.concatenate(
        [jnp.where(gsel[g:g + 1, :] > 0.0, sel[g * GROUP_SIZE:(g + 1) * GROUP_SIZE, :], -jnp.inf)
         for g in range(N_GROUPS)], axis=0)
    ei = lax.broadcasted_iota(jnp.int32, (N_EXPERTS, tm), 0)
    idx, wts = [], []
    for _ in range(TOP_K):
        mx = jnp.max(cur, axis=0, keepdims=True)
        first = jnp.min(jnp.where(cur == mx, ei, N_EXPERTS), axis=0, keepdims=True)
        pick = ei == first
        idx.append(first)
        wts.append(jnp.sum(jnp.where(pick, scores, 0.0), axis=0, keepdims=True))
        cur = jnp.where(pick, -jnp.inf, cur)
    tot = wts[0]
    for w in wts[1:]:
        tot = tot + w
    wts = [w / tot * ROUTED_SCALE for w in wts]
    return stack_rows(idx), stack_rows(wts)


def _merge_kernel(x_ref, mod_ref, oat_ref, hf_ref, hb_ref, om_ref, ng_ref, u_ref, up_ref, un_ref,
                  bc_ref, cw_ref, gt_ref, bgb_ref, wa_ref, wm_ref, wc_ref, wo_ref, n2_ref,
                  rw_ref, rb_ref, sgu_ref, sd_ref,
                  base_ref, h2_ref, idx_ref, wt_ref, cnt_ref):
    i = pl.program_id(1)
    tm = x_ref.shape[1]
    x = x_ref[0]
    g1 = mod_ref[0, 2:3, :]
    sh2 = mod_ref[0, 3:4, :]
    sc2 = mod_ref[0, 4:5, :]
    g2 = mod_ref[0, 5:6, :]

    ya = jnp.concatenate(
        [jnp.concatenate([oat_ref[0, qb, hh * ATT_BLOCK:(hh + 1) * ATT_BLOCK, :] for hh in range(ATT_GROUP)], axis=1)
         for qb in range(tm // ATT_BLOCK)], axis=0)

    hsum = hf_ref[0] + hb_ref[0]
    parts = []
    for hd in range(M_HEADS):
        hh_ = hsum[:, hd * M_V_DIM:(hd + 1) * M_V_DIM]
        parts.append(hh_ * lax.rsqrt(jnp.mean(hh_ * hh_, axis=-1, keepdims=True) + EPS))
    hn = jnp.concatenate(parts, axis=1) * ng_ref[...]
    ym = (_sigmoid(om_ref[0].astype(F32)) * hn).astype(BF16)

    u = u_ref[0].astype(F32)
    row = lax.broadcasted_iota(jnp.int32, (tm, 1), 0)
    has_prev = (i > 0).astype(F32)
    has_next = (i < pl.num_programs(1) - 1).astype(F32)
    prev_row = up_ref[0, BF16_SUBLANES - 1:BF16_SUBLANES, :].astype(F32) * has_prev
    next_row = un_ref[0, 0:1, :].astype(F32) * has_next
    u_m1 = jnp.where(row == 0, prev_row, pltpu.roll(u, 1, axis=0))
    u_p1 = jnp.where(row == tm - 1, next_row, pltpu.roll(u, tm - 1, axis=0))
    conv = cw_ref[0:1, :] * u_m1 + cw_ref[1:2, :] * u + cw_ref[2:3, :] * u_p1
    yc = (bc_ref[0].astype(F32) * conv).astype(BF16)

    gg = _sigmoid(gt_ref[0].astype(F32) + bgb_ref[...])
    ymix = (gg[:, 0:D_MODEL] * _dot(ya, wa_ref[...])
            + gg[:, D_MODEL:2 * D_MODEL] * _dot(ym, wm_ref[...])
            + gg[:, 2 * D_MODEL:3 * D_MODEL] * _dot(yc, wc_ref[...]))
    y = _dot(ymix.astype(BF16), wo_ref[...])
    xm = x + g1 * y

    h2f = _rms_mod(xm, n2_ref[...], sh2, sc2)
    h2 = h2f.astype(BF16)
    h2_ref[0] = h2

    logits_t = lax.dot_general(rw_ref[...], h2f, (((1,), (1,)), ((), ())),
                               preferred_element_type=F32, precision=lax.Precision.HIGHEST)
    scores = _sigmoid(logits_t)
    idx, wts = _route(scores, scores + rb_ref[...])
    idx_ref[0] = idx
    wt_ref[0] = wts
    ei = lax.broadcasted_iota(jnp.int32, (N_EXPERTS, tm), 0)
    pick = jnp.zeros((N_EXPERTS, tm), F32)
    for kk in range(TOP_K):
        pick = jnp.where(ei == idx[kk:kk + 1, :], 1.0, pick)
    cnt_ref[0] = jnp.broadcast_to(jnp.sum(pick, axis=1, keepdims=True), (N_EXPERTS, LANES)).astype(jnp.int32)

    a = _dot(h2, sgu_ref[...])
    act = (_silu(a[:, 0:SHARED_FF]) * a[:, SHARED_FF:2 * SHARED_FF]).astype(BF16)
    base_ref[0] = xm + g2 * _dot(act, sd_ref[...])


def _merge(x, mod, mod_row, oat, hf, hb, om, u, bc, gt, lw, tm):
    B, S, D = x.shape
    nt = S // tm
    hal = BF16_SUBLANES
    last_h = S // hal - 1
    tspec = lambda n: pl.BlockSpec((1, tm, n), lambda b, i: (b, i, 0))
    full = lambda a: pl.BlockSpec(a.shape, lambda b, i: (0,) * a.ndim)
    weights = [lw['mlstm_norm_g'], lw['conv_w'], lw['branch_gate_b'], lw['w_br_attn'], lw['w_br_mlstm'],
               lw['w_br_conv'], lw['w_out'], lw['norm2_g'], lw['router_wt'], lw['router_bias'],
               lw['sh_gu'], lw['sh_d']]
    in_specs = [tspec(D),
                pl.BlockSpec((1, 6, D), lambda b, i: (mod_row(b), 0, 0)),
                pl.BlockSpec((1, tm // ATT_BLOCK, ATT_GROUP * ATT_BLOCK, LANES), lambda b, i: (b, i, 0, 0)),
                tspec(M_OUT), tspec(M_OUT), tspec(M_OUT), full(weights[0]),
                tspec(CONV_WIDTH),
                pl.BlockSpec((1, hal, CONV_WIDTH), lambda b, i: (b, jnp.maximum(i * (tm // hal) - 1, 0), 0)),
                pl.BlockSpec((1, hal, CONV_WIDTH), lambda b, i: (b, jnp.minimum((i + 1) * (tm // hal), last_h), 0)),
                tspec(CONV_WIDTH), full(weights[1]), tspec(N_BRANCH * D), full(weights[2])]
    in_specs += [full(w) for w in weights[3:]]
    tr = lambda n, dt: (jax.ShapeDtypeStruct((B, n, S), dt), pl.BlockSpec((1, n, tm), lambda b, i: (b, 0, i)))
    outs = [(jax.ShapeDtypeStruct((B, S, D), F32), tspec(D)),
            (jax.ShapeDtypeStruct((B, S, D), BF16), tspec(D)),
            tr(TOP_K, jnp.int32), tr(TOP_K, F32),
            (jax.ShapeDtypeStruct((B * nt, N_EXPERTS, LANES), jnp.int32),
             pl.BlockSpec((1, N_EXPERTS, LANES), lambda b, i: (b * nt + i, 0, 0)))]
    return pl.pallas_call(
        _merge_kernel,
        grid=(B, nt),
        in_specs=in_specs,
        out_specs=[o[1] for o in outs],
        out_shape=[o[0] for o in outs],
        compiler_params=_params(("parallel", "parallel")),
        name="merge_route",
    )(x, mod, oat, hf, hb, om, weights[0], u, u, u, bc, weights[1], gt, weights[2], *weights[3:])


MOE_TILE = 256
CHUNK = BF16_SUBLANES
GROUP_CHUNKS = 16
GROUP_ROWS = GROUP_CHUNKS * CHUNK
TILE_ROWS = MOE_TILE * TOP_K + N_EXPERTS * CHUNK
TILE_CHUNKS = TILE_ROWS // CHUNK
TILE_GROUPS = TILE_ROWS // GROUP_ROWS


def _chunk_copy(src, src_chunk, dst, dst_chunk, sem):
    return pltpu.make_async_copy(src.at[pl.ds(pl.multiple_of(src_chunk * CHUNK, CHUNK), CHUNK)],
                                 dst.at[pl.ds(pl.multiple_of(dst_chunk * CHUNK, CHUNK), CHUNK)], sem)


def _dispatch_kernel(pos_ref, pad_ref, h_ref, idx_ref, wt_ref, xs_ref, pw_ref, xg_buf, zero_buf, sems, pad_sem):
    i = pl.program_id(0)
    n_steps = pl.num_programs(0)
    slot = i % 2
    tm = h_ref.shape[1]

    def wait_tile(s):
        pltpu.make_async_copy(xg_buf.at[s], xs_ref.at[pl.ds(0, TILE_ROWS)], sems.at[s]).wait()

    @pl.when(i >= 2)
    def _():
        wait_tile(slot)

    idx = idx_ref[0]
    wts = wt_ref[0]
    ei = lax.broadcasted_iota(jnp.int32, (N_EXPERTS, tm), 0)
    pick = jnp.zeros((N_EXPERTS, tm), F32)
    for kk in range(TOP_K):
        pick = jnp.where(ei == idx[kk:kk + 1, :], 1.0, pick)
    t0 = lax.broadcasted_iota(jnp.int32, (tm, tm), 0)
    t1 = lax.broadcasted_iota(jnp.int32, (tm, tm), 1)
    rank = _dot(pick.astype(BF16), jnp.where(t0 < t1, 1.0, 0.0).astype(BF16))
    n_e = jnp.sum(pick, axis=1, keepdims=True)
    n_pad = jnp.floor((n_e + (CHUNK - 1)) * (1.0 / CHUNK)) * CHUNK
    e0 = lax.broadcasted_iota(jnp.int32, (N_EXPERTS, N_EXPERTS), 0)
    e1 = lax.broadcasted_iota(jnp.int32, (N_EXPERTS, N_EXPERTS), 1)
    seg = _dot(jnp.where(e1 < e0, 1.0, 0.0).astype(BF16),
               jnp.broadcast_to(n_pad, (N_EXPERTS, tm)).astype(BF16))
    posmat = seg + rank
    riota = lax.broadcasted_iota(jnp.int32, (TILE_ROWS, tm), 0)
    pw = jnp.full((TILE_ROWS, tm), -1.0, F32)
    for kk in range(TOP_K):
        pos = jnp.sum(jnp.where(ei == idx[kk:kk + 1, :], posmat, 0.0), axis=0, keepdims=True).astype(jnp.int32)
        pw = jnp.where(riota == pos, wts[kk:kk + 1, :], pw)
    onehot = jnp.where(pw >= 0.0, 1.0, 0.0).astype(BF16)
    xg_buf[slot] = _dot(onehot, h_ref[0]).astype(BF16)
    pw_ref[...] = jnp.maximum(pw, 0.0).astype(BF16)

    def issue(c, carry):
        _chunk_copy(xg_buf.at[slot], c, xs_ref, pos_ref[i * TILE_CHUNKS + c], sems.at[slot]).start()
        return carry
    lax.fori_loop(0, TILE_CHUNKS, issue, 0)

    @pl.when(i == n_steps - 1)
    def _():
        zero_buf[...] = jnp.zeros_like(zero_buf)

        def pad_issue(c, carry):
            _chunk_copy(zero_buf, 0, xs_ref, pad_ref[c], pad_sem).start()
            return carry
        lax.fori_loop(0, N_PAD_CHUNKS, pad_issue, 0)

        def pad_wait(c, carry):
            _chunk_copy(zero_buf, 0, xs_ref, pad_ref[c], pad_sem).wait()
            return carry
        lax.fori_loop(0, N_PAD_CHUNKS, pad_wait, 0)

        @pl.when(n_steps >= 2)
        def _():
            wait_tile(1 - slot)
        wait_tile(slot)


N_PAD_CHUNKS = N_EXPERTS * (GROUP_CHUNKS - 1)
N_SPARE_CHUNKS = 2 * TILE_CHUNKS + N_PAD_CHUNKS


def _dispatch(h2, idx, wts, pos, pad_pos, n_slots):
    B, S, D = h2.shape
    tm = MOE_TILE
    nt = S // tm
    grid_spec = pltpu.PrefetchScalarGridSpec(
        num_scalar_prefetch=2,
        grid=(B * nt,),
        in_specs=[pl.BlockSpec((1, tm, D), lambda i, pos, pad: (i // nt, i % nt, 0)),
                  pl.BlockSpec((1, TOP_K, tm), lambda i, pos, pad: (i // nt, 0, i % nt)),
                  pl.BlockSpec((1, TOP_K, tm), lambda i, pos, pad: (i // nt, 0, i % nt))],
        out_specs=[pl.BlockSpec(memory_space=pl.ANY),
                   pl.BlockSpec((TILE_ROWS, tm), lambda i, pos, pad: (i, 0))],
        scratch_shapes=[pltpu.VMEM((2, TILE_ROWS, D), BF16),
                        pltpu.VMEM((CHUNK, D), BF16),
                        pltpu.SemaphoreType.DMA((2,)),
                        pltpu.SemaphoreType.DMA(())],
    )
    return pl.pallas_call(
        _dispatch_kernel,
        grid_spec=grid_spec,
        out_shape=[jax.ShapeDtypeStruct(((n_slots + N_SPARE_CHUNKS) * CHUNK, D), BF16),
                   jax.ShapeDtypeStruct((B * nt * TILE_ROWS, tm), BF16)],
        compiler_params=_params(("arbitrary",)),
        name="moe_dispatch",
    )(pos, pad_pos, h2, idx, wts)


def _moe_tables(cnt, g_max):
    nt = cnt.shape[0]
    cc = (cnt + (CHUNK - 1)) // CHUNK
    segblk = jnp.cumsum(cc, axis=1) - cc
    tile_chunks = jnp.sum(cc, axis=1)
    prior = jnp.cumsum(cc, axis=0) - cc
    ge_cnt = (jnp.sum(cc, axis=0) + (GROUP_CHUNKS - 1)) // GROUP_CHUNKS
    gbase = jnp.cumsum(ge_cnt) - ge_cnt
    n_groups = jnp.sum(ge_cnt)
    c = jnp.arange(TILE_CHUNKS, dtype=jnp.int32)
    e_of = jnp.sum(((segblk + cc)[:, None, :] <= c[None, :, None]).astype(jnp.int32), axis=-1)
    e_of = jnp.minimum(e_of, N_EXPERTS - 1)
    seg_base = gbase[None, :] * GROUP_CHUNKS + prior - segblk
    onehot = e_of[:, :, None] == jnp.arange(N_EXPERTS, dtype=jnp.int32)[None, None, :]
    pos = jnp.sum(jnp.where(onehot, seg_base[:, None, :], 0), axis=-1) + c[None, :]
    valid = c[None, :] < tile_chunks[:, None]
    n_slots = g_max * GROUP_CHUNKS
    parity = (jnp.arange(nt, dtype=jnp.int32) % 2)[:, None]
    pos_write = jnp.where(valid, pos, n_slots + parity * TILE_CHUNKS + c[None, :]).astype(jnp.int32)
    pos_read = jnp.where(valid, pos, pos[:, 0:1]).astype(jnp.int32)
    ce = jnp.sum(cc, axis=0)
    j = jnp.arange(GROUP_CHUNKS - 1, dtype=jnp.int32)
    pad_valid = j[None, :] < (ge_cnt * GROUP_CHUNKS - ce)[:, None]
    spare = n_slots + 2 * TILE_CHUNKS + jnp.arange(N_PAD_CHUNKS, dtype=jnp.int32).reshape(N_EXPERTS, -1)
    pad_pos = jnp.where(pad_valid, (gbase * GROUP_CHUNKS + ce)[:, None] + j[None, :], spare).astype(jnp.int32)
    g = jnp.arange(g_max, dtype=jnp.int32)
    grp_e = jnp.minimum(jnp.sum(((gbase + ge_cnt)[None, :] <= g[:, None]).astype(jnp.int32), axis=1),
                        N_EXPERTS - 1).astype(jnp.int32)
    return (pos_write.reshape(-1), pos_read.reshape(-1), pad_pos.reshape(-1), grp_e,
            n_groups.reshape(1).astype(jnp.int32))


def _ffn_kernel(ge_ref, na_ref, x_ref, wgu_ref, wd_ref, y_ref):
    g = pl.program_id(0)

    @pl.when(g < na_ref[0])
    def _():
        a = _dot(x_ref[...], wgu_ref[0])
        act = (_silu(a[:, 0:EXPERT_FF]) * a[:, EXPERT_FF:2 * EXPERT_FF]).astype(BF16)
        y_ref[...] = _dot(act, wd_ref[0]).astype(BF16)

    @pl.when(g >= na_ref[0])
    def _():
        y_ref[...] = jnp.zeros_like(y_ref)


def _ffn_grouped(xs, grp_e, n_groups, wgu, wd, g_max):
    D = xs.shape[1]
    live = lambda g, na: jnp.minimum(g, jnp.maximum(na[0] - 1, 0))
    grid_spec = pltpu.PrefetchScalarGridSpec(
        num_scalar_prefetch=2,
        grid=(g_max,),
        in_specs=[pl.BlockSpec((GROUP_ROWS, D), lambda g, ge, na: (live(g, na), 0)),
                  pl.BlockSpec((1, D, 2 * EXPERT_FF), lambda g, ge, na: (ge[live(g, na)], 0, 0)),
                  pl.BlockSpec((1, EXPERT_FF, D), lambda g, ge, na: (ge[live(g, na)], 0, 0))],
        out_specs=pl.BlockSpec((GROUP_ROWS, D), lambda g, ge, na: (g, 0)),
    )
    return pl.pallas_call(
        _ffn_kernel,
        grid_spec=grid_spec,
        out_shape=jax.ShapeDtypeStruct((g_max * GROUP_ROWS, D), BF16),
        compiler_params=_params(("arbitrary",)),
        name="moe_ffn",
    )(grp_e, n_groups, xs, wgu, wd)


def _combine_kernel(pos_ref, ys_ref, pw_ref, base_ref, mod_ref, o_ref, y_buf, sems):
    i = pl.program_id(0)
    n_steps = pl.num_programs(0)
    slot = i % 2

    def fetch(tile, s):
        def issue(c, carry):
            _chunk_copy(ys_ref, pos_ref[tile * TILE_CHUNKS + c], y_buf.at[s], c, sems.at[s]).start()
            return carry
        lax.fori_loop(0, TILE_CHUNKS, issue, 0)

    @pl.when(i == 0)
    def _():
        fetch(0, 0)

    @pl.when(i + 1 < n_steps)
    def _():
        fetch(i + 1, 1 - slot)

    pltpu.make_async_copy(ys_ref.at[pl.ds(0, TILE_ROWS)], y_buf.at[slot], sems.at[slot]).wait()
    routed = _dot_tn(pw_ref[...], y_buf[slot])
    o_ref[...] = base_ref[...] + mod_ref[0, 5:6, :] * routed


def _combine(ys, pos, pw, base, mod, mod_row):
    B, S, D = base.shape
    tm = MOE_TILE
    nt = S // tm
    grid_spec = pltpu.PrefetchScalarGridSpec(
        num_scalar_prefetch=1,
        grid=(B * nt,),
        in_specs=[pl.BlockSpec(memory_space=pl.ANY),
                  pl.BlockSpec((TILE_ROWS, tm), lambda i, pos: (i, 0)),
                  pl.BlockSpec((tm, D), lambda i, pos: (i, 0)),
                  pl.BlockSpec((1, 6, D), lambda i, pos: (mod_row(i // nt), 0, 0))],
        out_specs=pl.BlockSpec((tm, D), lambda i, pos: (i, 0)),
        scratch_shapes=[pltpu.VMEM((2, TILE_ROWS, D), BF16),
                        pltpu.SemaphoreType.DMA((2,))],
    )
    out = pl.pallas_call(
        _combine_kernel,
        grid_spec=grid_spec,
        out_shape=jax.ShapeDtypeStruct((B * S, D), F32),
        compiler_params=_params(("arbitrary",)),
        name="moe_combine",
    )(pos, ys, pw, base.reshape(B * S, D), mod)
    return out.reshape(B, S, D)


def _moe_sparse(h2, idx, wts, cnt, base, mod, mod_row, wgu, wd):
    B, S, D = h2.shape
    n_tiles = B * (S // MOE_TILE)
    g_max = (n_tiles * TILE_CHUNKS + N_PAD_CHUNKS + GROUP_CHUNKS - 1) // GROUP_CHUNKS
    pos_write, pos_read, pad_pos, grp_e, n_groups = _moe_tables(cnt[:, :, 0], g_max)
    xs, pw = _dispatch(h2, idx, wts, pos_write, pad_pos, g_max * GROUP_CHUNKS)
    ys = _ffn_grouped(xs, grp_e, n_groups, wgu, wd, g_max)
    return _combine(ys, pos_read, pw, base, mod, mod_row)


def _final_kernel(x_ref, g_ref, o_ref):
    x = x_ref[0]
    o_ref[0] = x * lax.rsqrt(jnp.mean(x * x, axis=-1, keepdims=True) + EPS) * g_ref[...]


def _final_norm(x, g, tm):
    B, S, D = x.shape
    spec = pl.BlockSpec((1, tm, D), lambda b, i: (b, i, 0))
    return pl.pallas_call(
        _final_kernel,
        grid=(B, S // tm),
        in_specs=[spec, pl.BlockSpec((1, D), lambda b, i: (0, 0))],
        out_specs=spec,
        out_shape=jax.ShapeDtypeStruct((B, S, D), F32),
        compiler_params=_params(("parallel", "parallel")),
        name="final_norm",
    )(x, g.reshape(1, D))


def _zero_state(batch):
    nq = M_HEADS * M_QK_DIM
    return (jnp.zeros((batch, 2, nq, M_V_DIM), F32),
            jnp.zeros((batch, 2, 1, nq), F32),
            jnp.zeros((batch, 2, 1, LANES), F32))


def kernel(x, c, ctx, c_ctx, ada_w, ada_b, norm1_g, norm2_g, w_in, attn_sink, mlstm_gate_b, mlstm_norm_g, conv_w, w_br_attn, w_br_mlstm, w_br_conv, branch_gate_b, w_out, router_w, router_bias, exp_w_gate, exp_w_up, exp_w_down, sh_w_gate, sh_w_up, sh_w_down, final_g):
    B, S, D = x.shape
    L = ctx.shape[1]
    depth = ada_w.shape[0]
    ctx_row = B

    pad_rows = (-(B + 1)) % 8
    cc = jnp.concatenate([c, c_ctx[None, :], jnp.zeros((pad_rows, D), F32)], axis=0)
    mod_all = _ada(cc, ada_w, ada_b).reshape(depth, B + 1 + pad_rows, 6, D)

    cos_t, sin_t = _rope_tables(S)
    cos_c = jnp.ones((L, LANES), F32)
    sin_c = jnp.zeros((L, LANES), F32)
    col_idx = _proj_column_index()
    att_idx = _attn_row_index()
    lat_row = lambda b: b
    ctx_mod = lambda b: ctx_row

    xc = ctx
    for l in range(depth):
        need_ctx = l < depth - 1
        mod = mod_all[l]
        w_ext = jnp.concatenate([w_in[l], jnp.zeros((D, 1), F32)], axis=1)
        w_p = jnp.take(w_ext, col_idx, axis=1).astype(BF16)
        lw = {
            'mlstm_norm_g': mlstm_norm_g[l].reshape(1, M_OUT),
            'conv_w': conv_w[l],
            'branch_gate_b': branch_gate_b[l].reshape(1, N_BRANCH * D),
            'w_br_attn': jnp.take(w_br_attn[l], att_idx, axis=0).astype(BF16),
            'w_br_mlstm': w_br_mlstm[l].astype(BF16),
            'w_br_conv': w_br_conv[l].astype(BF16),
            'w_out': w_out[l].astype(BF16),
            'norm2_g': norm2_g[l].reshape(1, D),
            'router_wt': router_w[l].T,
            'router_bias': router_bias[l].reshape(N_EXPERTS, 1),
            'sh_gu': jnp.concatenate([sh_w_gate[l], sh_w_up[l]], axis=1).astype(BF16),
            'sh_d': sh_w_down[l].astype(BF16),
        }
        wgu = jnp.concatenate([exp_w_gate[l], exp_w_up[l]], axis=2).astype(BF16)
        wd = exp_w_down[l].astype(BF16)

        pc = _in_proj(xc, mod, ctx_mod, norm1_g[l], w_p, cos_c, sin_c, tm=256)
        p = _in_proj(x, mod, lat_row, norm1_g[l], w_p, cos_t, sin_t, tm=256)
        qs_c, k_c, v_c, qm_c, km_c, vm_c, om_c, gm_c, bc_c, u_c, gt_c = pc
        qs, k, v, qm, km, vm, om, gm, bc, u, gt = p

        oat = _attention(qs, k, v, k_c, v_c, attn_sink[l], band=True)
        hf_c, hb_c, st = _mlstm(qm_c, km_c, vm_c, gm_c, mlstm_gate_b[l], _zero_state(B))
        hf, hb, _ = _mlstm(qm, km, vm, gm, mlstm_gate_b[l], st)

        base, h2, idx, wts, cnt = _merge(x, mod, lat_row, oat, hf, hb, om, u, bc, gt, lw, tm=MOE_TILE)
        x_new = _moe_sparse(h2, idx, wts, cnt, base, mod, lat_row, wgu, wd)

        if need_ctx:
            oat_c = _attention(qs_c, None, None, k_c, v_c, attn_sink[l], band=False)
            base_c, h2_c, idx_c, wts_c, cnt_c = _merge(xc, mod, ctx_mod, oat_c, hf_c, hb_c, om_c, u_c, bc_c, gt_c, lw,
                                                       tm=MOE_TILE)
            xc = _moe_sparse(h2_c, idx_c, wts_c, cnt_c, base_c, mod, ctx_mod, wgu, wd)
        x = x_new
    return _final_norm(x, final_g, tm=512)
```

```python
import functools

import numpy as np
import jax
import jax.numpy as jnp
from jax import lax
from jax.experimental import pallas as pl
from jax.experimental.pallas import tpu as pltpu

F32 = jnp.float32
BF16 = jnp.bfloat16

D_MODEL = 1024
GRID_W = 64
EPS = 1e-6
ATT_HEADS = 8
ATT_KV_HEADS = 2
ATT_HEAD_DIM = 64
ATT_GROUP = ATT_HEADS // ATT_KV_HEADS
ATT_BLOCK = 128
ATT_OUT = ATT_HEADS * ATT_HEAD_DIM
ROPE_BASE = 10000.0
M_HEADS = 4
M_QK_DIM = 64
M_V_DIM = 128
M_CHUNK = 64
M_OUT = M_HEADS * M_V_DIM
CONV_WIDTH = 512
N_BRANCH = 3
N_EXPERTS = 64
N_GROUPS = 8
GROUP_SIZE = N_EXPERTS // N_GROUPS
TOPK_GROUPS = 4
TOP_K = 8
EXPERT_FF = 256
SHARED_FF = 256
ROUTED_SCALE = 2.5

LANES = 128
BF16_SUBLANES = 16
VMEM_LIMIT = 56 * 1024 * 1024

_SEGS = (('q', 512), ('k', 128), ('v', 128), ('qm', 256), ('km', 256), ('vm', 512), ('om', 512),
         ('bc', 512), ('cc', 512), ('xc', 512), ('gt', 3072), ('gm', 128))
_OFF = {}
_o = 0
for _n, _s in _SEGS:
    _OFF[_n] = (_o, _o + _s)
    _o += _s
N_PROJ = _o
D_IN = 6928


def _proj_column_index():
    idx = []
    half = ATT_HEAD_DIM // 2
    for hh in range(ATT_GROUP):
        for g in range(ATT_KV_HEADS):
            head = g * ATT_GROUP + hh
            for par in range(2):
                idx += [head * ATT_HEAD_DIM + 2 * i + par for i in range(half)]
    for g in range(ATT_KV_HEADS):
        for par in range(2):
            idx += [512 + g * ATT_HEAD_DIM + 2 * i + par for i in range(half)]
    idx += list(range(640, 768))
    idx += list(range(768, 2304))
    idx += list(range(2320, 3856))
    idx += list(range(3856, 6928))
    idx += list(range(2304, 2320)) + [D_IN] * (LANES - 16)
    assert len(idx) == N_PROJ
    return np.asarray(idx, np.int32)


def _attn_row_index():
    idx = []
    for hh in range(ATT_GROUP):
        for g in range(ATT_KV_HEADS):
            head = g * ATT_GROUP + hh
            idx += [head * ATT_HEAD_DIM + d for d in range(ATT_HEAD_DIM)]
    return np.asarray(idx, np.int32)


def _rope_tables(seq):
    rows = seq // GRID_W
    row = jnp.repeat(jnp.arange(rows, dtype=F32), GRID_W)
    col = jnp.tile(jnp.arange(GRID_W, dtype=F32), rows)
    n_pairs = ATT_HEAD_DIM // 4
    inv_freq = ROPE_BASE ** (-jnp.arange(n_pairs, dtype=F32) / n_pairs)
    ang = jnp.concatenate([row[:, None] * inv_freq, col[:, None] * inv_freq], axis=-1)
    c, s = jnp.cos(ang), jnp.sin(ang)
    cos_t = jnp.concatenate([c, c, c, c], axis=-1)
    sin_t = jnp.concatenate([-s, s, -s, s], axis=-1)
    return cos_t, sin_t


def _dot(a, b):
    return jnp.dot(a, b, preferred_element_type=F32)


def _dot_nt(a, b):
    return lax.dot_general(a, b, (((1,), (1,)), ((), ())), preferred_element_type=F32)


def _dot_tn(a, b):
    return lax.dot_general(a, b, (((0,), (0,)), ((), ())), preferred_element_type=F32)


def _sigmoid(x):
    return 1.0 / (1.0 + jnp.exp(-x))


def _silu(x):
    return x * _sigmoid(x)


def _log_sigmoid(x):
    return jnp.minimum(x, 0.0) - jnp.log(1.0 + jnp.exp(-jnp.abs(x)))


def _rms_mod(x, g, shift, scale):
    y = x * lax.rsqrt(jnp.mean(x * x, axis=-1, keepdims=True) + EPS) * g
    return y * (1.0 + scale) + shift


def _params(sem):
    return pltpu.CompilerParams(dimension_semantics=sem, vmem_limit_bytes=VMEM_LIMIT)


def _ada_kernel(c_ref, w_ref, b_ref, o_ref):
    s = _silu(c_ref[...])
    o_ref[0] = jnp.dot(s, w_ref[0], preferred_element_type=F32,
                       precision=lax.Precision.HIGHEST) + b_ref[0]


def _ada(cc, ada_w, ada_b):
    depth, d, n = ada_w.shape
    rows = cc.shape[0]
    tn = 1536
    return pl.pallas_call(
        _ada_kernel,
        grid=(depth, n // tn),
        in_specs=[pl.BlockSpec((rows, d), lambda l, j: (0, 0)),
                  pl.BlockSpec((1, d, tn), lambda l, j: (l, 0, j)),
                  pl.BlockSpec((1, 1, tn), lambda l, j: (l, 0, j))],
        out_specs=pl.BlockSpec((1, rows, tn), lambda l, j: (l, 0, j)),
        out_shape=jax.ShapeDtypeStruct((depth, rows, n), F32),
        compiler_params=_params(("parallel", "parallel")),
        name="ada_mod",
    )(cc, ada_w, ada_b.reshape(depth, 1, n))


def _swap_halves(x):
    lane = lax.broadcasted_iota(jnp.int32, x.shape, 1)
    first = (lane % ATT_HEAD_DIM) < (ATT_HEAD_DIM // 2)
    return jnp.where(first, pltpu.roll(x, LANES - 32, axis=1), pltpu.roll(x, 32, axis=1))


def _in_kernel(x_ref, mod_ref, g_ref, w_ref, cos_ref, sin_ref,
               qs_ref, k_ref, v_ref, qm_ref, km_ref, vm_ref, om_ref, gm_ref, bc_ref, u_ref, gt_ref):
    tm = x_ref.shape[1]
    h = _rms_mod(x_ref[0], g_ref[...], mod_ref[0, 0:1, :], mod_ref[0, 1:2, :]).astype(BF16)

    def proj(name):
        lo, hi = _OFF[name]
        return _dot(h, w_ref[:, lo:hi])

    cos_t = cos_ref[...]
    sin_t = sin_ref[...]

    def rope(t):
        return t * cos_t + _swap_halves(t) * sin_t

    q = proj('q')
    scale = ATT_HEAD_DIM ** -0.5
    for hh in range(ATT_GROUP):
        r = (rope(q[:, hh * LANES:(hh + 1) * LANES]) * scale).astype(BF16)
        for qb in range(tm // ATT_BLOCK):
            qs_ref[0, qb, hh * ATT_BLOCK:(hh + 1) * ATT_BLOCK, :] = r[qb * ATT_BLOCK:(qb + 1) * ATT_BLOCK, :]
    k_ref[0] = rope(proj('k')).astype(BF16)
    v_ref[0] = proj('v').astype(BF16)
    qm_ref[0] = proj('qm').astype(BF16)
    km_ref[0] = proj('km').astype(BF16)
    vm_ref[0] = proj('vm').astype(BF16)
    om_ref[0] = proj('om').astype(BF16)
    gm_ref[0] = proj('gm')
    bc_ref[0] = proj('bc').astype(BF16)
    u_ref[0] = (proj('cc') * proj('xc')).astype(BF16)
    gt_ref[0] = proj('gt').astype(BF16)


def _in_proj(x, mod, mod_row, norm_g, w_p, cos_t, sin_t, tm):
    B, S, D = x.shape
    nb = S // ATT_BLOCK
    tok = lambda n, dt: jax.ShapeDtypeStruct((B, S, n), dt)
    tspec = lambda n: pl.BlockSpec((1, tm, n), lambda b, i: (b, i, 0))
    out_shape = (jax.ShapeDtypeStruct((B, nb, ATT_GROUP * ATT_BLOCK, LANES), BF16),
                 tok(128, BF16), tok(128, BF16), tok(256, BF16), tok(256, BF16), tok(512, BF16),
                 tok(512, BF16), tok(128, F32), tok(512, BF16), tok(512, BF16), tok(3072, BF16))
    out_specs = (pl.BlockSpec((1, tm // ATT_BLOCK, ATT_GROUP * ATT_BLOCK, LANES), lambda b, i: (b, i, 0, 0)),
                 tspec(128), tspec(128), tspec(256), tspec(256), tspec(512), tspec(512), tspec(128),
                 tspec(512), tspec(512), tspec(3072))
    return pl.pallas_call(
        _in_kernel,
        grid=(B, S // tm),
        in_specs=[pl.BlockSpec((1, tm, D), lambda b, i: (b, i, 0)),
                  pl.BlockSpec((1, 6, D), lambda b, i: (mod_row(b), 0, 0)),
                  pl.BlockSpec((1, D), lambda b, i: (0, 0)),
                  pl.BlockSpec((D, N_PROJ), lambda b, i: (0, 0), pipeline_mode=pl.Buffered(1)),
                  pl.BlockSpec((tm, LANES), lambda b, i: (i, 0)),
                  pl.BlockSpec((tm, LANES), lambda b, i: (i, 0))],
        out_specs=out_specs,
        out_shape=out_shape,
        compiler_params=_params(("parallel", "parallel")),
        name="in_proj",
    )(x, mod, norm_g.reshape(1, D), w_p, cos_t, sin_t)


def _attn_kernel(sink_ref, qs_ref, kc_ref, vc_ref, *rest, band):
    if band:
        kp_ref, kcur_ref, kn_ref, vp_ref, vcur_ref, vn_ref, o_ref = rest
    else:
        (o_ref,) = rest
    j = pl.program_id(1)
    nblk = pl.num_programs(1)
    q = qs_ref[0, 0]
    rows = q.shape[0]
    if band:
        kcat = jnp.concatenate([kc_ref[0], kp_ref[0], kcur_ref[0], kn_ref[0]], axis=0)
        vcat = jnp.concatenate([vc_ref[0], vp_ref[0], vcur_ref[0], vn_ref[0]], axis=0)
    else:
        kcat = kc_ref[0]
        vcat = vc_ref[0]
    nkeys = kcat.shape[0]
    lc = kc_ref.shape[1]
    lane = lax.broadcasted_iota(jnp.int32, (1, LANES), 1)
    t = lax.broadcasted_iota(jnp.int32, (rows, 1), 0) % ATT_BLOCK
    hh = lax.broadcasted_iota(jnp.int32, (rows, 1), 0) // ATT_BLOCK
    if band:
        c = lax.broadcasted_iota(jnp.int32, (1, nkeys), 1)
        t_prev = t + jnp.where(j > 0, 0, 2 * ATT_BLOCK)
        t_next = t - jnp.where(j < nblk - 1, 0, 2 * ATT_BLOCK)
        i_prev = c - lc
        i_next = c - (lc + 2 * ATT_BLOCK)
        valid = ((c < lc)
                 | ((c >= lc) & (c < lc + ATT_BLOCK) & (i_prev >= t_prev))
                 | ((c >= lc + ATT_BLOCK) & (c < lc + 2 * ATT_BLOCK))
                 | ((c >= lc + 2 * ATT_BLOCK) & (i_next <= t_next)))
    out = jnp.zeros((rows, LANES), F32)
    for g in range(ATT_KV_HEADS):
        lm = (lane < ATT_HEAD_DIM) if g == 0 else (lane >= ATT_HEAD_DIM)
        kz = jnp.where(lm, kcat, jnp.zeros_like(kcat))
        vz = jnp.where(lm, vcat, jnp.zeros_like(vcat))
        s = _dot_nt(q, kz)
        if band:
            s = jnp.where(valid, s, -jnp.inf)
        sink = jnp.zeros((rows, 1), F32)
        for a in range(ATT_GROUP):
            sink = jnp.where(hh == a, sink_ref[g * ATT_GROUP + a], sink)
        m = jnp.maximum(jnp.max(s, axis=-1, keepdims=True), sink)
        p = jnp.exp(s - m)
        l = jnp.sum(p, axis=-1, keepdims=True) + jnp.exp(sink - m)
        out = out + _dot(p.astype(BF16), vz) / l
    o_ref[0, 0] = out.astype(BF16)


def _attention(qs, k, v, kc, vc, sink, band):
    B, nb = qs.shape[:2]
    lc = kc.shape[1]
    last = nb - 1
    qspec = pl.BlockSpec((1, 1, ATT_GROUP * ATT_BLOCK, LANES), lambda b, j: (b, j, 0, 0))
    cspec = pl.BlockSpec((1, lc, LANES), lambda b, j: (b, 0, 0))
    in_specs = [pl.BlockSpec(memory_space=pltpu.SMEM), qspec, cspec, cspec]
    args = [sink.astype(F32), qs, kc, vc]
    if band:
        prev = pl.BlockSpec((1, ATT_BLOCK, LANES), lambda b, j: (b, jnp.maximum(j - 1, 0), 0))
        cur = pl.BlockSpec((1, ATT_BLOCK, LANES), lambda b, j: (b, j, 0))
        nxt = pl.BlockSpec((1, ATT_BLOCK, LANES), lambda b, j: (b, jnp.minimum(j + 1, last), 0))
        in_specs += [prev, cur, nxt, prev, cur, nxt]
        args += [k, k, k, v, v, v]
    return pl.pallas_call(
        functools.partial(_attn_kernel, band=band),
        grid=(B, nb),
        in_specs=in_specs,
        out_specs=qspec,
        out_shape=jax.ShapeDtypeStruct(qs.shape, BF16),
        compiler_params=_params(("parallel", "parallel")),
        name="attention_band" if band else "attention_ctx",
    )(*args)


def _mlstm_step(dirs, T):
    kscale = M_QK_DIM ** -0.5
    si = lax.broadcasted_iota(jnp.int32, (T, T), 0)
    ri = lax.broadcasted_iota(jnp.int32, (T, T), 1)
    lane_qk = lax.broadcasted_iota(jnp.int32, (1, M_HEADS * M_QK_DIM), 1) // M_QK_DIM
    lane_m = lax.broadcasted_iota(jnp.int32, (1, LANES), 1)
    row_c = lax.broadcasted_iota(jnp.int32, (M_HEADS * M_QK_DIM, 1), 0) // M_QK_DIM
    combos = [(d, hd) for d in range(2) for hd in range(M_HEADS)]

    tri, bcol, gt, bt, blast = [], [], [], [], []
    for d, (q, k, v, g, C, n, m) in enumerate(dirs):
        t = (ri <= si) if d == 0 else (ri >= si)
        tri.append(t)
        lf = _log_sigmoid(g)
        bc = jnp.dot(t.astype(F32), lf, preferred_element_type=F32, precision=lax.Precision.HIGHEST)
        bcol.append(bc)
        gt.append(g.T)
        bt.append(bc.T)
        blast.append(bc[T - 1:T, :] if d == 0 else bc[0:1, :])

    def lanes(d, hd):
        return (2 * d) * M_HEADS + hd, (2 * d + 1) * M_HEADS + hd

    b_col = {c: bcol[c[0]][:, lanes(*c)[1]:lanes(*c)[1] + 1] for c in combos}
    ig_col = {c: dirs[c[0]][3][:, lanes(*c)[0]:lanes(*c)[0] + 1] for c in combos}
    alpha = {c: gt[c[0]][lanes(*c)[0]:lanes(*c)[0] + 1, :] - bt[c[0]][lanes(*c)[1]:lanes(*c)[1] + 1, :]
             for c in combos}
    m_old = {c: dirs[c[0]][6][:, c[1]:c[1] + 1] for c in combos}
    b_last = {c: blast[c[0]][:, lanes(*c)[1]:lanes(*c)[1] + 1] for c in combos}
    hmask = {hd: lane_qk == hd for hd in range(M_HEADS)}

    a_mat = {c: jnp.where(tri[c[0]], alpha[c], -jnp.inf) for c in combos}
    a_max = {c: jnp.max(a_mat[c], axis=1, keepdims=True) for c in combos}
    a_int = {c: b_col[c] + m_old[c] for c in combos}
    m_s = {c: jnp.maximum(a_int[c], b_col[c] + a_max[c]) for c in combos}
    w_int = {c: jnp.exp(a_int[c] - m_s[c]) for c in combos}
    w_mat = {c: jnp.exp(a_mat[c] + (b_col[c] - m_s[c])) for c in combos}
    qmask = {c: jnp.where(hmask[c[1]], dirs[c[0]][0], jnp.zeros_like(dirs[c[0]][0])) for c in combos}
    s_qk = {c: w_mat[c] * (_dot_nt(qmask[c], dirs[c[0]][1]) * kscale) for c in combos}
    vh = {c: dirs[c[0]][2][:, c[1] * M_V_DIM:(c[1] + 1) * M_V_DIM] for c in combos}
    c_bf = [dirs[d][4].astype(BF16) for d in range(2)]
    num = {c: _dot(s_qk[c].astype(BF16), vh[c]) + w_int[c] * _dot(qmask[c], c_bf[c[0]]) for c in combos}
    qn_all = [dirs[d][0].astype(F32) * dirs[d][5] for d in range(2)]
    qn = {c: jnp.sum(jnp.where(hmask[c[1]], qn_all[c[0]], 0.0), axis=1, keepdims=True) for c in combos}
    den = {c: jnp.sum(s_qk[c], axis=1, keepdims=True) + w_int[c] * qn[c] for c in combos}
    h = {c: num[c] / jnp.maximum(jnp.abs(den[c]), jnp.exp(-m_s[c])) for c in combos}

    r_col = {c: b_last[c] - b_col[c] + ig_col[c] for c in combos}
    m_new = {c: jnp.maximum(b_last[c] + m_old[c], jnp.max(r_col[c], axis=0, keepdims=True)) for c in combos}
    decay = {c: jnp.exp(b_last[c] + m_old[c] - m_new[c]) for c in combos}
    w_r = {c: jnp.exp(r_col[c] - m_new[c]) for c in combos}

    outs = []
    for d, (q, k, v, g, C, n, m) in enumerate(dirs):
        w_lanes = jnp.zeros((T, M_HEADS * M_QK_DIM), F32)
        dec_lanes = jnp.zeros((1, M_HEADS * M_QK_DIM), F32)
        dec_rows = jnp.zeros((M_HEADS * M_QK_DIM, 1), F32)
        m_row = jnp.zeros((1, LANES), F32)
        for hd in range(M_HEADS):
            w_lanes = jnp.where(hmask[hd], w_r[(d, hd)], w_lanes)
            dec_lanes = jnp.where(hmask[hd], decay[(d, hd)], dec_lanes)
            dec_rows = jnp.where(row_c == hd, decay[(d, hd)], dec_rows)
            m_row = jnp.where(lane_m == hd, m_new[(d, hd)], m_row)
        kw = k.astype(F32) * (w_lanes * kscale)
        kwt = kw.T.astype(BF16)
        upd = jnp.concatenate(
            [_dot(kwt[hd * M_QK_DIM:(hd + 1) * M_QK_DIM, :], vh[(d, hd)]) for hd in range(M_HEADS)], axis=0)
        c_new = dec_rows * C + upd
        n_new = dec_lanes * n + jnp.sum(kw, axis=0, keepdims=True)
        h_all = jnp.concatenate([h[(d, hd)] for hd in range(M_HEADS)], axis=1)
        outs.append((h_all, c_new, n_new, m_row))
    return outs


def _mlstm_kernel(gb_ref, qf_ref, kf_ref, vf_ref, gf_ref, qb_ref, kb_ref, vb_ref, gbk_ref,
                  c0_ref, n0_ref, m0_ref, hf_ref, hb_ref, cf_ref, nf_ref, mf_ref,
                  c_s, n_s, m_s):
    ci = pl.program_id(1)
    T = qf_ref.shape[1]

    @pl.when(ci == 0)
    def _():
        c_s[...] = c0_ref[0]
        n_s[...] = n0_ref[0]
        m_s[...] = m0_ref[0]

    gb = gb_ref[...]
    dirs = [(q_ref[0], k_ref[0], v_ref[0], g_ref[0] + gb, c_s[d], n_s[d], m_s[d])
            for d, (q_ref, k_ref, v_ref, g_ref) in enumerate(
                ((qf_ref, kf_ref, vf_ref, gf_ref), (qb_ref, kb_ref, vb_ref, gbk_ref)))]
    outs = _mlstm_step(dirs, T)
    for d, h_ref in enumerate((hf_ref, hb_ref)):
        h_all, c_new, n_new, m_row = outs[d]
        h_ref[0] = h_all
        c_s[d] = c_new
        n_s[d] = n_new
        m_s[d] = m_row

    @pl.when(ci == pl.num_programs(1) - 1)
    def _():
        cf_ref[0] = c_s[...]
        nf_ref[0] = n_s[...]
        mf_ref[0] = m_s[...]


MLSTM_TILE = 128


def _mlstm(qm, km, vm, gm, gate_b, state):
    B, S, _ = qm.shape
    T = MLSTM_TILE
    nc = S // T
    nq = M_HEADS * M_QK_DIM
    fwd = lambda n: pl.BlockSpec((1, T, n), lambda b, c: (b, c, 0))
    bwd = lambda n: pl.BlockSpec((1, T, n), lambda b, c: (b, nc - 1 - c, 0))
    st_specs = [pl.BlockSpec((1, 2, nq, M_V_DIM), lambda b, c: (b, 0, 0, 0)),
                pl.BlockSpec((1, 2, 1, nq), lambda b, c: (b, 0, 0, 0)),
                pl.BlockSpec((1, 2, 1, LANES), lambda b, c: (b, 0, 0, 0))]
    st_shapes = [jax.ShapeDtypeStruct((B, 2, nq, M_V_DIM), F32),
                 jax.ShapeDtypeStruct((B, 2, 1, nq), F32),
                 jax.ShapeDtypeStruct((B, 2, 1, LANES), F32)]
    gb_row = jnp.pad(gate_b.reshape(1, -1).astype(F32), ((0, 0), (0, LANES - gate_b.size)))
    outs = pl.pallas_call(
        _mlstm_kernel,
        grid=(B, nc),
        in_specs=[pl.BlockSpec((1, LANES), lambda b, c: (0, 0)),
                  fwd(256), fwd(256), fwd(512), fwd(LANES),
                  bwd(256), bwd(256), bwd(512), bwd(LANES)] + st_specs,
        out_specs=[fwd(M_OUT), bwd(M_OUT)] + st_specs,
        out_shape=[jax.ShapeDtypeStruct((B, S, M_OUT), F32)] * 2 + st_shapes,
        scratch_shapes=[pltpu.VMEM((2, nq, M_V_DIM), F32),
                        pltpu.VMEM((2, 1, nq), F32),
                        pltpu.VMEM((2, 1, LANES), F32)],
        compiler_params=_params(("parallel", "arbitrary")),
        name="mlstm_scan",
    )(gb_row, qm, km, vm, gm, qm, km, vm, gm, *state)
    return outs[0], outs[1], tuple(outs[2:])


def _route(scores, sel):
    tm = scores.shape[1]
    gi8 = lax.broadcasted_iota(jnp.int32, (GROUP_SIZE, tm), 0)

    def stack_rows(rows):
        out = jnp.broadcast_to(rows[0], (len(rows), tm))
        for r, v in enumerate(rows[1:], start=1):
            out = jnp.where(gi8 == r, v, out)
        return out

    gs = []
    for g in range(N_GROUPS):
        blk = sel[g * GROUP_SIZE:(g + 1) * GROUP_SIZE, :]
        m1 = jnp.max(blk, axis=0, keepdims=True)
        first = jnp.min(jnp.where(blk == m1, gi8, GROUP_SIZE), axis=0, keepdims=True)
        m2 = jnp.max(jnp.where(gi8 == first, -jnp.inf, blk), axis=0, keepdims=True)
        gs.append(m1 + m2)
    gsc = stack_rows(gs)
    gsel = jnp.zeros((N_GROUPS, tm), F32)
    for _ in range(TOPK_GROUPS):
        mx = jnp.max(gsc, axis=0, keepdims=True)
        first = jnp.min(jnp.where(gsc == mx, gi8, N_GROUPS), axis=0, keepdims=True)
        pick = gi8 == first
        gsel = jnp.where(pick, 1.0, gsel)
        gsc = jnp.where(pick, -jnp.inf, gsc)
    cur = jnp.concatenate(
        [jnp.where(gsel[g:g + 1, :] > 0.0, sel[g * GROUP_SIZE:(g + 1) * GROUP_SIZE, :], -jnp.inf)
         for g in range(N_GROUPS)], axis=0)
    ei = lax.broadcasted_iota(jnp.int32, (N_EXPERTS, tm), 0)
    idx, wts = [], []
    for _ in range(TOP_K):
        mx = jnp.max(cur, axis=0, keepdims=True)
        first = jnp.min(jnp.where(cur == mx, ei, N_EXPERTS), axis=0, keepdims=True)
        pick = ei == first
        idx.append(first)
        wts.append(jnp.sum(jnp.where(pick, scores, 0.0), axis=0, keepdims=True))
        cur = jnp.where(pick, -jnp.inf, cur)
    tot = wts[0]
    for w in wts[1:]:
        tot = tot + w
    wts = [w / tot * ROUTED_SCALE for w in wts]
    return stack_rows(idx), stack_rows(wts)


def _merge_kernel(x_ref, mod_ref, oat_ref, hf_ref, hb_ref, om_ref, ng_ref, u_ref, up_ref, un_ref,
                  bc_ref, cw_ref, gt_ref, bgb_ref, wa_ref, wm_ref, wc_ref, wo_ref, n2_ref,
                  rw_ref, rb_ref, sgu_ref, sd_ref,
                  base_ref, h2_ref, idx_ref, wt_ref, cnt_ref):
    i = pl.program_id(1)
    tm = x_ref.shape[1]
    x = x_ref[0]
    g1 = mod_ref[0, 2:3, :]
    sh2 = mod_ref[0, 3:4, :]
    sc2 = mod_ref[0, 4:5, :]
    g2 = mod_ref[0, 5:6, :]

    ya = jnp.concatenate(
        [jnp.concatenate([oat_ref[0, qb, hh * ATT_BLOCK:(hh + 1) * ATT_BLOCK, :] for hh in range(ATT_GROUP)], axis=1)
         for qb in range(tm // ATT_BLOCK)], axis=0)

    hsum = hf_ref[0] + hb_ref[0]
    parts = []
    for hd in range(M_HEADS):
        hh_ = hsum[:, hd * M_V_DIM:(hd + 1) * M_V_DIM]
        parts.append(hh_ * lax.rsqrt(jnp.mean(hh_ * hh_, axis=-1, keepdims=True) + EPS))
    hn = jnp.concatenate(parts, axis=1) * ng_ref[...]
    ym = (_sigmoid(om_ref[0].astype(F32)) * hn).astype(BF16)

    u = u_ref[0].astype(F32)
    row = lax.broadcasted_iota(jnp.int32, (tm, 1), 0)
    has_prev = (i > 0).astype(F32)
    has_next = (i < pl.num_programs(1) - 1).astype(F32)
    prev_row = up_ref[0, BF16_SUBLANES - 1:BF16_SUBLANES, :].astype(F32) * has_prev
    next_row = un_ref[0, 0:1, :].astype(F32) * has_next
    u_m1 = jnp.where(row == 0, prev_row, pltpu.roll(u, 1, axis=0))
    u_p1 = jnp.where(row == tm - 1, next_row, pltpu.roll(u, tm - 1, axis=0))
    conv = cw_ref[0:1, :] * u_m1 + cw_ref[1:2, :] * u + cw_ref[2:3, :] * u_p1
    yc = (bc_ref[0].astype(F32) * conv).astype(BF16)

    gg = _sigmoid(gt_ref[0].astype(F32) + bgb_ref[...])
    ymix = (gg[:, 0:D_MODEL] * _dot(ya, wa_ref[...])
            + gg[:, D_MODEL:2 * D_MODEL] * _dot(ym, wm_ref[...])
            + gg[:, 2 * D_MODEL:3 * D_MODEL] * _dot(yc, wc_ref[...]))
    y = _dot(ymix.astype(BF16), wo_ref[...])
    xm = x + g1 * y

    h2f = _rms_mod(xm, n2_ref[...], sh2, sc2)
    h2 = h2f.astype(BF16)
    h2_ref[0] = h2

    logits_t = lax.dot_general(rw_ref[...], h2f, (((1,), (1,)), ((), ())),
                               preferred_element_type=F32, precision=lax.Precision.HIGHEST)
    scores = _sigmoid(logits_t)
    idx, wts = _route(scores, scores + rb_ref[...])
    idx_ref[0] = idx
    wt_ref[0] = wts
    ei = lax.broadcasted_iota(jnp.int32, (N_EXPERTS, tm), 0)
    pick = jnp.zeros((N_EXPERTS, tm), F32)
    for kk in range(TOP_K):
        pick = jnp.where(ei == idx[kk:kk + 1, :], 1.0, pick)
    cnt_ref[0] = jnp.broadcast_to(jnp.sum(pick, axis=1, keepdims=True), (N_EXPERTS, LANES)).astype(jnp.int32)

    a = _dot(h2, sgu_ref[...])
    act = (_silu(a[:, 0:SHARED_FF]) * a[:, SHARED_FF:2 * SHARED_FF]).astype(BF16)
    base_ref[0] = xm + g2 * _dot(act, sd_ref[...])


def _merge(x, mod, mod_row, oat, hf, hb, om, u, bc, gt, lw, tm):
    B, S, D = x.shape
    nt = S // tm
    hal = BF16_SUBLANES
    last_h = S // hal - 1
    tspec = lambda n: pl.BlockSpec((1, tm, n), lambda b, i: (b, i, 0))
    full = lambda a: pl.BlockSpec(a.shape, lambda b, i: (0,) * a.ndim)
    weights = [lw['mlstm_norm_g'], lw['conv_w'], lw['branch_gate_b'], lw['w_br_attn'], lw['w_br_mlstm'],
               lw['w_br_conv'], lw['w_out'], lw['norm2_g'], lw['router_wt'], lw['router_bias'],
               lw['sh_gu'], lw['sh_d']]
    in_specs = [tspec(D),
                pl.BlockSpec((1, 6, D), lambda b, i: (mod_row(b), 0, 0)),
                pl.BlockSpec((1, tm // ATT_BLOCK, ATT_GROUP * ATT_BLOCK, LANES), lambda b, i: (b, i, 0, 0)),
                tspec(M_OUT), tspec(M_OUT), tspec(M_OUT), full(weights[0]),
                tspec(CONV_WIDTH),
                pl.BlockSpec((1, hal, CONV_WIDTH), lambda b, i: (b, jnp.maximum(i * (tm // hal) - 1, 0), 0)),
                pl.BlockSpec((1, hal, CONV_WIDTH), lambda b, i: (b, jnp.minimum((i + 1) * (tm // hal), last_h), 0)),
                tspec(CONV_WIDTH), full(weights[1]), tspec(N_BRANCH * D), full(weights[2])]
    in_specs += [full(w) for w in weights[3:]]
    tr = lambda n, dt: (jax.ShapeDtypeStruct((B, n, S), dt), pl.BlockSpec((1, n, tm), lambda b, i: (b, 0, i)))
    outs = [(jax.ShapeDtypeStruct((B, S, D), F32), tspec(D)),
            (jax.ShapeDtypeStruct((B, S, D), BF16), tspec(D)),
            tr(TOP_K, jnp.int32), tr(TOP_K, F32),
            (jax.ShapeDtypeStruct((B * nt, N_EXPERTS, LANES), jnp.int32),
             pl.BlockSpec((1, N_EXPERTS, LANES), lambda b, i: (b * nt + i, 0, 0)))]
    return pl.pallas_call(
        _merge_kernel,
        grid=(B, nt),
        in_specs=in_specs,
        out_specs=[o[1] for o in outs],
        out_shape=[o[0] for o in outs],
        compiler_params=_params(("parallel", "parallel")),
        name="merge_route",
    )(x, mod, oat, hf, hb, om, weights[0], u, u, u, bc, weights[1], gt, weights[2], *weights[3:])


MOE_TILE = 256
CHUNK = BF16_SUBLANES
GROUP_CHUNKS = 64
GROUP_ROWS = GROUP_CHUNKS * CHUNK
TILE_ROWS = MOE_TILE * TOP_K + N_EXPERTS * CHUNK
TILE_CHUNKS = TILE_ROWS // CHUNK
N_PAD_CHUNKS = N_EXPERTS * (GROUP_CHUNKS - 1)
N_SPARE_CHUNKS = 2 * TILE_CHUNKS
DMA_UNROLL = 8


def _chunk_copy(src, src_chunk, dst, dst_chunk, sem):
    return pltpu.make_async_copy(src.at[pl.ds(pl.multiple_of(src_chunk * CHUNK, CHUNK), CHUNK)],
                                 dst.at[pl.ds(pl.multiple_of(dst_chunk * CHUNK, CHUNK), CHUNK)], sem)


def _dispatch_kernel(pos_ref, pad_ref, h_ref, idx_ref, wt_ref, xs_ref, pw_ref, xg_buf, zero_buf, sems, pad_sem):
    i = pl.program_id(0)
    n_steps = pl.num_programs(0)
    slot = i % 2
    tm = h_ref.shape[1]

    def wait_tile(s):
        pltpu.make_async_copy(xg_buf.at[s], xs_ref.at[pl.ds(0, TILE_ROWS)], sems.at[s]).wait()

    @pl.when(i >= 2)
    def _():
        wait_tile(slot)

    idx = idx_ref[0]
    wts = wt_ref[0]
    ei = lax.broadcasted_iota(jnp.int32, (N_EXPERTS, tm), 0)
    pick = jnp.zeros((N_EXPERTS, tm), F32)
    wmat = jnp.zeros((N_EXPERTS, tm), F32)
    for kk in range(TOP_K):
        chosen = ei == idx[kk:kk + 1, :]
        pick = jnp.where(chosen, 1.0, pick)
        wmat = jnp.where(chosen, wts[kk:kk + 1, :], wmat)
    t0 = lax.broadcasted_iota(jnp.int32, (tm, tm), 0)
    t1 = lax.broadcasted_iota(jnp.int32, (tm, tm), 1)
    rank = _dot(pick.astype(BF16), jnp.where(t0 < t1, 1.0, 0.0).astype(BF16))
    n_e = jnp.sum(pick, axis=1, keepdims=True)
    n_pad = jnp.floor((n_e + (CHUNK - 1)) * (1.0 / CHUNK)) * CHUNK
    e0 = lax.broadcasted_iota(jnp.int32, (N_EXPERTS, N_EXPERTS), 0)
    e1 = lax.broadcasted_iota(jnp.int32, (N_EXPERTS, N_EXPERTS), 1)
    seg = _dot(jnp.where(e1 < e0, 1.0, 0.0).astype(BF16),
               jnp.broadcast_to(n_pad, (N_EXPERTS, tm)).astype(BF16))
    posmat = seg + rank
    chunk_of = jnp.floor(posmat * (1.0 / CHUNK))
    offs_of = posmat - chunk_of * CHUNK
    chunk_row = jnp.where(pick > 0.0, chunk_of * CHUNK, -float(CHUNK))
    eye = e0 == e1
    to_row = lambda col: jnp.sum(jnp.where(eye, col, 0.0), axis=0, keepdims=True)
    seg_row = to_row(seg[:, 0:1])
    end_row = to_row(seg[:, 0:1] + n_pad)
    seg_row2 = jnp.concatenate([seg_row, seg_row], axis=1)
    end_row2 = jnp.concatenate([end_row, end_row], axis=1)
    r128 = lax.broadcasted_iota(jnp.int32, (TILE_ROWS, 2 * N_EXPERTS), 0).astype(F32)
    own2 = jnp.where(r128 >= seg_row2, jnp.where(r128 < end_row2, 1.0, 0.0), 0.0).astype(BF16)
    row_of = _dot(own2, jnp.concatenate([chunk_row, offs_of], axis=0).astype(BF16))
    w_of = _dot(own2[:, 0:N_EXPERTS], wmat.astype(BF16))
    riota = lax.broadcasted_iota(jnp.int32, (TILE_ROWS, tm), 0).astype(F32)
    hit = row_of == riota
    xg_buf[slot] = _dot(jnp.where(hit, 1.0, 0.0).astype(BF16), h_ref[0]).astype(BF16)
    pw_ref[...] = jnp.where(hit, w_of, 0.0).astype(BF16)

    def issue(c8, carry):
        for u in range(DMA_UNROLL):
            c = c8 * DMA_UNROLL + u
            _chunk_copy(xg_buf.at[slot], c, xs_ref, pos_ref[i * TILE_CHUNKS + c], sems.at[slot]).start()
        return carry
    lax.fori_loop(0, TILE_CHUNKS // DMA_UNROLL, issue, 0)

    @pl.when(i == n_steps - 1)
    def _():
        zero_buf[...] = jnp.zeros_like(zero_buf)

        def pad_expert(e, carry):
            def pad_issue(c, inner):
                _chunk_copy(zero_buf, 0, xs_ref, pad_ref[e] + c, pad_sem).start()
                return inner
            lax.fori_loop(0, pad_ref[N_EXPERTS + e], pad_issue, 0)

            def pad_wait(c, inner):
                _chunk_copy(zero_buf, 0, xs_ref, pad_ref[e] + c, pad_sem).wait()
                return inner
            lax.fori_loop(0, pad_ref[N_EXPERTS + e], pad_wait, 0)
            return carry
        lax.fori_loop(0, N_EXPERTS, pad_expert, 0)

        @pl.when(n_steps >= 2)
        def _():
            wait_tile(1 - slot)
        wait_tile(slot)


def _dispatch(h2, idx, wts, pos, pad_pos, n_slots):
    B, S, D = h2.shape
    tm = MOE_TILE
    nt = S // tm
    grid_spec = pltpu.PrefetchScalarGridSpec(
        num_scalar_prefetch=2,
        grid=(B * nt,),
        in_specs=[pl.BlockSpec((1, tm, D), lambda i, pos, pad: (i // nt, i % nt, 0)),
                  pl.BlockSpec((1, TOP_K, tm), lambda i, pos, pad: (i // nt, 0, i % nt)),
                  pl.BlockSpec((1, TOP_K, tm), lambda i, pos, pad: (i // nt, 0, i % nt))],
        out_specs=[pl.BlockSpec(memory_space=pl.ANY),
                   pl.BlockSpec((TILE_ROWS, tm), lambda i, pos, pad: (i, 0))],
        scratch_shapes=[pltpu.VMEM((2, TILE_ROWS, D), BF16),
                        pltpu.VMEM((CHUNK, D), BF16),
                        pltpu.SemaphoreType.DMA((2,)),
                        pltpu.SemaphoreType.DMA(())],
    )
    return pl.pallas_call(
        _dispatch_kernel,
        grid_spec=grid_spec,
        out_shape=[jax.ShapeDtypeStruct(((n_slots + N_SPARE_CHUNKS) * CHUNK, D), BF16),
                   jax.ShapeDtypeStruct((B * nt * TILE_ROWS, tm), BF16)],
        compiler_params=_params(("arbitrary",)),
        name="moe_dispatch",
    )(pos, pad_pos, h2, idx, wts)


def _moe_tables(cnt, g_max):
    nt = cnt.shape[0]
    cc = (cnt + (CHUNK - 1)) // CHUNK
    segblk = jnp.cumsum(cc, axis=1) - cc
    tile_chunks = jnp.sum(cc, axis=1)
    prior = jnp.cumsum(cc, axis=0) - cc
    ge_cnt = (jnp.sum(cc, axis=0) + (GROUP_CHUNKS - 1)) // GROUP_CHUNKS
    gbase = jnp.cumsum(ge_cnt) - ge_cnt
    n_groups = jnp.sum(ge_cnt)
    c = jnp.arange(TILE_CHUNKS, dtype=jnp.int32)
    e_of = jnp.sum(((segblk + cc)[:, None, :] <= c[None, :, None]).astype(jnp.int32), axis=-1)
    e_of = jnp.minimum(e_of, N_EXPERTS - 1)
    seg_base = gbase[None, :] * GROUP_CHUNKS + prior - segblk
    onehot = e_of[:, :, None] == jnp.arange(N_EXPERTS, dtype=jnp.int32)[None, None, :]
    pos = jnp.sum(jnp.where(onehot, seg_base[:, None, :], 0), axis=-1) + c[None, :]
    valid = c[None, :] < tile_chunks[:, None]
    n_slots = g_max * GROUP_CHUNKS
    parity = (jnp.arange(nt, dtype=jnp.int32) % 2)[:, None]
    pos_write = jnp.where(valid, pos, n_slots + parity * TILE_CHUNKS + c[None, :]).astype(jnp.int32)
    pos_read = jnp.where(valid, pos, pos[:, 0:1]).astype(jnp.int32)
    ce = jnp.sum(cc, axis=0)
    pad_pos = jnp.concatenate([gbase * GROUP_CHUNKS + ce, ge_cnt * GROUP_CHUNKS - ce]).astype(jnp.int32)
    g = jnp.arange(g_max, dtype=jnp.int32)
    grp_e = jnp.minimum(jnp.sum(((gbase + ge_cnt)[None, :] <= g[:, None]).astype(jnp.int32), axis=1),
                        N_EXPERTS - 1).astype(jnp.int32)
    return (pos_write.reshape(-1), pos_read.reshape(-1), pad_pos.reshape(-1), grp_e,
            n_groups.reshape(1).astype(jnp.int32))


def _ffn_kernel(ge_ref, na_ref, x_ref, wgu_ref, wd_ref, y_ref):
    g = pl.program_id(0)

    @pl.when(g < na_ref[0])
    def _():
        a = _dot(x_ref[...], wgu_ref[0])
        act = (_silu(a[:, 0:EXPERT_FF]) * a[:, EXPERT_FF:2 * EXPERT_FF]).astype(BF16)
        y_ref[...] = _dot(act, wd_ref[0]).astype(BF16)

    @pl.when(g >= na_ref[0])
    def _():
        y_ref[...] = jnp.zeros_like(y_ref)


def _ffn_grouped(xs, grp_e, n_groups, wgu, wd, g_max):
    D = xs.shape[1]
    live = lambda g, na: jnp.minimum(g, jnp.maximum(na[0] - 1, 0))
    grid_spec = pltpu.PrefetchScalarGridSpec(
        num_scalar_prefetch=2,
        grid=(g_max,),
        in_specs=[pl.BlockSpec((GROUP_ROWS, D), lambda g, ge, na: (live(g, na), 0)),
                  pl.BlockSpec((1, D, 2 * EXPERT_FF), lambda g, ge, na: (ge[live(g, na)], 0, 0)),
                  pl.BlockSpec((1, EXPERT_FF, D), lambda g, ge, na: (ge[live(g, na)], 0, 0))],
        out_specs=pl.BlockSpec((GROUP_ROWS, D), lambda g, ge, na: (g, 0)),
    )
    return pl.pallas_call(
        _ffn_kernel,
        grid_spec=grid_spec,
        out_shape=jax.ShapeDtypeStruct((g_max * GROUP_ROWS, D), BF16),
        compiler_params=_params(("arbitrary",)),
        name="moe_ffn",
    )(grp_e, n_groups, xs, wgu, wd)


def _combine_kernel(pos_ref, ys_ref, pw_ref, base_ref, mod_ref, o_ref, y_buf, sems):
    i = pl.program_id(0)
    n_steps = pl.num_programs(0)
    slot = i % 2

    def fetch(tile, s):
        def issue(c8, carry):
            for u in range(DMA_UNROLL):
                c = c8 * DMA_UNROLL + u
                _chunk_copy(ys_ref, pos_ref[tile * TILE_CHUNKS + c], y_buf.at[s], c, sems.at[s]).start()
            return carry
        lax.fori_loop(0, TILE_CHUNKS // DMA_UNROLL, issue, 0)

    @pl.when(i == 0)
    def _():
        fetch(0, 0)

    @pl.when(i + 1 < n_steps)
    def _():
        fetch(i + 1, 1 - slot)

    pltpu.make_async_copy(ys_ref.at[pl.ds(0, TILE_ROWS)], y_buf.at[slot], sems.at[slot]).wait()
    routed = _dot_tn(pw_ref[...], y_buf[slot])
    o_ref[...] = base_ref[...] + mod_ref[0, 5:6, :] * routed


def _combine(ys, pos, pw, base, mod, mod_row):
    B, S, D = base.shape
    tm = MOE_TILE
    nt = S // tm
    grid_spec = pltpu.PrefetchScalarGridSpec(
        num_scalar_prefetch=1,
        grid=(B * nt,),
        in_specs=[pl.BlockSpec(memory_space=pl.ANY),
                  pl.BlockSpec((TILE_ROWS, tm), lambda i, pos: (i, 0)),
                  pl.BlockSpec((tm, D), lambda i, pos: (i, 0)),
                  pl.BlockSpec((1, 6, D), lambda i, pos: (mod_row(i // nt), 0, 0))],
        out_specs=pl.BlockSpec((tm, D), lambda i, pos: (i, 0)),
        scratch_shapes=[pltpu.VMEM((2, TILE_ROWS, D), BF16),
                        pltpu.SemaphoreType.DMA((2,))],
    )
    out = pl.pallas_call(
        _combine_kernel,
        grid_spec=grid_spec,
        out_shape=jax.ShapeDtypeStruct((B * S, D), F32),
        compiler_params=_params(("arbitrary",)),
        name="moe_combine",
    )(pos, ys, pw, base.reshape(B * S, D), mod)
    return out.reshape(B, S, D)


def _moe_sparse(h2, idx, wts, cnt, base, mod, mod_row, wgu, wd):
    B, S, D = h2.shape
    n_tiles = B * (S // MOE_TILE)
    g_max = (n_tiles * TILE_CHUNKS + N_PAD_CHUNKS + GROUP_CHUNKS - 1) // GROUP_CHUNKS
    pos_write, pos_read, pad_pos, grp_e, n_groups = _moe_tables(cnt[:, :, 0], g_max)
    xs, pw = _dispatch(h2, idx, wts, pos_write, pad_pos, g_max * GROUP_CHUNKS)
    ys = _ffn_grouped(xs, grp_e, n_groups, wgu, wd, g_max)
    return _combine(ys, pos_read, pw, base, mod, mod_row)


def _final_kernel(x_ref, g_ref, o_ref):
    x = x_ref[0]
    o_ref[0] = x * lax.rsqrt(jnp.mean(x * x, axis=-1, keepdims=True) + EPS) * g_ref[...]


def _final_norm(x, g, tm):
    B, S, D = x.shape
    spec = pl.BlockSpec((1, tm, D), lambda b, i: (b, i, 0))
    return pl.pallas_call(
        _final_kernel,
        grid=(B, S // tm),
        in_specs=[spec, pl.BlockSpec((1, D), lambda b, i: (0, 0))],
        out_specs=spec,
        out_shape=jax.ShapeDtypeStruct((B, S, D), F32),
        compiler_params=_params(("parallel", "parallel")),
        name="final_norm",
    )(x, g.reshape(1, D))


def _zero_state(batch):
    nq = M_HEADS * M_QK_DIM
    return (jnp.zeros((batch, 2, nq, M_V_DIM), F32),
            jnp.zeros((batch, 2, 1, nq), F32),
            jnp.zeros((batch, 2, 1, LANES), F32))


def kernel(x, c, ctx, c_ctx, ada_w, ada_b, norm1_g, norm2_g, w_in, attn_sink, mlstm_gate_b, mlstm_norm_g, conv_w, w_br_attn, w_br_mlstm, w_br_conv, branch_gate_b, w_out, router_w, router_bias, exp_w_gate, exp_w_up, exp_w_down, sh_w_gate, sh_w_up, sh_w_down, final_g):
    B, S, D = x.shape
    L = ctx.shape[1]
    depth = ada_w.shape[0]
    ctx_row = B

    pad_rows = (-(B + 1)) % 8
    cc = jnp.concatenate([c, c_ctx[None, :], jnp.zeros((pad_rows, D), F32)], axis=0)
    mod_all = _ada(cc, ada_w, ada_b).reshape(depth, B + 1 + pad_rows, 6, D)

    cos_t, sin_t = _rope_tables(S)
    cos_c = jnp.ones((L, LANES), F32)
    sin_c = jnp.zeros((L, LANES), F32)
    col_idx = _proj_column_index()
    att_idx = _attn_row_index()
    lat_row = lambda b: b
    ctx_mod = lambda b: ctx_row

    xc = ctx
    for l in range(depth):
        need_ctx = l < depth - 1
        mod = mod_all[l]
        w_ext = jnp.concatenate([w_in[l], jnp.zeros((D, 1), F32)], axis=1)
        w_p = jnp.take(w_ext, col_idx, axis=1).astype(BF16)
        lw = {
            'mlstm_norm_g': mlstm_norm_g[l].reshape(1, M_OUT),
            'conv_w': conv_w[l],
            'branch_gate_b': branch_gate_b[l].reshape(1, N_BRANCH * D),
            'w_br_attn': jnp.take(w_br_attn[l], att_idx, axis=0).astype(BF16),
            'w_br_mlstm': w_br_mlstm[l].astype(BF16),
            'w_br_conv': w_br_conv[l].astype(BF16),
            'w_out': w_out[l].astype(BF16),
            'norm2_g': norm2_g[l].reshape(1, D),
            'router_wt': router_w[l].T,
            'router_bias': router_bias[l].reshape(N_EXPERTS, 1),
            'sh_gu': jnp.concatenate([sh_w_gate[l], sh_w_up[l]], axis=1).astype(BF16),
            'sh_d': sh_w_down[l].astype(BF16),
        }
        wgu = jnp.concatenate([exp_w_gate[l], exp_w_up[l]], axis=2).astype(BF16)
        wd = exp_w_down[l].astype(BF16)

        pc = _in_proj(xc, mod, ctx_mod, norm1_g[l], w_p, cos_c, sin_c, tm=256)
        p = _in_proj(x, mod, lat_row, norm1_g[l], w_p, cos_t, sin_t, tm=256)
        qs_c, k_c, v_c, qm_c, km_c, vm_c, om_c, gm_c, bc_c, u_c, gt_c = pc
        qs, k, v, qm, km, vm, om, gm, bc, u, gt = p

        oat = _attention(qs, k, v, k_c, v_c, attn_sink[l], band=True)
        hf_c, hb_c, st = _mlstm(qm_c, km_c, vm_c, gm_c, mlstm_gate_b[l], _zero_state(B))
        hf, hb, _ = _mlstm(qm, km, vm, gm, mlstm_gate_b[l], st)

        base, h2, idx, wts, cnt = _merge(x, mod, lat_row, oat, hf, hb, om, u, bc, gt, lw, tm=MOE_TILE)
        x_new = _moe_sparse(h2, idx, wts, cnt, base, mod, lat_row, wgu, wd)

        if need_ctx:
            oat_c = _attention(qs_c, None, None, k_c, v_c, attn_sink[l], band=False)
            base_c, h2_c, idx_c, wts_c, cnt_c = _merge(xc, mod, ctx_mod, oat_c, hf_c, hb_c, om_c, u_c, bc_c, gt_c, lw,
                                                       tm=MOE_TILE)
            xc = _moe_sparse(h2_c, idx_c, wts_c, cnt_c, base_c, mod, ctx_mod, wgu, wd)
        x = x_new
    return _final_norm(x, final_g, tm=512)
```

```python
import functools

import numpy as np
import jax
import jax.numpy as jnp
from jax import lax
from jax.experimental import pallas as pl
from jax.experimental.pallas import tpu as pltpu

F32 = jnp.float32
BF16 = jnp.bfloat16

D_MODEL = 1024
GRID_W = 64
EPS = 1e-6
ATT_HEADS = 8
ATT_KV_HEADS = 2
ATT_HEAD_DIM = 64
ATT_GROUP = ATT_HEADS // ATT_KV_HEADS
ATT_BLOCK = 128
ATT_OUT = ATT_HEADS * ATT_HEAD_DIM
ROPE_BASE = 10000.0
M_HEADS = 4
M_QK_DIM = 64
M_V_DIM = 128
M_CHUNK = 64
M_OUT = M_HEADS * M_V_DIM
CONV_WIDTH = 512
N_BRANCH = 3
N_EXPERTS = 64
N_GROUPS = 8
GROUP_SIZE = N_EXPERTS // N_GROUPS
TOPK_GROUPS = 4
TOP_K = 8
EXPERT_FF = 256
SHARED_FF = 256
ROUTED_SCALE = 2.5

LANES = 128
BF16_SUBLANES = 16
VMEM_LIMIT = 56 * 1024 * 1024

_SEGS = (('q', 512), ('k', 128), ('v', 128), ('qm', 256), ('km', 256), ('vm', 512), ('om', 512),
         ('bc', 512), ('cc', 512), ('xc', 512), ('gt', 3072), ('gm', 128))
_OFF = {}
_o = 0
for _n, _s in _SEGS:
    _OFF[_n] = (_o, _o + _s)
    _o += _s
N_PROJ = _o
D_IN = 6928


def _proj_column_index():
    idx = []
    half = ATT_HEAD_DIM // 2
    for hh in range(ATT_GROUP):
        for g in range(ATT_KV_HEADS):
            head = g * ATT_GROUP + hh
            for par in range(2):
                idx += [head * ATT_HEAD_DIM + 2 * i + par for i in range(half)]
    for g in range(ATT_KV_HEADS):
        for par in range(2):
            idx += [512 + g * ATT_HEAD_DIM + 2 * i + par for i in range(half)]
    idx += list(range(640, 768))
    idx += list(range(768, 2304))
    idx += list(range(2320, 3856))
    idx += list(range(3856, 6928))
    idx += list(range(2304, 2320)) + [D_IN] * (LANES - 16)
    assert len(idx) == N_PROJ
    return np.asarray(idx, np.int32)


def _attn_row_index():
    idx = []
    for hh in range(ATT_GROUP):
        for g in range(ATT_KV_HEADS):
            head = g * ATT_GROUP + hh
            idx += [head * ATT_HEAD_DIM + d for d in range(ATT_HEAD_DIM)]
    return np.asarray(idx, np.int32)


def _rope_tables(seq):
    rows = seq // GRID_W
    row = jnp.repeat(jnp.arange(rows, dtype=F32), GRID_W)
    col = jnp.tile(jnp.arange(GRID_W, dtype=F32), rows)
    n_pairs = ATT_HEAD_DIM // 4
    inv_freq = ROPE_BASE ** (-jnp.arange(n_pairs, dtype=F32) / n_pairs)
    ang = jnp.concatenate([row[:, None] * inv_freq, col[:, None] * inv_freq], axis=-1)
    c, s = jnp.cos(ang), jnp.sin(ang)
    cos_t = jnp.concatenate([c, c, c, c], axis=-1)
    sin_t = jnp.concatenate([-s, s, -s, s], axis=-1)
    return cos_t, sin_t


def _dot(a, b):
    return jnp.dot(a, b, preferred_element_type=F32)


def _dot_nt(a, b):
    return lax.dot_general(a, b, (((1,), (1,)), ((), ())), preferred_element_type=F32)


def _dot_tn(a, b):
    return lax.dot_general(a, b, (((0,), (0,)), ((), ())), preferred_element_type=F32)


def _sigmoid(x):
    return 1.0 / (1.0 + jnp.exp(-x))


def _silu(x):
    return x * _sigmoid(x)


def _log_sigmoid(x):
    return jnp.minimum(x, 0.0) - jnp.log(1.0 + jnp.exp(-jnp.abs(x)))


def _rms_mod(x, g, shift, scale):
    y = x * lax.rsqrt(jnp.mean(x * x, axis=-1, keepdims=True) + EPS) * g
    return y * (1.0 + scale) + shift


def _params(sem):
    return pltpu.CompilerParams(dimension_semantics=sem, vmem_limit_bytes=VMEM_LIMIT)


def _ada_kernel(c_ref, w_ref, b_ref, o_ref):
    s = _silu(c_ref[...])
    o_ref[0] = jnp.dot(s, w_ref[0], preferred_element_type=F32,
                       precision=lax.Precision.HIGHEST) + b_ref[0]


def _ada(cc, ada_w, ada_b):
    depth, d, n = ada_w.shape
    rows = cc.shape[0]
    tn = 1536
    return pl.pallas_call(
        _ada_kernel,
        grid=(depth, n // tn),
        in_specs=[pl.BlockSpec((rows, d), lambda l, j: (0, 0)),
                  pl.BlockSpec((1, d, tn), lambda l, j: (l, 0, j)),
                  pl.BlockSpec((1, 1, tn), lambda l, j: (l, 0, j))],
        out_specs=pl.BlockSpec((1, rows, tn), lambda l, j: (l, 0, j)),
        out_shape=jax.ShapeDtypeStruct((depth, rows, n), F32),
        compiler_params=_params(("parallel", "parallel")),
        name="ada_mod",
    )(cc, ada_w, ada_b.reshape(depth, 1, n))


def _swap_halves(x):
    lane = lax.broadcasted_iota(jnp.int32, x.shape, 1)
    first = (lane % ATT_HEAD_DIM) < (ATT_HEAD_DIM // 2)
    return jnp.where(first, pltpu.roll(x, LANES - 32, axis=1), pltpu.roll(x, 32, axis=1))


def _in_kernel(x_ref, mod_ref, g_ref, w_ref, cos_ref, sin_ref,
               qs_ref, k_ref, v_ref, qm_ref, km_ref, vm_ref, om_ref, gm_ref, bc_ref, u_ref, gt_ref):
    tm = x_ref.shape[1]
    h = _rms_mod(x_ref[0], g_ref[...], mod_ref[0, 0:1, :], mod_ref[0, 1:2, :]).astype(BF16)

    def proj(name):
        lo, hi = _OFF[name]
        return _dot(h, w_ref[:, lo:hi])

    cos_t = cos_ref[...]
    sin_t = sin_ref[...]

    def rope(t):
        return t * cos_t + _swap_halves(t) * sin_t

    q = proj('q')
    scale = ATT_HEAD_DIM ** -0.5
    for hh in range(ATT_GROUP):
        r = (rope(q[:, hh * LANES:(hh + 1) * LANES]) * scale).astype(BF16)
        for qb in range(tm // ATT_BLOCK):
            qs_ref[0, qb, hh * ATT_BLOCK:(hh + 1) * ATT_BLOCK, :] = r[qb * ATT_BLOCK:(qb + 1) * ATT_BLOCK, :]
    k_ref[0] = rope(proj('k')).astype(BF16)
    v_ref[0] = proj('v').astype(BF16)
    qm_ref[0] = proj('qm').astype(BF16)
    km_ref[0] = proj('km').astype(BF16)
    vm_ref[0] = proj('vm').astype(BF16)
    om_ref[0] = proj('om').astype(BF16)
    gm_ref[0] = proj('gm')
    bc_ref[0] = proj('bc').astype(BF16)
    u_ref[0] = (proj('cc') * proj('xc')).astype(BF16)
    gt_ref[0] = proj('gt').astype(BF16)


def _in_proj(x, mod, mod_row, norm_g, w_p, cos_t, sin_t, tm):
    B, S, D = x.shape
    nb = S // ATT_BLOCK
    tok = lambda n, dt: jax.ShapeDtypeStruct((B, S, n), dt)
    tspec = lambda n: pl.BlockSpec((1, tm, n), lambda b, i: (b, i, 0))
    out_shape = (jax.ShapeDtypeStruct((B, nb, ATT_GROUP * ATT_BLOCK, LANES), BF16),
                 tok(128, BF16), tok(128, BF16), tok(256, BF16), tok(256, BF16), tok(512, BF16),
                 tok(512, BF16), tok(128, F32), tok(512, BF16), tok(512, BF16), tok(3072, BF16))
    out_specs = (pl.BlockSpec((1, tm // ATT_BLOCK, ATT_GROUP * ATT_BLOCK, LANES), lambda b, i: (b, i, 0, 0)),
                 tspec(128), tspec(128), tspec(256), tspec(256), tspec(512), tspec(512), tspec(128),
                 tspec(512), tspec(512), tspec(3072))
    return pl.pallas_call(
        _in_kernel,
        grid=(B, S // tm),
        in_specs=[pl.BlockSpec((1, tm, D), lambda b, i: (b, i, 0)),
                  pl.BlockSpec((1, 6, D), lambda b, i: (mod_row(b), 0, 0)),
                  pl.BlockSpec((1, D), lambda b, i: (0, 0)),
                  pl.BlockSpec((D, N_PROJ), lambda b, i: (0, 0), pipeline_mode=pl.Buffered(1)),
                  pl.BlockSpec((tm, LANES), lambda b, i: (i, 0)),
                  pl.BlockSpec((tm, LANES), lambda b, i: (i, 0))],
        out_specs=out_specs,
        out_shape=out_shape,
        compiler_params=_params(("parallel", "parallel")),
        name="in_proj",
    )(x, mod, norm_g.reshape(1, D), w_p, cos_t, sin_t)


def _attn_kernel(sink_ref, qs_ref, kc_ref, vc_ref, *rest, band):
    if band:
        kp_ref, kcur_ref, kn_ref, vp_ref, vcur_ref, vn_ref, o_ref = rest
    else:
        (o_ref,) = rest
    j = pl.program_id(1)
    nblk = pl.num_programs(1)
    q = qs_ref[0, 0]
    rows = q.shape[0]
    if band:
        kcat = jnp.concatenate([kc_ref[0], kp_ref[0], kcur_ref[0], kn_ref[0]], axis=0)
        vcat = jnp.concatenate([vc_ref[0], vp_ref[0], vcur_ref[0], vn_ref[0]], axis=0)
    else:
        kcat = kc_ref[0]
        vcat = vc_ref[0]
    nkeys = kcat.shape[0]
    lc = kc_ref.shape[1]
    lane = lax.broadcasted_iota(jnp.int32, (1, LANES), 1)
    t = lax.broadcasted_iota(jnp.int32, (rows, 1), 0) % ATT_BLOCK
    hh = lax.broadcasted_iota(jnp.int32, (rows, 1), 0) // ATT_BLOCK
    if band:
        c = lax.broadcasted_iota(jnp.int32, (1, nkeys), 1)
        t_prev = t + jnp.where(j > 0, 0, 2 * ATT_BLOCK)
        t_next = t - jnp.where(j < nblk - 1, 0, 2 * ATT_BLOCK)
        i_prev = c - lc
        i_next = c - (lc + 2 * ATT_BLOCK)
        valid = ((c < lc)
                 | ((c >= lc) & (c < lc + ATT_BLOCK) & (i_prev >= t_prev))
                 | ((c >= lc + ATT_BLOCK) & (c < lc + 2 * ATT_BLOCK))
                 | ((c >= lc + 2 * ATT_BLOCK) & (i_next <= t_next)))
    out = jnp.zeros((rows, LANES), F32)
    for g in range(ATT_KV_HEADS):
        lm = (lane < ATT_HEAD_DIM) if g == 0 else (lane >= ATT_HEAD_DIM)
        kz = jnp.where(lm, kcat, jnp.zeros_like(kcat))
        vz = jnp.where(lm, vcat, jnp.zeros_like(vcat))
        s = _dot_nt(q, kz)
        if band:
            s = jnp.where(valid, s, -jnp.inf)
        sink = jnp.zeros((rows, 1), F32)
        for a in range(ATT_GROUP):
            sink = jnp.where(hh == a, sink_ref[g * ATT_GROUP + a], sink)
        m = jnp.maximum(jnp.max(s, axis=-1, keepdims=True), sink)
        p = jnp.exp(s - m)
        l = jnp.sum(p, axis=-1, keepdims=True) + jnp.exp(sink - m)
        out = out + _dot(p.astype(BF16), vz) / l
    o_ref[0, 0] = out.astype(BF16)


def _attention(qs, k, v, kc, vc, sink, band):
    B, nb = qs.shape[:2]
    lc = kc.shape[1]
    last = nb - 1
    qspec = pl.BlockSpec((1, 1, ATT_GROUP * ATT_BLOCK, LANES), lambda b, j: (b, j, 0, 0))
    cspec = pl.BlockSpec((1, lc, LANES), lambda b, j: (b, 0, 0))
    in_specs = [pl.BlockSpec(memory_space=pltpu.SMEM), qspec, cspec, cspec]
    args = [sink.astype(F32), qs, kc, vc]
    if band:
        prev = pl.BlockSpec((1, ATT_BLOCK, LANES), lambda b, j: (b, jnp.maximum(j - 1, 0), 0))
        cur = pl.BlockSpec((1, ATT_BLOCK, LANES), lambda b, j: (b, j, 0))
        nxt = pl.BlockSpec((1, ATT_BLOCK, LANES), lambda b, j: (b, jnp.minimum(j + 1, last), 0))
        in_specs += [prev, cur, nxt, prev, cur, nxt]
        args += [k, k, k, v, v, v]
    return pl.pallas_call(
        functools.partial(_attn_kernel, band=band),
        grid=(B, nb),
        in_specs=in_specs,
        out_specs=qspec,
        out_shape=jax.ShapeDtypeStruct(qs.shape, BF16),
        compiler_params=_params(("parallel", "parallel")),
        name="attention_band" if band else "attention_ctx",
    )(*args)


def _mlstm_step(dirs, T):
    kscale = M_QK_DIM ** -0.5
    si = lax.broadcasted_iota(jnp.int32, (T, T), 0)
    ri = lax.broadcasted_iota(jnp.int32, (T, T), 1)
    lane_qk = lax.broadcasted_iota(jnp.int32, (1, M_HEADS * M_QK_DIM), 1) // M_QK_DIM
    lane_m = lax.broadcasted_iota(jnp.int32, (1, LANES), 1)
    row_c = lax.broadcasted_iota(jnp.int32, (M_HEADS * M_QK_DIM, 1), 0) // M_QK_DIM
    combos = [(d, hd) for d in range(2) for hd in range(M_HEADS)]

    tri, bcol, gt, bt, blast = [], [], [], [], []
    for d, (q, k, v, g, C, n, m) in enumerate(dirs):
        t = (ri <= si) if d == 0 else (ri >= si)
        tri.append(t)
        lf = _log_sigmoid(g)
        bc = jnp.dot(t.astype(F32), lf, preferred_element_type=F32, precision=lax.Precision.HIGHEST)
        bcol.append(bc)
        gt.append(g.T)
        bt.append(bc.T)
        blast.append(bc[T - 1:T, :] if d == 0 else bc[0:1, :])

    def lanes(d, hd):
        return (2 * d) * M_HEADS + hd, (2 * d + 1) * M_HEADS + hd

    b_col = {c: bcol[c[0]][:, lanes(*c)[1]:lanes(*c)[1] + 1] for c in combos}
    ig_col = {c: dirs[c[0]][3][:, lanes(*c)[0]:lanes(*c)[0] + 1] for c in combos}
    alpha = {c: gt[c[0]][lanes(*c)[0]:lanes(*c)[0] + 1, :] - bt[c[0]][lanes(*c)[1]:lanes(*c)[1] + 1, :]
             for c in combos}
    m_old = {c: dirs[c[0]][6][:, c[1]:c[1] + 1] for c in combos}
    b_last = {c: blast[c[0]][:, lanes(*c)[1]:lanes(*c)[1] + 1] for c in combos}
    hmask = {hd: lane_qk == hd for hd in range(M_HEADS)}

    a_mat = {c: jnp.where(tri[c[0]], alpha[c], -jnp.inf) for c in combos}
    a_max = {c: jnp.max(a_mat[c], axis=1, keepdims=True) for c in combos}
    a_int = {c: b_col[c] + m_old[c] for c in combos}
    m_s = {c: jnp.maximum(a_int[c], b_col[c] + a_max[c]) for c in combos}
    w_int = {c: jnp.exp(a_int[c] - m_s[c]) for c in combos}
    w_mat = {c: jnp.exp(a_mat[c] + (b_col[c] - m_s[c])) for c in combos}
    qmask = {c: jnp.where(hmask[c[1]], dirs[c[0]][0], jnp.zeros_like(dirs[c[0]][0])) for c in combos}
    s_qk = {c: w_mat[c] * (_dot_nt(qmask[c], dirs[c[0]][1]) * kscale) for c in combos}
    vh = {c: dirs[c[0]][2][:, c[1] * M_V_DIM:(c[1] + 1) * M_V_DIM] for c in combos}
    c_bf = [dirs[d][4].astype(BF16) for d in range(2)]
    num = {c: _dot(s_qk[c].astype(BF16), vh[c]) + w_int[c] * _dot(qmask[c], c_bf[c[0]]) for c in combos}
    qn_all = [dirs[d][0].astype(F32) * dirs[d][5] for d in range(2)]
    qn = {c: jnp.sum(jnp.where(hmask[c[1]], qn_all[c[0]], 0.0), axis=1, keepdims=True) for c in combos}
    den = {c: jnp.sum(s_qk[c], axis=1, keepdims=True) + w_int[c] * qn[c] for c in combos}
    h = {c: num[c] / jnp.maximum(jnp.abs(den[c]), jnp.exp(-m_s[c])) for c in combos}

    r_col = {c: b_last[c] - b_col[c] + ig_col[c] for c in combos}
    m_new = {c: jnp.maximum(b_last[c] + m_old[c], jnp.max(r_col[c], axis=0, keepdims=True)) for c in combos}
    decay = {c: jnp.exp(b_last[c] + m_old[c] - m_new[c]) for c in combos}
    w_r = {c: jnp.exp(r_col[c] - m_new[c]) for c in combos}

    outs = []
    for d, (q, k, v, g, C, n, m) in enumerate(dirs):
        w_lanes = jnp.zeros((T, M_HEADS * M_QK_DIM), F32)
        dec_lanes = jnp.zeros((1, M_HEADS * M_QK_DIM), F32)
        dec_rows = jnp.zeros((M_HEADS * M_QK_DIM, 1), F32)
        m_row = jnp.zeros((1, LANES), F32)
        for hd in range(M_HEADS):
            w_lanes = jnp.where(hmask[hd], w_r[(d, hd)], w_lanes)
            dec_lanes = jnp.where(hmask[hd], decay[(d, hd)], dec_lanes)
            dec_rows = jnp.where(row_c == hd, decay[(d, hd)], dec_rows)
            m_row = jnp.where(lane_m == hd, m_new[(d, hd)], m_row)
        kw = k.astype(F32) * (w_lanes * kscale)
        kwt = kw.T.astype(BF16)
        upd = jnp.concatenate(
            [_dot(kwt[hd * M_QK_DIM:(hd + 1) * M_QK_DIM, :], vh[(d, hd)]) for hd in range(M_HEADS)], axis=0)
        c_new = dec_rows * C + upd
        n_new = dec_lanes * n + jnp.sum(kw, axis=0, keepdims=True)
        h_all = jnp.concatenate([h[(d, hd)] for hd in range(M_HEADS)], axis=1)
        outs.append((h_all, c_new, n_new, m_row))
    return outs


def _mlstm_kernel(gb_ref, qf_ref, kf_ref, vf_ref, gf_ref, qb_ref, kb_ref, vb_ref, gbk_ref,
                  c0_ref, n0_ref, m0_ref, hf_ref, hb_ref, cf_ref, nf_ref, mf_ref,
                  c_s, n_s, m_s):
    ci = pl.program_id(1)
    T = qf_ref.shape[1]

    @pl.when(ci == 0)
    def _():
        c_s[...] = c0_ref[0]
        n_s[...] = n0_ref[0]
        m_s[...] = m0_ref[0]

    gb = gb_ref[...]
    dirs = [(q_ref[0], k_ref[0], v_ref[0], g_ref[0] + gb, c_s[d], n_s[d], m_s[d])
            for d, (q_ref, k_ref, v_ref, g_ref) in enumerate(
                ((qf_ref, kf_ref, vf_ref, gf_ref), (qb_ref, kb_ref, vb_ref, gbk_ref)))]
    outs = _mlstm_step(dirs, T)
    for d, h_ref in enumerate((hf_ref, hb_ref)):
        h_all, c_new, n_new, m_row = outs[d]
        h_ref[0] = h_all
        c_s[d] = c_new
        n_s[d] = n_new
        m_s[d] = m_row

    @pl.when(ci == pl.num_programs(1) - 1)
    def _():
        cf_ref[0] = c_s[...]
        nf_ref[0] = n_s[...]
        mf_ref[0] = m_s[...]


MLSTM_TILE = 128


def _mlstm(qm, km, vm, gm, gate_b, state):
    B, S, _ = qm.shape
    T = MLSTM_TILE
    nc = S // T
    nq = M_HEADS * M_QK_DIM
    fwd = lambda n: pl.BlockSpec((1, T, n), lambda b, c: (b, c, 0))
    bwd = lambda n: pl.BlockSpec((1, T, n), lambda b, c: (b, nc - 1 - c, 0))
    st_specs = [pl.BlockSpec((1, 2, nq, M_V_DIM), lambda b, c: (b, 0, 0, 0)),
                pl.BlockSpec((1, 2, 1, nq), lambda b, c: (b, 0, 0, 0)),
                pl.BlockSpec((1, 2, 1, LANES), lambda b, c: (b, 0, 0, 0))]
    st_shapes = [jax.ShapeDtypeStruct((B, 2, nq, M_V_DIM), F32),
                 jax.ShapeDtypeStruct((B, 2, 1, nq), F32),
                 jax.ShapeDtypeStruct((B, 2, 1, LANES), F32)]
    gb_row = jnp.pad(gate_b.reshape(1, -1).astype(F32), ((0, 0), (0, LANES - gate_b.size)))
    outs = pl.pallas_call(
        _mlstm_kernel,
        grid=(B, nc),
        in_specs=[pl.BlockSpec((1, LANES), lambda b, c: (0, 0)),
                  fwd(256), fwd(256), fwd(512), fwd(LANES),
                  bwd(256), bwd(256), bwd(512), bwd(LANES)] + st_specs,
        out_specs=[fwd(M_OUT), bwd(M_OUT)] + st_specs,
        out_shape=[jax.ShapeDtypeStruct((B, S, M_OUT), F32)] * 2 + st_shapes,
        scratch_shapes=[pltpu.VMEM((2, nq, M_V_DIM), F32),
                        pltpu.VMEM((2, 1, nq), F32),
                        pltpu.VMEM((2, 1, LANES), F32)],
        compiler_params=_params(("parallel", "arbitrary")),
        name="mlstm_scan",
    )(gb_row, qm, km, vm, gm, qm, km, vm, gm, *state)
    return outs[0], outs[1], tuple(outs[2:])


def _route(scores, sel):
    tm = scores.shape[1]
    gi8 = lax.broadcasted_iota(jnp.int32, (GROUP_SIZE, tm), 0)

    def stack_rows(rows):
        out = jnp.broadcast_to(rows[0], (len(rows), tm))
        for r, v in enumerate(rows[1:], start=1):
            out = jnp.where(gi8 == r, v, out)
        return out

    gs = []
    for g in range(N_GROUPS):
        blk = sel[g * GROUP_SIZE:(g + 1) * GROUP_SIZE, :]
        m1 = jnp.max(blk, axis=0, keepdims=True)
        first = jnp.min(jnp.where(blk == m1, gi8, GROUP_SIZE), axis=0, keepdims=True)
        m2 = jnp.max(jnp.where(gi8 == first, -jnp.inf, blk), axis=0, keepdims=True)
        gs.append(m1 + m2)
    gsc = stack_rows(gs)
    gsel = jnp.zeros((N_GROUPS, tm), F32)
    for _ in range(TOPK_GROUPS):
        mx = jnp.max(gsc, axis=0, keepdims=True)
        first = jnp.min(jnp.where(gsc == mx, gi8, N_GROUPS), axis=0, keepdims=True)
        pick = gi8 == first
        gsel = jnp.where(pick, 1.0, gsel)
        gsc = jnp.where(pick, -jnp.inf, gsc)
    cur = jnp.concatenate(
        [jnp.where(gsel[g:g + 1, :] > 0.0, sel[g * GROUP_SIZE:(g + 1) * GROUP_SIZE, :], -jnp.inf)
         for g in range(N_GROUPS)], axis=0)
    ei = lax.broadcasted_iota(jnp.int32, (N_EXPERTS, tm), 0)
    idx, wts = [], []
    for _ in range(TOP_K):
        mx = jnp.max(cur, axis=0, keepdims=True)
        first = jnp.min(jnp.where(cur == mx, ei, N_EXPERTS), axis=0, keepdims=True)
        pick = ei == first
        idx.append(first)
        wts.append(jnp.sum(jnp.where(pick, scores, 0.0), axis=0, keepdims=True))
        cur = jnp.where(pick, -jnp.inf, cur)
    tot = wts[0]
    for w in wts[1:]:
        tot = tot + w
    wts = [w / tot * ROUTED_SCALE for w in wts]
    return stack_rows(idx), stack_rows(wts)


def _merge_kernel(x_ref, mod_ref, oat_ref, hf_ref, hb_ref, om_ref, ng_ref, u_ref, up_ref, un_ref,
                  bc_ref, cw_ref, gt_ref, bgb_ref, wa_ref, wm_ref, wc_ref, wo_ref, n2_ref,
                  rw_ref, rb_ref, sgu_ref, sd_ref,
                  base_ref, h2_ref, idx_ref, wt_ref, cnt_ref):
    i = pl.program_id(1)
    tm = x_ref.shape[1]
    x = x_ref[0]
    g1 = mod_ref[0, 2:3, :]
    sh2 = mod_ref[0, 3:4, :]
    sc2 = mod_ref[0, 4:5, :]
    g2 = mod_ref[0, 5:6, :]

    ya = jnp.concatenate(
        [jnp.concatenate([oat_ref[0, qb, hh * ATT_BLOCK:(hh + 1) * ATT_BLOCK, :] for hh in range(ATT_GROUP)], axis=1)
         for qb in range(tm // ATT_BLOCK)], axis=0)

    hsum = hf_ref[0] + hb_ref[0]
    parts = []
    for hd in range(M_HEADS):
        hh_ = hsum[:, hd * M_V_DIM:(hd + 1) * M_V_DIM]
        parts.append(hh_ * lax.rsqrt(jnp.mean(hh_ * hh_, axis=-1, keepdims=True) + EPS))
    hn = jnp.concatenate(parts, axis=1) * ng_ref[...]
    ym = (_sigmoid(om_ref[0].astype(F32)) * hn).astype(BF16)

    u = u_ref[0].astype(F32)
    row = lax.broadcasted_iota(jnp.int32, (tm, 1), 0)
    has_prev = (i > 0).astype(F32)
    has_next = (i < pl.num_programs(1) - 1).astype(F32)
    prev_row = up_ref[0, BF16_SUBLANES - 1:BF16_SUBLANES, :].astype(F32) * has_prev
    next_row = un_ref[0, 0:1, :].astype(F32) * has_next
    u_m1 = jnp.where(row == 0, prev_row, pltpu.roll(u, 1, axis=0))
    u_p1 = jnp.where(row == tm - 1, next_row, pltpu.roll(u, tm - 1, axis=0))
    conv = cw_ref[0:1, :] * u_m1 + cw_ref[1:2, :] * u + cw_ref[2:3, :] * u_p1
    yc = (bc_ref[0].astype(F32) * conv).astype(BF16)

    gg = _sigmoid(gt_ref[0].astype(F32) + bgb_ref[...])
    ymix = (gg[:, 0:D_MODEL] * _dot(ya, wa_ref[...])
            + gg[:, D_MODEL:2 * D_MODEL] * _dot(ym, wm_ref[...])
            + gg[:, 2 * D_MODEL:3 * D_MODEL] * _dot(yc, wc_ref[...]))
    y = _dot(ymix.astype(BF16), wo_ref[...])
    xm = x + g1 * y

    h2f = _rms_mod(xm, n2_ref[...], sh2, sc2)
    h2 = h2f.astype(BF16)
    h2_ref[0] = h2

    logits_t = lax.dot_general(rw_ref[...], h2f, (((1,), (1,)), ((), ())),
                               preferred_element_type=F32, precision=lax.Precision.HIGHEST)
    scores = _sigmoid(logits_t)
    idx, wts = _route(scores, scores + rb_ref[...])
    idx_ref[0] = idx
    wt_ref[0] = wts
    ei = lax.broadcasted_iota(jnp.int32, (N_EXPERTS, tm), 0)
    pick = jnp.zeros((N_EXPERTS, tm), F32)
    for kk in range(TOP_K):
        pick = jnp.where(ei == idx[kk:kk + 1, :], 1.0, pick)
    cnt_ref[0] = jnp.broadcast_to(jnp.sum(pick, axis=1, keepdims=True), (N_EXPERTS, LANES)).astype(jnp.int32)

    a = _dot(h2, sgu_ref[...])
    act = (_silu(a[:, 0:SHARED_FF]) * a[:, SHARED_FF:2 * SHARED_FF]).astype(BF16)
    base_ref[0] = xm + g2 * _dot(act, sd_ref[...])


def _merge(x, mod, mod_row, oat, hf, hb, om, u, bc, gt, lw, tm):
    B, S, D = x.shape
    nt = S // tm
    hal = BF16_SUBLANES
    last_h = S // hal - 1
    tspec = lambda n: pl.BlockSpec((1, tm, n), lambda b, i: (b, i, 0))
    full = lambda a: pl.BlockSpec(a.shape, lambda b, i: (0,) * a.ndim)
    weights = [lw['mlstm_norm_g'], lw['conv_w'], lw['branch_gate_b'], lw['w_br_attn'], lw['w_br_mlstm'],
               lw['w_br_conv'], lw['w_out'], lw['norm2_g'], lw['router_wt'], lw['router_bias'],
               lw['sh_gu'], lw['sh_d']]
    in_specs = [tspec(D),
                pl.BlockSpec((1, 6, D), lambda b, i: (mod_row(b), 0, 0)),
                pl.BlockSpec((1, tm // ATT_BLOCK, ATT_GROUP * ATT_BLOCK, LANES), lambda b, i: (b, i, 0, 0)),
                tspec(M_OUT), tspec(M_OUT), tspec(M_OUT), full(weights[0]),
                tspec(CONV_WIDTH),
                pl.BlockSpec((1, hal, CONV_WIDTH), lambda b, i: (b, jnp.maximum(i * (tm // hal) - 1, 0), 0)),
                pl.BlockSpec((1, hal, CONV_WIDTH), lambda b, i: (b, jnp.minimum((i + 1) * (tm // hal), last_h), 0)),
                tspec(CONV_WIDTH), full(weights[1]), tspec(N_BRANCH * D), full(weights[2])]
    in_specs += [full(w) for w in weights[3:]]
    tr = lambda n, dt: (jax.ShapeDtypeStruct((B, n, S), dt), pl.BlockSpec((1, n, tm), lambda b, i: (b, 0, i)))
    outs = [(jax.ShapeDtypeStruct((B, S, D), F32), tspec(D)),
            (jax.ShapeDtypeStruct((B, S, D), BF16), tspec(D)),
            tr(TOP_K, jnp.int32), tr(TOP_K, F32),
            (jax.ShapeDtypeStruct((B * nt, N_EXPERTS, LANES), jnp.int32),
             pl.BlockSpec((1, N_EXPERTS, LANES), lambda b, i: (b * nt + i, 0, 0)))]
    return pl.pallas_call(
        _merge_kernel,
        grid=(B, nt),
        in_specs=in_specs,
        out_specs=[o[1] for o in outs],
        out_shape=[o[0] for o in outs],
        compiler_params=_params(("parallel", "parallel")),
        name="merge_route",
    )(x, mod, oat, hf, hb, om, weights[0], u, u, u, bc, weights[1], gt, weights[2], *weights[3:])


MOE_TILE = 256
CHUNK = BF16_SUBLANES
GROUP_CHUNKS = 64
GROUP_ROWS = GROUP_CHUNKS * CHUNK
TILE_ROWS = MOE_TILE * TOP_K + N_EXPERTS * CHUNK
TILE_CHUNKS = TILE_ROWS // CHUNK
N_PAD_CHUNKS = N_EXPERTS * (GROUP_CHUNKS - 1)
N_SPARE_CHUNKS = 2 * TILE_CHUNKS
DMA_UNROLL = 8


def _chunk_copy(src, src_chunk, dst, dst_chunk, sem):
    return pltpu.make_async_copy(src.at[pl.ds(pl.multiple_of(src_chunk * CHUNK, CHUNK), CHUNK)],
                                 dst.at[pl.ds(pl.multiple_of(dst_chunk * CHUNK, CHUNK), CHUNK)], sem)


def _dispatch_tile(h, idx, wts):
    tm = h.shape[0]
    ei = lax.broadcasted_iota(jnp.int32, (N_EXPERTS, tm), 0)
    pick = jnp.zeros((N_EXPERTS, tm), F32)
    wmat = jnp.zeros((N_EXPERTS, tm), F32)
    for kk in range(TOP_K):
        chosen = ei == idx[kk:kk + 1, :]
        pick = jnp.where(chosen, 1.0, pick)
        wmat = jnp.where(chosen, wts[kk:kk + 1, :], wmat)
    t0 = lax.broadcasted_iota(jnp.int32, (tm, tm), 0)
    t1 = lax.broadcasted_iota(jnp.int32, (tm, tm), 1)
    rank = _dot(pick.astype(BF16), jnp.where(t0 < t1, 1.0, 0.0).astype(BF16))
    n_e = jnp.sum(pick, axis=1, keepdims=True)
    n_pad = jnp.floor((n_e + (CHUNK - 1)) * (1.0 / CHUNK)) * CHUNK
    e0 = lax.broadcasted_iota(jnp.int32, (N_EXPERTS, N_EXPERTS), 0)
    e1 = lax.broadcasted_iota(jnp.int32, (N_EXPERTS, N_EXPERTS), 1)
    seg = _dot(jnp.where(e1 < e0, 1.0, 0.0).astype(BF16),
               jnp.broadcast_to(n_pad, (N_EXPERTS, tm)).astype(BF16))
    posmat = seg + rank
    chunk_of = jnp.floor(posmat * (1.0 / CHUNK))
    offs_of = posmat - chunk_of * CHUNK
    chunk_row = jnp.where(pick > 0.0, chunk_of * CHUNK, -float(CHUNK))
    eye = e0 == e1
    to_row = lambda col: jnp.sum(jnp.where(eye, col, 0.0), axis=0, keepdims=True)
    seg_row = to_row(seg[:, 0:1])
    end_row = to_row(seg[:, 0:1] + n_pad)
    seg_row2 = jnp.concatenate([seg_row, seg_row], axis=1)
    end_row2 = jnp.concatenate([end_row, end_row], axis=1)
    r128 = lax.broadcasted_iota(jnp.int32, (TILE_ROWS, 2 * N_EXPERTS), 0).astype(F32)
    own2 = jnp.where(r128 >= seg_row2, jnp.where(r128 < end_row2, 1.0, 0.0), 0.0).astype(BF16)
    row_of = _dot(own2, jnp.concatenate([chunk_row, offs_of], axis=0).astype(BF16))
    w_of = _dot(own2[:, 0:N_EXPERTS], wmat.astype(BF16))
    riota = lax.broadcasted_iota(jnp.int32, (TILE_ROWS, tm), 0).astype(F32)
    hit = row_of == riota
    xg = _dot(jnp.where(hit, 1.0, 0.0).astype(BF16), h).astype(BF16)
    return xg, jnp.where(hit, w_of, 0.0).astype(BF16)


def _dispatch_kernel(pos_ref, pad_ref, h_ref, idx_ref, wt_ref, xs_ref, pw_ref, buf0, buf1, zero_buf, sems, pad_sem,
                     *, n_tiles):
    j = pl.program_id(0)
    bufs = (buf0, buf1)

    def compute(p):
        xg, pw = _dispatch_tile(h_ref[0], idx_ref[0], wt_ref[0])
        bufs[p][...] = xg
        pw_ref[...] = pw

    def issue(tile, p):
        for c in range(TILE_CHUNKS):
            _chunk_copy(bufs[p], c, xs_ref, pos_ref[tile * TILE_CHUNKS + c], sems.at[p]).start()

    def wait(p):
        pltpu.make_async_copy(bufs[p], xs_ref.at[pl.ds(0, TILE_ROWS)], sems.at[p]).wait()

    for p in range(2):
        @pl.when((j >= 2) & (j % 2 == p))
        def _(p=p):
            wait(p)

        @pl.when((j >= 1) & (j < n_tiles) & (j % 2 == p))
        def _(p=p):
            issue(j - 1, 1 - p)
            compute(p)

    @pl.when(j == 0)
    def _():
        compute(0)

    @pl.when(j == n_tiles)
    def _():
        last = (n_tiles - 1) % 2
        issue(n_tiles - 1, last)
        zero_buf[...] = jnp.zeros_like(zero_buf)

        def pad_expert(e, carry):
            def pad_issue(c, inner):
                _chunk_copy(zero_buf, 0, xs_ref, pad_ref[e] + c, pad_sem).start()
                return inner
            lax.fori_loop(0, pad_ref[N_EXPERTS + e], pad_issue, 0)

            def pad_wait(c, inner):
                _chunk_copy(zero_buf, 0, xs_ref, pad_ref[e] + c, pad_sem).wait()
                return inner
            lax.fori_loop(0, pad_ref[N_EXPERTS + e], pad_wait, 0)
            return carry
        lax.fori_loop(0, N_EXPERTS, pad_expert, 0)
        wait(last)


def _dispatch(h2, idx, wts, pos, pad_pos, n_slots):
    B, S, D = h2.shape
    tm = MOE_TILE
    nt = S // tm
    n_tiles = B * nt
    tile = lambda j: jnp.minimum(j, n_tiles - 1)
    grid_spec = pltpu.PrefetchScalarGridSpec(
        num_scalar_prefetch=2,
        grid=(n_tiles + 1,),
        in_specs=[pl.BlockSpec((1, tm, D), lambda j, pos, pad: (tile(j) // nt, tile(j) % nt, 0)),
                  pl.BlockSpec((1, TOP_K, tm), lambda j, pos, pad: (tile(j) // nt, 0, tile(j) % nt)),
                  pl.BlockSpec((1, TOP_K, tm), lambda j, pos, pad: (tile(j) // nt, 0, tile(j) % nt))],
        out_specs=[pl.BlockSpec(memory_space=pl.ANY),
                   pl.BlockSpec((TILE_ROWS, tm), lambda j, pos, pad: (tile(j), 0))],
        scratch_shapes=[pltpu.VMEM((TILE_ROWS, D), BF16),
                        pltpu.VMEM((TILE_ROWS, D), BF16),
                        pltpu.VMEM((CHUNK, D), BF16),
                        pltpu.SemaphoreType.DMA((2,)),
                        pltpu.SemaphoreType.DMA(())],
    )
    return pl.pallas_call(
        functools.partial(_dispatch_kernel, n_tiles=n_tiles),
        grid_spec=grid_spec,
        out_shape=[jax.ShapeDtypeStruct(((n_slots + N_SPARE_CHUNKS) * CHUNK, D), BF16),
                   jax.ShapeDtypeStruct((n_tiles * TILE_ROWS, tm), BF16)],
        compiler_params=_params(("arbitrary",)),
        name="moe_dispatch",
    )(pos, pad_pos, h2, idx, wts)


def _moe_tables(cnt, g_max):
    nt = cnt.shape[0]
    cc = (cnt + (CHUNK - 1)) // CHUNK
    segblk = jnp.cumsum(cc, axis=1) - cc
    tile_chunks = jnp.sum(cc, axis=1)
    prior = jnp.cumsum(cc, axis=0) - cc
    ge_cnt = (jnp.sum(cc, axis=0) + (GROUP_CHUNKS - 1)) // GROUP_CHUNKS
    gbase = jnp.cumsum(ge_cnt) - ge_cnt
    n_groups = jnp.sum(ge_cnt)
    c = jnp.arange(TILE_CHUNKS, dtype=jnp.int32)
    e_of = jnp.sum(((segblk + cc)[:, None, :] <= c[None, :, None]).astype(jnp.int32), axis=-1)
    e_of = jnp.minimum(e_of, N_EXPERTS - 1)
    seg_base = gbase[None, :] * GROUP_CHUNKS + prior - segblk
    onehot = e_of[:, :, None] == jnp.arange(N_EXPERTS, dtype=jnp.int32)[None, None, :]
    pos = jnp.sum(jnp.where(onehot, seg_base[:, None, :], 0), axis=-1) + c[None, :]
    valid = c[None, :] < tile_chunks[:, None]
    n_slots = g_max * GROUP_CHUNKS
    parity = (jnp.arange(nt, dtype=jnp.int32) % 2)[:, None]
    pos_write = jnp.where(valid, pos, n_slots + parity * TILE_CHUNKS + c[None, :]).astype(jnp.int32)
    pos_read = jnp.where(valid, pos, pos[:, 0:1]).astype(jnp.int32)
    ce = jnp.sum(cc, axis=0)
    pad_pos = jnp.concatenate([gbase * GROUP_CHUNKS + ce, ge_cnt * GROUP_CHUNKS - ce]).astype(jnp.int32)
    g = jnp.arange(g_max, dtype=jnp.int32)
    grp_e = jnp.minimum(jnp.sum(((gbase + ge_cnt)[None, :] <= g[:, None]).astype(jnp.int32), axis=1),
                        N_EXPERTS - 1).astype(jnp.int32)
    return (pos_write.reshape(-1), pos_read.reshape(-1), pad_pos.reshape(-1), grp_e,
            n_groups.reshape(1).astype(jnp.int32))


def _ffn_kernel(ge_ref, na_ref, x_ref, wg_ref, wu_ref, wd_ref, y_ref):
    g = pl.program_id(0)

    @pl.when(g < na_ref[0])
    def _():
        x = x_ref[...]
        gate = _dot(x, wg_ref[0].astype(BF16))
        up = _dot(x, wu_ref[0].astype(BF16))
        act = (_silu(gate) * up).astype(BF16)
        y_ref[...] = _dot(act, wd_ref[0].astype(BF16)).astype(BF16)

    @pl.when(g >= na_ref[0])
    def _():
        y_ref[...] = jnp.zeros_like(y_ref)


def _ffn_grouped(xs, grp_e, n_groups, w_gate, w_up, w_down, g_max):
    D = xs.shape[1]
    live = lambda g, na: jnp.minimum(g, jnp.maximum(na[0] - 1, 0))
    grid_spec = pltpu.PrefetchScalarGridSpec(
        num_scalar_prefetch=2,
        grid=(g_max,),
        in_specs=[pl.BlockSpec((GROUP_ROWS, D), lambda g, ge, na: (live(g, na), 0)),
                  pl.BlockSpec((1, D, EXPERT_FF), lambda g, ge, na: (ge[live(g, na)], 0, 0)),
                  pl.BlockSpec((1, D, EXPERT_FF), lambda g, ge, na: (ge[live(g, na)], 0, 0)),
                  pl.BlockSpec((1, EXPERT_FF, D), lambda g, ge, na: (ge[live(g, na)], 0, 0))],
        out_specs=pl.BlockSpec((GROUP_ROWS, D), lambda g, ge, na: (g, 0)),
    )
    return pl.pallas_call(
        _ffn_kernel,
        grid_spec=grid_spec,
        out_shape=jax.ShapeDtypeStruct((g_max * GROUP_ROWS, D), BF16),
        compiler_params=_params(("arbitrary",)),
        name="moe_ffn",
    )(grp_e, n_groups, xs, w_gate, w_up, w_down)


def _combine_kernel(pos_ref, ys_ref, pw_ref, base_ref, mod_ref, fg_ref, o_ref, buf0, buf1, sems, *,
                    n_steps, final):
    i = pl.program_id(0)
    bufs = (buf0, buf1)

    def fetch(tile, p):
        for c in range(TILE_CHUNKS):
            _chunk_copy(ys_ref, pos_ref[tile * TILE_CHUNKS + c], bufs[p], c, sems.at[p]).start()

    def wait(p):
        pltpu.make_async_copy(ys_ref.at[pl.ds(0, TILE_ROWS)], bufs[p], sems.at[p]).wait()

    @pl.when(i == 0)
    def _():
        fetch(0, 0)

    nxt = jnp.minimum(i + 1, n_steps - 1)
    for p in range(2):
        @pl.when(i % 2 == p)
        def _(p=p):
            wait(p)
            fetch(nxt, 1 - p)
            routed = _dot_tn(pw_ref[...], bufs[p][...])
            out = base_ref[...] + mod_ref[0, 5:6, :] * routed
            if final:
                out = out * lax.rsqrt(jnp.mean(out * out, axis=-1, keepdims=True) + EPS) * fg_ref[...]
            o_ref[...] = out

    @pl.when(i == n_steps - 1)
    def _():
        wait(n_steps % 2)


def _combine(ys, pos, pw, base, mod, mod_row, final_g):
    B, S, D = base.shape
    tm = MOE_TILE
    nt = S // tm
    final = final_g is not None
    fg = (final_g if final else jnp.ones((D,), F32)).reshape(1, D)
    grid_spec = pltpu.PrefetchScalarGridSpec(
        num_scalar_prefetch=1,
        grid=(B * nt,),
        in_specs=[pl.BlockSpec(memory_space=pl.ANY),
                  pl.BlockSpec((TILE_ROWS, tm), lambda i, pos: (i, 0)),
                  pl.BlockSpec((tm, D), lambda i, pos: (i, 0)),
                  pl.BlockSpec((1, 6, D), lambda i, pos: (mod_row(i // nt), 0, 0)),
                  pl.BlockSpec((1, D), lambda i, pos: (0, 0))],
        out_specs=pl.BlockSpec((tm, D), lambda i, pos: (i, 0)),
        scratch_shapes=[pltpu.VMEM((TILE_ROWS, D), BF16),
                        pltpu.VMEM((TILE_ROWS, D), BF16),
                        pltpu.SemaphoreType.DMA((2,))],
    )
    out = pl.pallas_call(
        functools.partial(_combine_kernel, n_steps=B * nt, final=final),
        grid_spec=grid_spec,
        out_shape=jax.ShapeDtypeStruct((B * S, D), F32),
        compiler_params=_params(("arbitrary",)),
        name="moe_combine",
    )(pos, ys, pw, base.reshape(B * S, D), mod, fg)
    return out.reshape(B, S, D)


def _moe_sparse(h2, idx, wts, cnt, base, mod, mod_row, w_gate, w_up, w_down, final_g=None):
    B, S, D = h2.shape
    n_tiles = B * (S // MOE_TILE)
    g_max = (n_tiles * TILE_CHUNKS + N_PAD_CHUNKS + GROUP_CHUNKS - 1) // GROUP_CHUNKS
    pos_write, pos_read, pad_pos, grp_e, n_groups = _moe_tables(cnt[:, :, 0], g_max)
    xs, pw = _dispatch(h2, idx, wts, pos_write, pad_pos, g_max * GROUP_CHUNKS)
    ys = _ffn_grouped(xs, grp_e, n_groups, w_gate, w_up, w_down, g_max)
    return _combine(ys, pos_read, pw, base, mod, mod_row, final_g)


def _zero_state(batch):
    nq = M_HEADS * M_QK_DIM
    return (jnp.zeros((batch, 2, nq, M_V_DIM), F32),
            jnp.zeros((batch, 2, 1, nq), F32),
            jnp.zeros((batch, 2, 1, LANES), F32))


def kernel(x, c, ctx, c_ctx, ada_w, ada_b, norm1_g, norm2_g, w_in, attn_sink, mlstm_gate_b, mlstm_norm_g, conv_w, w_br_attn, w_br_mlstm, w_br_conv, branch_gate_b, w_out, router_w, router_bias, exp_w_gate, exp_w_up, exp_w_down, sh_w_gate, sh_w_up, sh_w_down, final_g):
    B, S, D = x.shape
    L = ctx.shape[1]
    depth = ada_w.shape[0]
    ctx_row = B

    pad_rows = (-(B + 1)) % 8
    cc = jnp.concatenate([c, c_ctx[None, :], jnp.zeros((pad_rows, D), F32)], axis=0)
    mod_all = _ada(cc, ada_w, ada_b).reshape(depth, B + 1 + pad_rows, 6, D)

    cos_t, sin_t = _rope_tables(S)
    cos_c = jnp.ones((L, LANES), F32)
    sin_c = jnp.zeros((L, LANES), F32)
    col_idx = _proj_column_index()
    att_idx = _attn_row_index()
    lat_row = lambda b: b
    ctx_mod = lambda b: ctx_row

    xc = ctx
    for l in range(depth):
        need_ctx = l < depth - 1
        mod = mod_all[l]
        w_ext = jnp.concatenate([w_in[l], jnp.zeros((D, 1), F32)], axis=1)
        w_p = jnp.take(w_ext, col_idx, axis=1).astype(BF16)
        lw = {
            'mlstm_norm_g': mlstm_norm_g[l].reshape(1, M_OUT),
            'conv_w': conv_w[l],
            'branch_gate_b': branch_gate_b[l].reshape(1, N_BRANCH * D),
            'w_br_attn': jnp.take(w_br_attn[l], att_idx, axis=0).astype(BF16),
            'w_br_mlstm': w_br_mlstm[l].astype(BF16),
            'w_br_conv': w_br_conv[l].astype(BF16),
            'w_out': w_out[l].astype(BF16),
            'norm2_g': norm2_g[l].reshape(1, D),
            'router_wt': router_w[l].T,
            'router_bias': router_bias[l].reshape(N_EXPERTS, 1),
            'sh_gu': jnp.concatenate([sh_w_gate[l], sh_w_up[l]], axis=1).astype(BF16),
            'sh_d': sh_w_down[l].astype(BF16),
        }
        experts = (exp_w_gate[l], exp_w_up[l], exp_w_down[l])

        pc = _in_proj(xc, mod, ctx_mod, norm1_g[l], w_p, cos_c, sin_c, tm=256)
        p = _in_proj(x, mod, lat_row, norm1_g[l], w_p, cos_t, sin_t, tm=256)
        qs_c, k_c, v_c, qm_c, km_c, vm_c, om_c, gm_c, bc_c, u_c, gt_c = pc
        qs, k, v, qm, km, vm, om, gm, bc, u, gt = p

        oat = _attention(qs, k, v, k_c, v_c, attn_sink[l], band=True)
        hf_c, hb_c, st = _mlstm(qm_c, km_c, vm_c, gm_c, mlstm_gate_b[l], _zero_state(B))
        hf, hb, _ = _mlstm(qm, km, vm, gm, mlstm_gate_b[l], st)

        base, h2, idx, wts, cnt = _merge(x, mod, lat_row, oat, hf, hb, om, u, bc, gt, lw, tm=MOE_TILE)
        x_new = _moe_sparse(h2, idx, wts, cnt, base, mod, lat_row, *experts,
                            final_g=final_g if l == depth - 1 else None)

        if need_ctx:
            oat_c = _attention(qs_c, None, None, k_c, v_c, attn_sink[l], band=False)
            base_c, h2_c, idx_c, wts_c, cnt_c = _merge(xc, mod, ctx_mod, oat_c, hf_c, hb_c, om_c, u_c, bc_c, gt_c, lw,
                                                       tm=MOE_TILE)
            xc = _moe_sparse(h2_c, idx_c, wts_c, cnt_c, base_c, mod, ctx_mod, *experts)
        x = x_new
    return x
```

```python
import functools

import numpy as np
import jax
import jax.numpy as jnp
from jax import lax
from jax.experimental import pallas as pl
from jax.experimental.pallas import tpu as pltpu

F32 = jnp.float32
BF16 = jnp.bfloat16

D_MODEL = 1024
GRID_W = 64
EPS = 1e-6
ATT_HEADS = 8
ATT_KV_HEADS = 2
ATT_HEAD_DIM = 64
ATT_GROUP = ATT_HEADS // ATT_KV_HEADS
ATT_BLOCK = 128
ATT_OUT = ATT_HEADS * ATT_HEAD_DIM
ROPE_BASE = 10000.0
M_HEADS = 4
M_QK_DIM = 64
M_V_DIM = 128
M_CHUNK = 64
M_OUT = M_HEADS * M_V_DIM
CONV_WIDTH = 512
N_BRANCH = 3
N_EXPERTS = 64
N_GROUPS = 8
GROUP_SIZE = N_EXPERTS // N_GROUPS
TOPK_GROUPS = 4
TOP_K = 8
EXPERT_FF = 256
SHARED_FF = 256
ROUTED_SCALE = 2.5

LANES = 128
BF16_SUBLANES = 16
VMEM_LIMIT = 56 * 1024 * 1024

_SEGS = (('q', 512), ('k', 128), ('v', 128), ('qm', 256), ('km', 256), ('vm', 512), ('om', 512),
         ('bc', 512), ('cc', 512), ('xc', 512), ('gt', 3072), ('gm', 128))
_OFF = {}
_o = 0
for _n, _s in _SEGS:
    _OFF[_n] = (_o, _o + _s)
    _o += _s
N_PROJ = _o
D_IN = 6928


def _proj_column_index():
    idx = []
    half = ATT_HEAD_DIM // 2
    for hh in range(ATT_GROUP):
        for g in range(ATT_KV_HEADS):
            head = g * ATT_GROUP + hh
            for par in range(2):
                idx += [head * ATT_HEAD_DIM + 2 * i + par for i in range(half)]
    for g in range(ATT_KV_HEADS):
        for par in range(2):
            idx += [512 + g * ATT_HEAD_DIM + 2 * i + par for i in range(half)]
    idx += list(range(640, 768))
    idx += list(range(768, 2304))
    idx += list(range(2320, 3856))
    idx += list(range(3856, 6928))
    idx += list(range(2304, 2320)) + [D_IN] * (LANES - 16)
    assert len(idx) == N_PROJ
    return np.asarray(idx, np.int32)


def _attn_row_index():
    idx = []
    for hh in range(ATT_GROUP):
        for g in range(ATT_KV_HEADS):
            head = g * ATT_GROUP + hh
            idx += [head * ATT_HEAD_DIM + d for d in range(ATT_HEAD_DIM)]
    return np.asarray(idx, np.int32)


def _rope_tables(seq):
    rows = seq // GRID_W
    row = jnp.repeat(jnp.arange(rows, dtype=F32), GRID_W)
    col = jnp.tile(jnp.arange(GRID_W, dtype=F32), rows)
    n_pairs = ATT_HEAD_DIM // 4
    inv_freq = ROPE_BASE ** (-jnp.arange(n_pairs, dtype=F32) / n_pairs)
    ang = jnp.concatenate([row[:, None] * inv_freq, col[:, None] * inv_freq], axis=-1)
    c, s = jnp.cos(ang), jnp.sin(ang)
    cos_t = jnp.concatenate([c, c, c, c], axis=-1)
    sin_t = jnp.concatenate([-s, s, -s, s], axis=-1)
    return cos_t, sin_t


def _dot(a, b):
    return jnp.dot(a, b, preferred_element_type=F32)


def _dot_nt(a, b):
    return lax.dot_general(a, b, (((1,), (1,)), ((), ())), preferred_element_type=F32)


def _dot_tn(a, b):
    return lax.dot_general(a, b, (((0,), (0,)), ((), ())), preferred_element_type=F32)


def _sigmoid(x):
    return 1.0 / (1.0 + jnp.exp(-x))


def _silu(x):
    return x * _sigmoid(x)


def _log_sigmoid(x):
    return jnp.minimum(x, 0.0) - jnp.log(1.0 + jnp.exp(-jnp.abs(x)))


def _rms_mod(x, g, shift, scale):
    y = x * lax.rsqrt(jnp.mean(x * x, axis=-1, keepdims=True) + EPS) * g
    return y * (1.0 + scale) + shift


def _params(sem):
    return pltpu.CompilerParams(dimension_semantics=sem, vmem_limit_bytes=VMEM_LIMIT)


def _ada_kernel(c_ref, w_ref, b_ref, o_ref):
    s = _silu(c_ref[...])
    o_ref[0] = jnp.dot(s, w_ref[0], preferred_element_type=F32,
                       precision=lax.Precision.HIGHEST) + b_ref[0]


def _ada(cc, ada_w, ada_b):
    depth, d, n = ada_w.shape
    rows = cc.shape[0]
    tn = 1536
    return pl.pallas_call(
        _ada_kernel,
        grid=(depth, n // tn),
        in_specs=[pl.BlockSpec((rows, d), lambda l, j: (0, 0)),
                  pl.BlockSpec((1, d, tn), lambda l, j: (l, 0, j)),
                  pl.BlockSpec((1, 1, tn), lambda l, j: (l, 0, j))],
        out_specs=pl.BlockSpec((1, rows, tn), lambda l, j: (l, 0, j)),
        out_shape=jax.ShapeDtypeStruct((depth, rows, n), F32),
        compiler_params=_params(("parallel", "parallel")),
        name="ada_mod",
    )(cc, ada_w, ada_b.reshape(depth, 1, n))


def _swap_halves(x):
    lane = lax.broadcasted_iota(jnp.int32, x.shape, 1)
    first = (lane % ATT_HEAD_DIM) < (ATT_HEAD_DIM // 2)
    return jnp.where(first, pltpu.roll(x, LANES - 32, axis=1), pltpu.roll(x, 32, axis=1))


def _in_kernel(x_ref, mod_ref, g_ref, w_ref, cos_ref, sin_ref,
               qs_ref, k_ref, v_ref, qm_ref, km_ref, vm_ref, om_ref, gm_ref, bc_ref, u_ref, gt_ref):
    tm = x_ref.shape[1]
    h = _rms_mod(x_ref[0], g_ref[...], mod_ref[0, 0:1, :], mod_ref[0, 1:2, :]).astype(BF16)

    def proj(name):
        lo, hi = _OFF[name]
        return _dot(h, w_ref[:, lo:hi])

    cos_t = cos_ref[...]
    sin_t = sin_ref[...]

    def rope(t):
        return t * cos_t + _swap_halves(t) * sin_t

    q = proj('q')
    scale = ATT_HEAD_DIM ** -0.5
    for hh in range(ATT_GROUP):
        r = (rope(q[:, hh * LANES:(hh + 1) * LANES]) * scale).astype(BF16)
        for qb in range(tm // ATT_BLOCK):
            qs_ref[0, qb, hh * ATT_BLOCK:(hh + 1) * ATT_BLOCK, :] = r[qb * ATT_BLOCK:(qb + 1) * ATT_BLOCK, :]
    k_ref[0] = rope(proj('k')).astype(BF16)
    v_ref[0] = proj('v').astype(BF16)
    qm_ref[0] = proj('qm').astype(BF16)
    km_ref[0] = proj('km').astype(BF16)
    vm_ref[0] = proj('vm').astype(BF16)
    om_ref[0] = proj('om').astype(BF16)
    gm_ref[0] = proj('gm')
    bc_ref[0] = proj('bc').astype(BF16)
    u_ref[0] = (proj('cc') * proj('xc')).astype(BF16)
    gt_ref[0] = proj('gt').astype(BF16)


def _in_proj(x, mod, mod_row, norm_g, w_p, cos_t, sin_t, tm):
    B, S, D = x.shape
    nb = S // ATT_BLOCK
    tok = lambda n, dt: jax.ShapeDtypeStruct((B, S, n), dt)
    tspec = lambda n: pl.BlockSpec((1, tm, n), lambda b, i: (b, i, 0))
    out_shape = (jax.ShapeDtypeStruct((B, nb, ATT_GROUP * ATT_BLOCK, LANES), BF16),
                 tok(128, BF16), tok(128, BF16), tok(256, BF16), tok(256, BF16), tok(512, BF16),
                 tok(512, BF16), tok(128, F32), tok(512, BF16), tok(512, BF16), tok(3072, BF16))
    out_specs = (pl.BlockSpec((1, tm // ATT_BLOCK, ATT_GROUP * ATT_BLOCK, LANES), lambda b, i: (b, i, 0, 0)),
                 tspec(128), tspec(128), tspec(256), tspec(256), tspec(512), tspec(512), tspec(128),
                 tspec(512), tspec(512), tspec(3072))
    return pl.pallas_call(
        _in_kernel,
        grid=(B, S // tm),
        in_specs=[pl.BlockSpec((1, tm, D), lambda b, i: (b, i, 0)),
                  pl.BlockSpec((1, 6, D), lambda b, i: (mod_row(b), 0, 0)),
                  pl.BlockSpec((1, D), lambda b, i: (0, 0)),
                  pl.BlockSpec((D, N_PROJ), lambda b, i: (0, 0), pipeline_mode=pl.Buffered(1)),
                  pl.BlockSpec((tm, LANES), lambda b, i: (i, 0)),
                  pl.BlockSpec((tm, LANES), lambda b, i: (i, 0))],
        out_specs=out_specs,
        out_shape=out_shape,
        compiler_params=_params(("parallel", "parallel")),
        name="in_proj",
    )(x, mod, norm_g.reshape(1, D), w_p, cos_t, sin_t)


def _attn_kernel(sink_ref, qs_ref, kc_ref, vc_ref, *rest, band):
    if band:
        kp_ref, kcur_ref, kn_ref, vp_ref, vcur_ref, vn_ref, o_ref = rest
    else:
        (o_ref,) = rest
    j = pl.program_id(1)
    nblk = pl.num_programs(1)
    q = qs_ref[0, 0]
    rows = q.shape[0]
    if band:
        kcat = jnp.concatenate([kc_ref[0], kp_ref[0], kcur_ref[0], kn_ref[0]], axis=0)
        vcat = jnp.concatenate([vc_ref[0], vp_ref[0], vcur_ref[0], vn_ref[0]], axis=0)
    else:
        kcat = kc_ref[0]
        vcat = vc_ref[0]
    nkeys = kcat.shape[0]
    lc = kc_ref.shape[1]
    lane = lax.broadcasted_iota(jnp.int32, (1, LANES), 1)
    t = lax.broadcasted_iota(jnp.int32, (rows, 1), 0) % ATT_BLOCK
    hh = lax.broadcasted_iota(jnp.int32, (rows, 1), 0) // ATT_BLOCK
    if band:
        c = lax.broadcasted_iota(jnp.int32, (1, nkeys), 1)
        t_prev = t + jnp.where(j > 0, 0, 2 * ATT_BLOCK)
        t_next = t - jnp.where(j < nblk - 1, 0, 2 * ATT_BLOCK)
        i_prev = c - lc
        i_next = c - (lc + 2 * ATT_BLOCK)
        valid = ((c < lc)
                 | ((c >= lc) & (c < lc + ATT_BLOCK) & (i_prev >= t_prev))
                 | ((c >= lc + ATT_BLOCK) & (c < lc + 2 * ATT_BLOCK))
                 | ((c >= lc + 2 * ATT_BLOCK) & (i_next <= t_next)))
    out = jnp.zeros((rows, LANES), F32)
    for g in range(ATT_KV_HEADS):
        lm = (lane < ATT_HEAD_DIM) if g == 0 else (lane >= ATT_HEAD_DIM)
        kz = jnp.where(lm, kcat, jnp.zeros_like(kcat))
        vz = jnp.where(lm, vcat, jnp.zeros_like(vcat))
        s = _dot_nt(q, kz)
        if band:
            s = jnp.where(valid, s, -jnp.inf)
        sink = jnp.zeros((rows, 1), F32)
        for a in range(ATT_GROUP):
            sink = jnp.where(hh == a, sink_ref[g * ATT_GROUP + a], sink)
        m = jnp.maximum(jnp.max(s, axis=-1, keepdims=True), sink)
        p = jnp.exp(s - m)
        l = jnp.sum(p, axis=-1, keepdims=True) + jnp.exp(sink - m)
        out = out + _dot(p.astype(BF16), vz) / l
    o_ref[0, 0] = out.astype(BF16)


def _attention(qs, k, v, kc, vc, sink, band):
    B, nb = qs.shape[:2]
    lc = kc.shape[1]
    last = nb - 1
    qspec = pl.BlockSpec((1, 1, ATT_GROUP * ATT_BLOCK, LANES), lambda b, j: (b, j, 0, 0))
    cspec = pl.BlockSpec((1, lc, LANES), lambda b, j: (b, 0, 0))
    in_specs = [pl.BlockSpec(memory_space=pltpu.SMEM), qspec, cspec, cspec]
    args = [sink.astype(F32), qs, kc, vc]
    if band:
        prev = pl.BlockSpec((1, ATT_BLOCK, LANES), lambda b, j: (b, jnp.maximum(j - 1, 0), 0))
        cur = pl.BlockSpec((1, ATT_BLOCK, LANES), lambda b, j: (b, j, 0))
        nxt = pl.BlockSpec((1, ATT_BLOCK, LANES), lambda b, j: (b, jnp.minimum(j + 1, last), 0))
        in_specs += [prev, cur, nxt, prev, cur, nxt]
        args += [k, k, k, v, v, v]
    return pl.pallas_call(
        functools.partial(_attn_kernel, band=band),
        grid=(B, nb),
        in_specs=in_specs,
        out_specs=qspec,
        out_shape=jax.ShapeDtypeStruct(qs.shape, BF16),
        compiler_params=_params(("parallel", "parallel")),
        name="attention_band" if band else "attention_ctx",
    )(*args)


def _mlstm_step(dirs, T):
    kscale = M_QK_DIM ** -0.5
    si = lax.broadcasted_iota(jnp.int32, (T, T), 0)
    ri = lax.broadcasted_iota(jnp.int32, (T, T), 1)
    lane_qk = lax.broadcasted_iota(jnp.int32, (1, M_HEADS * M_QK_DIM), 1) // M_QK_DIM
    lane_m = lax.broadcasted_iota(jnp.int32, (1, LANES), 1)
    row_c = lax.broadcasted_iota(jnp.int32, (M_HEADS * M_QK_DIM, 1), 0) // M_QK_DIM
    combos = [(d, hd) for d in range(2) for hd in range(M_HEADS)]

    tri, bcol, gt, bt, blast = [], [], [], [], []
    for d, (q, k, v, g, C, n, m) in enumerate(dirs):
        t = (ri <= si) if d == 0 else (ri >= si)
        tri.append(t)
        lf = _log_sigmoid(g)
        bc = jnp.dot(t.astype(F32), lf, preferred_element_type=F32, precision=lax.Precision.HIGHEST)
        bcol.append(bc)
        gt.append(g.T)
        bt.append(bc.T)
        blast.append(bc[T - 1:T, :] if d == 0 else bc[0:1, :])

    def lanes(d, hd):
        return (2 * d) * M_HEADS + hd, (2 * d + 1) * M_HEADS + hd

    b_col = {c: bcol[c[0]][:, lanes(*c)[1]:lanes(*c)[1] + 1] for c in combos}
    ig_col = {c: dirs[c[0]][3][:, lanes(*c)[0]:lanes(*c)[0] + 1] for c in combos}
    alpha = {c: gt[c[0]][lanes(*c)[0]:lanes(*c)[0] + 1, :] - bt[c[0]][lanes(*c)[1]:lanes(*c)[1] + 1, :]
             for c in combos}
    m_old = {c: dirs[c[0]][6][:, c[1]:c[1] + 1] for c in combos}
    b_last = {c: blast[c[0]][:, lanes(*c)[1]:lanes(*c)[1] + 1] for c in combos}
    hmask = {hd: lane_qk == hd for hd in range(M_HEADS)}

    a_mat = {c: jnp.where(tri[c[0]], alpha[c], -jnp.inf) for c in combos}
    a_max = {c: jnp.max(a_mat[c], axis=1, keepdims=True) for c in combos}
    a_int = {c: b_col[c] + m_old[c] for c in combos}
    m_s = {c: jnp.maximum(a_int[c], b_col[c] + a_max[c]) for c in combos}
    w_int = {c: jnp.exp(a_int[c] - m_s[c]) for c in combos}
    w_mat = {c: jnp.exp(a_mat[c] + (b_col[c] - m_s[c])) for c in combos}
    qmask = {c: jnp.where(hmask[c[1]], dirs[c[0]][0], jnp.zeros_like(dirs[c[0]][0])) for c in combos}
    s_qk = {c: w_mat[c] * (_dot_nt(qmask[c], dirs[c[0]][1]) * kscale) for c in combos}
    vh = {c: dirs[c[0]][2][:, c[1] * M_V_DIM:(c[1] + 1) * M_V_DIM] for c in combos}
    c_bf = [dirs[d][4].astype(BF16) for d in range(2)]
    num = {c: _dot(s_qk[c].astype(BF16), vh[c]) + w_int[c] * _dot(qmask[c], c_bf[c[0]]) for c in combos}
    qn_all = [dirs[d][0].astype(F32) * dirs[d][5] for d in range(2)]
    qn = {c: jnp.sum(jnp.where(hmask[c[1]], qn_all[c[0]], 0.0), axis=1, keepdims=True) for c in combos}
    den = {c: jnp.sum(s_qk[c], axis=1, keepdims=True) + w_int[c] * qn[c] for c in combos}
    h = {c: num[c] / jnp.maximum(jnp.abs(den[c]), jnp.exp(-m_s[c])) for c in combos}

    r_col = {c: b_last[c] - b_col[c] + ig_col[c] for c in combos}
    m_new = {c: jnp.maximum(b_last[c] + m_old[c], jnp.max(r_col[c], axis=0, keepdims=True)) for c in combos}
    decay = {c: jnp.exp(b_last[c] + m_old[c] - m_new[c]) for c in combos}
    w_r = {c: jnp.exp(r_col[c] - m_new[c]) for c in combos}

    outs = []
    for d, (q, k, v, g, C, n, m) in enumerate(dirs):
        w_lanes = jnp.zeros((T, M_HEADS * M_QK_DIM), F32)
        dec_lanes = jnp.zeros((1, M_HEADS * M_QK_DIM), F32)
        dec_rows = jnp.zeros((M_HEADS * M_QK_DIM, 1), F32)
        m_row = jnp.zeros((1, LANES), F32)
        for hd in range(M_HEADS):
            w_lanes = jnp.where(hmask[hd], w_r[(d, hd)], w_lanes)
            dec_lanes = jnp.where(hmask[hd], decay[(d, hd)], dec_lanes)
            dec_rows = jnp.where(row_c == hd, decay[(d, hd)], dec_rows)
            m_row = jnp.where(lane_m == hd, m_new[(d, hd)], m_row)
        kw = k.astype(F32) * (w_lanes * kscale)
        kwt = kw.T.astype(BF16)
        upd = jnp.concatenate(
            [_dot(kwt[hd * M_QK_DIM:(hd + 1) * M_QK_DIM, :], vh[(d, hd)]) for hd in range(M_HEADS)], axis=0)
        c_new = dec_rows * C + upd
        n_new = dec_lanes * n + jnp.sum(kw, axis=0, keepdims=True)
        h_all = jnp.concatenate([h[(d, hd)] for hd in range(M_HEADS)], axis=1)
        outs.append((h_all, c_new, n_new, m_row))
    return outs


def _mlstm_kernel(gb_ref, qf_ref, kf_ref, vf_ref, gf_ref, qb_ref, kb_ref, vb_ref, gbk_ref,
                  c0_ref, n0_ref, m0_ref, hf_ref, hb_ref, cf_ref, nf_ref, mf_ref,
                  c_s, n_s, m_s):
    ci = pl.program_id(1)
    T = qf_ref.shape[1]

    @pl.when(ci == 0)
    def _():
        c_s[...] = c0_ref[0]
        n_s[...] = n0_ref[0]
        m_s[...] = m0_ref[0]

    gb = gb_ref[...]
    dirs = [(q_ref[0], k_ref[0], v_ref[0], g_ref[0] + gb, c_s[d], n_s[d], m_s[d])
            for d, (q_ref, k_ref, v_ref, g_ref) in enumerate(
                ((qf_ref, kf_ref, vf_ref, gf_ref), (qb_ref, kb_ref, vb_ref, gbk_ref)))]
    outs = _mlstm_step(dirs, T)
    for d, h_ref in enumerate((hf_ref, hb_ref)):
        h_all, c_new, n_new, m_row = outs[d]
        h_ref[0] = h_all
        c_s[d] = c_new
        n_s[d] = n_new
        m_s[d] = m_row

    @pl.when(ci == pl.num_programs(1) - 1)
    def _():
        cf_ref[0] = c_s[...]
        nf_ref[0] = n_s[...]
        mf_ref[0] = m_s[...]


MLSTM_TILE = 128


def _mlstm(qm, km, vm, gm, gate_b, state):
    B, S, _ = qm.shape
    T = MLSTM_TILE
    nc = S // T
    nq = M_HEADS * M_QK_DIM
    fwd = lambda n: pl.BlockSpec((1, T, n), lambda b, c: (b, c, 0))
    bwd = lambda n: pl.BlockSpec((1, T, n), lambda b, c: (b, nc - 1 - c, 0))
    st_specs = [pl.BlockSpec((1, 2, nq, M_V_DIM), lambda b, c: (b, 0, 0, 0)),
                pl.BlockSpec((1, 2, 1, nq), lambda b, c: (b, 0, 0, 0)),
                pl.BlockSpec((1, 2, 1, LANES), lambda b, c: (b, 0, 0, 0))]
    st_shapes = [jax.ShapeDtypeStruct((B, 2, nq, M_V_DIM), F32),
                 jax.ShapeDtypeStruct((B, 2, 1, nq), F32),
                 jax.ShapeDtypeStruct((B, 2, 1, LANES), F32)]
    gb_row = jnp.pad(gate_b.reshape(1, -1).astype(F32), ((0, 0), (0, LANES - gate_b.size)))
    outs = pl.pallas_call(
        _mlstm_kernel,
        grid=(B, nc),
        in_specs=[pl.BlockSpec((1, LANES), lambda b, c: (0, 0)),
                  fwd(256), fwd(256), fwd(512), fwd(LANES),
                  bwd(256), bwd(256), bwd(512), bwd(LANES)] + st_specs,
        out_specs=[fwd(M_OUT), bwd(M_OUT)] + st_specs,
        out_shape=[jax.ShapeDtypeStruct((B, S, M_OUT), F32)] * 2 + st_shapes,
        scratch_shapes=[pltpu.VMEM((2, nq, M_V_DIM), F32),
                        pltpu.VMEM((2, 1, nq), F32),
                        pltpu.VMEM((2, 1, LANES), F32)],
        compiler_params=_params(("parallel", "arbitrary")),
        name="mlstm_scan",
    )(gb_row, qm, km, vm, gm, qm, km, vm, gm, *state)
    return outs[0], outs[1], tuple(outs[2:])


def _route(scores, sel):
    tm = scores.shape[1]
    gi8 = lax.broadcasted_iota(jnp.int32, (GROUP_SIZE, tm), 0)

    def stack_rows(rows):
        out = jnp.broadcast_to(rows[0], (len(rows), tm))
        for r, v in enumerate(rows[1:], start=1):
            out = jnp.where(gi8 == r, v, out)
        return out

    gs = []
    for g in range(N_GROUPS):
        blk = sel[g * GROUP_SIZE:(g + 1) * GROUP_SIZE, :]
        m1 = jnp.max(blk, axis=0, keepdims=True)
        first = jnp.min(jnp.where(blk == m1, gi8, GROUP_SIZE), axis=0, keepdims=True)
        m2 = jnp.max(jnp.where(gi8 == first, -jnp.inf, blk), axis=0, keepdims=True)
        gs.append(m1 + m2)
    gsc = stack_rows(gs)
    gsel = jnp.zeros((N_GROUPS, tm), F32)
    for _ in range(TOPK_GROUPS):
        mx = jnp.max(gsc, axis=0, keepdims=True)
        first = jnp.min(jnp.where(gsc == mx, gi8, N_GROUPS), axis=0, keepdims=True)
        pick = gi8 == first
        gsel = jnp.where(pick, 1.0, gsel)
        gsc = jnp.where(pick, -jnp.inf, gsc)
    cur = jnp.concatenate(
        [jnp.where(gsel[g:g + 1, :] > 0.0, sel[g * GROUP_SIZE:(g + 1) * GROUP_SIZE, :], -jnp.inf)
         for g in range(N_GROUPS)], axis=0)
    ei = lax.broadcasted_iota(jnp.int32, (N_EXPERTS, tm), 0)
    idx, wts = [], []
    for _ in range(TOP_K):
        mx = jnp.max(cur, axis=0, keepdims=True)
        first = jnp.min(jnp.where(cur == mx, ei, N_EXPERTS), axis=0, keepdims=True)
        pick = ei == first
        idx.append(first)
        wts.append(jnp.sum(jnp.where(pick, scores, 0.0), axis=0, keepdims=True))
        cur = jnp.where(pick, -jnp.inf, cur)
    tot = wts[0]
    for w in wts[1:]:
        tot = tot + w
    wts = [w / tot * ROUTED_SCALE for w in wts]
    return stack_rows(idx), stack_rows(wts)


def _merge_kernel(x_ref, mod_ref, oat_ref, hf_ref, hb_ref, om_ref, ng_ref, u_ref, up_ref, un_ref,
                  bc_ref, cw_ref, gt_ref, bgb_ref, wa_ref, wm_ref, wc_ref, wo_ref, n2_ref,
                  rw_ref, rb_ref, sgu_ref, sd_ref,
                  base_ref, h2_ref, idx_ref, wt_ref, cnt_ref):
    i = pl.program_id(1)
    tm = x_ref.shape[1]
    x = x_ref[0]
    g1 = mod_ref[0, 2:3, :]
    sh2 = mod_ref[0, 3:4, :]
    sc2 = mod_ref[0, 4:5, :]
    g2 = mod_ref[0, 5:6, :]

    ya = jnp.concatenate(
        [jnp.concatenate([oat_ref[0, qb, hh * ATT_BLOCK:(hh + 1) * ATT_BLOCK, :] for hh in range(ATT_GROUP)], axis=1)
         for qb in range(tm // ATT_BLOCK)], axis=0)

    hsum = hf_ref[0] + hb_ref[0]
    parts = []
    for hd in range(M_HEADS):
        hh_ = hsum[:, hd * M_V_DIM:(hd + 1) * M_V_DIM]
        parts.append(hh_ * lax.rsqrt(jnp.mean(hh_ * hh_, axis=-1, keepdims=True) + EPS))
    hn = jnp.concatenate(parts, axis=1) * ng_ref[...]
    ym = (_sigmoid(om_ref[0].astype(F32)) * hn).astype(BF16)

    u = u_ref[0].astype(F32)
    row = lax.broadcasted_iota(jnp.int32, (tm, 1), 0)
    has_prev = (i > 0).astype(F32)
    has_next = (i < pl.num_programs(1) - 1).astype(F32)
    prev_row = up_ref[0, BF16_SUBLANES - 1:BF16_SUBLANES, :].astype(F32) * has_prev
    next_row = un_ref[0, 0:1, :].astype(F32) * has_next
    u_m1 = jnp.where(row == 0, prev_row, pltpu.roll(u, 1, axis=0))
    u_p1 = jnp.where(row == tm - 1, next_row, pltpu.roll(u, tm - 1, axis=0))
    conv = cw_ref[0:1, :] * u_m1 + cw_ref[1:2, :] * u + cw_ref[2:3, :] * u_p1
    yc = (bc_ref[0].astype(F32) * conv).astype(BF16)

    gg = _sigmoid(gt_ref[0].astype(F32) + bgb_ref[...])
    ymix = (gg[:, 0:D_MODEL] * _dot(ya, wa_ref[...])
            + gg[:, D_MODEL:2 * D_MODEL] * _dot(ym, wm_ref[...])
            + gg[:, 2 * D_MODEL:3 * D_MODEL] * _dot(yc, wc_ref[...]))
    y = _dot(ymix.astype(BF16), wo_ref[...])
    xm = x + g1 * y

    h2f = _rms_mod(xm, n2_ref[...], sh2, sc2)
    h2 = h2f.astype(BF16)
    h2_ref[0] = h2

    logits_t = lax.dot_general(rw_ref[...], h2f, (((1,), (1,)), ((), ())),
                               preferred_element_type=F32, precision=lax.Precision.HIGHEST)
    scores = _sigmoid(logits_t)
    idx, wts = _route(scores, scores + rb_ref[...])
    idx_ref[0] = idx
    wt_ref[0] = wts
    ei = lax.broadcasted_iota(jnp.int32, (N_EXPERTS, tm), 0)
    pick = jnp.zeros((N_EXPERTS, tm), F32)
    for kk in range(TOP_K):
        pick = jnp.where(ei == idx[kk:kk + 1, :], 1.0, pick)
    cnt_ref[0] = jnp.broadcast_to(jnp.sum(pick, axis=1, keepdims=True), (N_EXPERTS, LANES)).astype(jnp.int32)

    a = _dot(h2, sgu_ref[...])
    act = (_silu(a[:, 0:SHARED_FF]) * a[:, SHARED_FF:2 * SHARED_FF]).astype(BF16)
    base_ref[0] = xm + g2 * _dot(act, sd_ref[...])


def _merge(x, mod, mod_row, oat, hf, hb, om, u, bc, gt, lw, tm):
    B, S, D = x.shape
    nt = S // tm
    hal = BF16_SUBLANES
    last_h = S // hal - 1
    tspec = lambda n: pl.BlockSpec((1, tm, n), lambda b, i: (b, i, 0))
    full = lambda a: pl.BlockSpec(a.shape, lambda b, i: (0,) * a.ndim)
    weights = [lw['mlstm_norm_g'], lw['conv_w'], lw['branch_gate_b'], lw['w_br_attn'], lw['w_br_mlstm'],
               lw['w_br_conv'], lw['w_out'], lw['norm2_g'], lw['router_wt'], lw['router_bias'],
               lw['sh_gu'], lw['sh_d']]
    in_specs = [tspec(D),
                pl.BlockSpec((1, 6, D), lambda b, i: (mod_row(b), 0, 0)),
                pl.BlockSpec((1, tm // ATT_BLOCK, ATT_GROUP * ATT_BLOCK, LANES), lambda b, i: (b, i, 0, 0)),
                tspec(M_OUT), tspec(M_OUT), tspec(M_OUT), full(weights[0]),
                tspec(CONV_WIDTH),
                pl.BlockSpec((1, hal, CONV_WIDTH), lambda b, i: (b, jnp.maximum(i * (tm // hal) - 1, 0), 0)),
                pl.BlockSpec((1, hal, CONV_WIDTH), lambda b, i: (b, jnp.minimum((i + 1) * (tm // hal), last_h), 0)),
                tspec(CONV_WIDTH), full(weights[1]), tspec(N_BRANCH * D), full(weights[2])]
    in_specs += [full(w) for w in weights[3:]]
    tr = lambda n, dt: (jax.ShapeDtypeStruct((B, n, S), dt), pl.BlockSpec((1, n, tm), lambda b, i: (b, 0, i)))
    outs = [(jax.ShapeDtypeStruct((B, S, D), F32), tspec(D)),
            (jax.ShapeDtypeStruct((B, S, D), BF16), tspec(D)),
            tr(TOP_K, jnp.int32), tr(TOP_K, F32),
            (jax.ShapeDtypeStruct((B * nt, N_EXPERTS, LANES), jnp.int32),
             pl.BlockSpec((1, N_EXPERTS, LANES), lambda b, i: (b * nt + i, 0, 0)))]
    return pl.pallas_call(
        _merge_kernel,
        grid=(B, nt),
        in_specs=in_specs,
        out_specs=[o[1] for o in outs],
        out_shape=[o[0] for o in outs],
        compiler_params=_params(("parallel", "parallel")),
        name="merge_route",
    )(x, mod, oat, hf, hb, om, weights[0], u, u, u, bc, weights[1], gt, weights[2], *weights[3:])


MOE_TILE = 256
CHUNK = BF16_SUBLANES
GROUP_CHUNKS = 64
GROUP_ROWS = GROUP_CHUNKS * CHUNK
TILE_ROWS = MOE_TILE * TOP_K + N_EXPERTS * CHUNK
TILE_CHUNKS = TILE_ROWS // CHUNK
N_PAD_CHUNKS = N_EXPERTS * (GROUP_CHUNKS - 1)
N_SPARE_CHUNKS = 2 * TILE_CHUNKS
DMA_UNROLL = 8


def _chunk_copy(src, src_chunk, dst, dst_chunk, sem):
    return pltpu.make_async_copy(src.at[pl.ds(pl.multiple_of(src_chunk * CHUNK, CHUNK), CHUNK)],
                                 dst.at[pl.ds(pl.multiple_of(dst_chunk * CHUNK, CHUNK), CHUNK)], sem)


def _dispatch_tile(h, idx, wts):
    tm = h.shape[0]
    ei = lax.broadcasted_iota(jnp.int32, (N_EXPERTS, tm), 0)
    pick = jnp.zeros((N_EXPERTS, tm), F32)
    wmat = jnp.zeros((N_EXPERTS, tm), F32)
    for kk in range(TOP_K):
        chosen = ei == idx[kk:kk + 1, :]
        pick = jnp.where(chosen, 1.0, pick)
        wmat = jnp.where(chosen, wts[kk:kk + 1, :], wmat)
    t0 = lax.broadcasted_iota(jnp.int32, (tm, tm), 0)
    t1 = lax.broadcasted_iota(jnp.int32, (tm, tm), 1)
    rank = _dot(pick.astype(BF16), jnp.where(t0 < t1, 1.0, 0.0).astype(BF16))
    n_e = jnp.sum(pick, axis=1, keepdims=True)
    n_pad = jnp.floor((n_e + (CHUNK - 1)) * (1.0 / CHUNK)) * CHUNK
    e0 = lax.broadcasted_iota(jnp.int32, (N_EXPERTS, N_EXPERTS), 0)
    e1 = lax.broadcasted_iota(jnp.int32, (N_EXPERTS, N_EXPERTS), 1)
    seg = _dot(jnp.where(e1 < e0, 1.0, 0.0).astype(BF16),
               jnp.broadcast_to(n_pad, (N_EXPERTS, tm)).astype(BF16))
    posmat = seg + rank
    chunk_of = jnp.floor(posmat * (1.0 / CHUNK))
    offs_of = posmat - chunk_of * CHUNK
    chunk_row = jnp.where(pick > 0.0, chunk_of * CHUNK, -float(CHUNK))
    eye = e0 == e1
    to_row = lambda col: jnp.sum(jnp.where(eye, col, 0.0), axis=0, keepdims=True)
    seg_row = to_row(seg[:, 0:1])
    end_row = to_row(seg[:, 0:1] + n_pad)
    seg_row2 = jnp.concatenate([seg_row, seg_row], axis=1)
    end_row2 = jnp.concatenate([end_row, end_row], axis=1)
    r128 = lax.broadcasted_iota(jnp.int32, (TILE_ROWS, 2 * N_EXPERTS), 0).astype(F32)
    own2 = jnp.where(r128 >= seg_row2, jnp.where(r128 < end_row2, 1.0, 0.0), 0.0).astype(BF16)
    row_of = _dot(own2, jnp.concatenate([chunk_row, offs_of], axis=0).astype(BF16))
    w_of = _dot(own2[:, 0:N_EXPERTS], wmat.astype(BF16))
    riota = lax.broadcasted_iota(jnp.int32, (TILE_ROWS, tm), 0).astype(F32)
    hit = row_of == riota
    xg = _dot(jnp.where(hit, 1.0, 0.0).astype(BF16), h).astype(BF16)
    return xg, jnp.where(hit, w_of, 0.0).astype(BF16)


def _dispatch_kernel(pos_ref, pad_ref, h_ref, idx_ref, wt_ref, xs_ref, pw_ref, buf0, buf1, zero_buf, sems, pad_sem,
                     *, n_tiles):
    j = pl.program_id(0)
    bufs = (buf0, buf1)

    def compute(p):
        xg, pw = _dispatch_tile(h_ref[0], idx_ref[0], wt_ref[0])
        bufs[p][...] = xg
        pw_ref[...] = pw

    def issue(tile, p):
        for c in range(TILE_CHUNKS):
            _chunk_copy(bufs[p], c, xs_ref, pos_ref[tile * TILE_CHUNKS + c], sems.at[p]).start()

    def wait(p):
        pltpu.make_async_copy(bufs[p], xs_ref.at[pl.ds(0, TILE_ROWS)], sems.at[p]).wait()

    for p in range(2):
        @pl.when((j >= 2) & (j % 2 == p))
        def _(p=p):
            wait(p)

        @pl.when((j >= 1) & (j < n_tiles) & (j % 2 == p))
        def _(p=p):
            issue(j - 1, 1 - p)
            compute(p)

    @pl.when(j == 0)
    def _():
        compute(0)

    @pl.when(j == n_tiles)
    def _():
        last = (n_tiles - 1) % 2
        issue(n_tiles - 1, last)
        zero_buf[...] = jnp.zeros_like(zero_buf)

        def pad_expert(e, carry):
            def pad_issue(c, inner):
                _chunk_copy(zero_buf, 0, xs_ref, pad_ref[e] + c, pad_sem).start()
                return inner
            lax.fori_loop(0, pad_ref[N_EXPERTS + e], pad_issue, 0)

            def pad_wait(c, inner):
                _chunk_copy(zero_buf, 0, xs_ref, pad_ref[e] + c, pad_sem).wait()
                return inner
            lax.fori_loop(0, pad_ref[N_EXPERTS + e], pad_wait, 0)
            return carry
        lax.fori_loop(0, N_EXPERTS, pad_expert, 0)
        wait(last)


def _dispatch(h2, idx, wts, pos, pad_pos, n_slots):
    B, S, D = h2.shape
    tm = MOE_TILE
    nt = S // tm
    n_tiles = B * nt
    tile = lambda j: jnp.minimum(j, n_tiles - 1)
    grid_spec = pltpu.PrefetchScalarGridSpec(
        num_scalar_prefetch=2,
        grid=(n_tiles + 1,),
        in_specs=[pl.BlockSpec((1, tm, D), lambda j, pos, pad: (tile(j) // nt, tile(j) % nt, 0)),
                  pl.BlockSpec((1, TOP_K, tm), lambda j, pos, pad: (tile(j) // nt, 0, tile(j) % nt)),
                  pl.BlockSpec((1, TOP_K, tm), lambda j, pos, pad: (tile(j) // nt, 0, tile(j) % nt))],
        out_specs=[pl.BlockSpec(memory_space=pl.ANY),
                   pl.BlockSpec((TILE_ROWS, tm), lambda j, pos, pad: (tile(j), 0))],
        scratch_shapes=[pltpu.VMEM((TILE_ROWS, D), BF16),
                        pltpu.VMEM((TILE_ROWS, D), BF16),
                        pltpu.VMEM((CHUNK, D), BF16),
                        pltpu.SemaphoreType.DMA((2,)),
                        pltpu.SemaphoreType.DMA(())],
    )
    return pl.pallas_call(
        functools.partial(_dispatch_kernel, n_tiles=n_tiles),
        grid_spec=grid_spec,
        out_shape=[jax.ShapeDtypeStruct(((n_slots + N_SPARE_CHUNKS) * CHUNK, D), BF16),
                   jax.ShapeDtypeStruct((n_tiles * TILE_ROWS, tm), BF16)],
        compiler_params=_params(("arbitrary",)),
        name="moe_dispatch",
    )(pos, pad_pos, h2, idx, wts)


def _moe_tables(cnt, g_max):
    nt = cnt.shape[0]
    cc = (cnt + (CHUNK - 1)) // CHUNK
    segblk = jnp.cumsum(cc, axis=1) - cc
    tile_chunks = jnp.sum(cc, axis=1)
    prior = jnp.cumsum(cc, axis=0) - cc
    ge_cnt = (jnp.sum(cc, axis=0) + (GROUP_CHUNKS - 1)) // GROUP_CHUNKS
    gbase = jnp.cumsum(ge_cnt) - ge_cnt
    n_groups = jnp.sum(ge_cnt)
    c = jnp.arange(TILE_CHUNKS, dtype=jnp.int32)
    e_of = jnp.sum(((segblk + cc)[:, None, :] <= c[None, :, None]).astype(jnp.int32), axis=-1)
    e_of = jnp.minimum(e_of, N_EXPERTS - 1)
    seg_base = gbase[None, :] * GROUP_CHUNKS + prior - segblk
    onehot = e_of[:, :, None] == jnp.arange(N_EXPERTS, dtype=jnp.int32)[None, None, :]
    pos = jnp.sum(jnp.where(onehot, seg_base[:, None, :], 0), axis=-1) + c[None, :]
    valid = c[None, :] < tile_chunks[:, None]
    n_slots = g_max * GROUP_CHUNKS
    parity = (jnp.arange(nt, dtype=jnp.int32) % 2)[:, None]
    pos_write = jnp.where(valid, pos, n_slots + parity * TILE_CHUNKS + c[None, :]).astype(jnp.int32)
    ce = jnp.sum(cc, axis=0)
    pad_pos = jnp.concatenate([gbase * GROUP_CHUNKS + ce, ge_cnt * GROUP_CHUNKS - ce]).astype(jnp.int32)
    g = jnp.arange(g_max, dtype=jnp.int32)
    grp_e = jnp.minimum(jnp.sum(((gbase + ge_cnt)[None, :] <= g[:, None]).astype(jnp.int32), axis=1),
                        N_EXPERTS - 1).astype(jnp.int32)
    hi = lax.Precision.HIGHEST
    experts = jnp.arange(N_EXPERTS, dtype=jnp.int32)
    slot = jnp.arange(n_slots, dtype=jnp.int32)
    e_slot = jnp.minimum(jnp.sum((((gbase + ge_cnt) * GROUP_CHUNKS)[None, :] <= slot[:, None]).astype(jnp.int32),
                                 axis=1), N_EXPERTS - 1)
    oh_e = (e_slot[:, None] == experts[None, :]).astype(F32)
    q = slot - jnp.sum(oh_e * (gbase * GROUP_CHUNKS).astype(F32)[None, :], axis=1).astype(jnp.int32)
    slot_valid = q < jnp.sum(oh_e * ce.astype(F32)[None, :], axis=1).astype(jnp.int32)
    cum_end = jnp.dot(oh_e, (prior + cc).astype(F32).T, precision=hi)
    t_slot = jnp.minimum(jnp.sum((cum_end <= q[:, None].astype(F32)).astype(jnp.int32), axis=1), nt - 1)
    oh_t = (t_slot[:, None] == jnp.arange(nt, dtype=jnp.int32)[None, :]).astype(F32)
    shift = jnp.sum(jnp.dot(oh_t, (segblk - prior).astype(F32), precision=hi) * oh_e, axis=1).astype(jnp.int32)
    spare = nt * TILE_CHUNKS + ((slot // GROUP_CHUNKS) % 2) * GROUP_CHUNKS + slot % GROUP_CHUNKS
    inv = jnp.where(slot_valid, t_slot * TILE_CHUNKS + q + shift, spare).astype(jnp.int32)
    tiles = jnp.arange(nt, dtype=jnp.int32)
    fill = jnp.concatenate([tiles * TILE_CHUNKS + tile_chunks, TILE_CHUNKS - tile_chunks]).astype(jnp.int32)
    return (pos_write.reshape(-1), inv, pad_pos.reshape(-1), fill, grp_e,
            n_groups.reshape(1).astype(jnp.int32))


def _ffn_kernel(ge_ref, na_ref, inv_ref, fill_ref, x_ref, wg_ref, wu_ref, wd_ref, yt_ref,
                buf0, buf1, zero_buf, sems, fill_sem, *, n_tiles):
    g = pl.program_id(0)
    na = na_ref[0]
    bufs = (buf0, buf1)

    def compute(p):
        x = x_ref[...]
        gate = _dot(x, wg_ref[0, 0].astype(BF16))
        up = _dot(x, wu_ref[0, 0].astype(BF16))
        act = (_silu(gate) * up).astype(BF16)
        bufs[p][...] = _dot(act, wd_ref[0, 0].astype(BF16)).astype(BF16)

    def issue(grp, p):
        for c in range(GROUP_CHUNKS):
            _chunk_copy(bufs[p], c, yt_ref, inv_ref[grp * GROUP_CHUNKS + c], sems.at[p]).start()

    def wait(p):
        pltpu.make_async_copy(bufs[p], yt_ref.at[pl.ds(0, GROUP_ROWS)], sems.at[p]).wait()

    @pl.when(g == 0)
    def _():
        zero_buf[...] = jnp.zeros_like(zero_buf)

        def fill_tile(t, carry):
            def fill_issue(c, inner):
                _chunk_copy(zero_buf, 0, yt_ref, fill_ref[t] + c, fill_sem).start()
                return inner
            lax.fori_loop(0, fill_ref[n_tiles + t], fill_issue, 0)

            def fill_wait(c, inner):
                _chunk_copy(zero_buf, 0, yt_ref, fill_ref[t] + c, fill_sem).wait()
                return inner
            lax.fori_loop(0, fill_ref[n_tiles + t], fill_wait, 0)
            return carry
        lax.fori_loop(0, n_tiles, fill_tile, 0)

    for p in range(2):
        @pl.when((g >= 2) & (g - 2 < na) & (g % 2 == p))
        def _(p=p):
            wait(p)

        @pl.when((g >= 1) & (g < na) & (g % 2 == p))
        def _(p=p):
            issue(g - 1, 1 - p)
            compute(p)

        @pl.when((g >= 1) & (g == na) & (g % 2 == p))
        def _(p=p):
            issue(g - 1, 1 - p)

    @pl.when((g == 0) & (na > 0))
    def _():
        compute(0)


def _ffn_grouped(xs, inv, fill, grp_e, n_groups, layer, w_gate, w_up, w_down, g_max, n_tiles):
    D = xs.shape[1]
    live = lambda g, na: jnp.minimum(g, jnp.maximum(na[0] - 1, 0))
    wspec = lambda shape: pl.BlockSpec((1, 1) + shape, lambda g, ge, na, inv, fill: (layer, ge[live(g, na)], 0, 0))
    grid_spec = pltpu.PrefetchScalarGridSpec(
        num_scalar_prefetch=4,
        grid=(g_max + 2,),
        in_specs=[pl.BlockSpec((GROUP_ROWS, D), lambda g, ge, na, inv, fill: (live(g, na), 0)),
                  wspec((D, EXPERT_FF)), wspec((D, EXPERT_FF)), wspec((EXPERT_FF, D))],
        out_specs=pl.BlockSpec(memory_space=pl.ANY),
        scratch_shapes=[pltpu.VMEM((GROUP_ROWS, D), BF16),
                        pltpu.VMEM((GROUP_ROWS, D), BF16),
                        pltpu.VMEM((CHUNK, D), BF16),
                        pltpu.SemaphoreType.DMA((2,)),
                        pltpu.SemaphoreType.DMA(())],
    )
    return pl.pallas_call(
        functools.partial(_ffn_kernel, n_tiles=n_tiles),
        grid_spec=grid_spec,
        out_shape=jax.ShapeDtypeStruct(((n_tiles * TILE_CHUNKS + 2 * GROUP_CHUNKS) * CHUNK, D), BF16),
        compiler_params=_params(("arbitrary",)),
        name="moe_ffn",
    )(grp_e, n_groups, inv, fill, xs, w_gate, w_up, w_down)


def _combine_kernel(yt_ref, pw_ref, base_ref, mod_ref, fg_ref, o_ref, *, final):
    routed = _dot_tn(pw_ref[...], yt_ref[...])
    out = base_ref[...] + mod_ref[0, 5:6, :] * routed
    if final:
        out = out * lax.rsqrt(jnp.mean(out * out, axis=-1, keepdims=True) + EPS) * fg_ref[...]
    o_ref[...] = out


def _combine(yt, pw, base, mod, mod_row, final_g):
    B, S, D = base.shape
    tm = MOE_TILE
    nt = S // tm
    final = final_g is not None
    fg = (final_g if final else jnp.ones((D,), F32)).reshape(1, D)
    out = pl.pallas_call(
        functools.partial(_combine_kernel, final=final),
        grid=(B * nt,),
        in_specs=[pl.BlockSpec((TILE_ROWS, D), lambda i: (i, 0)),
                  pl.BlockSpec((TILE_ROWS, tm), lambda i: (i, 0)),
                  pl.BlockSpec((tm, D), lambda i: (i, 0)),
                  pl.BlockSpec((1, 6, D), lambda i: (mod_row(i // nt), 0, 0)),
                  pl.BlockSpec((1, D), lambda i: (0, 0))],
        out_specs=pl.BlockSpec((tm, D), lambda i: (i, 0)),
        out_shape=jax.ShapeDtypeStruct((B * S, D), F32),
        compiler_params=_params(("parallel",)),
        name="moe_combine",
    )(yt, pw, base.reshape(B * S, D), mod, fg)
    return out.reshape(B, S, D)


def _moe_sparse(h2, idx, wts, cnt, base, mod, mod_row, layer, w_gate, w_up, w_down, final_g=None):
    B, S, D = h2.shape
    n_tiles = B * (S // MOE_TILE)
    g_max = (n_tiles * TILE_CHUNKS + N_PAD_CHUNKS + GROUP_CHUNKS - 1) // GROUP_CHUNKS
    pos_write, inv, pad_pos, fill, grp_e, n_groups = _moe_tables(cnt[:, :, 0], g_max)
    xs, pw = _dispatch(h2, idx, wts, pos_write, pad_pos, g_max * GROUP_CHUNKS)
    yt = _ffn_grouped(xs, inv, fill, grp_e, n_groups, layer, w_gate, w_up, w_down, g_max, n_tiles)
    return _combine(yt, pw, base, mod, mod_row, final_g)


def _zero_state(batch):
    nq = M_HEADS * M_QK_DIM
    return (jnp.zeros((batch, 2, nq, M_V_DIM), F32),
            jnp.zeros((batch, 2, 1, nq), F32),
            jnp.zeros((batch, 2, 1, LANES), F32))


def kernel(x, c, ctx, c_ctx, ada_w, ada_b, norm1_g, norm2_g, w_in, attn_sink, mlstm_gate_b, mlstm_norm_g, conv_w, w_br_attn, w_br_mlstm, w_br_conv, branch_gate_b, w_out, router_w, router_bias, exp_w_gate, exp_w_up, exp_w_down, sh_w_gate, sh_w_up, sh_w_down, final_g):
    B, S, D = x.shape
    L = ctx.shape[1]
    depth = ada_w.shape[0]
    ctx_row = B

    pad_rows = (-(B + 1)) % 8
    cc = jnp.concatenate([c, c_ctx[None, :], jnp.zeros((pad_rows, D), F32)], axis=0)
    mod_all = _ada(cc, ada_w, ada_b).reshape(depth, B + 1 + pad_rows, 6, D)

    cos_t, sin_t = _rope_tables(S)
    cos_c = jnp.ones((L, LANES), F32)
    sin_c = jnp.zeros((L, LANES), F32)
    col_idx = _proj_column_index()
    att_idx = _attn_row_index()
    lat_row = lambda b: b
    ctx_mod = lambda b: ctx_row

    xc = ctx
    for l in range(depth):
        need_ctx = l < depth - 1
        mod = mod_all[l]
        w_ext = jnp.concatenate([w_in[l], jnp.zeros((D, 1), F32)], axis=1)
        w_p = jnp.take(w_ext, col_idx, axis=1).astype(BF16)
        lw = {
            'mlstm_norm_g': mlstm_norm_g[l].reshape(1, M_OUT),
            'conv_w': conv_w[l],
            'branch_gate_b': branch_gate_b[l].reshape(1, N_BRANCH * D),
            'w_br_attn': jnp.take(w_br_attn[l], att_idx, axis=0).astype(BF16),
            'w_br_mlstm': w_br_mlstm[l].astype(BF16),
            'w_br_conv': w_br_conv[l].astype(BF16),
            'w_out': w_out[l].astype(BF16),
            'norm2_g': norm2_g[l].reshape(1, D),
            'router_wt': router_w[l].T,
            'router_bias': router_bias[l].reshape(N_EXPERTS, 1),
            'sh_gu': jnp.concatenate([sh_w_gate[l], sh_w_up[l]], axis=1).astype(BF16),
            'sh_d': sh_w_down[l].astype(BF16),
        }
        experts = (l, exp_w_gate, exp_w_up, exp_w_down)

        pc = _in_proj(xc, mod, ctx_mod, norm1_g[l], w_p, cos_c, sin_c, tm=256)
        p = _in_proj(x, mod, lat_row, norm1_g[l], w_p, cos_t, sin_t, tm=256)
        qs_c, k_c, v_c, qm_c, km_c, vm_c, om_c, gm_c, bc_c, u_c, gt_c = pc
        qs, k, v, qm, km, vm, om, gm, bc, u, gt = p

        oat = _attention(qs, k, v, k_c, v_c, attn_sink[l], band=True)
        hf_c, hb_c, st = _mlstm(qm_c, km_c, vm_c, gm_c, mlstm_gate_b[l], _zero_state(B))
        hf, hb, _ = _mlstm(qm, km, vm, gm, mlstm_gate_b[l], st)

        base, h2, idx, wts, cnt = _merge(x, mod, lat_row, oat, hf, hb, om, u, bc, gt, lw, tm=MOE_TILE)
        x_new = _moe_sparse(h2, idx, wts, cnt, base, mod, lat_row, *experts,
                            final_g=final_g if l == depth - 1 else None)

        if need_ctx:
            oat_c = _attention(qs_c, None, None, k_c, v_c, attn_sink[l], band=False)
            base_c, h2_c, idx_c, wts_c, cnt_c = _merge(xc, mod, ctx_mod, oat_c, hf_c, hb_c, om_c, u_c, bc_c, gt_c, lw,
                                                       tm=MOE_TILE)
            xc = _moe_sparse(h2_c, idx_c, wts_c, cnt_c, base_c, mod, ctx_mod, *experts)
        x = x_new
    return x
```

```python
import functools

import numpy as np
import jax
import jax.numpy as jnp
from jax import lax
from jax.experimental import pallas as pl
from jax.experimental.pallas import tpu as pltpu

F32 = jnp.float32
BF16 = jnp.bfloat16

D_MODEL = 1024
GRID_W = 64
EPS = 1e-6
ATT_HEADS = 8
ATT_KV_HEADS = 2
ATT_HEAD_DIM = 64
ATT_GROUP = ATT_HEADS // ATT_KV_HEADS
ATT_BLOCK = 128
ATT_OUT = ATT_HEADS * ATT_HEAD_DIM
ROPE_BASE = 10000.0
M_HEADS = 4
M_QK_DIM = 64
M_V_DIM = 128
M_CHUNK = 64
M_OUT = M_HEADS * M_V_DIM
CONV_WIDTH = 512
N_BRANCH = 3
N_EXPERTS = 64
N_GROUPS = 8
GROUP_SIZE = N_EXPERTS // N_GROUPS
TOPK_GROUPS = 4
TOP_K = 8
EXPERT_FF = 256
SHARED_FF = 256
ROUTED_SCALE = 2.5

LOG2E = 1.4426950408889634
LANES = 128
BF16_SUBLANES = 16
VMEM_LIMIT = 56 * 1024 * 1024

_SEGS = (('q', 512), ('k', 128), ('v', 128), ('qm', 256), ('km', 256), ('vm', 512), ('om', 512),
         ('bc', 512), ('cc', 512), ('xc', 512), ('gt', 3072), ('gm', 128))
_OFF = {}
_o = 0
for _n, _s in _SEGS:
    _OFF[_n] = (_o, _o + _s)
    _o += _s
N_PROJ = _o
D_IN = 6928


def _proj_column_index():
    idx = []
    half = ATT_HEAD_DIM // 2
    for hh in range(ATT_GROUP):
        for g in range(ATT_KV_HEADS):
            head = g * ATT_GROUP + hh
            for par in range(2):
                idx += [head * ATT_HEAD_DIM + 2 * i + par for i in range(half)]
    for g in range(ATT_KV_HEADS):
        for par in range(2):
            idx += [512 + g * ATT_HEAD_DIM + 2 * i + par for i in range(half)]
    idx += list(range(640, 768))
    idx += list(range(768, 2304))
    idx += list(range(2320, 3856))
    idx += list(range(3856, 6928))
    idx += list(range(2304, 2320)) + [D_IN] * (LANES - 16)
    assert len(idx) == N_PROJ
    return np.asarray(idx, np.int32)


def _attn_row_index():
    idx = []
    for hh in range(ATT_GROUP):
        for g in range(ATT_KV_HEADS):
            head = g * ATT_GROUP + hh
            idx += [head * ATT_HEAD_DIM + d for d in range(ATT_HEAD_DIM)]
    return np.asarray(idx, np.int32)


def _rope_tables(seq):
    rows = seq // GRID_W
    row = jnp.repeat(jnp.arange(rows, dtype=F32), GRID_W)
    col = jnp.tile(jnp.arange(GRID_W, dtype=F32), rows)
    n_pairs = ATT_HEAD_DIM // 4
    inv_freq = ROPE_BASE ** (-jnp.arange(n_pairs, dtype=F32) / n_pairs)
    ang = jnp.concatenate([row[:, None] * inv_freq, col[:, None] * inv_freq], axis=-1)
    c, s = jnp.cos(ang), jnp.sin(ang)
    cos_t = jnp.concatenate([c, c, c, c], axis=-1)
    sin_t = jnp.concatenate([-s, s, -s, s], axis=-1)
    return cos_t, sin_t


def _dot(a, b):
    return jnp.dot(a, b, preferred_element_type=F32)


def _dot_nt(a, b):
    return lax.dot_general(a, b, (((1,), (1,)), ((), ())), preferred_element_type=F32)


def _dot_tn(a, b):
    return lax.dot_general(a, b, (((0,), (0,)), ((), ())), preferred_element_type=F32)


def _split_hi_lo(w):
    hi = w.astype(BF16)
    lo = (w - hi.astype(F32)).astype(BF16)
    return jnp.concatenate([hi, lo], axis=0)


def _sigmoid(x):
    return 1.0 / (1.0 + jnp.exp(-x))


def _sigmoid_tanh(x):
    return 0.5 * jnp.tanh(0.5 * x) + 0.5


def _silu(x):
    return x * _sigmoid(x)


def _log_sigmoid(x):
    return jnp.minimum(x, 0.0) - jnp.log(1.0 + jnp.exp(-jnp.abs(x)))


def _rms_mod(x, g, shift, scale):
    y = x * lax.rsqrt(jnp.mean(x * x, axis=-1, keepdims=True) + EPS) * g
    return y * (1.0 + scale) + shift


def _params(sem):
    return pltpu.CompilerParams(dimension_semantics=sem, vmem_limit_bytes=VMEM_LIMIT)


def _ada_kernel(c_ref, w_ref, b_ref, o_ref):
    s = _silu(c_ref[...])
    o_ref[0] = jnp.dot(s, w_ref[0], preferred_element_type=F32,
                       precision=lax.Precision.HIGHEST) + b_ref[0]


def _ada(cc, ada_w, ada_b):
    depth, d, n = ada_w.shape
    rows = cc.shape[0]
    tn = 1536
    return pl.pallas_call(
        _ada_kernel,
        grid=(depth, n // tn),
        in_specs=[pl.BlockSpec((rows, d), lambda l, j: (0, 0)),
                  pl.BlockSpec((1, d, tn), lambda l, j: (l, 0, j)),
                  pl.BlockSpec((1, 1, tn), lambda l, j: (l, 0, j))],
        out_specs=pl.BlockSpec((1, rows, tn), lambda l, j: (l, 0, j)),
        out_shape=jax.ShapeDtypeStruct((depth, rows, n), F32),
        compiler_params=_params(("parallel", "parallel")),
        name="ada_mod",
    )(cc, ada_w, ada_b.reshape(depth, 1, n))


def _swap_halves(x):
    lane = lax.broadcasted_iota(jnp.int32, x.shape, 1)
    first = (lane % ATT_HEAD_DIM) < (ATT_HEAD_DIM // 2)
    return jnp.where(first, pltpu.roll(x, LANES - 32, axis=1), pltpu.roll(x, 32, axis=1))


def _in_kernel(x_ref, mod_ref, g_ref, w_ref, cos_ref, sin_ref,
               qs_ref, k_ref, v_ref, qm_ref, km_ref, vm_ref, om_ref, gm_ref, bc_ref, u_ref, gt_ref):
    tm = x_ref.shape[1]
    h = _rms_mod(x_ref[0], g_ref[...], mod_ref[0, 0:1, :], mod_ref[0, 1:2, :]).astype(BF16)

    def proj(name):
        lo, hi = _OFF[name]
        return _dot(h, w_ref[:, lo:hi])

    cos_t = cos_ref[...]
    sin_t = sin_ref[...]

    def rope(t):
        return t * cos_t + _swap_halves(t) * sin_t

    q = proj('q')
    scale = ATT_HEAD_DIM ** -0.5 * LOG2E
    for hh in range(ATT_GROUP):
        r = (rope(q[:, hh * LANES:(hh + 1) * LANES]) * scale).astype(BF16)
        for qb in range(tm // ATT_BLOCK):
            qs_ref[0, qb, hh * ATT_BLOCK:(hh + 1) * ATT_BLOCK, :] = r[qb * ATT_BLOCK:(qb + 1) * ATT_BLOCK, :]
    k_ref[0] = rope(proj('k')).astype(BF16)
    v_ref[0] = proj('v').astype(BF16)
    qm_ref[0] = proj('qm').astype(BF16)
    km_ref[0] = proj('km').astype(BF16)
    vm_ref[0] = proj('vm').astype(BF16)
    om_ref[0] = proj('om').astype(BF16)
    gm_ref[0] = proj('gm')
    bc_ref[0] = proj('bc').astype(BF16)
    u_ref[0] = (proj('cc') * proj('xc')).astype(BF16)
    gt_ref[0] = proj('gt').astype(BF16)


def _in_proj(x, mod, mod_row, norm_g, w_p, cos_t, sin_t, tm):
    B, S, D = x.shape
    nb = S // ATT_BLOCK
    tok = lambda n, dt: jax.ShapeDtypeStruct((B, S, n), dt)
    tspec = lambda n: pl.BlockSpec((1, tm, n), lambda b, i: (b, i, 0))
    out_shape = (jax.ShapeDtypeStruct((B, nb, ATT_GROUP * ATT_BLOCK, LANES), BF16),
                 tok(128, BF16), tok(128, BF16), tok(256, BF16), tok(256, BF16), tok(512, BF16),
                 tok(512, BF16), tok(128, F32), tok(512, BF16), tok(512, BF16), tok(3072, BF16))
    out_specs = (pl.BlockSpec((1, tm // ATT_BLOCK, ATT_GROUP * ATT_BLOCK, LANES), lambda b, i: (b, i, 0, 0)),
                 tspec(128), tspec(128), tspec(256), tspec(256), tspec(512), tspec(512), tspec(128),
                 tspec(512), tspec(512), tspec(3072))
    return pl.pallas_call(
        _in_kernel,
        grid=(B, S // tm),
        in_specs=[pl.BlockSpec((1, tm, D), lambda b, i: (b, i, 0)),
                  pl.BlockSpec((1, 6, D), lambda b, i: (mod_row(b), 0, 0)),
                  pl.BlockSpec((1, D), lambda b, i: (0, 0)),
                  pl.BlockSpec((D, N_PROJ), lambda b, i: (0, 0), pipeline_mode=pl.Buffered(1)),
                  pl.BlockSpec((tm, LANES), lambda b, i: (i, 0)),
                  pl.BlockSpec((tm, LANES), lambda b, i: (i, 0))],
        out_specs=out_specs,
        out_shape=out_shape,
        compiler_params=_params(("parallel", "parallel")),
        name="in_proj",
    )(x, mod, norm_g.reshape(1, D), w_p, cos_t, sin_t)


def _attn_kernel(sink_ref, qs_ref, kc_ref, vc_ref, *rest, band):
    if band:
        kp_ref, kcur_ref, kn_ref, vp_ref, vcur_ref, vn_ref, o_ref = rest
    else:
        (o_ref,) = rest
    j = pl.program_id(1)
    nblk = pl.num_programs(1)
    q = qs_ref[0, 0]
    rows = q.shape[0]
    if band:
        kcat = jnp.concatenate([kc_ref[0], kp_ref[0], kcur_ref[0], kn_ref[0]], axis=0)
        vcat = jnp.concatenate([vc_ref[0], vp_ref[0], vcur_ref[0], vn_ref[0]], axis=0)
    else:
        kcat = kc_ref[0]
        vcat = vc_ref[0]
    nkeys = kcat.shape[0]
    lc = kc_ref.shape[1]
    lane = lax.broadcasted_iota(jnp.int32, (1, LANES), 1)
    t = lax.broadcasted_iota(jnp.int32, (rows, 1), 0) % ATT_BLOCK
    hh = lax.broadcasted_iota(jnp.int32, (rows, 1), 0) // ATT_BLOCK
    if band:
        i = lax.broadcasted_iota(jnp.int32, (1, ATT_BLOCK), 1)
        ok_prev = i >= t + jnp.where(j > 0, 0, 2 * ATT_BLOCK)
        ok_next = i <= t - jnp.where(j < nblk - 1, 0, 2 * ATT_BLOCK)
    out = jnp.zeros((rows, LANES), F32)
    for g in range(ATT_KV_HEADS):
        lm = (lane < ATT_HEAD_DIM) if g == 0 else (lane >= ATT_HEAD_DIM)
        kz = jnp.where(lm, kcat, jnp.zeros_like(kcat))
        ones_lane = ATT_HEAD_DIM if g == 0 else 0
        vz = jnp.where(lm, vcat, jnp.where(lane == ones_lane, 1.0, 0.0).astype(BF16))
        s = _dot_nt(q, kz)
        if band:
            s = jnp.concatenate(
                [s[:, 0:lc], jnp.where(ok_prev, s[:, lc:lc + ATT_BLOCK], -jnp.inf),
                 s[:, lc + ATT_BLOCK:lc + 2 * ATT_BLOCK],
                 jnp.where(ok_next, s[:, lc + 2 * ATT_BLOCK:lc + 3 * ATT_BLOCK], -jnp.inf)], axis=1)
        sink = jnp.zeros((rows, 1), F32)
        for a in range(ATT_GROUP):
            sink = jnp.where(hh == a, sink_ref[g * ATT_GROUP + a] * LOG2E, sink)
        m = jnp.maximum(jnp.max(s, axis=-1, keepdims=True), sink)
        p = jnp.exp2((s - m).astype(BF16))
        pv = _dot(p, vz)
        l = pv[:, ones_lane:ones_lane + 1] + jnp.exp2(sink - m)
        out = out + jnp.where(lm, pv, 0.0) / l
    o_ref[0, 0] = out.astype(BF16)


def _attention(qs, k, v, kc, vc, sink, band):
    B, nb = qs.shape[:2]
    lc = kc.shape[1]
    last = nb - 1
    qspec = pl.BlockSpec((1, 1, ATT_GROUP * ATT_BLOCK, LANES), lambda b, j: (b, j, 0, 0))
    cspec = pl.BlockSpec((1, lc, LANES), lambda b, j: (b, 0, 0))
    in_specs = [pl.BlockSpec(memory_space=pltpu.SMEM), qspec, cspec, cspec]
    args = [sink.astype(F32), qs, kc, vc]
    if band:
        prev = pl.BlockSpec((1, ATT_BLOCK, LANES), lambda b, j: (b, jnp.maximum(j - 1, 0), 0))
        cur = pl.BlockSpec((1, ATT_BLOCK, LANES), lambda b, j: (b, j, 0))
        nxt = pl.BlockSpec((1, ATT_BLOCK, LANES), lambda b, j: (b, jnp.minimum(j + 1, last), 0))
        in_specs += [prev, cur, nxt, prev, cur, nxt]
        args += [k, k, k, v, v, v]
    return pl.pallas_call(
        functools.partial(_attn_kernel, band=band),
        grid=(B, nb),
        in_specs=in_specs,
        out_specs=qspec,
        out_shape=jax.ShapeDtypeStruct(qs.shape, BF16),
        compiler_params=_params(("parallel", "parallel")),
        name="attention_band" if band else "attention_ctx",
    )(*args)


def _mlstm_step(dirs, T):
    kscale = M_QK_DIM ** -0.5
    si = lax.broadcasted_iota(jnp.int32, (T, T), 0)
    ri = lax.broadcasted_iota(jnp.int32, (T, T), 1)
    lane_qk = lax.broadcasted_iota(jnp.int32, (1, M_HEADS * M_QK_DIM), 1) // M_QK_DIM
    lane_m = lax.broadcasted_iota(jnp.int32, (1, LANES), 1)
    row_c = lax.broadcasted_iota(jnp.int32, (M_HEADS * M_QK_DIM, 1), 0) // M_QK_DIM
    combos = [(d, hd) for d in range(2) for hd in range(M_HEADS)]

    tri, bcol, gt, bt, blast = [], [], [], [], []
    for d, (q, k, v, g, C, n, m) in enumerate(dirs):
        t = (ri <= si) if d == 0 else (ri >= si)
        tri.append(t)
        lf = _log_sigmoid(g)
        bc = jnp.dot(t.astype(F32), lf, preferred_element_type=F32, precision=lax.Precision.HIGHEST)
        bcol.append(bc)
        gt.append(g.T)
        bt.append(bc.T)
        blast.append(bc[T - 1:T, :] if d == 0 else bc[0:1, :])

    def lanes(d, hd):
        return (2 * d) * M_HEADS + hd, (2 * d + 1) * M_HEADS + hd

    b_col = {c: bcol[c[0]][:, lanes(*c)[1]:lanes(*c)[1] + 1] for c in combos}
    ig_col = {c: dirs[c[0]][3][:, lanes(*c)[0]:lanes(*c)[0] + 1] for c in combos}
    alpha = {c: gt[c[0]][lanes(*c)[0]:lanes(*c)[0] + 1, :] - bt[c[0]][lanes(*c)[1]:lanes(*c)[1] + 1, :]
             for c in combos}
    m_old = {c: dirs[c[0]][6][:, c[1]:c[1] + 1] for c in combos}
    b_last = {c: blast[c[0]][:, lanes(*c)[1]:lanes(*c)[1] + 1] for c in combos}
    hmask = {hd: lane_qk == hd for hd in range(M_HEADS)}

    a_mat = {c: jnp.where(tri[c[0]], alpha[c], -jnp.inf) for c in combos}
    a_max = {c: jnp.max(a_mat[c], axis=1, keepdims=True) for c in combos}
    a_int = {c: b_col[c] + m_old[c] for c in combos}
    m_s = {c: jnp.maximum(a_int[c], b_col[c] + a_max[c]) for c in combos}
    w_int = {c: jnp.exp(a_int[c] - m_s[c]) for c in combos}
    w_mat = {c: jnp.exp(a_mat[c] + (b_col[c] - m_s[c])) for c in combos}
    qmask = {c: jnp.where(hmask[c[1]], dirs[c[0]][0], jnp.zeros_like(dirs[c[0]][0])) for c in combos}
    s_qk = {c: w_mat[c] * (_dot_nt(qmask[c], dirs[c[0]][1]) * kscale) for c in combos}
    vh = {c: dirs[c[0]][2][:, c[1] * M_V_DIM:(c[1] + 1) * M_V_DIM] for c in combos}
    c_bf = [dirs[d][4].astype(BF16) for d in range(2)]
    num = {c: _dot(s_qk[c].astype(BF16), vh[c]) + w_int[c] * _dot(qmask[c], c_bf[c[0]]) for c in combos}
    qn_all = [dirs[d][0].astype(F32) * dirs[d][5] for d in range(2)]
    qn = {c: jnp.sum(jnp.where(hmask[c[1]], qn_all[c[0]], 0.0), axis=1, keepdims=True) for c in combos}
    den = {c: jnp.sum(s_qk[c], axis=1, keepdims=True) + w_int[c] * qn[c] for c in combos}
    h = {c: num[c] / jnp.maximum(jnp.abs(den[c]), jnp.exp(-m_s[c])) for c in combos}

    r_col = {c: b_last[c] - b_col[c] + ig_col[c] for c in combos}
    m_new = {c: jnp.maximum(b_last[c] + m_old[c], jnp.max(r_col[c], axis=0, keepdims=True)) for c in combos}
    decay = {c: jnp.exp(b_last[c] + m_old[c] - m_new[c]) for c in combos}
    w_r = {c: jnp.exp(r_col[c] - m_new[c]) for c in combos}

    outs = []
    for d, (q, k, v, g, C, n, m) in enumerate(dirs):
        w_lanes = jnp.zeros((T, M_HEADS * M_QK_DIM), F32)
        dec_lanes = jnp.zeros((1, M_HEADS * M_QK_DIM), F32)
        dec_rows = jnp.zeros((M_HEADS * M_QK_DIM, 1), F32)
        m_row = jnp.zeros((1, LANES), F32)
        for hd in range(M_HEADS):
            w_lanes = jnp.where(hmask[hd], w_r[(d, hd)], w_lanes)
            dec_lanes = jnp.where(hmask[hd], decay[(d, hd)], dec_lanes)
            dec_rows = jnp.where(row_c == hd, decay[(d, hd)], dec_rows)
            m_row = jnp.where(lane_m == hd, m_new[(d, hd)], m_row)
        kw = k.astype(F32) * (w_lanes * kscale)
        kwt = kw.T.astype(BF16)
        upd = jnp.concatenate(
            [_dot(kwt[hd * M_QK_DIM:(hd + 1) * M_QK_DIM, :], vh[(d, hd)]) for hd in range(M_HEADS)], axis=0)
        c_new = dec_rows * C + upd
        n_new = dec_lanes * n + jnp.sum(kw, axis=0, keepdims=True)
        h_all = jnp.concatenate([h[(d, hd)] for hd in range(M_HEADS)], axis=1)
        outs.append((h_all, c_new, n_new, m_row))
    return outs


def _mlstm_kernel(gb_ref, qf_ref, kf_ref, vf_ref, gf_ref, qb_ref, kb_ref, vb_ref, gbk_ref,
                  c0_ref, n0_ref, m0_ref, hf_ref, hb_ref, cf_ref, nf_ref, mf_ref,
                  c_s, n_s, m_s):
    ci = pl.program_id(1)
    T = qf_ref.shape[1]

    @pl.when(ci == 0)
    def _():
        c_s[...] = c0_ref[0]
        n_s[...] = n0_ref[0]
        m_s[...] = m0_ref[0]

    gb = gb_ref[...]
    dirs = [(q_ref[0], k_ref[0], v_ref[0], g_ref[0] + gb, c_s[d], n_s[d], m_s[d])
            for d, (q_ref, k_ref, v_ref, g_ref) in enumerate(
                ((qf_ref, kf_ref, vf_ref, gf_ref), (qb_ref, kb_ref, vb_ref, gbk_ref)))]
    outs = _mlstm_step(dirs, T)
    for d, h_ref in enumerate((hf_ref, hb_ref)):
        h_all, c_new, n_new, m_row = outs[d]
        h_ref[0] = h_all
        c_s[d] = c_new
        n_s[d] = n_new
        m_s[d] = m_row

    @pl.when(ci == pl.num_programs(1) - 1)
    def _():
        cf_ref[0] = c_s[...]
        nf_ref[0] = n_s[...]
        mf_ref[0] = m_s[...]


MLSTM_TILE = 128


def _mlstm(qm, km, vm, gm, gate_b, state):
    B, S, _ = qm.shape
    T = MLSTM_TILE
    nc = S // T
    nq = M_HEADS * M_QK_DIM
    fwd = lambda n: pl.BlockSpec((1, T, n), lambda b, c: (b, c, 0))
    bwd = lambda n: pl.BlockSpec((1, T, n), lambda b, c: (b, nc - 1 - c, 0))
    st_specs = [pl.BlockSpec((1, 2, nq, M_V_DIM), lambda b, c: (b, 0, 0, 0)),
                pl.BlockSpec((1, 2, 1, nq), lambda b, c: (b, 0, 0, 0)),
                pl.BlockSpec((1, 2, 1, LANES), lambda b, c: (b, 0, 0, 0))]
    st_shapes = [jax.ShapeDtypeStruct((B, 2, nq, M_V_DIM), F32),
                 jax.ShapeDtypeStruct((B, 2, 1, nq), F32),
                 jax.ShapeDtypeStruct((B, 2, 1, LANES), F32)]
    gb_row = jnp.pad(gate_b.reshape(1, -1).astype(F32), ((0, 0), (0, LANES - gate_b.size)))
    outs = pl.pallas_call(
        _mlstm_kernel,
        grid=(B, nc),
        in_specs=[pl.BlockSpec((1, LANES), lambda b, c: (0, 0)),
                  fwd(256), fwd(256), fwd(512), fwd(LANES),
                  bwd(256), bwd(256), bwd(512), bwd(LANES)] + st_specs,
        out_specs=[fwd(M_OUT), bwd(M_OUT)] + st_specs,
        out_shape=[jax.ShapeDtypeStruct((B, S, M_OUT), F32)] * 2 + st_shapes,
        scratch_shapes=[pltpu.VMEM((2, nq, M_V_DIM), F32),
                        pltpu.VMEM((2, 1, nq), F32),
                        pltpu.VMEM((2, 1, LANES), F32)],
        compiler_params=_params(("parallel", "arbitrary")),
        name="mlstm_scan",
    )(gb_row, qm, km, vm, gm, qm, km, vm, gm, *state)
    return outs[0], outs[1], tuple(outs[2:])


def _route(scores, sel):
    tm = scores.shape[1]
    gi8 = lax.broadcasted_iota(jnp.int32, (GROUP_SIZE, tm), 0)

    def stack_rows(rows):
        out = jnp.broadcast_to(rows[0], (len(rows), tm))
        for r, v in enumerate(rows[1:], start=1):
            out = jnp.where(gi8 == r, v, out)
        return out

    gs = []
    for g in range(N_GROUPS):
        blk = sel[g * GROUP_SIZE:(g + 1) * GROUP_SIZE, :]
        m1 = jnp.max(blk, axis=0, keepdims=True)
        first = jnp.min(jnp.where(blk == m1, gi8, GROUP_SIZE), axis=0, keepdims=True)
        m2 = jnp.max(jnp.where(gi8 == first, -jnp.inf, blk), axis=0, keepdims=True)
        gs.append(m1 + m2)
    gsc = stack_rows(gs)
    gsel = jnp.zeros((N_GROUPS, tm), F32)
    for _ in range(TOPK_GROUPS):
        mx = jnp.max(gsc, axis=0, keepdims=True)
        first = jnp.min(jnp.where(gsc == mx, gi8, N_GROUPS), axis=0, keepdims=True)
        pick = gi8 == first
        gsel = jnp.where(pick, 1.0, gsel)
        gsc = jnp.where(pick, -jnp.inf, gsc)
    cur = jnp.concatenate(
        [jnp.where(gsel[g:g + 1, :] > 0.0, sel[g * GROUP_SIZE:(g + 1) * GROUP_SIZE, :], -jnp.inf)
         for g in range(N_GROUPS)], axis=0)
    ei = lax.broadcasted_iota(jnp.int32, (N_EXPERTS, tm), 0)
    idx, wts = [], []
    for _ in range(TOP_K):
        mx = jnp.max(cur, axis=0, keepdims=True)
        first = jnp.min(jnp.where(cur == mx, ei, N_EXPERTS), axis=0, keepdims=True)
        pick = ei == first
        idx.append(first)
        wts.append(jnp.sum(jnp.where(pick, scores, 0.0), axis=0, keepdims=True))
        cur = jnp.where(pick, -jnp.inf, cur)
    tot = wts[0]
    for w in wts[1:]:
        tot = tot + w
    wts = [w / tot * ROUTED_SCALE for w in wts]
    return stack_rows(idx), stack_rows(wts)


def _merge_kernel(x_ref, mod_ref, oat_ref, hf_ref, hb_ref, om_ref, ng_ref, u_ref, up_ref, un_ref,
                  bc_ref, cw_ref, gt_ref, bgb_ref, wa_ref, wm_ref, wc_ref, wo_ref, n2_ref,
                  rw_ref, rb_ref, sgu_ref, sd_ref,
                  base_ref, h2_ref, idx_ref, wt_ref, cnt_ref):
    i = pl.program_id(1)
    tm = x_ref.shape[1]
    x = x_ref[0]
    g1 = mod_ref[0, 2:3, :]
    sh2 = mod_ref[0, 3:4, :]
    sc2 = mod_ref[0, 4:5, :]
    g2 = mod_ref[0, 5:6, :]

    ya = jnp.concatenate(
        [jnp.concatenate([oat_ref[0, qb, hh * ATT_BLOCK:(hh + 1) * ATT_BLOCK, :] for hh in range(ATT_GROUP)], axis=1)
         for qb in range(tm // ATT_BLOCK)], axis=0)

    hsum = hf_ref[0] + hb_ref[0]
    parts = []
    for hd in range(M_HEADS):
        hh_ = hsum[:, hd * M_V_DIM:(hd + 1) * M_V_DIM]
        parts.append(hh_ * lax.rsqrt(jnp.mean(hh_ * hh_, axis=-1, keepdims=True) + EPS))
    hn = jnp.concatenate(parts, axis=1) * ng_ref[...]
    ym = (_sigmoid_tanh(om_ref[0].astype(F32)) * hn).astype(BF16)

    u = u_ref[0].astype(F32)
    row = lax.broadcasted_iota(jnp.int32, (tm, 1), 0)
    has_prev = (i > 0).astype(F32)
    has_next = (i < pl.num_programs(1) - 1).astype(F32)
    prev_row = up_ref[0, BF16_SUBLANES - 1:BF16_SUBLANES, :].astype(F32) * has_prev
    next_row = un_ref[0, 0:1, :].astype(F32) * has_next
    u_m1 = jnp.where(row == 0, prev_row, pltpu.roll(u, 1, axis=0))
    u_p1 = jnp.where(row == tm - 1, next_row, pltpu.roll(u, tm - 1, axis=0))
    conv = cw_ref[0:1, :] * u_m1 + cw_ref[1:2, :] * u + cw_ref[2:3, :] * u_p1
    yc = (bc_ref[0].astype(F32) * conv).astype(BF16)

    gg = _sigmoid_tanh(gt_ref[0].astype(F32) + bgb_ref[...])
    ymix = (gg[:, 0:D_MODEL] * _dot(ya, wa_ref[...])
            + gg[:, D_MODEL:2 * D_MODEL] * _dot(ym, wm_ref[...])
            + gg[:, 2 * D_MODEL:3 * D_MODEL] * _dot(yc, wc_ref[...]))
    y = _dot(ymix.astype(BF16), wo_ref[...])
    xm = x + g1 * y

    h2f = _rms_mod(xm, n2_ref[...], sh2, sc2)
    h2 = h2f.astype(BF16)
    h2_ref[0] = h2

    h2_lo = (h2f - h2.astype(F32)).astype(BF16)
    rw = rw_ref[...]
    part = _dot_nt(rw, h2)
    logits_t = part[0:N_EXPERTS, :] + part[N_EXPERTS:2 * N_EXPERTS, :] + _dot_nt(rw[0:N_EXPERTS, :], h2_lo)
    scores = _sigmoid(logits_t)
    idx, wts = _route(scores, scores + rb_ref[...])
    idx_ref[0] = idx
    wt_ref[0] = wts
    ei = lax.broadcasted_iota(jnp.int32, (N_EXPERTS, tm), 0)
    pick = jnp.zeros((N_EXPERTS, tm), F32)
    for kk in range(TOP_K):
        pick = jnp.where(ei == idx[kk:kk + 1, :], 1.0, pick)
    cnt_ref[0] = jnp.broadcast_to(jnp.sum(pick, axis=1, keepdims=True), (N_EXPERTS, LANES)).astype(jnp.int32)

    a = _dot(h2, sgu_ref[...])
    act = (_silu(a[:, 0:SHARED_FF]) * a[:, SHARED_FF:2 * SHARED_FF]).astype(BF16)
    base_ref[0] = xm + g2 * _dot(act, sd_ref[...])


def _merge(x, mod, mod_row, oat, hf, hb, om, u, bc, gt, lw, tm):
    B, S, D = x.shape
    nt = S // tm
    hal = BF16_SUBLANES
    last_h = S // hal - 1
    tspec = lambda n: pl.BlockSpec((1, tm, n), lambda b, i: (b, i, 0))
    full = lambda a: pl.BlockSpec(a.shape, lambda b, i: (0,) * a.ndim)
    weights = [lw['mlstm_norm_g'], lw['conv_w'], lw['branch_gate_b'], lw['w_br_attn'], lw['w_br_mlstm'],
               lw['w_br_conv'], lw['w_out'], lw['norm2_g'], lw['router_wt'], lw['router_bias'],
               lw['sh_gu'], lw['sh_d']]
    in_specs = [tspec(D),
                pl.BlockSpec((1, 6, D), lambda b, i: (mod_row(b), 0, 0)),
                pl.BlockSpec((1, tm // ATT_BLOCK, ATT_GROUP * ATT_BLOCK, LANES), lambda b, i: (b, i, 0, 0)),
                tspec(M_OUT), tspec(M_OUT), tspec(M_OUT), full(weights[0]),
                tspec(CONV_WIDTH),
                pl.BlockSpec((1, hal, CONV_WIDTH), lambda b, i: (b, jnp.maximum(i * (tm // hal) - 1, 0), 0)),
                pl.BlockSpec((1, hal, CONV_WIDTH), lambda b, i: (b, jnp.minimum((i + 1) * (tm // hal), last_h), 0)),
                tspec(CONV_WIDTH), full(weights[1]), tspec(N_BRANCH * D), full(weights[2])]
    in_specs += [full(w) for w in weights[3:]]
    tr = lambda n, dt: (jax.ShapeDtypeStruct((B, n, S), dt), pl.BlockSpec((1, n, tm), lambda b, i: (b, 0, i)))
    outs = [(jax.ShapeDtypeStruct((B, S, D), F32), tspec(D)),
            (jax.ShapeDtypeStruct((B, S, D), BF16), tspec(D)),
            tr(TOP_K, jnp.int32), tr(TOP_K, F32),
            (jax.ShapeDtypeStruct((B * nt, N_EXPERTS, LANES), jnp.int32),
             pl.BlockSpec((1, N_EXPERTS, LANES), lambda b, i: (b * nt + i, 0, 0)))]
    return pl.pallas_call(
        _merge_kernel,
        grid=(B, nt),
        in_specs=in_specs,
        out_specs=[o[1] for o in outs],
        out_shape=[o[0] for o in outs],
        compiler_params=_params(("parallel", "parallel")),
        name="merge_route",
    )(x, mod, oat, hf, hb, om, weights[0], u, u, u, bc, weights[1], gt, weights[2], *weights[3:])


MOE_TILE = 256
CHUNK = BF16_SUBLANES
GROUP_CHUNKS = 64
GROUP_ROWS = GROUP_CHUNKS * CHUNK
TILE_ROWS = MOE_TILE * TOP_K + N_EXPERTS * CHUNK
TILE_CHUNKS = TILE_ROWS // CHUNK
N_PAD_CHUNKS = N_EXPERTS * (GROUP_CHUNKS - 1)
N_SPARE_CHUNKS = 2 * TILE_CHUNKS


def _chunk_copy(src, src_chunk, dst, dst_chunk, sem):
    return pltpu.make_async_copy(src.at[pl.ds(pl.multiple_of(src_chunk * CHUNK, CHUNK), CHUNK)],
                                 dst.at[pl.ds(pl.multiple_of(dst_chunk * CHUNK, CHUNK), CHUNK)], sem)


def _dispatch_tile(h, idx, wts):
    tm = h.shape[0]
    ei = lax.broadcasted_iota(jnp.int32, (N_EXPERTS, tm), 0)
    pick = jnp.zeros((N_EXPERTS, tm), F32)
    wmat = jnp.zeros((N_EXPERTS, tm), F32)
    for kk in range(TOP_K):
        chosen = ei == idx[kk:kk + 1, :]
        pick = jnp.where(chosen, 1.0, pick)
        wmat = jnp.where(chosen, wts[kk:kk + 1, :], wmat)
    t0 = lax.broadcasted_iota(jnp.int32, (tm, tm), 0)
    t1 = lax.broadcasted_iota(jnp.int32, (tm, tm), 1)
    rank = _dot(pick.astype(BF16), jnp.where(t0 < t1, 1.0, 0.0).astype(BF16))
    n_e = jnp.sum(pick, axis=1, keepdims=True)
    n_pad = jnp.floor((n_e + (CHUNK - 1)) * (1.0 / CHUNK)) * CHUNK
    e0 = lax.broadcasted_iota(jnp.int32, (N_EXPERTS, N_EXPERTS), 0)
    e1 = lax.broadcasted_iota(jnp.int32, (N_EXPERTS, N_EXPERTS), 1)
    seg = _dot(jnp.where(e1 < e0, 1.0, 0.0).astype(BF16),
               jnp.broadcast_to(n_pad, (N_EXPERTS, tm)).astype(BF16))
    posmat = seg + rank
    chunk_of = jnp.floor(posmat * (1.0 / CHUNK))
    offs_of = posmat - chunk_of * CHUNK
    chunk_row = jnp.where(pick > 0.0, chunk_of * CHUNK, -float(CHUNK))
    eye = e0 == e1
    to_row = lambda col: jnp.sum(jnp.where(eye, col, 0.0), axis=0, keepdims=True)
    seg_row = to_row(seg[:, 0:1])
    end_row = to_row(seg[:, 0:1] + n_pad)
    seg_row2 = jnp.concatenate([seg_row, seg_row], axis=1)
    end_row2 = jnp.concatenate([end_row, end_row], axis=1)
    r128 = lax.broadcasted_iota(jnp.int32, (TILE_ROWS, 2 * N_EXPERTS), 0).astype(F32)
    own2 = jnp.where(r128 >= seg_row2, jnp.where(r128 < end_row2, 1.0, 0.0), 0.0).astype(BF16)
    row_of = _dot(own2, jnp.concatenate([chunk_row, offs_of], axis=0).astype(BF16))
    w_of = _dot(own2[:, 0:N_EXPERTS], wmat.astype(BF16))
    riota = lax.broadcasted_iota(jnp.int32, (TILE_ROWS, tm), 0).astype(F32)
    hit = row_of == riota
    xg = _dot(jnp.where(hit, 1.0, 0.0).astype(BF16), h).astype(BF16)
    return xg, jnp.where(hit, w_of, 0.0).astype(BF16)


def _dispatch_kernel(pos_ref, pad_ref, h_ref, idx_ref, wt_ref, xs_ref, pw_ref, buf0, buf1, zero_buf, sems, pad_sem,
                     *, n_tiles):
    j = pl.program_id(0)
    bufs = (buf0, buf1)

    def compute(p):
        xg, pw = _dispatch_tile(h_ref[0], idx_ref[0], wt_ref[0])
        bufs[p][...] = xg
        pw_ref[...] = pw

    def issue(tile, p):
        for c in range(TILE_CHUNKS):
            _chunk_copy(bufs[p], c, xs_ref, pos_ref[tile * TILE_CHUNKS + c], sems.at[p]).start()

    def wait(p):
        pltpu.make_async_copy(bufs[p], xs_ref.at[pl.ds(0, TILE_ROWS)], sems.at[p]).wait()

    for p in range(2):
        @pl.when((j >= 2) & (j % 2 == p))
        def _(p=p):
            wait(p)

        @pl.when((j >= 1) & (j < n_tiles) & (j % 2 == p))
        def _(p=p):
            issue(j - 1, 1 - p)
            compute(p)

    @pl.when(j == 0)
    def _():
        compute(0)

    @pl.when(j == n_tiles)
    def _():
        last = (n_tiles - 1) % 2
        issue(n_tiles - 1, last)
        zero_buf[...] = jnp.zeros_like(zero_buf)

        def pad_expert(e, carry):
            def pad_issue(c, inner):
                _chunk_copy(zero_buf, 0, xs_ref, pad_ref[e] + c, pad_sem).start()
                return inner
            lax.fori_loop(0, pad_ref[N_EXPERTS + e], pad_issue, 0)

            def pad_wait(c, inner):
                _chunk_copy(zero_buf, 0, xs_ref, pad_ref[e] + c, pad_sem).wait()
                return inner
            lax.fori_loop(0, pad_ref[N_EXPERTS + e], pad_wait, 0)
            return carry
        lax.fori_loop(0, N_EXPERTS, pad_expert, 0)
        wait(last)


def _dispatch(h2, idx, wts, pos, pad_pos, n_slots):
    B, S, D = h2.shape
    tm = MOE_TILE
    nt = S // tm
    n_tiles = B * nt
    tile = lambda j: jnp.minimum(j, n_tiles - 1)
    grid_spec = pltpu.PrefetchScalarGridSpec(
        num_scalar_prefetch=2,
        grid=(n_tiles + 1,),
        in_specs=[pl.BlockSpec((1, tm, D), lambda j, pos, pad: (tile(j) // nt, tile(j) % nt, 0)),
                  pl.BlockSpec((1, TOP_K, tm), lambda j, pos, pad: (tile(j) // nt, 0, tile(j) % nt)),
                  pl.BlockSpec((1, TOP_K, tm), lambda j, pos, pad: (tile(j) // nt, 0, tile(j) % nt))],
        out_specs=[pl.BlockSpec(memory_space=pl.ANY),
                   pl.BlockSpec((TILE_ROWS, tm), lambda j, pos, pad: (tile(j), 0))],
        scratch_shapes=[pltpu.VMEM((TILE_ROWS, D), BF16),
                        pltpu.VMEM((TILE_ROWS, D), BF16),
                        pltpu.VMEM((CHUNK, D), BF16),
                        pltpu.SemaphoreType.DMA((2,)),
                        pltpu.SemaphoreType.DMA(())],
    )
    return pl.pallas_call(
        functools.partial(_dispatch_kernel, n_tiles=n_tiles),
        grid_spec=grid_spec,
        out_shape=[jax.ShapeDtypeStruct(((n_slots + N_SPARE_CHUNKS) * CHUNK, D), BF16),
                   jax.ShapeDtypeStruct((n_tiles * TILE_ROWS, tm), BF16)],
        compiler_params=_params(("arbitrary",)),
        name="moe_dispatch",
    )(pos, pad_pos, h2, idx, wts)


def _moe_tables(cnt, g_max):
    nt = cnt.shape[0]
    cc = (cnt + (CHUNK - 1)) // CHUNK
    segblk = jnp.cumsum(cc, axis=1) - cc
    tile_chunks = jnp.sum(cc, axis=1)
    prior = jnp.cumsum(cc, axis=0) - cc
    ge_cnt = (jnp.sum(cc, axis=0) + (GROUP_CHUNKS - 1)) // GROUP_CHUNKS
    gbase = jnp.cumsum(ge_cnt) - ge_cnt
    n_groups = jnp.sum(ge_cnt)
    c = jnp.arange(TILE_CHUNKS, dtype=jnp.int32)
    e_of = jnp.sum(((segblk + cc)[:, None, :] <= c[None, :, None]).astype(jnp.int32), axis=-1)
    e_of = jnp.minimum(e_of, N_EXPERTS - 1)
    seg_base = gbase[None, :] * GROUP_CHUNKS + prior - segblk
    onehot = e_of[:, :, None] == jnp.arange(N_EXPERTS, dtype=jnp.int32)[None, None, :]
    pos = jnp.sum(jnp.where(onehot, seg_base[:, None, :], 0), axis=-1) + c[None, :]
    valid = c[None, :] < tile_chunks[:, None]
    n_slots = g_max * GROUP_CHUNKS
    parity = (jnp.arange(nt, dtype=jnp.int32) % 2)[:, None]
    pos_write = jnp.where(valid, pos, n_slots + parity * TILE_CHUNKS + c[None, :]).astype(jnp.int32)
    ce = jnp.sum(cc, axis=0)
    pad_pos = jnp.concatenate([gbase * GROUP_CHUNKS + ce, ge_cnt * GROUP_CHUNKS - ce]).astype(jnp.int32)
    g = jnp.arange(g_max, dtype=jnp.int32)
    grp_e = jnp.minimum(jnp.sum(((gbase + ge_cnt)[None, :] <= g[:, None]).astype(jnp.int32), axis=1),
                        N_EXPERTS - 1).astype(jnp.int32)
    hi = lax.Precision.HIGHEST
    experts = jnp.arange(N_EXPERTS, dtype=jnp.int32)
    slot = jnp.arange(n_slots, dtype=jnp.int32)
    e_slot = jnp.minimum(jnp.sum((((gbase + ge_cnt) * GROUP_CHUNKS)[None, :] <= slot[:, None]).astype(jnp.int32),
                                 axis=1), N_EXPERTS - 1)
    oh_e = (e_slot[:, None] == experts[None, :]).astype(F32)
    q = slot - jnp.sum(oh_e * (gbase * GROUP_CHUNKS).astype(F32)[None, :], axis=1).astype(jnp.int32)
    slot_valid = q < jnp.sum(oh_e * ce.astype(F32)[None, :], axis=1).astype(jnp.int32)
    cum_end = jnp.dot(oh_e, (prior + cc).astype(F32).T, precision=hi)
    t_slot = jnp.minimum(jnp.sum((cum_end <= q[:, None].astype(F32)).astype(jnp.int32), axis=1), nt - 1)
    oh_t = (t_slot[:, None] == jnp.arange(nt, dtype=jnp.int32)[None, :]).astype(F32)
    shift = jnp.sum(jnp.dot(oh_t, (segblk - prior).astype(F32), precision=hi) * oh_e, axis=1).astype(jnp.int32)
    spare = nt * TILE_CHUNKS + ((slot // GROUP_CHUNKS) % 2) * GROUP_CHUNKS + slot % GROUP_CHUNKS
    inv = jnp.where(slot_valid, t_slot * TILE_CHUNKS + q + shift, spare).astype(jnp.int32)
    tiles = jnp.arange(nt, dtype=jnp.int32)
    fill = jnp.concatenate([tiles * TILE_CHUNKS + tile_chunks, TILE_CHUNKS - tile_chunks]).astype(jnp.int32)
    return (pos_write.reshape(-1), inv, pad_pos.reshape(-1), fill, grp_e,
            n_groups.reshape(1).astype(jnp.int32))


def _ffn_kernel(ge_ref, na_ref, inv_ref, fill_ref, x_ref, wg_ref, wu_ref, wd_ref, yt_ref,
                buf0, buf1, zero_buf, sems, fill_sem, *, n_tiles):
    g = pl.program_id(0)
    na = na_ref[0]
    bufs = (buf0, buf1)

    def compute(p):
        x = x_ref[...]
        gate = _dot(x, wg_ref[0, 0].astype(BF16))
        up = _dot(x, wu_ref[0, 0].astype(BF16))
        act = (_silu(gate) * up).astype(BF16)
        bufs[p][...] = _dot(act, wd_ref[0, 0].astype(BF16)).astype(BF16)

    def issue(grp, p):
        for c in range(GROUP_CHUNKS):
            _chunk_copy(bufs[p], c, yt_ref, inv_ref[grp * GROUP_CHUNKS + c], sems.at[p]).start()

    def wait(p):
        pltpu.make_async_copy(bufs[p], yt_ref.at[pl.ds(0, GROUP_ROWS)], sems.at[p]).wait()

    @pl.when(g == 0)
    def _():
        zero_buf[...] = jnp.zeros_like(zero_buf)

        def fill_tile(t, carry):
            def fill_issue(c, inner):
                _chunk_copy(zero_buf, 0, yt_ref, fill_ref[t] + c, fill_sem).start()
                return inner
            lax.fori_loop(0, fill_ref[n_tiles + t], fill_issue, 0)

            def fill_wait(c, inner):
                _chunk_copy(zero_buf, 0, yt_ref, fill_ref[t] + c, fill_sem).wait()
                return inner
            lax.fori_loop(0, fill_ref[n_tiles + t], fill_wait, 0)
            return carry
        lax.fori_loop(0, n_tiles, fill_tile, 0)

    for p in range(2):
        @pl.when((g >= 2) & (g - 2 < na) & (g % 2 == p))
        def _(p=p):
            wait(p)

        @pl.when((g >= 1) & (g < na) & (g % 2 == p))
        def _(p=p):
            issue(g - 1, 1 - p)
            compute(p)

        @pl.when((g >= 1) & (g == na) & (g % 2 == p))
        def _(p=p):
            issue(g - 1, 1 - p)

    @pl.when((g == 0) & (na > 0))
    def _():
        compute(0)


def _ffn_grouped(xs, inv, fill, grp_e, n_groups, layer, w_gate, w_up, w_down, g_max, n_tiles):
    D = xs.shape[1]
    live = lambda g, na: jnp.minimum(g, jnp.maximum(na[0] - 1, 0))
    wspec = lambda shape: pl.BlockSpec((1, 1) + shape, lambda g, ge, na, inv, fill: (layer, ge[live(g, na)], 0, 0))
    grid_spec = pltpu.PrefetchScalarGridSpec(
        num_scalar_prefetch=4,
        grid=(g_max + 2,),
        in_specs=[pl.BlockSpec((GROUP_ROWS, D), lambda g, ge, na, inv, fill: (live(g, na), 0)),
                  wspec((D, EXPERT_FF)), wspec((D, EXPERT_FF)), wspec((EXPERT_FF, D))],
        out_specs=pl.BlockSpec(memory_space=pl.ANY),
        scratch_shapes=[pltpu.VMEM((GROUP_ROWS, D), BF16),
                        pltpu.VMEM((GROUP_ROWS, D), BF16),
                        pltpu.VMEM((CHUNK, D), BF16),
                        pltpu.SemaphoreType.DMA((2,)),
                        pltpu.SemaphoreType.DMA(())],
    )
    return pl.pallas_call(
        functools.partial(_ffn_kernel, n_tiles=n_tiles),
        grid_spec=grid_spec,
        out_shape=jax.ShapeDtypeStruct(((n_tiles * TILE_CHUNKS + 2 * GROUP_CHUNKS) * CHUNK, D), BF16),
        compiler_params=_params(("arbitrary",)),
        name="moe_ffn",
    )(grp_e, n_groups, inv, fill, xs, w_gate, w_up, w_down)


def _combine_kernel(yt_ref, pw_ref, base_ref, mod_ref, fg_ref, o_ref, *, final):
    routed = _dot_tn(pw_ref[...], yt_ref[...])
    out = base_ref[...] + mod_ref[0, 5:6, :] * routed
    if final:
        out = out * lax.rsqrt(jnp.mean(out * out, axis=-1, keepdims=True) + EPS) * fg_ref[...]
    o_ref[...] = out


def _combine(yt, pw, base, mod, mod_row, final_g):
    B, S, D = base.shape
    tm = MOE_TILE
    nt = S // tm
    final = final_g is not None
    fg = (final_g if final else jnp.ones((D,), F32)).reshape(1, D)
    out = pl.pallas_call(
        functools.partial(_combine_kernel, final=final),
        grid=(B * nt,),
        in_specs=[pl.BlockSpec((TILE_ROWS, D), lambda i: (i, 0)),
                  pl.BlockSpec((TILE_ROWS, tm), lambda i: (i, 0)),
                  pl.BlockSpec((tm, D), lambda i: (i, 0)),
                  pl.BlockSpec((1, 6, D), lambda i: (mod_row(i // nt), 0, 0)),
                  pl.BlockSpec((1, D), lambda i: (0, 0))],
        out_specs=pl.BlockSpec((tm, D), lambda i: (i, 0)),
        out_shape=jax.ShapeDtypeStruct((B * S, D), F32),
        compiler_params=_params(("parallel",)),
        name="moe_combine",
    )(yt, pw, base.reshape(B * S, D), mod, fg)
    return out.reshape(B, S, D)


def _moe_sparse(h2, idx, wts, cnt, base, mod, mod_row, layer, w_gate, w_up, w_down, final_g=None):
    B, S, D = h2.shape
    n_tiles = B * (S // MOE_TILE)
    g_max = (n_tiles * TILE_CHUNKS + N_PAD_CHUNKS + GROUP_CHUNKS - 1) // GROUP_CHUNKS
    pos_write, inv, pad_pos, fill, grp_e, n_groups = _moe_tables(cnt[:, :, 0], g_max)
    xs, pw = _dispatch(h2, idx, wts, pos_write, pad_pos, g_max * GROUP_CHUNKS)
    yt = _ffn_grouped(xs, inv, fill, grp_e, n_groups, layer, w_gate, w_up, w_down, g_max, n_tiles)
    return _combine(yt, pw, base, mod, mod_row, final_g)


def _zero_state(batch):
    nq = M_HEADS * M_QK_DIM
    return (jnp.zeros((batch, 2, nq, M_V_DIM), F32),
            jnp.zeros((batch, 2, 1, nq), F32),
            jnp.zeros((batch, 2, 1, LANES), F32))


def kernel(x, c, ctx, c_ctx, ada_w, ada_b, norm1_g, norm2_g, w_in, attn_sink, mlstm_gate_b, mlstm_norm_g, conv_w, w_br_attn, w_br_mlstm, w_br_conv, branch_gate_b, w_out, router_w, router_bias, exp_w_gate, exp_w_up, exp_w_down, sh_w_gate, sh_w_up, sh_w_down, final_g):
    B, S, D = x.shape
    L = ctx.shape[1]
    depth = ada_w.shape[0]
    ctx_row = B

    pad_rows = (-(B + 1)) % 8
    cc = jnp.concatenate([c, c_ctx[None, :], jnp.zeros((pad_rows, D), F32)], axis=0)
    mod_all = _ada(cc, ada_w, ada_b).reshape(depth, B + 1 + pad_rows, 6, D)

    cos_t, sin_t = _rope_tables(S)
    cos_c = jnp.ones((L, LANES), F32)
    sin_c = jnp.zeros((L, LANES), F32)
    col_idx = _proj_column_index()
    att_idx = _attn_row_index()
    lat_row = lambda b: b
    ctx_mod = lambda b: ctx_row

    xc = ctx
    for l in range(depth):
        need_ctx = l < depth - 1
        mod = mod_all[l]
        w_ext = jnp.concatenate([w_in[l], jnp.zeros((D, 1), F32)], axis=1)
        w_p = jnp.take(w_ext, col_idx, axis=1).astype(BF16)
        lw = {
            'mlstm_norm_g': mlstm_norm_g[l].reshape(1, M_OUT),
            'conv_w': conv_w[l],
            'branch_gate_b': branch_gate_b[l].reshape(1, N_BRANCH * D),
            'w_br_attn': jnp.take(w_br_attn[l], att_idx, axis=0).astype(BF16),
            'w_br_mlstm': w_br_mlstm[l].astype(BF16),
            'w_br_conv': w_br_conv[l].astype(BF16),
            'w_out': w_out[l].astype(BF16),
            'norm2_g': norm2_g[l].reshape(1, D),
            'router_wt': _split_hi_lo(router_w[l].T),
            'router_bias': router_bias[l].reshape(N_EXPERTS, 1),
            'sh_gu': jnp.concatenate([sh_w_gate[l], sh_w_up[l]], axis=1).astype(BF16),
            'sh_d': sh_w_down[l].astype(BF16),
        }
        experts = (l, exp_w_gate, exp_w_up, exp_w_down)

        pc = _in_proj(xc, mod, ctx_mod, norm1_g[l], w_p, cos_c, sin_c, tm=256)
        p = _in_proj(x, mod, lat_row, norm1_g[l], w_p, cos_t, sin_t, tm=min(512, S))
        qs_c, k_c, v_c, qm_c, km_c, vm_c, om_c, gm_c, bc_c, u_c, gt_c = pc
        qs, k, v, qm, km, vm, om, gm, bc, u, gt = p

        oat = _attention(qs, k, v, k_c, v_c, attn_sink[l], band=True)
        hf_c, hb_c, st = _mlstm(qm_c, km_c, vm_c, gm_c, mlstm_gate_b[l], _zero_state(B))
        hf, hb, _ = _mlstm(qm, km, vm, gm, mlstm_gate_b[l], st)

        base, h2, idx, wts, cnt = _merge(x, mod, lat_row, oat, hf, hb, om, u, bc, gt, lw, tm=MOE_TILE)
        x_new = _moe_sparse(h2, idx, wts, cnt, base, mod, lat_row, *experts,
                            final_g=final_g if l == depth - 1 else None)

        if need_ctx:
            oat_c = _attention(qs_c, None, None, k_c, v_c, attn_sink[l], band=False)
            base_c, h2_c, idx_c, wts_c, cnt_c = _merge(xc, mod, ctx_mod, oat_c, hf_c, hb_c, om_c, u_c, bc_c, gt_c, lw,
                                                       tm=MOE_TILE)
            xc = _moe_sparse(h2_c, idx_c, wts_c, cnt_c, base_c, mod, ctx_mod, *experts)
        x = x_new
    return x
```

```python
import functools

import numpy as np
import jax
import jax.numpy as jnp
from jax import lax
from jax.experimental import pallas as pl
from jax.experimental.pallas import tpu as pltpu

F32 = jnp.float32
BF16 = jnp.bfloat16

D_MODEL = 1024
GRID_W = 64
EPS = 1e-6
ATT_HEADS = 8
ATT_KV_HEADS = 2
ATT_HEAD_DIM = 64
ATT_GROUP = ATT_HEADS // ATT_KV_HEADS
ATT_BLOCK = 128
ATT_OUT = ATT_HEADS * ATT_HEAD_DIM
ROPE_BASE = 10000.0
M_HEADS = 4
M_QK_DIM = 64
M_V_DIM = 128
M_CHUNK = 64
M_OUT = M_HEADS * M_V_DIM
CONV_WIDTH = 512
N_BRANCH = 3
N_EXPERTS = 64
N_GROUPS = 8
GROUP_SIZE = N_EXPERTS // N_GROUPS
TOPK_GROUPS = 4
TOP_K = 8
EXPERT_FF = 256
SHARED_FF = 256
ROUTED_SCALE = 2.5

LOG2E = 1.4426950408889634
LANES = 128
BF16_SUBLANES = 16
VMEM_LIMIT = 56 * 1024 * 1024

_SEGS = (('q', 512), ('k', 128), ('v', 128), ('qm', 256), ('km', 256), ('vm', 512), ('om', 512),
         ('bc', 512), ('cc', 512), ('xc', 512), ('gt', 3072), ('gm', 128))
_OFF = {}
_o = 0
for _n, _s in _SEGS:
    _OFF[_n] = (_o, _o + _s)
    _o += _s
N_PROJ = _o
D_IN = 6928


def _proj_column_index():
    idx = []
    half = ATT_HEAD_DIM // 2
    for hh in range(ATT_GROUP):
        for g in range(ATT_KV_HEADS):
            head = g * ATT_GROUP + hh
            for par in range(2):
                idx += [head * ATT_HEAD_DIM + 2 * i + par for i in range(half)]
    for g in range(ATT_KV_HEADS):
        for par in range(2):
            idx += [512 + g * ATT_HEAD_DIM + 2 * i + par for i in range(half)]
    idx += list(range(640, 768))
    idx += list(range(768, 2304))
    idx += list(range(2320, 3856))
    idx += list(range(3856, 6928))
    idx += list(range(2304, 2320)) + [D_IN] * (LANES - 16)
    assert len(idx) == N_PROJ
    return np.asarray(idx, np.int32)


def _attn_row_index():
    idx = []
    for hh in range(ATT_GROUP):
        for g in range(ATT_KV_HEADS):
            head = g * ATT_GROUP + hh
            idx += [head * ATT_HEAD_DIM + d for d in range(ATT_HEAD_DIM)]
    return np.asarray(idx, np.int32)


def _rope_tables(seq):
    rows = seq // GRID_W
    row = jnp.repeat(jnp.arange(rows, dtype=F32), GRID_W)
    col = jnp.tile(jnp.arange(GRID_W, dtype=F32), rows)
    n_pairs = ATT_HEAD_DIM // 4
    inv_freq = ROPE_BASE ** (-jnp.arange(n_pairs, dtype=F32) / n_pairs)
    ang = jnp.concatenate([row[:, None] * inv_freq, col[:, None] * inv_freq], axis=-1)
    c, s = jnp.cos(ang), jnp.sin(ang)
    cos_t = jnp.concatenate([c, c, c, c], axis=-1)
    sin_t = jnp.concatenate([-s, s, -s, s], axis=-1)
    return cos_t, sin_t


def _dot(a, b):
    return jnp.dot(a, b, preferred_element_type=F32)


def _dot_nt(a, b):
    return lax.dot_general(a, b, (((1,), (1,)), ((), ())), preferred_element_type=F32)


def _dot_tn(a, b):
    return lax.dot_general(a, b, (((0,), (0,)), ((), ())), preferred_element_type=F32)


def _split_hi_lo(w):
    hi = w.astype(BF16)
    lo = (w - hi.astype(F32)).astype(BF16)
    return jnp.concatenate([hi, lo], axis=0)


def _sigmoid(x):
    return 1.0 / (1.0 + jnp.exp(-x))


def _sigmoid_tanh(x):
    return 0.5 * jnp.tanh(0.5 * x) + 0.5


def _silu(x):
    return x * _sigmoid(x)


def _log_sigmoid(x):
    return jnp.minimum(x, 0.0) - jnp.log(1.0 + jnp.exp(-jnp.abs(x)))


def _rms_mod(x, g, shift, scale):
    y = x * lax.rsqrt(jnp.mean(x * x, axis=-1, keepdims=True) + EPS) * g
    return y * (1.0 + scale) + shift


def _params(sem):
    return pltpu.CompilerParams(dimension_semantics=sem, vmem_limit_bytes=VMEM_LIMIT)


def _ada_kernel(c_ref, w_ref, b_ref, o_ref):
    s = _silu(c_ref[...])
    o_ref[0] = jnp.dot(s, w_ref[0], preferred_element_type=F32,
                       precision=lax.Precision.HIGHEST) + b_ref[0]


def _ada(cc, ada_w, ada_b):
    depth, d, n = ada_w.shape
    rows = cc.shape[0]
    tn = 1536
    return pl.pallas_call(
        _ada_kernel,
        grid=(depth, n // tn),
        in_specs=[pl.BlockSpec((rows, d), lambda l, j: (0, 0)),
                  pl.BlockSpec((1, d, tn), lambda l, j: (l, 0, j)),
                  pl.BlockSpec((1, 1, tn), lambda l, j: (l, 0, j))],
        out_specs=pl.BlockSpec((1, rows, tn), lambda l, j: (l, 0, j)),
        out_shape=jax.ShapeDtypeStruct((depth, rows, n), F32),
        compiler_params=_params(("parallel", "parallel")),
        name="ada_mod",
    )(cc, ada_w, ada_b.reshape(depth, 1, n))


def _swap_halves(x):
    lane = lax.broadcasted_iota(jnp.int32, x.shape, 1)
    first = (lane % ATT_HEAD_DIM) < (ATT_HEAD_DIM // 2)
    return jnp.where(first, pltpu.roll(x, LANES - 32, axis=1), pltpu.roll(x, 32, axis=1))


def _in_kernel(x_ref, mod_ref, g_ref, w_ref, cos_ref, sin_ref,
               qs_ref, k_ref, v_ref, qm_ref, km_ref, vm_ref, om_ref, gm_ref, bc_ref, u_ref, gt_ref):
    tm = x_ref.shape[1]
    h = _rms_mod(x_ref[0], g_ref[...], mod_ref[0, 0:1, :], mod_ref[0, 1:2, :]).astype(BF16)

    def proj(name):
        lo, hi = _OFF[name]
        return _dot(h, w_ref[:, lo:hi])

    cos_t = cos_ref[...]
    sin_t = sin_ref[...]

    def rope(t):
        return t * cos_t + _swap_halves(t) * sin_t

    q = proj('q')
    scale = ATT_HEAD_DIM ** -0.5 * LOG2E
    for hh in range(ATT_GROUP):
        r = (rope(q[:, hh * LANES:(hh + 1) * LANES]) * scale).astype(BF16)
        for qb in range(tm // ATT_BLOCK):
            qs_ref[0, qb, hh * ATT_BLOCK:(hh + 1) * ATT_BLOCK, :] = r[qb * ATT_BLOCK:(qb + 1) * ATT_BLOCK, :]
    k_ref[0] = rope(proj('k')).astype(BF16)
    v_ref[0] = proj('v').astype(BF16)
    qm_ref[0] = proj('qm').astype(BF16)
    km_ref[0] = proj('km').astype(BF16)
    vm_ref[0] = proj('vm').astype(BF16)
    om_ref[0] = proj('om').astype(BF16)
    gm_ref[0] = proj('gm')
    bc_ref[0] = proj('bc').astype(BF16)
    u_ref[0] = (proj('cc') * proj('xc')).astype(BF16)
    gt_ref[0] = proj('gt').astype(BF16)


def _in_proj(x, mod, mod_row, norm_g, w_p, cos_t, sin_t, tm):
    B, S, D = x.shape
    nb = S // ATT_BLOCK
    tok = lambda n, dt: jax.ShapeDtypeStruct((B, S, n), dt)
    tspec = lambda n: pl.BlockSpec((1, tm, n), lambda b, i: (b, i, 0))
    out_shape = (jax.ShapeDtypeStruct((B, nb, ATT_GROUP * ATT_BLOCK, LANES), BF16),
                 tok(128, BF16), tok(128, BF16), tok(256, BF16), tok(256, BF16), tok(512, BF16),
                 tok(512, BF16), tok(128, F32), tok(512, BF16), tok(512, BF16), tok(3072, BF16))
    out_specs = (pl.BlockSpec((1, tm // ATT_BLOCK, ATT_GROUP * ATT_BLOCK, LANES), lambda b, i: (b, i, 0, 0)),
                 tspec(128), tspec(128), tspec(256), tspec(256), tspec(512), tspec(512), tspec(128),
                 tspec(512), tspec(512), tspec(3072))
    return pl.pallas_call(
        _in_kernel,
        grid=(B, S // tm),
        in_specs=[pl.BlockSpec((1, tm, D), lambda b, i: (b, i, 0)),
                  pl.BlockSpec((1, 6, D), lambda b, i: (mod_row(b), 0, 0)),
                  pl.BlockSpec((1, D), lambda b, i: (0, 0)),
                  pl.BlockSpec((D, N_PROJ), lambda b, i: (0, 0), pipeline_mode=pl.Buffered(1)),
                  pl.BlockSpec((tm, LANES), lambda b, i: (i, 0)),
                  pl.BlockSpec((tm, LANES), lambda b, i: (i, 0))],
        out_specs=out_specs,
        out_shape=out_shape,
        compiler_params=_params(("parallel", "parallel")),
        name="in_proj",
    )(x, mod, norm_g.reshape(1, D), w_p, cos_t, sin_t)


def _attn_block(q, sink_ref, kcat, vcat, masks):
    rows = q.shape[0]
    lane = lax.broadcasted_iota(jnp.int32, (1, LANES), 1)
    hh = lax.broadcasted_iota(jnp.int32, (rows, 1), 0) // ATT_BLOCK
    out = jnp.zeros((rows, LANES), F32)
    for g in range(ATT_KV_HEADS):
        lm = (lane < ATT_HEAD_DIM) if g == 0 else (lane >= ATT_HEAD_DIM)
        kz = jnp.where(lm, kcat, jnp.zeros_like(kcat))
        ones_lane = ATT_HEAD_DIM if g == 0 else 0
        vz = jnp.where(lm, vcat, jnp.where(lane == ones_lane, 1.0, 0.0).astype(BF16))
        s = _dot_nt(q, kz)
        if any(mk is not None for mk in masks):
            s = jnp.concatenate(
                [s[:, n * ATT_BLOCK:(n + 1) * ATT_BLOCK] if mk is None
                 else jnp.where(mk, s[:, n * ATT_BLOCK:(n + 1) * ATT_BLOCK], -jnp.inf)
                 for n, mk in enumerate(masks)], axis=1)
        sink = jnp.zeros((rows, 1), F32)
        for a in range(ATT_GROUP):
            sink = jnp.where(hh == a, sink_ref[g * ATT_GROUP + a] * LOG2E, sink)
        m = jnp.maximum(jnp.max(s, axis=-1, keepdims=True), sink)
        p = jnp.exp2((s - m).astype(BF16))
        pv = _dot(p, vz)
        l = pv[:, ones_lane:ones_lane + 1] + jnp.exp2(sink - m)
        out = out + jnp.where(lm, pv, 0.0) / l
    return out.astype(BF16)


ATT_STEP_BLOCKS = 2


def _attn_kernel(sink_ref, qs_ref, kc_ref, vc_ref, *rest, band):
    if not band:
        (o_ref,) = rest
        n_ctx = kc_ref.shape[1] // ATT_BLOCK
        for sb in range(ATT_STEP_BLOCKS):
            o_ref[0, sb] = _attn_block(qs_ref[0, sb], sink_ref, kc_ref[0], vc_ref[0], [None] * n_ctx)
        return
    kp_ref, kcur_ref, kn_ref, vp_ref, vcur_ref, vn_ref, o_ref = rest
    j = pl.program_id(1)
    nstep = pl.num_programs(1)
    n_ctx = kc_ref.shape[1] // ATT_BLOCK
    rows = qs_ref.shape[2]
    t = lax.broadcasted_iota(jnp.int32, (rows, 1), 0) % ATT_BLOCK
    i = lax.broadcasted_iota(jnp.int32, (1, ATT_BLOCK), 1)
    below = i >= t
    above = i <= t
    first = i >= t + jnp.where(j > 0, 0, 2 * ATT_BLOCK)
    last = i <= t - jnp.where(j < nstep - 1, 0, 2 * ATT_BLOCK)
    kcur, vcur = kcur_ref[0], vcur_ref[0]
    keys = ((kp_ref[0], kcur), (kcur, kn_ref[0]))
    vals = ((vp_ref[0], vcur), (vcur, vn_ref[0]))
    masks = ((first, None, above), (below, None, last))
    for sb in range(ATT_STEP_BLOCKS):
        kcat = jnp.concatenate((kc_ref[0],) + keys[sb], axis=0)
        vcat = jnp.concatenate((vc_ref[0],) + vals[sb], axis=0)
        o_ref[0, sb] = _attn_block(qs_ref[0, sb], sink_ref, kcat, vcat, [None] * n_ctx + list(masks[sb]))


def _attention(qs, k, v, kc, vc, sink, band):
    B, nb = qs.shape[:2]
    lc = kc.shape[1]
    nstep = nb // ATT_STEP_BLOCKS
    last = nb - 1
    qspec = pl.BlockSpec((1, ATT_STEP_BLOCKS, ATT_GROUP * ATT_BLOCK, LANES), lambda b, j: (b, j, 0, 0))
    cspec = pl.BlockSpec((1, lc, LANES), lambda b, j: (b, 0, 0))
    in_specs = [pl.BlockSpec(memory_space=pltpu.SMEM), qspec, cspec, cspec]
    args = [sink.astype(F32), qs, kc, vc]
    if band:
        prev = pl.BlockSpec((1, ATT_BLOCK, LANES), lambda b, j: (b, jnp.maximum(ATT_STEP_BLOCKS * j - 1, 0), 0))
        cur = pl.BlockSpec((1, ATT_STEP_BLOCKS * ATT_BLOCK, LANES), lambda b, j: (b, j, 0))
        nxt = pl.BlockSpec((1, ATT_BLOCK, LANES),
                           lambda b, j: (b, jnp.minimum(ATT_STEP_BLOCKS * (j + 1), last), 0))
        in_specs += [prev, cur, nxt, prev, cur, nxt]
        args += [k, k, k, v, v, v]
    return pl.pallas_call(
        functools.partial(_attn_kernel, band=band),
        grid=(B, nstep),
        in_specs=in_specs,
        out_specs=qspec,
        out_shape=jax.ShapeDtypeStruct(qs.shape, BF16),
        compiler_params=_params(("parallel", "parallel")),
        name="attention_band" if band else "attention_ctx",
    )(*args)


def _mlstm_step(dirs, T):
    kscale = M_QK_DIM ** -0.5
    si = lax.broadcasted_iota(jnp.int32, (T, T), 0)
    ri = lax.broadcasted_iota(jnp.int32, (T, T), 1)
    lane_qk = lax.broadcasted_iota(jnp.int32, (1, M_HEADS * M_QK_DIM), 1) // M_QK_DIM
    lane_m = lax.broadcasted_iota(jnp.int32, (1, LANES), 1)
    row_c = lax.broadcasted_iota(jnp.int32, (M_HEADS * M_QK_DIM, 1), 0) // M_QK_DIM
    combos = [(d, hd) for d in range(2) for hd in range(M_HEADS)]

    tri, bcol, gt, bt, blast = [], [], [], [], []
    for d, (q, k, v, g, C, n, m) in enumerate(dirs):
        t = (ri <= si) if d == 0 else (ri >= si)
        tri.append(t)
        lf = _log_sigmoid(g)
        bc = jnp.dot(t.astype(F32), lf, preferred_element_type=F32, precision=lax.Precision.HIGHEST)
        bcol.append(bc)
        gt.append(g.T)
        bt.append(bc.T)
        blast.append(bc[T - 1:T, :] if d == 0 else bc[0:1, :])

    def lanes(d, hd):
        return (2 * d) * M_HEADS + hd, (2 * d + 1) * M_HEADS + hd

    b_col = {c: bcol[c[0]][:, lanes(*c)[1]:lanes(*c)[1] + 1] for c in combos}
    ig_col = {c: dirs[c[0]][3][:, lanes(*c)[0]:lanes(*c)[0] + 1] for c in combos}
    alpha = {c: gt[c[0]][lanes(*c)[0]:lanes(*c)[0] + 1, :] - bt[c[0]][lanes(*c)[1]:lanes(*c)[1] + 1, :]
             for c in combos}
    m_old = {c: dirs[c[0]][6][:, c[1]:c[1] + 1] for c in combos}
    b_last = {c: blast[c[0]][:, lanes(*c)[1]:lanes(*c)[1] + 1] for c in combos}
    hmask = {hd: lane_qk == hd for hd in range(M_HEADS)}

    a_mat = {c: jnp.where(tri[c[0]], alpha[c], -jnp.inf) for c in combos}
    a_max = {c: jnp.max(a_mat[c], axis=1, keepdims=True) for c in combos}
    a_int = {c: b_col[c] + m_old[c] for c in combos}
    m_s = {c: jnp.maximum(a_int[c], b_col[c] + a_max[c]) for c in combos}
    w_int = {c: jnp.exp(a_int[c] - m_s[c]) for c in combos}
    w_mat = {c: jnp.exp(a_mat[c] + (b_col[c] - m_s[c])) for c in combos}
    qmask = {c: jnp.where(hmask[c[1]], dirs[c[0]][0], jnp.zeros_like(dirs[c[0]][0])) for c in combos}
    s_qk = {c: w_mat[c] * (_dot_nt(qmask[c], dirs[c[0]][1]) * kscale) for c in combos}
    vh = {c: dirs[c[0]][2][:, c[1] * M_V_DIM:(c[1] + 1) * M_V_DIM] for c in combos}
    c_bf = [dirs[d][4].astype(BF16) for d in range(2)]
    num = {c: _dot(s_qk[c].astype(BF16), vh[c]) + w_int[c] * _dot(qmask[c], c_bf[c[0]]) for c in combos}
    qn_all = [dirs[d][0].astype(F32) * dirs[d][5] for d in range(2)]
    qn = {c: jnp.sum(jnp.where(hmask[c[1]], qn_all[c[0]], 0.0), axis=1, keepdims=True) for c in combos}
    den = {c: jnp.sum(s_qk[c], axis=1, keepdims=True) + w_int[c] * qn[c] for c in combos}
    h = {c: num[c] / jnp.maximum(jnp.abs(den[c]), jnp.exp(-m_s[c])) for c in combos}

    r_col = {c: b_last[c] - b_col[c] + ig_col[c] for c in combos}
    m_new = {c: jnp.maximum(b_last[c] + m_old[c], jnp.max(r_col[c], axis=0, keepdims=True)) for c in combos}
    decay = {c: jnp.exp(b_last[c] + m_old[c] - m_new[c]) for c in combos}
    w_r = {c: jnp.exp(r_col[c] - m_new[c]) for c in combos}

    outs = []
    for d, (q, k, v, g, C, n, m) in enumerate(dirs):
        w_lanes = jnp.zeros((T, M_HEADS * M_QK_DIM), F32)
        dec_lanes = jnp.zeros((1, M_HEADS * M_QK_DIM), F32)
        dec_rows = jnp.zeros((M_HEADS * M_QK_DIM, 1), F32)
        m_row = jnp.zeros((1, LANES), F32)
        for hd in range(M_HEADS):
            w_lanes = jnp.where(hmask[hd], w_r[(d, hd)], w_lanes)
            dec_lanes = jnp.where(hmask[hd], decay[(d, hd)], dec_lanes)
            dec_rows = jnp.where(row_c == hd, decay[(d, hd)], dec_rows)
            m_row = jnp.where(lane_m == hd, m_new[(d, hd)], m_row)
        kw = k.astype(F32) * (w_lanes * kscale)
        kwt = kw.T.astype(BF16)
        upd = jnp.concatenate(
            [_dot(kwt[hd * M_QK_DIM:(hd + 1) * M_QK_DIM, :], vh[(d, hd)]) for hd in range(M_HEADS)], axis=0)
        c_new = dec_rows * C + upd
        n_new = dec_lanes * n + jnp.sum(kw, axis=0, keepdims=True)
        h_all = jnp.concatenate([h[(d, hd)] for hd in range(M_HEADS)], axis=1)
        outs.append((h_all, c_new, n_new, m_row))
    return outs


def _mlstm_kernel(gb_ref, qf_ref, kf_ref, vf_ref, gf_ref, qb_ref, kb_ref, vb_ref, gbk_ref,
                  c0_ref, n0_ref, m0_ref, hf_ref, hb_ref, cf_ref, nf_ref, mf_ref,
                  c_s, n_s, m_s):
    ci = pl.program_id(1)
    T = MLSTM_TILE
    n_sub = qf_ref.shape[1] // T

    @pl.when(ci == 0)
    def _():
        c_s[...] = c0_ref[0]
        n_s[...] = n0_ref[0]
        m_s[...] = m0_ref[0]

    gb = gb_ref[...]
    state = [(c_s[d], n_s[d], m_s[d]) for d in range(2)]
    refs = ((qf_ref, kf_ref, vf_ref, gf_ref, hf_ref), (qb_ref, kb_ref, vb_ref, gbk_ref, hb_ref))
    for s in range(n_sub):
        lo = (s * T, (n_sub - 1 - s) * T)
        dirs = [(q_ref[0, lo[d]:lo[d] + T, :], k_ref[0, lo[d]:lo[d] + T, :], v_ref[0, lo[d]:lo[d] + T, :],
                 g_ref[0, lo[d]:lo[d] + T, :] + gb) + state[d]
                for d, (q_ref, k_ref, v_ref, g_ref, _) in enumerate(refs)]
        outs = _mlstm_step(dirs, T)
        for d in range(2):
            refs[d][4][0, lo[d]:lo[d] + T, :] = outs[d][0]
        state = [outs[d][1:] for d in range(2)]
    for d in range(2):
        c_s[d], n_s[d], m_s[d] = state[d]

    @pl.when(ci == pl.num_programs(1) - 1)
    def _():
        cf_ref[0] = c_s[...]
        nf_ref[0] = n_s[...]
        mf_ref[0] = m_s[...]


MLSTM_TILE = 128
MLSTM_STEP_CHUNKS = 2


def _mlstm(qm, km, vm, gm, gate_b, state):
    B, S, _ = qm.shape
    T = MLSTM_TILE * MLSTM_STEP_CHUNKS
    nc = S // T
    nq = M_HEADS * M_QK_DIM
    fwd = lambda n: pl.BlockSpec((1, T, n), lambda b, c: (b, c, 0))
    bwd = lambda n: pl.BlockSpec((1, T, n), lambda b, c: (b, nc - 1 - c, 0))
    st_specs = [pl.BlockSpec((1, 2, nq, M_V_DIM), lambda b, c: (b, 0, 0, 0)),
                pl.BlockSpec((1, 2, 1, nq), lambda b, c: (b, 0, 0, 0)),
                pl.BlockSpec((1, 2, 1, LANES), lambda b, c: (b, 0, 0, 0))]
    st_shapes = [jax.ShapeDtypeStruct((B, 2, nq, M_V_DIM), F32),
                 jax.ShapeDtypeStruct((B, 2, 1, nq), F32),
                 jax.ShapeDtypeStruct((B, 2, 1, LANES), F32)]
    gb_row = jnp.pad(gate_b.reshape(1, -1).astype(F32), ((0, 0), (0, LANES - gate_b.size)))
    outs = pl.pallas_call(
        _mlstm_kernel,
        grid=(B, nc),
        in_specs=[pl.BlockSpec((1, LANES), lambda b, c: (0, 0)),
                  fwd(256), fwd(256), fwd(512), fwd(LANES),
                  bwd(256), bwd(256), bwd(512), bwd(LANES)] + st_specs,
        out_specs=[fwd(M_OUT), bwd(M_OUT)] + st_specs,
        out_shape=[jax.ShapeDtypeStruct((B, S, M_OUT), F32)] * 2 + st_shapes,
        scratch_shapes=[pltpu.VMEM((2, nq, M_V_DIM), F32),
                        pltpu.VMEM((2, 1, nq), F32),
                        pltpu.VMEM((2, 1, LANES), F32)],
        compiler_params=_params(("parallel", "arbitrary")),
        name="mlstm_scan",
    )(gb_row, qm, km, vm, gm, qm, km, vm, gm, *state)
    return outs[0], outs[1], tuple(outs[2:])


def _route(scores, sel):
    tm = scores.shape[1]
    gi8 = lax.broadcasted_iota(jnp.int32, (GROUP_SIZE, tm), 0)

    def stack_rows(rows):
        out = jnp.broadcast_to(rows[0], (len(rows), tm))
        for r, v in enumerate(rows[1:], start=1):
            out = jnp.where(gi8 == r, v, out)
        return out

    gs = []
    for g in range(N_GROUPS):
        blk = sel[g * GROUP_SIZE:(g + 1) * GROUP_SIZE, :]
        m1 = jnp.max(blk, axis=0, keepdims=True)
        first = jnp.min(jnp.where(blk == m1, gi8, GROUP_SIZE), axis=0, keepdims=True)
        m2 = jnp.max(jnp.where(gi8 == first, -jnp.inf, blk), axis=0, keepdims=True)
        gs.append(m1 + m2)
    gsc = stack_rows(gs)
    gsel = jnp.zeros((N_GROUPS, tm), F32)
    for _ in range(TOPK_GROUPS):
        mx = jnp.max(gsc, axis=0, keepdims=True)
        first = jnp.min(jnp.where(gsc == mx, gi8, N_GROUPS), axis=0, keepdims=True)
        pick = gi8 == first
        gsel = jnp.where(pick, 1.0, gsel)
        gsc = jnp.where(pick, -jnp.inf, gsc)
    cur = jnp.concatenate(
        [jnp.where(gsel[g:g + 1, :] > 0.0, sel[g * GROUP_SIZE:(g + 1) * GROUP_SIZE, :], -jnp.inf)
         for g in range(N_GROUPS)], axis=0)
    ei = lax.broadcasted_iota(jnp.int32, (N_EXPERTS, tm), 0)
    idx, wts = [], []
    for _ in range(TOP_K):
        mx = jnp.max(cur, axis=0, keepdims=True)
        first = jnp.min(jnp.where(cur == mx, ei, N_EXPERTS), axis=0, keepdims=True)
        pick = ei == first
        idx.append(first)
        wts.append(jnp.sum(jnp.where(pick, scores, 0.0), axis=0, keepdims=True))
        cur = jnp.where(pick, -jnp.inf, cur)
    tot = wts[0]
    for w in wts[1:]:
        tot = tot + w
    wts = [w / tot * ROUTED_SCALE for w in wts]
    return stack_rows(idx), stack_rows(wts)


def _merge_kernel(x_ref, mod_ref, oat_ref, hf_ref, hb_ref, om_ref, ng_ref, u_ref, up_ref, un_ref,
                  bc_ref, cw_ref, gt_ref, bgb_ref, wa_ref, wm_ref, wc_ref, wo_ref, n2_ref,
                  rw_ref, rb_ref, sgu_ref, sd_ref,
                  base_ref, h2_ref, idx_ref, wt_ref, cnt_ref):
    i = pl.program_id(1)
    tm = x_ref.shape[1]
    x = x_ref[0]
    g1 = mod_ref[0, 2:3, :]
    sh2 = mod_ref[0, 3:4, :]
    sc2 = mod_ref[0, 4:5, :]
    g2 = mod_ref[0, 5:6, :]

    ya = jnp.concatenate(
        [jnp.concatenate([oat_ref[0, qb, hh * ATT_BLOCK:(hh + 1) * ATT_BLOCK, :] for hh in range(ATT_GROUP)], axis=1)
         for qb in range(tm // ATT_BLOCK)], axis=0)

    hsum = hf_ref[0] + hb_ref[0]
    parts = []
    for hd in range(M_HEADS):
        hh_ = hsum[:, hd * M_V_DIM:(hd + 1) * M_V_DIM]
        parts.append(hh_ * lax.rsqrt(jnp.mean(hh_ * hh_, axis=-1, keepdims=True) + EPS))
    hn = jnp.concatenate(parts, axis=1) * ng_ref[...]
    ym = (_sigmoid_tanh(om_ref[0].astype(F32)) * hn).astype(BF16)

    u = u_ref[0].astype(F32)
    row = lax.broadcasted_iota(jnp.int32, (tm, 1), 0)
    has_prev = (i > 0).astype(F32)
    has_next = (i < pl.num_programs(1) - 1).astype(F32)
    prev_row = up_ref[0, BF16_SUBLANES - 1:BF16_SUBLANES, :].astype(F32) * has_prev
    next_row = un_ref[0, 0:1, :].astype(F32) * has_next
    u_m1 = jnp.where(row == 0, prev_row, pltpu.roll(u, 1, axis=0))
    u_p1 = jnp.where(row == tm - 1, next_row, pltpu.roll(u, tm - 1, axis=0))
    conv = cw_ref[0:1, :] * u_m1 + cw_ref[1:2, :] * u + cw_ref[2:3, :] * u_p1
    yc = (bc_ref[0].astype(F32) * conv).astype(BF16)

    gg = _sigmoid_tanh(gt_ref[0].astype(F32) + bgb_ref[...])
    ymix = (gg[:, 0:D_MODEL] * _dot(ya, wa_ref[...])
            + gg[:, D_MODEL:2 * D_MODEL] * _dot(ym, wm_ref[...])
            + gg[:, 2 * D_MODEL:3 * D_MODEL] * _dot(yc, wc_ref[...]))
    y = _dot(ymix.astype(BF16), wo_ref[...])
    xm = x + g1 * y

    h2f = _rms_mod(xm, n2_ref[...], sh2, sc2)
    h2 = h2f.astype(BF16)
    h2_ref[0] = h2

    h2_lo = (h2f - h2.astype(F32)).astype(BF16)
    rw = rw_ref[...]
    part = _dot_nt(rw, h2)
    logits_t = part[0:N_EXPERTS, :] + part[N_EXPERTS:2 * N_EXPERTS, :] + _dot_nt(rw[0:N_EXPERTS, :], h2_lo)
    scores = _sigmoid(logits_t)
    idx, wts = _route(scores, scores + rb_ref[...])
    idx_ref[0] = idx
    wt_ref[0] = wts
    ei = lax.broadcasted_iota(jnp.int32, (N_EXPERTS, tm), 0)
    pick = jnp.zeros((N_EXPERTS, tm), F32)
    for kk in range(TOP_K):
        pick = jnp.where(ei == idx[kk:kk + 1, :], 1.0, pick)
    cnt_ref[0] = jnp.broadcast_to(jnp.sum(pick, axis=1, keepdims=True), (N_EXPERTS, LANES)).astype(jnp.int32)

    a = _dot(h2, sgu_ref[...])
    act = (_silu(a[:, 0:SHARED_FF]) * a[:, SHARED_FF:2 * SHARED_FF]).astype(BF16)
    base_ref[0] = xm + g2 * _dot(act, sd_ref[...])


def _merge(x, mod, mod_row, oat, hf, hb, om, u, bc, gt, lw, tm):
    B, S, D = x.shape
    nt = S // tm
    hal = BF16_SUBLANES
    last_h = S // hal - 1
    tspec = lambda n: pl.BlockSpec((1, tm, n), lambda b, i: (b, i, 0))
    full = lambda a: pl.BlockSpec(a.shape, lambda b, i: (0,) * a.ndim)
    weights = [lw['mlstm_norm_g'], lw['conv_w'], lw['branch_gate_b'], lw['w_br_attn'], lw['w_br_mlstm'],
               lw['w_br_conv'], lw['w_out'], lw['norm2_g'], lw['router_wt'], lw['router_bias'],
               lw['sh_gu'], lw['sh_d']]
    in_specs = [tspec(D),
                pl.BlockSpec((1, 6, D), lambda b, i: (mod_row(b), 0, 0)),
                pl.BlockSpec((1, tm // ATT_BLOCK, ATT_GROUP * ATT_BLOCK, LANES), lambda b, i: (b, i, 0, 0)),
                tspec(M_OUT), tspec(M_OUT), tspec(M_OUT), full(weights[0]),
                tspec(CONV_WIDTH),
                pl.BlockSpec((1, hal, CONV_WIDTH), lambda b, i: (b, jnp.maximum(i * (tm // hal) - 1, 0), 0)),
                pl.BlockSpec((1, hal, CONV_WIDTH), lambda b, i: (b, jnp.minimum((i + 1) * (tm // hal), last_h), 0)),
                tspec(CONV_WIDTH), full(weights[1]), tspec(N_BRANCH * D), full(weights[2])]
    in_specs += [full(w) for w in weights[3:]]
    tr = lambda n, dt: (jax.ShapeDtypeStruct((B, n, S), dt), pl.BlockSpec((1, n, tm), lambda b, i: (b, 0, i)))
    outs = [(jax.ShapeDtypeStruct((B, S, D), F32), tspec(D)),
            (jax.ShapeDtypeStruct((B, S, D), BF16), tspec(D)),
            tr(TOP_K, jnp.int32), tr(TOP_K, F32),
            (jax.ShapeDtypeStruct((B * nt, N_EXPERTS, LANES), jnp.int32),
             pl.BlockSpec((1, N_EXPERTS, LANES), lambda b, i: (b * nt + i, 0, 0)))]
    return pl.pallas_call(
        _merge_kernel,
        grid=(B, nt),
        in_specs=in_specs,
        out_specs=[o[1] for o in outs],
        out_shape=[o[0] for o in outs],
        compiler_params=_params(("parallel", "parallel")),
        name="merge_route",
    )(x, mod, oat, hf, hb, om, weights[0], u, u, u, bc, weights[1], gt, weights[2], *weights[3:])


MOE_TILE = 256
CHUNK = BF16_SUBLANES
GROUP_CHUNKS = 64
GROUP_ROWS = GROUP_CHUNKS * CHUNK
TILE_ROWS = MOE_TILE * TOP_K + N_EXPERTS * CHUNK
TILE_CHUNKS = TILE_ROWS // CHUNK
N_PAD_CHUNKS = N_EXPERTS * (GROUP_CHUNKS - 1)
N_SPARE_CHUNKS = 2 * TILE_CHUNKS


def _chunk_copy(src, src_chunk, dst, dst_chunk, sem):
    return pltpu.make_async_copy(src.at[pl.ds(pl.multiple_of(src_chunk * CHUNK, CHUNK), CHUNK)],
                                 dst.at[pl.ds(pl.multiple_of(dst_chunk * CHUNK, CHUNK), CHUNK)], sem)


def _dispatch_tile(h, idx, wts):
    tm = h.shape[0]
    ei = lax.broadcasted_iota(jnp.int32, (N_EXPERTS, tm), 0)
    pick = jnp.zeros((N_EXPERTS, tm), F32)
    wmat = jnp.zeros((N_EXPERTS, tm), F32)
    for kk in range(TOP_K):
        chosen = ei == idx[kk:kk + 1, :]
        pick = jnp.where(chosen, 1.0, pick)
        wmat = jnp.where(chosen, wts[kk:kk + 1, :], wmat)
    t0 = lax.broadcasted_iota(jnp.int32, (tm, tm), 0)
    t1 = lax.broadcasted_iota(jnp.int32, (tm, tm), 1)
    rank = _dot(pick.astype(BF16), jnp.where(t0 < t1, 1.0, 0.0).astype(BF16))
    n_e = jnp.sum(pick, axis=1, keepdims=True)
    n_pad = jnp.floor((n_e + (CHUNK - 1)) * (1.0 / CHUNK)) * CHUNK
    e0 = lax.broadcasted_iota(jnp.int32, (N_EXPERTS, N_EXPERTS), 0)
    e1 = lax.broadcasted_iota(jnp.int32, (N_EXPERTS, N_EXPERTS), 1)
    seg = _dot(jnp.where(e1 < e0, 1.0, 0.0).astype(BF16),
               jnp.broadcast_to(n_pad, (N_EXPERTS, tm)).astype(BF16))
    posmat = seg + rank
    chunk_of = jnp.floor(posmat * (1.0 / CHUNK))
    offs_of = posmat - chunk_of * CHUNK
    chunk_row = jnp.where(pick > 0.0, chunk_of * CHUNK, -float(CHUNK))
    eye = e0 == e1
    to_row = lambda col: jnp.sum(jnp.where(eye, col, 0.0), axis=0, keepdims=True)
    seg_row = to_row(seg[:, 0:1])
    end_row = to_row(seg[:, 0:1] + n_pad)
    seg_row2 = jnp.concatenate([seg_row, seg_row], axis=1)
    end_row2 = jnp.concatenate([end_row, end_row], axis=1)
    r128 = lax.broadcasted_iota(jnp.int32, (TILE_ROWS, 2 * N_EXPERTS), 0).astype(F32)
    own2 = jnp.where(r128 >= seg_row2, jnp.where(r128 < end_row2, 1.0, 0.0), 0.0).astype(BF16)
    row_of = _dot(own2, jnp.concatenate([chunk_row, offs_of], axis=0).astype(BF16))
    w_of = _dot(own2[:, 0:N_EXPERTS], wmat.astype(BF16))
    riota = lax.broadcasted_iota(jnp.int32, (TILE_ROWS, tm), 0).astype(F32)
    hit = row_of == riota
    xg = _dot(jnp.where(hit, 1.0, 0.0).astype(BF16), h).astype(BF16)
    return xg, jnp.where(hit, w_of, 0.0).astype(BF16)


def _dispatch_kernel(pos_ref, pad_ref, h_ref, idx_ref, wt_ref, xs_ref, pw_ref, buf0, buf1, zero_buf, sems, pad_sem,
                     *, n_tiles):
    j = pl.program_id(0)
    bufs = (buf0, buf1)

    def compute(p):
        xg, pw = _dispatch_tile(h_ref[0], idx_ref[0], wt_ref[0])
        bufs[p][...] = xg
        pw_ref[...] = pw

    def issue(tile, p):
        for c in range(TILE_CHUNKS):
            _chunk_copy(bufs[p], c, xs_ref, pos_ref[tile * TILE_CHUNKS + c], sems.at[p]).start()

    def wait(p):
        pltpu.make_async_copy(bufs[p], xs_ref.at[pl.ds(0, TILE_ROWS)], sems.at[p]).wait()

    for p in range(2):
        @pl.when((j >= 2) & (j % 2 == p))
        def _(p=p):
            wait(p)

        @pl.when((j >= 1) & (j < n_tiles) & (j % 2 == p))
        def _(p=p):
            issue(j - 1, 1 - p)
            compute(p)

    @pl.when(j == 0)
    def _():
        compute(0)

    @pl.when(j == n_tiles)
    def _():
        last = (n_tiles - 1) % 2
        issue(n_tiles - 1, last)
        zero_buf[...] = jnp.zeros_like(zero_buf)

        def pad_expert(e, carry):
            def pad_issue(c, inner):
                _chunk_copy(zero_buf, 0, xs_ref, pad_ref[e] + c, pad_sem).start()
                return inner
            lax.fori_loop(0, pad_ref[N_EXPERTS + e], pad_issue, 0)

            def pad_wait(c, inner):
                _chunk_copy(zero_buf, 0, xs_ref, pad_ref[e] + c, pad_sem).wait()
                return inner
            lax.fori_loop(0, pad_ref[N_EXPERTS + e], pad_wait, 0)
            return carry
        lax.fori_loop(0, N_EXPERTS, pad_expert, 0)
        wait(last)


def _dispatch(h2, idx, wts, pos, pad_pos, n_slots):
    B, S, D = h2.shape
    tm = MOE_TILE
    nt = S // tm
    n_tiles = B * nt
    tile = lambda j: jnp.minimum(j, n_tiles - 1)
    grid_spec = pltpu.PrefetchScalarGridSpec(
        num_scalar_prefetch=2,
        grid=(n_tiles + 1,),
        in_specs=[pl.BlockSpec((1, tm, D), lambda j, pos, pad: (tile(j) // nt, tile(j) % nt, 0)),
                  pl.BlockSpec((1, TOP_K, tm), lambda j, pos, pad: (tile(j) // nt, 0, tile(j) % nt)),
                  pl.BlockSpec((1, TOP_K, tm), lambda j, pos, pad: (tile(j) // nt, 0, tile(j) % nt))],
        out_specs=[pl.BlockSpec(memory_space=pl.ANY),
                   pl.BlockSpec((TILE_ROWS, tm), lambda j, pos, pad: (tile(j), 0))],
        scratch_shapes=[pltpu.VMEM((TILE_ROWS, D), BF16),
                        pltpu.VMEM((TILE_ROWS, D), BF16),
                        pltpu.VMEM((CHUNK, D), BF16),
                        pltpu.SemaphoreType.DMA((2,)),
                        pltpu.SemaphoreType.DMA(())],
    )
    return pl.pallas_call(
        functools.partial(_dispatch_kernel, n_tiles=n_tiles),
        grid_spec=grid_spec,
        out_shape=[jax.ShapeDtypeStruct(((n_slots + N_SPARE_CHUNKS) * CHUNK, D), BF16),
                   jax.ShapeDtypeStruct((n_tiles * TILE_ROWS, tm), BF16)],
        compiler_params=_params(("arbitrary",)),
        name="moe_dispatch",
    )(pos, pad_pos, h2, idx, wts)


def _moe_tables(cnt, g_max):
    nt = cnt.shape[0]
    cc = (cnt + (CHUNK - 1)) // CHUNK
    segblk = jnp.cumsum(cc, axis=1) - cc
    tile_chunks = jnp.sum(cc, axis=1)
    prior = jnp.cumsum(cc, axis=0) - cc
    ge_cnt = (jnp.sum(cc, axis=0) + (GROUP_CHUNKS - 1)) // GROUP_CHUNKS
    gbase = jnp.cumsum(ge_cnt) - ge_cnt
    n_groups = jnp.sum(ge_cnt)
    c = jnp.arange(TILE_CHUNKS, dtype=jnp.int32)
    e_of = jnp.sum(((segblk + cc)[:, None, :] <= c[None, :, None]).astype(jnp.int32), axis=-1)
    e_of = jnp.minimum(e_of, N_EXPERTS - 1)
    seg_base = gbase[None, :] * GROUP_CHUNKS + prior - segblk
    onehot = e_of[:, :, None] == jnp.arange(N_EXPERTS, dtype=jnp.int32)[None, None, :]
    pos = jnp.sum(jnp.where(onehot, seg_base[:, None, :], 0), axis=-1) + c[None, :]
    valid = c[None, :] < tile_chunks[:, None]
    n_slots = g_max * GROUP_CHUNKS
    parity = (jnp.arange(nt, dtype=jnp.int32) % 2)[:, None]
    pos_write = jnp.where(valid, pos, n_slots + parity * TILE_CHUNKS + c[None, :]).astype(jnp.int32)
    ce = jnp.sum(cc, axis=0)
    pad_pos = jnp.concatenate([gbase * GROUP_CHUNKS + ce, ge_cnt * GROUP_CHUNKS - ce]).astype(jnp.int32)
    g = jnp.arange(g_max, dtype=jnp.int32)
    grp_e = jnp.minimum(jnp.sum(((gbase + ge_cnt)[None, :] <= g[:, None]).astype(jnp.int32), axis=1),
                        N_EXPERTS - 1).astype(jnp.int32)
    hi = lax.Precision.HIGHEST
    experts = jnp.arange(N_EXPERTS, dtype=jnp.int32)
    slot = jnp.arange(n_slots, dtype=jnp.int32)
    e_slot = jnp.minimum(jnp.sum((((gbase + ge_cnt) * GROUP_CHUNKS)[None, :] <= slot[:, None]).astype(jnp.int32),
                                 axis=1), N_EXPERTS - 1)
    oh_e = (e_slot[:, None] == experts[None, :]).astype(F32)
    q = slot - jnp.sum(oh_e * (gbase * GROUP_CHUNKS).astype(F32)[None, :], axis=1).astype(jnp.int32)
    slot_valid = q < jnp.sum(oh_e * ce.astype(F32)[None, :], axis=1).astype(jnp.int32)
    cum_end = jnp.dot(oh_e, (prior + cc).astype(F32).T, precision=hi)
    t_slot = jnp.minimum(jnp.sum((cum_end <= q[:, None].astype(F32)).astype(jnp.int32), axis=1), nt - 1)
    oh_t = (t_slot[:, None] == jnp.arange(nt, dtype=jnp.int32)[None, :]).astype(F32)
    shift = jnp.sum(jnp.dot(oh_t, (segblk - prior).astype(F32), precision=hi) * oh_e, axis=1).astype(jnp.int32)
    spare = nt * TILE_CHUNKS + ((slot // GROUP_CHUNKS) % 2) * GROUP_CHUNKS + slot % GROUP_CHUNKS
    inv = jnp.where(slot_valid, t_slot * TILE_CHUNKS + q + shift, spare).astype(jnp.int32)
    tiles = jnp.arange(nt, dtype=jnp.int32)
    fill = jnp.concatenate([tiles * TILE_CHUNKS + tile_chunks, TILE_CHUNKS - tile_chunks]).astype(jnp.int32)
    return (pos_write.reshape(-1), inv, pad_pos.reshape(-1), fill, grp_e,
            n_groups.reshape(1).astype(jnp.int32))


def _ffn_kernel(ge_ref, na_ref, inv_ref, fill_ref, x_ref, wg_ref, wu_ref, wd_ref, yt_ref,
                buf0, buf1, zero_buf, sems, fill_sem, *, n_tiles):
    g = pl.program_id(0)
    na = na_ref[0]
    bufs = (buf0, buf1)

    def compute(p):
        x = x_ref[...]
        gate = _dot(x, wg_ref[0, 0].astype(BF16))
        up = _dot(x, wu_ref[0, 0].astype(BF16))
        act = (_silu(gate) * up).astype(BF16)
        bufs[p][...] = _dot(act, wd_ref[0, 0].astype(BF16)).astype(BF16)

    def issue(grp, p):
        for c in range(GROUP_CHUNKS):
            _chunk_copy(bufs[p], c, yt_ref, inv_ref[grp * GROUP_CHUNKS + c], sems.at[p]).start()

    def wait(p):
        pltpu.make_async_copy(bufs[p], yt_ref.at[pl.ds(0, GROUP_ROWS)], sems.at[p]).wait()

    @pl.when(g == 0)
    def _():
        zero_buf[...] = jnp.zeros_like(zero_buf)

        def fill_tile(t, carry):
            def fill_issue(c, inner):
                _chunk_copy(zero_buf, 0, yt_ref, fill_ref[t] + c, fill_sem).start()
                return inner
            lax.fori_loop(0, fill_ref[n_tiles + t], fill_issue, 0)

            def fill_wait(c, inner):
                _chunk_copy(zero_buf, 0, yt_ref, fill_ref[t] + c, fill_sem).wait()
                return inner
            lax.fori_loop(0, fill_ref[n_tiles + t], fill_wait, 0)
            return carry
        lax.fori_loop(0, n_tiles, fill_tile, 0)

    for p in range(2):
        @pl.when((g >= 2) & (g - 2 < na) & (g % 2 == p))
        def _(p=p):
            wait(p)

        @pl.when((g >= 1) & (g < na) & (g % 2 == p))
        def _(p=p):
            issue(g - 1, 1 - p)
            compute(p)

        @pl.when((g >= 1) & (g == na) & (g % 2 == p))
        def _(p=p):
            issue(g - 1, 1 - p)

    @pl.when((g == 0) & (na > 0))
    def _():
        compute(0)


def _ffn_grouped(xs, inv, fill, grp_e, n_groups, layer, w_gate, w_up, w_down, g_max, n_tiles):
    D = xs.shape[1]
    live = lambda g, na: jnp.minimum(g, jnp.maximum(na[0] - 1, 0))
    wspec = lambda shape: pl.BlockSpec((1, 1) + shape, lambda g, ge, na, inv, fill: (layer, ge[live(g, na)], 0, 0))
    grid_spec = pltpu.PrefetchScalarGridSpec(
        num_scalar_prefetch=4,
        grid=(g_max + 2,),
        in_specs=[pl.BlockSpec((GROUP_ROWS, D), lambda g, ge, na, inv, fill: (live(g, na), 0)),
                  wspec((D, EXPERT_FF)), wspec((D, EXPERT_FF)), wspec((EXPERT_FF, D))],
        out_specs=pl.BlockSpec(memory_space=pl.ANY),
        scratch_shapes=[pltpu.VMEM((GROUP_ROWS, D), BF16),
                        pltpu.VMEM((GROUP_ROWS, D), BF16),
                        pltpu.VMEM((CHUNK, D), BF16),
                        pltpu.SemaphoreType.DMA((2,)),
                        pltpu.SemaphoreType.DMA(())],
    )
    return pl.pallas_call(
        functools.partial(_ffn_kernel, n_tiles=n_tiles),
        grid_spec=grid_spec,
        out_shape=jax.ShapeDtypeStruct(((n_tiles * TILE_CHUNKS + 2 * GROUP_CHUNKS) * CHUNK, D), BF16),
        compiler_params=_params(("arbitrary",)),
        name="moe_ffn",
    )(grp_e, n_groups, inv, fill, xs, w_gate, w_up, w_down)


def _combine_kernel(yt_ref, pw_ref, base_ref, mod_ref, fg_ref, o_ref, *, final):
    routed = _dot_tn(pw_ref[...], yt_ref[...])
    out = base_ref[...] + mod_ref[0, 5:6, :] * routed
    if final:
        out = out * lax.rsqrt(jnp.mean(out * out, axis=-1, keepdims=True) + EPS) * fg_ref[...]
    o_ref[...] = out


def _combine(yt, pw, base, mod, mod_row, final_g):
    B, S, D = base.shape
    tm = MOE_TILE
    nt = S // tm
    final = final_g is not None
    fg = (final_g if final else jnp.ones((D,), F32)).reshape(1, D)
    out = pl.pallas_call(
        functools.partial(_combine_kernel, final=final),
        grid=(B * nt,),
        in_specs=[pl.BlockSpec((TILE_ROWS, D), lambda i: (i, 0)),
                  pl.BlockSpec((TILE_ROWS, tm), lambda i: (i, 0)),
                  pl.BlockSpec((tm, D), lambda i: (i, 0)),
                  pl.BlockSpec((1, 6, D), lambda i: (mod_row(i // nt), 0, 0)),
                  pl.BlockSpec((1, D), lambda i: (0, 0))],
        out_specs=pl.BlockSpec((tm, D), lambda i: (i, 0)),
        out_shape=jax.ShapeDtypeStruct((B * S, D), F32),
        compiler_params=_params(("parallel",)),
        name="moe_combine",
    )(yt, pw, base.reshape(B * S, D), mod, fg)
    return out.reshape(B, S, D)


def _moe_sparse(h2, idx, wts, cnt, base, mod, mod_row, layer, w_gate, w_up, w_down, final_g=None):
    B, S, D = h2.shape
    n_tiles = B * (S // MOE_TILE)
    g_max = (n_tiles * TILE_CHUNKS + N_PAD_CHUNKS + GROUP_CHUNKS - 1) // GROUP_CHUNKS
    pos_write, inv, pad_pos, fill, grp_e, n_groups = _moe_tables(cnt[:, :, 0], g_max)
    xs, pw = _dispatch(h2, idx, wts, pos_write, pad_pos, g_max * GROUP_CHUNKS)
    yt = _ffn_grouped(xs, inv, fill, grp_e, n_groups, layer, w_gate, w_up, w_down, g_max, n_tiles)
    return _combine(yt, pw, base, mod, mod_row, final_g)


def _zero_state(batch):
    nq = M_HEADS * M_QK_DIM
    return (jnp.zeros((batch, 2, nq, M_V_DIM), F32),
            jnp.zeros((batch, 2, 1, nq), F32),
            jnp.zeros((batch, 2, 1, LANES), F32))


def kernel(x, c, ctx, c_ctx, ada_w, ada_b, norm1_g, norm2_g, w_in, attn_sink, mlstm_gate_b, mlstm_norm_g, conv_w, w_br_attn, w_br_mlstm, w_br_conv, branch_gate_b, w_out, router_w, router_bias, exp_w_gate, exp_w_up, exp_w_down, sh_w_gate, sh_w_up, sh_w_down, final_g):
    B, S, D = x.shape
    L = ctx.shape[1]
    depth = ada_w.shape[0]
    ctx_row = B

    pad_rows = (-(B + 1)) % 8
    cc = jnp.concatenate([c, c_ctx[None, :], jnp.zeros((pad_rows, D), F32)], axis=0)
    mod_all = _ada(cc, ada_w, ada_b).reshape(depth, B + 1 + pad_rows, 6, D)

    cos_t, sin_t = _rope_tables(S)
    cos_c = jnp.ones((L, LANES), F32)
    sin_c = jnp.zeros((L, LANES), F32)
    col_idx = _proj_column_index()
    att_idx = _attn_row_index()
    lat_row = lambda b: b
    ctx_mod = lambda b: ctx_row

    xc = ctx
    for l in range(depth):
        need_ctx = l < depth - 1
        mod = mod_all[l]
        w_ext = jnp.concatenate([w_in[l], jnp.zeros((D, 1), F32)], axis=1)
        w_p = jnp.take(w_ext, col_idx, axis=1).astype(BF16)
        lw = {
            'mlstm_norm_g': mlstm_norm_g[l].reshape(1, M_OUT),
            'conv_w': conv_w[l],
            'branch_gate_b': branch_gate_b[l].reshape(1, N_BRANCH * D),
            'w_br_attn': jnp.take(w_br_attn[l], att_idx, axis=0).astype(BF16),
            'w_br_mlstm': w_br_mlstm[l].astype(BF16),
            'w_br_conv': w_br_conv[l].astype(BF16),
            'w_out': w_out[l].astype(BF16),
            'norm2_g': norm2_g[l].reshape(1, D),
            'router_wt': _split_hi_lo(router_w[l].T),
            'router_bias': router_bias[l].reshape(N_EXPERTS, 1),
            'sh_gu': jnp.concatenate([sh_w_gate[l], sh_w_up[l]], axis=1).astype(BF16),
            'sh_d': sh_w_down[l].astype(BF16),
        }
        experts = (l, exp_w_gate, exp_w_up, exp_w_down)

        pc = _in_proj(xc, mod, ctx_mod, norm1_g[l], w_p, cos_c, sin_c, tm=256)
        p = _in_proj(x, mod, lat_row, norm1_g[l], w_p, cos_t, sin_t, tm=min(512, S))
        qs_c, k_c, v_c, qm_c, km_c, vm_c, om_c, gm_c, bc_c, u_c, gt_c = pc
        qs, k, v, qm, km, vm, om, gm, bc, u, gt = p

        oat = _attention(qs, k, v, k_c, v_c, attn_sink[l], band=True)
        hf_c, hb_c, st = _mlstm(qm_c, km_c, vm_c, gm_c, mlstm_gate_b[l], _zero_state(B))
        hf, hb, _ = _mlstm(qm, km, vm, gm, mlstm_gate_b[l], st)

        base, h2, idx, wts, cnt = _merge(x, mod, lat_row, oat, hf, hb, om, u, bc, gt, lw, tm=MOE_TILE)
        x_new = _moe_sparse(h2, idx, wts, cnt, base, mod, lat_row, *experts,
                            final_g=final_g if l == depth - 1 else None)

        if need_ctx:
            oat_c = _attention(qs_c, None, None, k_c, v_c, attn_sink[l], band=False)
            base_c, h2_c, idx_c, wts_c, cnt_c = _merge(xc, mod, ctx_mod, oat_c, hf_c, hb_c, om_c, u_c, bc_c, gt_c, lw,
                                                       tm=MOE_TILE)
            xc = _moe_sparse(h2_c, idx_c, wts_c, cnt_c, base_c, mod, ctx_mod, *experts)
        x = x_new
    return x
```

```python
import functools

import numpy as np
import jax
import jax.numpy as jnp
from jax import lax
from jax.experimental import pallas as pl
from jax.experimental.pallas import tpu as pltpu

F32 = jnp.float32
BF16 = jnp.bfloat16

D_MODEL = 1024
GRID_W = 64
EPS = 1e-6
ATT_HEADS = 8
ATT_KV_HEADS = 2
ATT_HEAD_DIM = 64
ATT_GROUP = ATT_HEADS // ATT_KV_HEADS
ATT_BLOCK = 128
ATT_OUT = ATT_HEADS * ATT_HEAD_DIM
ROPE_BASE = 10000.0
M_HEADS = 4
M_QK_DIM = 64
M_V_DIM = 128
M_CHUNK = 64
M_OUT = M_HEADS * M_V_DIM
CONV_WIDTH = 512
N_BRANCH = 3
N_EXPERTS = 64
N_GROUPS = 8
GROUP_SIZE = N_EXPERTS // N_GROUPS
TOPK_GROUPS = 4
TOP_K = 8
EXPERT_FF = 256
SHARED_FF = 256
ROUTED_SCALE = 2.5

LOG2E = 1.4426950408889634
LANES = 128
BF16_SUBLANES = 16
VMEM_LIMIT = 56 * 1024 * 1024

_SEGS = (('q', 512), ('k', 128), ('v', 128), ('qm', 256), ('km', 256), ('vm', 512), ('om', 512),
         ('bc', 512), ('cc', 512), ('xc', 512), ('gt', 3072), ('gm', 128))
_OFF = {}
_o = 0
for _n, _s in _SEGS:
    _OFF[_n] = (_o, _o + _s)
    _o += _s
N_PROJ = _o
D_IN = 6928


def _proj_column_index():
    idx = []
    half = ATT_HEAD_DIM // 2
    for hh in range(ATT_GROUP):
        for g in range(ATT_KV_HEADS):
            head = g * ATT_GROUP + hh
            for par in range(2):
                idx += [head * ATT_HEAD_DIM + 2 * i + par for i in range(half)]
    for g in range(ATT_KV_HEADS):
        for par in range(2):
            idx += [512 + g * ATT_HEAD_DIM + 2 * i + par for i in range(half)]
    idx += list(range(640, 768))
    idx += list(range(768, 2304))
    idx += list(range(2320, 3856))
    idx += list(range(3856, 6928))
    idx += list(range(2304, 2320)) + [D_IN] * (LANES - 16)
    assert len(idx) == N_PROJ
    return np.asarray(idx, np.int32)


def _attn_row_index():
    idx = []
    for hh in range(ATT_GROUP):
        for g in range(ATT_KV_HEADS):
            head = g * ATT_GROUP + hh
            idx += [head * ATT_HEAD_DIM + d for d in range(ATT_HEAD_DIM)]
    return np.asarray(idx, np.int32)


def _rope_tables(seq):
    rows = seq // GRID_W
    row = jnp.repeat(jnp.arange(rows, dtype=F32), GRID_W)
    col = jnp.tile(jnp.arange(GRID_W, dtype=F32), rows)
    n_pairs = ATT_HEAD_DIM // 4
    inv_freq = ROPE_BASE ** (-jnp.arange(n_pairs, dtype=F32) / n_pairs)
    ang = jnp.concatenate([row[:, None] * inv_freq, col[:, None] * inv_freq], axis=-1)
    c, s = jnp.cos(ang), jnp.sin(ang)
    cos_t = jnp.concatenate([c, c, c, c], axis=-1)
    sin_t = jnp.concatenate([-s, s, -s, s], axis=-1)
    return cos_t, sin_t


def _dot(a, b):
    return jnp.dot(a, b, preferred_element_type=F32)


def _dot_nt(a, b):
    return lax.dot_general(a, b, (((1,), (1,)), ((), ())), preferred_element_type=F32)


def _dot_tn(a, b):
    return lax.dot_general(a, b, (((0,), (0,)), ((), ())), preferred_element_type=F32)


def _split_hi_lo(w):
    hi = w.astype(BF16)
    lo = (w - hi.astype(F32)).astype(BF16)
    return jnp.concatenate([hi, lo], axis=0)


def _sigmoid(x):
    return 1.0 / (1.0 + jnp.exp(-x))


def _sigmoid_tanh(x):
    return 0.5 * jnp.tanh(0.5 * x) + 0.5


def _silu(x):
    return x * _sigmoid(x)


def _log_sigmoid(x):
    return jnp.minimum(x, 0.0) - jnp.log(1.0 + jnp.exp(-jnp.abs(x)))


def _rms_mod(x, g, shift, scale):
    y = x * lax.rsqrt(jnp.mean(x * x, axis=-1, keepdims=True) + EPS) * g
    return y * (1.0 + scale) + shift


def _params(sem):
    return pltpu.CompilerParams(dimension_semantics=sem, vmem_limit_bytes=VMEM_LIMIT)


def _ada_kernel(c_ref, w_ref, b_ref, o_ref):
    s = _silu(c_ref[...])
    o_ref[0] = jnp.dot(s, w_ref[0], preferred_element_type=F32,
                       precision=lax.Precision.HIGHEST) + b_ref[0]


def _ada(cc, ada_w, ada_b):
    depth, d, n = ada_w.shape
    rows = cc.shape[0]
    tn = 1536
    return pl.pallas_call(
        _ada_kernel,
        grid=(depth, n // tn),
        in_specs=[pl.BlockSpec((rows, d), lambda l, j: (0, 0)),
                  pl.BlockSpec((1, d, tn), lambda l, j: (l, 0, j)),
                  pl.BlockSpec((1, 1, tn), lambda l, j: (l, 0, j))],
        out_specs=pl.BlockSpec((1, rows, tn), lambda l, j: (l, 0, j)),
        out_shape=jax.ShapeDtypeStruct((depth, rows, n), F32),
        compiler_params=_params(("parallel", "parallel")),
        name="ada_mod",
    )(cc, ada_w, ada_b.reshape(depth, 1, n))


def _swap_halves(x):
    lane = lax.broadcasted_iota(jnp.int32, x.shape, 1)
    first = (lane % ATT_HEAD_DIM) < (ATT_HEAD_DIM // 2)
    return jnp.where(first, pltpu.roll(x, LANES - 32, axis=1), pltpu.roll(x, 32, axis=1))


def _in_kernel(x_ref, mod_ref, g_ref, w_ref, cos_ref, sin_ref,
               qs_ref, k_ref, v_ref, qm_ref, km_ref, vm_ref, om_ref, gm_ref, bc_ref, u_ref, gt_ref):
    tm = x_ref.shape[1]
    h = _rms_mod(x_ref[0], g_ref[...], mod_ref[0, 0:1, :], mod_ref[0, 1:2, :]).astype(BF16)

    def proj(name):
        lo, hi = _OFF[name]
        return _dot(h, w_ref[:, lo:hi])

    cos_t = cos_ref[...]
    sin_t = sin_ref[...]

    def rope(t):
        return t * cos_t + _swap_halves(t) * sin_t

    q = proj('q')
    scale = ATT_HEAD_DIM ** -0.5 * LOG2E
    for hh in range(ATT_GROUP):
        r = (rope(q[:, hh * LANES:(hh + 1) * LANES]) * scale).astype(BF16)
        for qb in range(tm // ATT_BLOCK):
            qs_ref[0, qb, hh * ATT_BLOCK:(hh + 1) * ATT_BLOCK, :] = r[qb * ATT_BLOCK:(qb + 1) * ATT_BLOCK, :]
    k_ref[0] = rope(proj('k')).astype(BF16)
    v_ref[0] = proj('v').astype(BF16)
    qm_ref[0] = proj('qm').astype(BF16)
    km_ref[0] = proj('km').astype(BF16)
    vm_ref[0] = proj('vm').astype(BF16)
    om_ref[0] = proj('om').astype(BF16)
    gm_ref[0] = proj('gm')
    bc_ref[0] = proj('bc').astype(BF16)
    u_ref[0] = (proj('cc') * proj('xc')).astype(BF16)
    gt_ref[0] = proj('gt').astype(BF16)


def _in_proj(x, mod, mod_row, norm_g, w_p, cos_t, sin_t, tm):
    B, S, D = x.shape
    nb = S // ATT_BLOCK
    tok = lambda n, dt: jax.ShapeDtypeStruct((B, S, n), dt)
    tspec = lambda n: pl.BlockSpec((1, tm, n), lambda b, i: (b, i, 0))
    out_shape = (jax.ShapeDtypeStruct((B, nb, ATT_GROUP * ATT_BLOCK, LANES), BF16),
                 tok(128, BF16), tok(128, BF16), tok(256, BF16), tok(256, BF16), tok(512, BF16),
                 tok(512, BF16), tok(128, F32), tok(512, BF16), tok(512, BF16), tok(3072, BF16))
    out_specs = (pl.BlockSpec((1, tm // ATT_BLOCK, ATT_GROUP * ATT_BLOCK, LANES), lambda b, i: (b, i, 0, 0)),
                 tspec(128), tspec(128), tspec(256), tspec(256), tspec(512), tspec(512), tspec(128),
                 tspec(512), tspec(512), tspec(3072))
    return pl.pallas_call(
        _in_kernel,
        grid=(B, S // tm),
        in_specs=[pl.BlockSpec((1, tm, D), lambda b, i: (b, i, 0)),
                  pl.BlockSpec((1, 6, D), lambda b, i: (mod_row(b), 0, 0)),
                  pl.BlockSpec((1, D), lambda b, i: (0, 0)),
                  pl.BlockSpec((D, N_PROJ), lambda b, i: (0, 0), pipeline_mode=pl.Buffered(1)),
                  pl.BlockSpec((tm, LANES), lambda b, i: (i, 0)),
                  pl.BlockSpec((tm, LANES), lambda b, i: (i, 0))],
        out_specs=out_specs,
        out_shape=out_shape,
        compiler_params=_params(("parallel", "parallel")),
        name="in_proj",
    )(x, mod, norm_g.reshape(1, D), w_p, cos_t, sin_t)


def _attn_block(q, sink_ref, kcat, vcat, masks):
    rows = q.shape[0]
    lane = lax.broadcasted_iota(jnp.int32, (1, LANES), 1)
    hh = lax.broadcasted_iota(jnp.int32, (rows, 1), 0) // ATT_BLOCK
    out = jnp.zeros((rows, LANES), F32)
    for g in range(ATT_KV_HEADS):
        lm = (lane < ATT_HEAD_DIM) if g == 0 else (lane >= ATT_HEAD_DIM)
        kz = jnp.where(lm, kcat, jnp.zeros_like(kcat))
        ones_lane = ATT_HEAD_DIM if g == 0 else 0
        vz = jnp.where(lm, vcat, jnp.where(lane == ones_lane, 1.0, 0.0).astype(BF16))
        s = _dot_nt(q, kz)
        if any(mk is not None for mk in masks):
            s = jnp.concatenate(
                [s[:, n * ATT_BLOCK:(n + 1) * ATT_BLOCK] if mk is None
                 else jnp.where(mk, s[:, n * ATT_BLOCK:(n + 1) * ATT_BLOCK], -jnp.inf)
                 for n, mk in enumerate(masks)], axis=1)
        sink = jnp.zeros((rows, 1), F32)
        for a in range(ATT_GROUP):
            sink = jnp.where(hh == a, sink_ref[g * ATT_GROUP + a] * LOG2E, sink)
        m = jnp.maximum(jnp.max(s, axis=-1, keepdims=True), sink)
        p = jnp.exp2((s - m).astype(BF16))
        pv = _dot(p, vz)
        l = pv[:, ones_lane:ones_lane + 1] + jnp.exp2(sink - m)
        out = out + jnp.where(lm, pv, 0.0) / l
    return out.astype(BF16)


ATT_STEP_BLOCKS = 4


def _attn_kernel(sink_ref, qs_ref, kc_ref, vc_ref, *rest, band):
    nsb = qs_ref.shape[1]
    if not band:
        (o_ref,) = rest
        n_ctx = kc_ref.shape[1] // ATT_BLOCK
        for sb in range(nsb):
            o_ref[0, sb] = _attn_block(qs_ref[0, sb], sink_ref, kc_ref[0], vc_ref[0], [None] * n_ctx)
        return
    kp_ref, kcur_ref, kn_ref, vp_ref, vcur_ref, vn_ref, o_ref = rest
    j = pl.program_id(1)
    nstep = pl.num_programs(1)
    n_ctx = kc_ref.shape[1] // ATT_BLOCK
    rows = qs_ref.shape[2]
    t = lax.broadcasted_iota(jnp.int32, (rows, 1), 0) % ATT_BLOCK
    i = lax.broadcasted_iota(jnp.int32, (1, ATT_BLOCK), 1)
    below = i >= t
    above = i <= t
    first = i >= t + jnp.where(j > 0, 0, 2 * ATT_BLOCK)
    last = i <= t - jnp.where(j < nstep - 1, 0, 2 * ATT_BLOCK)
    kblk = [kp_ref[0]] + [kcur_ref[0, n * ATT_BLOCK:(n + 1) * ATT_BLOCK, :] for n in range(nsb)] + [kn_ref[0]]
    vblk = [vp_ref[0]] + [vcur_ref[0, n * ATT_BLOCK:(n + 1) * ATT_BLOCK, :] for n in range(nsb)] + [vn_ref[0]]
    for sb in range(nsb):
        kcat = jnp.concatenate([kc_ref[0]] + kblk[sb:sb + 3], axis=0)
        vcat = jnp.concatenate([vc_ref[0]] + vblk[sb:sb + 3], axis=0)
        masks = [first if sb == 0 else below, None, last if sb == nsb - 1 else above]
        o_ref[0, sb] = _attn_block(qs_ref[0, sb], sink_ref, kcat, vcat, [None] * n_ctx + masks)


def _attention(qs, k, v, kc, vc, sink, band):
    B, nb = qs.shape[:2]
    lc = kc.shape[1]
    blocks = min(ATT_STEP_BLOCKS, nb)
    nstep = nb // blocks
    last = nb - 1
    qspec = pl.BlockSpec((1, blocks, ATT_GROUP * ATT_BLOCK, LANES), lambda b, j: (b, j, 0, 0))
    cspec = pl.BlockSpec((1, lc, LANES), lambda b, j: (b, 0, 0))
    in_specs = [pl.BlockSpec(memory_space=pltpu.SMEM), qspec, cspec, cspec]
    args = [sink.astype(F32), qs, kc, vc]
    if band:
        prev = pl.BlockSpec((1, ATT_BLOCK, LANES), lambda b, j: (b, jnp.maximum(blocks * j - 1, 0), 0))
        cur = pl.BlockSpec((1, blocks * ATT_BLOCK, LANES), lambda b, j: (b, j, 0))
        nxt = pl.BlockSpec((1, ATT_BLOCK, LANES), lambda b, j: (b, jnp.minimum(blocks * (j + 1), last), 0))
        in_specs += [prev, cur, nxt, prev, cur, nxt]
        args += [k, k, k, v, v, v]
    return pl.pallas_call(
        functools.partial(_attn_kernel, band=band),
        grid=(B, nstep),
        in_specs=in_specs,
        out_specs=qspec,
        out_shape=jax.ShapeDtypeStruct(qs.shape, BF16),
        compiler_params=_params(("parallel", "parallel")),
        name="attention_band" if band else "attention_ctx",
    )(*args)


def _mlstm_step(dirs, T):
    kscale = M_QK_DIM ** -0.5
    si = lax.broadcasted_iota(jnp.int32, (T, T), 0)
    ri = lax.broadcasted_iota(jnp.int32, (T, T), 1)
    lane_qk = lax.broadcasted_iota(jnp.int32, (1, M_HEADS * M_QK_DIM), 1) // M_QK_DIM
    lane_m = lax.broadcasted_iota(jnp.int32, (1, LANES), 1)
    row_c = lax.broadcasted_iota(jnp.int32, (M_HEADS * M_QK_DIM, 1), 0) // M_QK_DIM
    combos = [(d, hd) for d in range(2) for hd in range(M_HEADS)]

    tri, bcol, gt, bt, blast = [], [], [], [], []
    for d, (q, k, v, g, C, n, m) in enumerate(dirs):
        t = (ri <= si) if d == 0 else (ri >= si)
        tri.append(t)
        lf = _log_sigmoid(g)
        bc = jnp.dot(t.astype(F32), lf, preferred_element_type=F32, precision=lax.Precision.HIGHEST)
        bcol.append(bc)
        gt.append(g.T)
        bt.append(bc.T)
        blast.append(bc[T - 1:T, :] if d == 0 else bc[0:1, :])

    def lanes(d, hd):
        return (2 * d) * M_HEADS + hd, (2 * d + 1) * M_HEADS + hd

    b_col = {c: bcol[c[0]][:, lanes(*c)[1]:lanes(*c)[1] + 1] for c in combos}
    ig_col = {c: dirs[c[0]][3][:, lanes(*c)[0]:lanes(*c)[0] + 1] for c in combos}
    alpha = {c: gt[c[0]][lanes(*c)[0]:lanes(*c)[0] + 1, :] - bt[c[0]][lanes(*c)[1]:lanes(*c)[1] + 1, :]
             for c in combos}
    m_old = {c: dirs[c[0]][6][:, c[1]:c[1] + 1] for c in combos}
    b_last = {c: blast[c[0]][:, lanes(*c)[1]:lanes(*c)[1] + 1] for c in combos}
    hmask = {hd: lane_qk == hd for hd in range(M_HEADS)}

    a_mat = {c: jnp.where(tri[c[0]], alpha[c], -jnp.inf) for c in combos}
    a_max = {c: jnp.max(a_mat[c], axis=1, keepdims=True) for c in combos}
    a_int = {c: b_col[c] + m_old[c] for c in combos}
    m_s = {c: jnp.maximum(a_int[c], b_col[c] + a_max[c]) for c in combos}
    w_int = {c: jnp.exp(a_int[c] - m_s[c]) for c in combos}
    w_mat = {c: jnp.exp(a_mat[c] + (b_col[c] - m_s[c])) for c in combos}
    qmask = {c: jnp.where(hmask[c[1]], dirs[c[0]][0], jnp.zeros_like(dirs[c[0]][0])) for c in combos}
    s_qk = {c: w_mat[c] * (_dot_nt(qmask[c], dirs[c[0]][1]) * kscale) for c in combos}
    vh = {c: dirs[c[0]][2][:, c[1] * M_V_DIM:(c[1] + 1) * M_V_DIM] for c in combos}
    c_bf = [dirs[d][4].astype(BF16) for d in range(2)]
    num = {c: _dot(s_qk[c].astype(BF16), vh[c]) + w_int[c] * _dot(qmask[c], c_bf[c[0]]) for c in combos}
    qn_all = [dirs[d][0].astype(F32) * dirs[d][5] for d in range(2)]
    qn = {c: jnp.sum(jnp.where(hmask[c[1]], qn_all[c[0]], 0.0), axis=1, keepdims=True) for c in combos}
    den = {c: jnp.sum(s_qk[c], axis=1, keepdims=True) + w_int[c] * qn[c] for c in combos}
    h = {c: num[c] / jnp.maximum(jnp.abs(den[c]), jnp.exp(-m_s[c])) for c in combos}

    r_col = {c: b_last[c] - b_col[c] + ig_col[c] for c in combos}
    m_new = {c: jnp.maximum(b_last[c] + m_old[c], jnp.max(r_col[c], axis=0, keepdims=True)) for c in combos}
    decay = {c: jnp.exp(b_last[c] + m_old[c] - m_new[c]) for c in combos}
    w_r = {c: jnp.exp(r_col[c] - m_new[c]) for c in combos}

    outs = []
    for d, (q, k, v, g, C, n, m) in enumerate(dirs):
        w_lanes = jnp.zeros((T, M_HEADS * M_QK_DIM), F32)
        dec_lanes = jnp.zeros((1, M_HEADS * M_QK_DIM), F32)
        dec_rows = jnp.zeros((M_HEADS * M_QK_DIM, 1), F32)
        m_row = jnp.zeros((1, LANES), F32)
        for hd in range(M_HEADS):
            w_lanes = jnp.where(hmask[hd], w_r[(d, hd)], w_lanes)
            dec_lanes = jnp.where(hmask[hd], decay[(d, hd)], dec_lanes)
            dec_rows = jnp.where(row_c == hd, decay[(d, hd)], dec_rows)
            m_row = jnp.where(lane_m == hd, m_new[(d, hd)], m_row)
        kw = k.astype(F32) * (w_lanes * kscale)
        kwt = kw.T.astype(BF16)
        upd = jnp.concatenate(
            [_dot(kwt[hd * M_QK_DIM:(hd + 1) * M_QK_DIM, :], vh[(d, hd)]) for hd in range(M_HEADS)], axis=0)
        c_new = dec_rows * C + upd
        n_new = dec_lanes * n + jnp.sum(kw, axis=0, keepdims=True)
        h_all = jnp.concatenate([h[(d, hd)] for hd in range(M_HEADS)], axis=1)
        outs.append((h_all, c_new, n_new, m_row))
    return outs


def _mlstm_kernel(gb_ref, qf_ref, kf_ref, vf_ref, gf_ref, qb_ref, kb_ref, vb_ref, gbk_ref,
                  c0_ref, n0_ref, m0_ref, hf_ref, hb_ref, cf_ref, nf_ref, mf_ref,
                  c_s, n_s, m_s):
    ci = pl.program_id(1)
    T = MLSTM_TILE
    n_sub = qf_ref.shape[1] // T

    @pl.when(ci == 0)
    def _():
        c_s[...] = c0_ref[0]
        n_s[...] = n0_ref[0]
        m_s[...] = m0_ref[0]

    gb = gb_ref[...]
    state = [(c_s[d], n_s[d], m_s[d]) for d in range(2)]
    refs = ((qf_ref, kf_ref, vf_ref, gf_ref, hf_ref), (qb_ref, kb_ref, vb_ref, gbk_ref, hb_ref))
    for s in range(n_sub):
        lo = (s * T, (n_sub - 1 - s) * T)
        dirs = [(q_ref[0, lo[d]:lo[d] + T, :], k_ref[0, lo[d]:lo[d] + T, :], v_ref[0, lo[d]:lo[d] + T, :],
                 g_ref[0, lo[d]:lo[d] + T, :] + gb) + state[d]
                for d, (q_ref, k_ref, v_ref, g_ref, _) in enumerate(refs)]
        outs = _mlstm_step(dirs, T)
        for d in range(2):
            refs[d][4][0, lo[d]:lo[d] + T, :] = outs[d][0]
        state = [outs[d][1:] for d in range(2)]
    for d in range(2):
        c_s[d], n_s[d], m_s[d] = state[d]

    @pl.when(ci == pl.num_programs(1) - 1)
    def _():
        cf_ref[0] = c_s[...]
        nf_ref[0] = n_s[...]
        mf_ref[0] = m_s[...]


MLSTM_TILE = 128
MLSTM_STEP_CHUNKS = 4


def _mlstm(qm, km, vm, gm, gate_b, state):
    B, S, _ = qm.shape
    T = MLSTM_TILE * min(MLSTM_STEP_CHUNKS, S // MLSTM_TILE)
    nc = S // T
    nq = M_HEADS * M_QK_DIM
    fwd = lambda n: pl.BlockSpec((1, T, n), lambda b, c: (b, c, 0))
    bwd = lambda n: pl.BlockSpec((1, T, n), lambda b, c: (b, nc - 1 - c, 0))
    st_specs = [pl.BlockSpec((1, 2, nq, M_V_DIM), lambda b, c: (b, 0, 0, 0)),
                pl.BlockSpec((1, 2, 1, nq), lambda b, c: (b, 0, 0, 0)),
                pl.BlockSpec((1, 2, 1, LANES), lambda b, c: (b, 0, 0, 0))]
    st_shapes = [jax.ShapeDtypeStruct((B, 2, nq, M_V_DIM), F32),
                 jax.ShapeDtypeStruct((B, 2, 1, nq), F32),
                 jax.ShapeDtypeStruct((B, 2, 1, LANES), F32)]
    gb_row = jnp.pad(gate_b.reshape(1, -1).astype(F32), ((0, 0), (0, LANES - gate_b.size)))
    outs = pl.pallas_call(
        _mlstm_kernel,
        grid=(B, nc),
        in_specs=[pl.BlockSpec((1, LANES), lambda b, c: (0, 0)),
                  fwd(256), fwd(256), fwd(512), fwd(LANES),
                  bwd(256), bwd(256), bwd(512), bwd(LANES)] + st_specs,
        out_specs=[fwd(M_OUT), bwd(M_OUT)] + st_specs,
        out_shape=[jax.ShapeDtypeStruct((B, S, M_OUT), F32)] * 2 + st_shapes,
        scratch_shapes=[pltpu.VMEM((2, nq, M_V_DIM), F32),
                        pltpu.VMEM((2, 1, nq), F32),
                        pltpu.VMEM((2, 1, LANES), F32)],
        compiler_params=_params(("parallel", "arbitrary")),
        name="mlstm_scan",
    )(gb_row, qm, km, vm, gm, qm, km, vm, gm, *state)
    return outs[0], outs[1], tuple(outs[2:])


def _route(scores, sel):
    tm = scores.shape[1]
    gi8 = lax.broadcasted_iota(jnp.int32, (GROUP_SIZE, tm), 0)

    def stack_rows(rows):
        out = jnp.broadcast_to(rows[0], (len(rows), tm))
        for r, v in enumerate(rows[1:], start=1):
            out = jnp.where(gi8 == r, v, out)
        return out

    gs = []
    for g in range(N_GROUPS):
        blk = sel[g * GROUP_SIZE:(g + 1) * GROUP_SIZE, :]
        m1 = jnp.max(blk, axis=0, keepdims=True)
        first = jnp.min(jnp.where(blk == m1, gi8, GROUP_SIZE), axis=0, keepdims=True)
        m2 = jnp.max(jnp.where(gi8 == first, -jnp.inf, blk), axis=0, keepdims=True)
        gs.append(m1 + m2)
    gsc = stack_rows(gs)
    gsel = jnp.zeros((N_GROUPS, tm), F32)
    for _ in range(TOPK_GROUPS):
        mx = jnp.max(gsc, axis=0, keepdims=True)
        first = jnp.min(jnp.where(gsc == mx, gi8, N_GROUPS), axis=0, keepdims=True)
        pick = gi8 == first
        gsel = jnp.where(pick, 1.0, gsel)
        gsc = jnp.where(pick, -jnp.inf, gsc)
    cur = jnp.concatenate(
        [jnp.where(gsel[g:g + 1, :] > 0.0, sel[g * GROUP_SIZE:(g + 1) * GROUP_SIZE, :], -jnp.inf)
         for g in range(N_GROUPS)], axis=0)
    ei = lax.broadcasted_iota(jnp.int32, (N_EXPERTS, tm), 0)
    idx, wts = [], []
    for _ in range(TOP_K):
        mx = jnp.max(cur, axis=0, keepdims=True)
        first = jnp.min(jnp.where(cur == mx, ei, N_EXPERTS), axis=0, keepdims=True)
        pick = ei == first
        idx.append(first)
        wts.append(jnp.sum(jnp.where(pick, scores, 0.0), axis=0, keepdims=True))
        cur = jnp.where(pick, -jnp.inf, cur)
    tot = wts[0]
    for w in wts[1:]:
        tot = tot + w
    wts = [w / tot * ROUTED_SCALE for w in wts]
    return stack_rows(idx), stack_rows(wts)


def _merge_kernel(x_ref, mod_ref, oat_ref, hf_ref, hb_ref, om_ref, ng_ref, u_ref, up_ref, un_ref,
                  bc_ref, cw_ref, gt_ref, bgb_ref, wa_ref, wm_ref, wc_ref, wo_ref, n2_ref,
                  rw_ref, rb_ref, sgu_ref, sd_ref,
                  base_ref, h2_ref, idx_ref, wt_ref, cnt_ref):
    i = pl.program_id(1)
    tm = x_ref.shape[1]
    x = x_ref[0]
    g1 = mod_ref[0, 2:3, :]
    sh2 = mod_ref[0, 3:4, :]
    sc2 = mod_ref[0, 4:5, :]
    g2 = mod_ref[0, 5:6, :]

    ya = jnp.concatenate(
        [jnp.concatenate([oat_ref[0, qb, hh * ATT_BLOCK:(hh + 1) * ATT_BLOCK, :] for hh in range(ATT_GROUP)], axis=1)
         for qb in range(tm // ATT_BLOCK)], axis=0)

    hsum = hf_ref[0] + hb_ref[0]
    parts = []
    for hd in range(M_HEADS):
        hh_ = hsum[:, hd * M_V_DIM:(hd + 1) * M_V_DIM]
        parts.append(hh_ * lax.rsqrt(jnp.mean(hh_ * hh_, axis=-1, keepdims=True) + EPS))
    hn = jnp.concatenate(parts, axis=1) * ng_ref[...]
    ym = (_sigmoid_tanh(om_ref[0].astype(F32)) * hn).astype(BF16)

    u = u_ref[0].astype(F32)
    row = lax.broadcasted_iota(jnp.int32, (tm, 1), 0)
    has_prev = (i > 0).astype(F32)
    has_next = (i < pl.num_programs(1) - 1).astype(F32)
    prev_row = up_ref[0, BF16_SUBLANES - 1:BF16_SUBLANES, :].astype(F32) * has_prev
    next_row = un_ref[0, 0:1, :].astype(F32) * has_next
    u_m1 = jnp.where(row == 0, prev_row, pltpu.roll(u, 1, axis=0))
    u_p1 = jnp.where(row == tm - 1, next_row, pltpu.roll(u, tm - 1, axis=0))
    conv = cw_ref[0:1, :] * u_m1 + cw_ref[1:2, :] * u + cw_ref[2:3, :] * u_p1
    yc = (bc_ref[0].astype(F32) * conv).astype(BF16)

    gg = _sigmoid_tanh(gt_ref[0].astype(F32) + bgb_ref[...])
    ymix = (gg[:, 0:D_MODEL] * _dot(ya, wa_ref[...])
            + gg[:, D_MODEL:2 * D_MODEL] * _dot(ym, wm_ref[...])
            + gg[:, 2 * D_MODEL:3 * D_MODEL] * _dot(yc, wc_ref[...]))
    y = _dot(ymix.astype(BF16), wo_ref[...])
    xm = x + g1 * y

    h2f = _rms_mod(xm, n2_ref[...], sh2, sc2)
    h2 = h2f.astype(BF16)
    h2_ref[0] = h2

    h2_lo = (h2f - h2.astype(F32)).astype(BF16)
    rw = rw_ref[...]
    part = _dot_nt(rw, h2)
    logits_t = part[0:N_EXPERTS, :] + part[N_EXPERTS:2 * N_EXPERTS, :] + _dot_nt(rw[0:N_EXPERTS, :], h2_lo)
    scores = _sigmoid(logits_t)
    idx, wts = _route(scores, scores + rb_ref[...])
    idx_ref[0] = idx
    wt_ref[0] = wts
    ei = lax.broadcasted_iota(jnp.int32, (N_EXPERTS, tm), 0)
    pick = jnp.zeros((N_EXPERTS, tm), F32)
    for kk in range(TOP_K):
        pick = jnp.where(ei == idx[kk:kk + 1, :], 1.0, pick)
    cnt_ref[0] = jnp.broadcast_to(jnp.sum(pick, axis=1, keepdims=True), (N_EXPERTS, LANES)).astype(jnp.int32)

    a = _dot(h2, sgu_ref[...])
    act = (_silu(a[:, 0:SHARED_FF]) * a[:, SHARED_FF:2 * SHARED_FF]).astype(BF16)
    base_ref[0] = xm + g2 * _dot(act, sd_ref[...])


def _merge(x, mod, mod_row, oat, hf, hb, om, u, bc, gt, lw, tm):
    B, S, D = x.shape
    nt = S // tm
    hal = BF16_SUBLANES
    last_h = S // hal - 1
    tspec = lambda n: pl.BlockSpec((1, tm, n), lambda b, i: (b, i, 0))
    full = lambda a: pl.BlockSpec(a.shape, lambda b, i: (0,) * a.ndim)
    weights = [lw['mlstm_norm_g'], lw['conv_w'], lw['branch_gate_b'], lw['w_br_attn'], lw['w_br_mlstm'],
               lw['w_br_conv'], lw['w_out'], lw['norm2_g'], lw['router_wt'], lw['router_bias'],
               lw['sh_gu'], lw['sh_d']]
    in_specs = [tspec(D),
                pl.BlockSpec((1, 6, D), lambda b, i: (mod_row(b), 0, 0)),
                pl.BlockSpec((1, tm // ATT_BLOCK, ATT_GROUP * ATT_BLOCK, LANES), lambda b, i: (b, i, 0, 0)),
                tspec(M_OUT), tspec(M_OUT), tspec(M_OUT), full(weights[0]),
                tspec(CONV_WIDTH),
                pl.BlockSpec((1, hal, CONV_WIDTH), lambda b, i: (b, jnp.maximum(i * (tm // hal) - 1, 0), 0)),
                pl.BlockSpec((1, hal, CONV_WIDTH), lambda b, i: (b, jnp.minimum((i + 1) * (tm // hal), last_h), 0)),
                tspec(CONV_WIDTH), full(weights[1]), tspec(N_BRANCH * D), full(weights[2])]
    in_specs += [full(w) for w in weights[3:]]
    tr = lambda n, dt: (jax.ShapeDtypeStruct((B, n, S), dt), pl.BlockSpec((1, n, tm), lambda b, i: (b, 0, i)))
    outs = [(jax.ShapeDtypeStruct((B, S, D), F32), tspec(D)),
            (jax.ShapeDtypeStruct((B, S, D), BF16), tspec(D)),
            tr(TOP_K, jnp.int32), tr(TOP_K, F32),
            (jax.ShapeDtypeStruct((B * nt, N_EXPERTS, LANES), jnp.int32),
             pl.BlockSpec((1, N_EXPERTS, LANES), lambda b, i: (b * nt + i, 0, 0)))]
    return pl.pallas_call(
        _merge_kernel,
        grid=(B, nt),
        in_specs=in_specs,
        out_specs=[o[1] for o in outs],
        out_shape=[o[0] for o in outs],
        compiler_params=_params(("parallel", "parallel")),
        name="merge_route",
    )(x, mod, oat, hf, hb, om, weights[0], u, u, u, bc, weights[1], gt, weights[2], *weights[3:])


MOE_TILE = 256
CHUNK = BF16_SUBLANES
GROUP_CHUNKS = 64
GROUP_ROWS = GROUP_CHUNKS * CHUNK
TILE_ROWS = MOE_TILE * TOP_K + N_EXPERTS * CHUNK
TILE_CHUNKS = TILE_ROWS // CHUNK
N_PAD_CHUNKS = N_EXPERTS * (GROUP_CHUNKS - 1)
N_SPARE_CHUNKS = 2 * TILE_CHUNKS


def _chunk_copy(src, src_chunk, dst, dst_chunk, sem):
    return pltpu.make_async_copy(src.at[pl.ds(pl.multiple_of(src_chunk * CHUNK, CHUNK), CHUNK)],
                                 dst.at[pl.ds(pl.multiple_of(dst_chunk * CHUNK, CHUNK), CHUNK)], sem)


def _dispatch_tile(h, idx, wts):
    tm = h.shape[0]
    ei = lax.broadcasted_iota(jnp.int32, (N_EXPERTS, tm), 0)
    pick = jnp.zeros((N_EXPERTS, tm), F32)
    wmat = jnp.zeros((N_EXPERTS, tm), F32)
    for kk in range(TOP_K):
        chosen = ei == idx[kk:kk + 1, :]
        pick = jnp.where(chosen, 1.0, pick)
        wmat = jnp.where(chosen, wts[kk:kk + 1, :], wmat)
    t0 = lax.broadcasted_iota(jnp.int32, (tm, tm), 0)
    t1 = lax.broadcasted_iota(jnp.int32, (tm, tm), 1)
    rank = _dot(pick.astype(BF16), jnp.where(t0 < t1, 1.0, 0.0).astype(BF16))
    n_e = jnp.sum(pick, axis=1, keepdims=True)
    n_pad = jnp.floor((n_e + (CHUNK - 1)) * (1.0 / CHUNK)) * CHUNK
    e0 = lax.broadcasted_iota(jnp.int32, (N_EXPERTS, N_EXPERTS), 0)
    e1 = lax.broadcasted_iota(jnp.int32, (N_EXPERTS, N_EXPERTS), 1)
    seg = _dot(jnp.where(e1 < e0, 1.0, 0.0).astype(BF16),
               jnp.broadcast_to(n_pad, (N_EXPERTS, tm)).astype(BF16))
    posmat = seg + rank
    chunk_of = jnp.floor(posmat * (1.0 / CHUNK))
    offs_of = posmat - chunk_of * CHUNK
    chunk_row = jnp.where(pick > 0.0, chunk_of * CHUNK, -float(CHUNK))
    eye = e0 == e1
    to_row = lambda col: jnp.sum(jnp.where(eye, col, 0.0), axis=0, keepdims=True)
    seg_row = to_row(seg[:, 0:1])
    end_row = to_row(seg[:, 0:1] + n_pad)
    seg_row2 = jnp.concatenate([seg_row, seg_row], axis=1)
    end_row2 = jnp.concatenate([end_row, end_row], axis=1)
    r128 = lax.broadcasted_iota(jnp.int32, (TILE_ROWS, 2 * N_EXPERTS), 0).astype(F32)
    own2 = jnp.where(r128 >= seg_row2, jnp.where(r128 < end_row2, 1.0, 0.0), 0.0).astype(BF16)
    row_of = _dot(own2, jnp.concatenate([chunk_row, offs_of], axis=0).astype(BF16))
    w_of = _dot(own2[:, 0:N_EXPERTS], wmat.astype(BF16))
    riota = lax.broadcasted_iota(jnp.int32, (TILE_ROWS, tm), 0).astype(F32)
    hit = row_of == riota
    xg = _dot(jnp.where(hit, 1.0, 0.0).astype(BF16), h).astype(BF16)
    return xg, jnp.where(hit, w_of, 0.0).astype(BF16)


def _dispatch_kernel(pos_ref, pad_ref, h_ref, idx_ref, wt_ref, xs_ref, pw_ref, buf0, buf1, zero_buf, sems, pad_sem,
                     *, n_tiles):
    j = pl.program_id(0)
    bufs = (buf0, buf1)

    def compute(p):
        xg, pw = _dispatch_tile(h_ref[0], idx_ref[0], wt_ref[0])
        bufs[p][...] = xg
        pw_ref[...] = pw

    def issue(tile, p):
        for c in range(TILE_CHUNKS):
            _chunk_copy(bufs[p], c, xs_ref, pos_ref[tile * TILE_CHUNKS + c], sems.at[p]).start()

    def wait(p):
        pltpu.make_async_copy(bufs[p], xs_ref.at[pl.ds(0, TILE_ROWS)], sems.at[p]).wait()

    for p in range(2):
        @pl.when((j >= 2) & (j % 2 == p))
        def _(p=p):
            wait(p)

        @pl.when((j >= 1) & (j < n_tiles) & (j % 2 == p))
        def _(p=p):
            issue(j - 1, 1 - p)
            compute(p)

    @pl.when(j == 0)
    def _():
        compute(0)

    @pl.when(j == n_tiles)
    def _():
        last = (n_tiles - 1) % 2
        issue(n_tiles - 1, last)
        zero_buf[...] = jnp.zeros_like(zero_buf)

        def pad_expert(e, carry):
            def pad_issue(c, inner):
                _chunk_copy(zero_buf, 0, xs_ref, pad_ref[e] + c, pad_sem).start()
                return inner
            lax.fori_loop(0, pad_ref[N_EXPERTS + e], pad_issue, 0)

            def pad_wait(c, inner):
                _chunk_copy(zero_buf, 0, xs_ref, pad_ref[e] + c, pad_sem).wait()
                return inner
            lax.fori_loop(0, pad_ref[N_EXPERTS + e], pad_wait, 0)
            return carry
        lax.fori_loop(0, N_EXPERTS, pad_expert, 0)
        wait(last)


def _dispatch(h2, idx, wts, pos, pad_pos, n_slots):
    B, S, D = h2.shape
    tm = MOE_TILE
    nt = S // tm
    n_tiles = B * nt
    tile = lambda j: jnp.minimum(j, n_tiles - 1)
    grid_spec = pltpu.PrefetchScalarGridSpec(
        num_scalar_prefetch=2,
        grid=(n_tiles + 1,),
        in_specs=[pl.BlockSpec((1, tm, D), lambda j, pos, pad: (tile(j) // nt, tile(j) % nt, 0)),
                  pl.BlockSpec((1, TOP_K, tm), lambda j, pos, pad: (tile(j) // nt, 0, tile(j) % nt)),
                  pl.BlockSpec((1, TOP_K, tm), lambda j, pos, pad: (tile(j) // nt, 0, tile(j) % nt))],
        out_specs=[pl.BlockSpec(memory_space=pl.ANY),
                   pl.BlockSpec((TILE_ROWS, tm), lambda j, pos, pad: (tile(j), 0))],
        scratch_shapes=[pltpu.VMEM((TILE_ROWS, D), BF16),
                        pltpu.VMEM((TILE_ROWS, D), BF16),
                        pltpu.VMEM((CHUNK, D), BF16),
                        pltpu.SemaphoreType.DMA((2,)),
                        pltpu.SemaphoreType.DMA(())],
    )
    return pl.pallas_call(
        functools.partial(_dispatch_kernel, n_tiles=n_tiles),
        grid_spec=grid_spec,
        out_shape=[jax.ShapeDtypeStruct(((n_slots + N_SPARE_CHUNKS) * CHUNK, D), BF16),
                   jax.ShapeDtypeStruct((n_tiles * TILE_ROWS, tm), BF16)],
        compiler_params=_params(("arbitrary",)),
        name="moe_dispatch",
    )(pos, pad_pos, h2, idx, wts)


def _moe_tables(cnt, g_max):
    nt = cnt.shape[0]
    cc = (cnt + (CHUNK - 1)) // CHUNK
    segblk = jnp.cumsum(cc, axis=1) - cc
    tile_chunks = jnp.sum(cc, axis=1)
    prior = jnp.cumsum(cc, axis=0) - cc
    ge_cnt = (jnp.sum(cc, axis=0) + (GROUP_CHUNKS - 1)) // GROUP_CHUNKS
    gbase = jnp.cumsum(ge_cnt) - ge_cnt
    n_groups = jnp.sum(ge_cnt)
    c = jnp.arange(TILE_CHUNKS, dtype=jnp.int32)
    e_of = jnp.sum(((segblk + cc)[:, None, :] <= c[None, :, None]).astype(jnp.int32), axis=-1)
    e_of = jnp.minimum(e_of, N_EXPERTS - 1)
    seg_base = gbase[None, :] * GROUP_CHUNKS + prior - segblk
    onehot = e_of[:, :, None] == jnp.arange(N_EXPERTS, dtype=jnp.int32)[None, None, :]
    pos = jnp.sum(jnp.where(onehot, seg_base[:, None, :], 0), axis=-1) + c[None, :]
    valid = c[None, :] < tile_chunks[:, None]
    n_slots = g_max * GROUP_CHUNKS
    parity = (jnp.arange(nt, dtype=jnp.int32) % 2)[:, None]
    pos_write = jnp.where(valid, pos, n_slots + parity * TILE_CHUNKS + c[None, :]).astype(jnp.int32)
    ce = jnp.sum(cc, axis=0)
    pad_pos = jnp.concatenate([gbase * GROUP_CHUNKS + ce, ge_cnt * GROUP_CHUNKS - ce]).astype(jnp.int32)
    g = jnp.arange(g_max, dtype=jnp.int32)
    grp_e = jnp.minimum(jnp.sum(((gbase + ge_cnt)[None, :] <= g[:, None]).astype(jnp.int32), axis=1),
                        N_EXPERTS - 1).astype(jnp.int32)
    hi = lax.Precision.HIGHEST
    experts = jnp.arange(N_EXPERTS, dtype=jnp.int32)
    slot = jnp.arange(n_slots, dtype=jnp.int32)
    e_slot = jnp.minimum(jnp.sum((((gbase + ge_cnt) * GROUP_CHUNKS)[None, :] <= slot[:, None]).astype(jnp.int32),
                                 axis=1), N_EXPERTS - 1)
    oh_e = (e_slot[:, None] == experts[None, :]).astype(F32)
    q = slot - jnp.sum(oh_e * (gbase * GROUP_CHUNKS).astype(F32)[None, :], axis=1).astype(jnp.int32)
    slot_valid = q < jnp.sum(oh_e * ce.astype(F32)[None, :], axis=1).astype(jnp.int32)
    cum_end = jnp.dot(oh_e, (prior + cc).astype(F32).T, precision=hi)
    t_slot = jnp.minimum(jnp.sum((cum_end <= q[:, None].astype(F32)).astype(jnp.int32), axis=1), nt - 1)
    oh_t = (t_slot[:, None] == jnp.arange(nt, dtype=jnp.int32)[None, :]).astype(F32)
    shift = jnp.sum(jnp.dot(oh_t, (segblk - prior).astype(F32), precision=hi) * oh_e, axis=1).astype(jnp.int32)
    spare = nt * TILE_CHUNKS + ((slot // GROUP_CHUNKS) % 2) * GROUP_CHUNKS + slot % GROUP_CHUNKS
    inv = jnp.where(slot_valid, t_slot * TILE_CHUNKS + q + shift, spare).astype(jnp.int32)
    tiles = jnp.arange(nt, dtype=jnp.int32)
    fill = jnp.concatenate([tiles * TILE_CHUNKS + tile_chunks, TILE_CHUNKS - tile_chunks]).astype(jnp.int32)
    return (pos_write.reshape(-1), inv, pad_pos.reshape(-1), fill, grp_e,
            n_groups.reshape(1).astype(jnp.int32))


def _ffn_kernel(ge_ref, na_ref, inv_ref, fill_ref, x_ref, wg_ref, wu_ref, wd_ref, yt_ref,
                buf0, buf1, zero_buf, wg_s, wu_s, wd_s, sems, fill_sem, *, n_tiles):
    g = pl.program_id(0)
    na = na_ref[0]
    bufs = (buf0, buf1)

    g_live = jnp.minimum(g, jnp.maximum(na - 1, 0))
    @pl.when((g < na) & ((g == 0) | (ge_ref[g_live] != ge_ref[jnp.maximum(g_live - 1, 0)])))
    def _():
        wg_s[...] = wg_ref[0, 0].astype(BF16)
        wu_s[...] = wu_ref[0, 0].astype(BF16)
        wd_s[...] = wd_ref[0, 0].astype(BF16)

    def compute(p):
        x = x_ref[...]
        gate = _dot(x, wg_s[...])
        up = _dot(x, wu_s[...])
        act = (_silu(gate) * up).astype(BF16)
        bufs[p][...] = _dot(act, wd_s[...]).astype(BF16)

    def issue(grp, p):
        for c in range(GROUP_CHUNKS):
            _chunk_copy(bufs[p], c, yt_ref, inv_ref[grp * GROUP_CHUNKS + c], sems.at[p]).start()

    def wait(p):
        pltpu.make_async_copy(bufs[p], yt_ref.at[pl.ds(0, GROUP_ROWS)], sems.at[p]).wait()

    @pl.when(g == 0)
    def _():
        zero_buf[...] = jnp.zeros_like(zero_buf)

        def fill_tile(t, carry):
            def fill_issue(c, inner):
                _chunk_copy(zero_buf, 0, yt_ref, fill_ref[t] + c, fill_sem).start()
                return inner
            lax.fori_loop(0, fill_ref[n_tiles + t], fill_issue, 0)

            def fill_wait(c, inner):
                _chunk_copy(zero_buf, 0, yt_ref, fill_ref[t] + c, fill_sem).wait()
                return inner
            lax.fori_loop(0, fill_ref[n_tiles + t], fill_wait, 0)
            return carry
        lax.fori_loop(0, n_tiles, fill_tile, 0)

    for p in range(2):
        @pl.when((g >= 2) & (g - 2 < na) & (g % 2 == p))
        def _(p=p):
            wait(p)

        @pl.when((g >= 1) & (g < na) & (g % 2 == p))
        def _(p=p):
            issue(g - 1, 1 - p)
            compute(p)

        @pl.when((g >= 1) & (g == na) & (g % 2 == p))
        def _(p=p):
            issue(g - 1, 1 - p)

    @pl.when((g == 0) & (na > 0))
    def _():
        compute(0)


def _ffn_grouped(xs, inv, fill, grp_e, n_groups, layer, w_gate, w_up, w_down, g_max, n_tiles):
    D = xs.shape[1]
    live = lambda g, na: jnp.minimum(g, jnp.maximum(na[0] - 1, 0))
    wspec = lambda shape: pl.BlockSpec((1, 1) + shape, lambda g, ge, na, inv, fill: (layer, ge[live(g, na)], 0, 0))
    grid_spec = pltpu.PrefetchScalarGridSpec(
        num_scalar_prefetch=4,
        grid=(g_max + 2,),
        in_specs=[pl.BlockSpec((GROUP_ROWS, D), lambda g, ge, na, inv, fill: (live(g, na), 0)),
                  wspec((D, EXPERT_FF)), wspec((D, EXPERT_FF)), wspec((EXPERT_FF, D))],
        out_specs=pl.BlockSpec(memory_space=pl.ANY),
        scratch_shapes=[pltpu.VMEM((GROUP_ROWS, D), BF16),
                        pltpu.VMEM((GROUP_ROWS, D), BF16),
                        pltpu.VMEM((CHUNK, D), BF16),
                        pltpu.VMEM((D, EXPERT_FF), BF16),
                        pltpu.VMEM((D, EXPERT_FF), BF16),
                        pltpu.VMEM((EXPERT_FF, D), BF16),
                        pltpu.SemaphoreType.DMA((2,)),
                        pltpu.SemaphoreType.DMA(())],
    )
    return pl.pallas_call(
        functools.partial(_ffn_kernel, n_tiles=n_tiles),
        grid_spec=grid_spec,
        out_shape=jax.ShapeDtypeStruct(((n_tiles * TILE_CHUNKS + 2 * GROUP_CHUNKS) * CHUNK, D), BF16),
        compiler_params=_params(("arbitrary",)),
        name="moe_ffn",
    )(grp_e, n_groups, inv, fill, xs, w_gate, w_up, w_down)


def _combine_kernel(yt_ref, pw_ref, base_ref, mod_ref, fg_ref, o_ref, *, final):
    routed = _dot_tn(pw_ref[...], yt_ref[...])
    out = base_ref[...] + mod_ref[0, 5:6, :] * routed
    if final:
        out = out * lax.rsqrt(jnp.mean(out * out, axis=-1, keepdims=True) + EPS) * fg_ref[...]
    o_ref[...] = out


def _combine(yt, pw, base, mod, mod_row, final_g):
    B, S, D = base.shape
    tm = MOE_TILE
    nt = S // tm
    final = final_g is not None
    fg = (final_g if final else jnp.ones((D,), F32)).reshape(1, D)
    out = pl.pallas_call(
        functools.partial(_combine_kernel, final=final),
        grid=(B * nt,),
        in_specs=[pl.BlockSpec((TILE_ROWS, D), lambda i: (i, 0)),
                  pl.BlockSpec((TILE_ROWS, tm), lambda i: (i, 0)),
                  pl.BlockSpec((tm, D), lambda i: (i, 0)),
                  pl.BlockSpec((1, 6, D), lambda i: (mod_row(i // nt), 0, 0)),
                  pl.BlockSpec((1, D), lambda i: (0, 0))],
        out_specs=pl.BlockSpec((tm, D), lambda i: (i, 0)),
        out_shape=jax.ShapeDtypeStruct((B * S, D), F32),
        compiler_params=_params(("parallel",)),
        name="moe_combine",
    )(yt, pw, base.reshape(B * S, D), mod, fg)
    return out.reshape(B, S, D)


def _moe_sparse(h2, idx, wts, cnt, base, mod, mod_row, layer, w_gate, w_up, w_down, final_g=None):
    B, S, D = h2.shape
    n_tiles = B * (S // MOE_TILE)
    g_max = (n_tiles * TILE_CHUNKS + N_PAD_CHUNKS + GROUP_CHUNKS - 1) // GROUP_CHUNKS
    pos_write, inv, pad_pos, fill, grp_e, n_groups = _moe_tables(cnt[:, :, 0], g_max)
    xs, pw = _dispatch(h2, idx, wts, pos_write, pad_pos, g_max * GROUP_CHUNKS)
    yt = _ffn_grouped(xs, inv, fill, grp_e, n_groups, layer, w_gate, w_up, w_down, g_max, n_tiles)
    return _combine(yt, pw, base, mod, mod_row, final_g)


def _zero_state(batch):
    nq = M_HEADS * M_QK_DIM
    return (jnp.zeros((batch, 2, nq, M_V_DIM), F32),
            jnp.zeros((batch, 2, 1, nq), F32),
            jnp.zeros((batch, 2, 1, LANES), F32))


def kernel(x, c, ctx, c_ctx, ada_w, ada_b, norm1_g, norm2_g, w_in, attn_sink, mlstm_gate_b, mlstm_norm_g, conv_w, w_br_attn, w_br_mlstm, w_br_conv, branch_gate_b, w_out, router_w, router_bias, exp_w_gate, exp_w_up, exp_w_down, sh_w_gate, sh_w_up, sh_w_down, final_g):
    B, S, D = x.shape
    L = ctx.shape[1]
    depth = ada_w.shape[0]
    ctx_row = B

    pad_rows = (-(B + 1)) % 8
    cc = jnp.concatenate([c, c_ctx[None, :], jnp.zeros((pad_rows, D), F32)], axis=0)
    mod_all = _ada(cc, ada_w, ada_b).reshape(depth, B + 1 + pad_rows, 6, D)

    cos_t, sin_t = _rope_tables(S)
    cos_c = jnp.ones((L, LANES), F32)
    sin_c = jnp.zeros((L, LANES), F32)
    col_idx = _proj_column_index()
    att_idx = _attn_row_index()
    lat_row = lambda b: b
    ctx_mod = lambda b: ctx_row

    xc = ctx
    for l in range(depth):
        need_ctx = l < depth - 1
        mod = mod_all[l]
        w_ext = jnp.concatenate([w_in[l], jnp.zeros((D, 1), F32)], axis=1)
        w_p = jnp.take(w_ext, col_idx, axis=1).astype(BF16)
        lw = {
            'mlstm_norm_g': mlstm_norm_g[l].reshape(1, M_OUT),
            'conv_w': conv_w[l],
            'branch_gate_b': branch_gate_b[l].reshape(1, N_BRANCH * D),
            'w_br_attn': jnp.take(w_br_attn[l], att_idx, axis=0).astype(BF16),
            'w_br_mlstm': w_br_mlstm[l].astype(BF16),
            'w_br_conv': w_br_conv[l].astype(BF16),
            'w_out': w_out[l].astype(BF16),
            'norm2_g': norm2_g[l].reshape(1, D),
            'router_wt': _split_hi_lo(router_w[l].T),
            'router_bias': router_bias[l].reshape(N_EXPERTS, 1),
            'sh_gu': jnp.concatenate([sh_w_gate[l], sh_w_up[l]], axis=1).astype(BF16),
            'sh_d': sh_w_down[l].astype(BF16),
        }
        experts = (l, exp_w_gate, exp_w_up, exp_w_down)

        pc = _in_proj(xc, mod, ctx_mod, norm1_g[l], w_p, cos_c, sin_c, tm=256)
        p = _in_proj(x, mod, lat_row, norm1_g[l], w_p, cos_t, sin_t, tm=min(512, S))
        qs_c, k_c, v_c, qm_c, km_c, vm_c, om_c, gm_c, bc_c, u_c, gt_c = pc
        qs, k, v, qm, km, vm, om, gm, bc, u, gt = p

        oat = _attention(qs, k, v, k_c, v_c, attn_sink[l], band=True)
        hf_c, hb_c, st = _mlstm(qm_c, km_c, vm_c, gm_c, mlstm_gate_b[l], _zero_state(B))
        hf, hb, _ = _mlstm(qm, km, vm, gm, mlstm_gate_b[l], st)

        base, h2, idx, wts, cnt = _merge(x, mod, lat_row, oat, hf, hb, om, u, bc, gt, lw, tm=MOE_TILE)
        x_new = _moe_sparse(h2, idx, wts, cnt, base, mod, lat_row, *experts,
                            final_g=final_g if l == depth - 1 else None)

        if need_ctx:
            oat_c = _attention(qs_c, None, None, k_c, v_c, attn_sink[l], band=False)
            base_c, h2_c, idx_c, wts_c, cnt_c = _merge(xc, mod, ctx_mod, oat_c, hf_c, hb_c, om_c, u_c, bc_c, gt_c, lw,
                                                       tm=MOE_TILE)
            xc = _moe_sparse(h2_c, idx_c, wts_c, cnt_c, base_c, mod, ctx_mod, *experts)
        x = x_new
    return x
```

```python
import functools

import numpy as np
import jax
import jax.numpy as jnp
from jax import lax
from jax.experimental import pallas as pl
from jax.experimental.pallas import tpu as pltpu

F32 = jnp.float32
BF16 = jnp.bfloat16

D_MODEL = 1024
GRID_W = 64
EPS = 1e-6
ATT_HEADS = 8
ATT_KV_HEADS = 2
ATT_HEAD_DIM = 64
ATT_GROUP = ATT_HEADS // ATT_KV_HEADS
ATT_BLOCK = 128
ATT_OUT = ATT_HEADS * ATT_HEAD_DIM
ROPE_BASE = 10000.0
M_HEADS = 4
M_QK_DIM = 64
M_V_DIM = 128
M_CHUNK = 64
M_OUT = M_HEADS * M_V_DIM
CONV_WIDTH = 512
N_BRANCH = 3
N_EXPERTS = 64
N_GROUPS = 8
GROUP_SIZE = N_EXPERTS // N_GROUPS
TOPK_GROUPS = 4
TOP_K = 8
EXPERT_FF = 256
SHARED_FF = 256
ROUTED_SCALE = 2.5

LOG2E = 1.4426950408889634
LANES = 128
BF16_SUBLANES = 16
VMEM_LIMIT = 56 * 1024 * 1024

_SEGS = (('q', 512), ('k', 128), ('v', 128), ('qm', 256), ('km', 256), ('vm', 512), ('om', 512),
         ('bc', 512), ('cc', 512), ('xc', 512), ('gt', 3072), ('gm', 128))
_OFF = {}
_o = 0
for _n, _s in _SEGS:
    _OFF[_n] = (_o, _o + _s)
    _o += _s
N_PROJ = _o
D_IN = 6928


def _permute_w_in(w):
    d = w.shape[0]
    half = ATT_HEAD_DIM // 2
    q = w[:, 0:512].reshape(d, ATT_KV_HEADS, ATT_GROUP, half, 2).transpose(0, 2, 1, 4, 3).reshape(d, 512)
    k = w[:, 512:640].reshape(d, ATT_KV_HEADS, half, 2).transpose(0, 1, 3, 2).reshape(d, 128)
    gm = jnp.pad(w[:, 2304:2320], ((0, 0), (0, LANES - 16)))
    out = jnp.concatenate([q, k, w[:, 640:2304], w[:, 2320:D_IN], gm], axis=1)
    assert out.shape[1] == N_PROJ
    return out.astype(BF16)


def _rope_tables(seq):
    rows = seq // GRID_W
    row = jnp.repeat(jnp.arange(rows, dtype=F32), GRID_W)
    col = jnp.tile(jnp.arange(GRID_W, dtype=F32), rows)
    n_pairs = ATT_HEAD_DIM // 4
    inv_freq = ROPE_BASE ** (-jnp.arange(n_pairs, dtype=F32) / n_pairs)
    ang = jnp.concatenate([row[:, None] * inv_freq, col[:, None] * inv_freq], axis=-1)
    c, s = jnp.cos(ang), jnp.sin(ang)
    cos_t = jnp.concatenate([c, c, c, c], axis=-1)
    sin_t = jnp.concatenate([-s, s, -s, s], axis=-1)
    return cos_t, sin_t


def _dot(a, b):
    return jnp.dot(a, b, preferred_element_type=F32)


def _dot_nt(a, b):
    return lax.dot_general(a, b, (((1,), (1,)), ((), ())), preferred_element_type=F32)


def _dot_tn(a, b):
    return lax.dot_general(a, b, (((0,), (0,)), ((), ())), preferred_element_type=F32)


def _split_hi_lo(w):
    hi = w.astype(BF16)
    lo = (w - hi.astype(F32)).astype(BF16)
    return jnp.concatenate([hi, lo], axis=0)


def _sigmoid(x):
    return 1.0 / (1.0 + jnp.exp(-x))


def _sigmoid_tanh(x):
    return 0.5 * jnp.tanh(0.5 * x) + 0.5


def _silu(x):
    return x * _sigmoid(x)


def _log_sigmoid(x):
    return jnp.minimum(x, 0.0) - jnp.log(1.0 + jnp.exp(-jnp.abs(x)))


def _rms_mod(x, g, shift, scale):
    y = x * lax.rsqrt(jnp.mean(x * x, axis=-1, keepdims=True) + EPS) * g
    return y * (1.0 + scale) + shift


def _params(sem):
    return pltpu.CompilerParams(dimension_semantics=sem, vmem_limit_bytes=VMEM_LIMIT)


def _ada_kernel(c_ref, w_ref, b_ref, o_ref):
    s = _silu(c_ref[...])
    o_ref[0] = jnp.dot(s, w_ref[0], preferred_element_type=F32,
                       precision=lax.Precision.HIGHEST) + b_ref[0]


def _ada(cc, ada_w, ada_b):
    depth, d, n = ada_w.shape
    rows = cc.shape[0]
    tn = 1536
    return pl.pallas_call(
        _ada_kernel,
        grid=(depth, n // tn),
        in_specs=[pl.BlockSpec((rows, d), lambda l, j: (0, 0)),
                  pl.BlockSpec((1, d, tn), lambda l, j: (l, 0, j)),
                  pl.BlockSpec((1, 1, tn), lambda l, j: (l, 0, j))],
        out_specs=pl.BlockSpec((1, rows, tn), lambda l, j: (l, 0, j)),
        out_shape=jax.ShapeDtypeStruct((depth, rows, n), F32),
        compiler_params=_params(("parallel", "parallel")),
        name="ada_mod",
    )(cc, ada_w, ada_b.reshape(depth, 1, n))


def _swap_halves(x):
    lane = lax.broadcasted_iota(jnp.int32, x.shape, 1)
    first = (lane % ATT_HEAD_DIM) < (ATT_HEAD_DIM // 2)
    return jnp.where(first, pltpu.roll(x, LANES - 32, axis=1), pltpu.roll(x, 32, axis=1))


def _in_kernel(x_ref, mod_ref, g_ref, w_ref, cos_ref, sin_ref,
               qs_ref, k_ref, v_ref, qm_ref, km_ref, vm_ref, om_ref, gm_ref, bc_ref, u_ref, gt_ref):
    tm = x_ref.shape[1]
    h = _rms_mod(x_ref[0], g_ref[...], mod_ref[0, 0:1, :], mod_ref[0, 1:2, :]).astype(BF16)

    def proj(name):
        lo, hi = _OFF[name]
        return _dot(h, w_ref[:, lo:hi])

    cos_t = cos_ref[...]
    sin_t = sin_ref[...]

    def rope(t):
        return t * cos_t + _swap_halves(t) * sin_t

    q = proj('q')
    scale = ATT_HEAD_DIM ** -0.5 * LOG2E
    for hh in range(ATT_GROUP):
        r = (rope(q[:, hh * LANES:(hh + 1) * LANES]) * scale).astype(BF16)
        for qb in range(tm // ATT_BLOCK):
            qs_ref[0, qb, hh * ATT_BLOCK:(hh + 1) * ATT_BLOCK, :] = r[qb * ATT_BLOCK:(qb + 1) * ATT_BLOCK, :]
    k_ref[0] = rope(proj('k')).astype(BF16)
    v_ref[0] = proj('v').astype(BF16)
    qm_ref[0] = proj('qm').astype(BF16)
    km_ref[0] = proj('km').astype(BF16)
    vm_ref[0] = proj('vm').astype(BF16)
    om_ref[0] = proj('om').astype(BF16)
    gm_ref[0] = proj('gm')
    bc_ref[0] = proj('bc').astype(BF16)
    u_ref[0] = (proj('cc') * proj('xc')).astype(BF16)
    gt_ref[0] = proj('gt').astype(BF16)


def _in_proj(x, mod, mod_row, norm_g, w_p, cos_t, sin_t, tm):
    B, S, D = x.shape
    nb = S // ATT_BLOCK
    tok = lambda n, dt: jax.ShapeDtypeStruct((B, S, n), dt)
    tspec = lambda n: pl.BlockSpec((1, tm, n), lambda b, i: (b, i, 0))
    out_shape = (jax.ShapeDtypeStruct((B, nb, ATT_GROUP * ATT_BLOCK, LANES), BF16),
                 tok(128, BF16), tok(128, BF16), tok(256, BF16), tok(256, BF16), tok(512, BF16),
                 tok(512, BF16), tok(128, F32), tok(512, BF16), tok(512, BF16), tok(3072, BF16))
    out_specs = (pl.BlockSpec((1, tm // ATT_BLOCK, ATT_GROUP * ATT_BLOCK, LANES), lambda b, i: (b, i, 0, 0)),
                 tspec(128), tspec(128), tspec(256), tspec(256), tspec(512), tspec(512), tspec(128),
                 tspec(512), tspec(512), tspec(3072))
    return pl.pallas_call(
        _in_kernel,
        grid=(B, S // tm),
        in_specs=[pl.BlockSpec((1, tm, D), lambda b, i: (b, i, 0)),
                  pl.BlockSpec((1, 6, D), lambda b, i: (mod_row(b), 0, 0)),
                  pl.BlockSpec((1, D), lambda b, i: (0, 0)),
                  pl.BlockSpec((D, N_PROJ), lambda b, i: (0, 0), pipeline_mode=pl.Buffered(1)),
                  pl.BlockSpec((tm, LANES), lambda b, i: (i, 0)),
                  pl.BlockSpec((tm, LANES), lambda b, i: (i, 0))],
        out_specs=out_specs,
        out_shape=out_shape,
        compiler_params=_params(("parallel", "parallel")),
        name="in_proj",
    )(x, mod, norm_g.reshape(1, D), w_p, cos_t, sin_t)


def _attn_block(q, sink_ref, kcat, vcat, masks):
    rows = q.shape[0]
    lane = lax.broadcasted_iota(jnp.int32, (1, LANES), 1)
    hh = lax.broadcasted_iota(jnp.int32, (rows, 1), 0) // ATT_BLOCK
    out = jnp.zeros((rows, LANES), F32)
    for g in range(ATT_KV_HEADS):
        lm = (lane < ATT_HEAD_DIM) if g == 0 else (lane >= ATT_HEAD_DIM)
        kz = jnp.where(lm, kcat, jnp.zeros_like(kcat))
        ones_lane = ATT_HEAD_DIM if g == 0 else 0
        vz = jnp.where(lm, vcat, jnp.where(lane == ones_lane, 1.0, 0.0).astype(BF16))
        s = _dot_nt(q, kz)
        if any(mk is not None for mk in masks):
            s = jnp.concatenate(
                [s[:, n * ATT_BLOCK:(n + 1) * ATT_BLOCK] if mk is None
                 else jnp.where(mk, s[:, n * ATT_BLOCK:(n + 1) * ATT_BLOCK], -jnp.inf)
                 for n, mk in enumerate(masks)], axis=1)
        sink = jnp.zeros((rows, 1), F32)
        for a in range(ATT_GROUP):
            sink = jnp.where(hh == a, sink_ref[g * ATT_GROUP + a] * LOG2E, sink)
        m = jnp.maximum(jnp.max(s, axis=-1, keepdims=True), sink)
        p = jnp.exp2((s - m).astype(BF16))
        pv = _dot(p, vz)
        l = pv[:, ones_lane:ones_lane + 1] + jnp.exp2(sink - m)
        out = out + jnp.where(lm, pv, 0.0) / l
    return out.astype(BF16)


ATT_STEP_BLOCKS = 4


def _attn_kernel(sink_ref, qs_ref, kc_ref, vc_ref, *rest, band):
    nsb = qs_ref.shape[1]
    if not band:
        (o_ref,) = rest
        n_ctx = kc_ref.shape[1] // ATT_BLOCK
        for sb in range(nsb):
            o_ref[0, sb] = _attn_block(qs_ref[0, sb], sink_ref, kc_ref[0], vc_ref[0], [None] * n_ctx)
        return
    kp_ref, kcur_ref, kn_ref, vp_ref, vcur_ref, vn_ref, o_ref = rest
    j = pl.program_id(1)
    nstep = pl.num_programs(1)
    n_ctx = kc_ref.shape[1] // ATT_BLOCK
    rows = qs_ref.shape[2]
    t = lax.broadcasted_iota(jnp.int32, (rows, 1), 0) % ATT_BLOCK
    i = lax.broadcasted_iota(jnp.int32, (1, ATT_BLOCK), 1)
    below = i >= t
    above = i <= t
    first = i >= t + jnp.where(j > 0, 0, 2 * ATT_BLOCK)
    last = i <= t - jnp.where(j < nstep - 1, 0, 2 * ATT_BLOCK)
    kblk = [kp_ref[0]] + [kcur_ref[0, n * ATT_BLOCK:(n + 1) * ATT_BLOCK, :] for n in range(nsb)] + [kn_ref[0]]
    vblk = [vp_ref[0]] + [vcur_ref[0, n * ATT_BLOCK:(n + 1) * ATT_BLOCK, :] for n in range(nsb)] + [vn_ref[0]]
    for sb in range(nsb):
        kcat = jnp.concatenate([kc_ref[0]] + kblk[sb:sb + 3], axis=0)
        vcat = jnp.concatenate([vc_ref[0]] + vblk[sb:sb + 3], axis=0)
        masks = [first if sb == 0 else below, None, last if sb == nsb - 1 else above]
        o_ref[0, sb] = _attn_block(qs_ref[0, sb], sink_ref, kcat, vcat, [None] * n_ctx + masks)


def _attention(qs, k, v, kc, vc, sink, band):
    B, nb = qs.shape[:2]
    lc = kc.shape[1]
    blocks = min(ATT_STEP_BLOCKS, nb)
    nstep = nb // blocks
    last = nb - 1
    qspec = pl.BlockSpec((1, blocks, ATT_GROUP * ATT_BLOCK, LANES), lambda b, j: (b, j, 0, 0))
    cspec = pl.BlockSpec((1, lc, LANES), lambda b, j: (b, 0, 0))
    in_specs = [pl.BlockSpec(memory_space=pltpu.SMEM), qspec, cspec, cspec]
    args = [sink.astype(F32), qs, kc, vc]
    if band:
        prev = pl.BlockSpec((1, ATT_BLOCK, LANES), lambda b, j: (b, jnp.maximum(blocks * j - 1, 0), 0))
        cur = pl.BlockSpec((1, blocks * ATT_BLOCK, LANES), lambda b, j: (b, j, 0))
        nxt = pl.BlockSpec((1, ATT_BLOCK, LANES), lambda b, j: (b, jnp.minimum(blocks * (j + 1), last), 0))
        in_specs += [prev, cur, nxt, prev, cur, nxt]
        args += [k, k, k, v, v, v]
    return pl.pallas_call(
        functools.partial(_attn_kernel, band=band),
        grid=(B, nstep),
        in_specs=in_specs,
        out_specs=qspec,
        out_shape=jax.ShapeDtypeStruct(qs.shape, BF16),
        compiler_params=_params(("parallel", "parallel")),
        name="attention_band" if band else "attention_ctx",
    )(*args)


def _mlstm_step(dirs, T):
    kscale = M_QK_DIM ** -0.5
    si = lax.broadcasted_iota(jnp.int32, (T, T), 0)
    ri = lax.broadcasted_iota(jnp.int32, (T, T), 1)
    lane_qk = lax.broadcasted_iota(jnp.int32, (1, M_HEADS * M_QK_DIM), 1) // M_QK_DIM
    lane_m = lax.broadcasted_iota(jnp.int32, (1, LANES), 1)
    row_c = lax.broadcasted_iota(jnp.int32, (M_HEADS * M_QK_DIM, 1), 0) // M_QK_DIM
    combos = [(d, hd) for d in range(2) for hd in range(M_HEADS)]

    tri, bcol, gt, bt, blast = [], [], [], [], []
    for d, (q, k, v, g, C, n, m) in enumerate(dirs):
        t = (ri <= si) if d == 0 else (ri >= si)
        tri.append(t)
        lf = _log_sigmoid(g)
        bc = jnp.dot(t.astype(F32), lf, preferred_element_type=F32, precision=lax.Precision.HIGHEST)
        bcol.append(bc)
        gt.append(g.T)
        bt.append(bc.T)
        blast.append(bc[T - 1:T, :] if d == 0 else bc[0:1, :])

    def lanes(d, hd):
        return (2 * d) * M_HEADS + hd, (2 * d + 1) * M_HEADS + hd

    b_col = {c: bcol[c[0]][:, lanes(*c)[1]:lanes(*c)[1] + 1] for c in combos}
    ig_col = {c: dirs[c[0]][3][:, lanes(*c)[0]:lanes(*c)[0] + 1] for c in combos}
    alpha = {c: gt[c[0]][lanes(*c)[0]:lanes(*c)[0] + 1, :] - bt[c[0]][lanes(*c)[1]:lanes(*c)[1] + 1, :]
             for c in combos}
    m_old = {c: dirs[c[0]][6][:, c[1]:c[1] + 1] for c in combos}
    b_last = {c: blast[c[0]][:, lanes(*c)[1]:lanes(*c)[1] + 1] for c in combos}
    hmask = {hd: lane_qk == hd for hd in range(M_HEADS)}

    a_mat = {c: jnp.where(tri[c[0]], alpha[c], -jnp.inf) for c in combos}
    a_max = {c: jnp.max(a_mat[c], axis=1, keepdims=True) for c in combos}
    a_int = {c: b_col[c] + m_old[c] for c in combos}
    m_s = {c: jnp.maximum(a_int[c], b_col[c] + a_max[c]) for c in combos}
    w_int = {c: jnp.exp(a_int[c] - m_s[c]) for c in combos}
    w_mat = {c: jnp.exp(a_mat[c] + (b_col[c] - m_s[c])) for c in combos}
    qmask = {c: jnp.where(hmask[c[1]], dirs[c[0]][0], jnp.zeros_like(dirs[c[0]][0])) for c in combos}
    s_qk = {c: w_mat[c] * (_dot_nt(qmask[c], dirs[c[0]][1]) * kscale) for c in combos}
    vh = {c: dirs[c[0]][2][:, c[1] * M_V_DIM:(c[1] + 1) * M_V_DIM] for c in combos}
    c_bf = [dirs[d][4].astype(BF16) for d in range(2)]
    num = {c: _dot(s_qk[c].astype(BF16), vh[c]) + w_int[c] * _dot(qmask[c], c_bf[c[0]]) for c in combos}
    qn_all = [dirs[d][0].astype(F32) * dirs[d][5] for d in range(2)]
    qn = {c: jnp.sum(jnp.where(hmask[c[1]], qn_all[c[0]], 0.0), axis=1, keepdims=True) for c in combos}
    den = {c: jnp.sum(s_qk[c], axis=1, keepdims=True) + w_int[c] * qn[c] for c in combos}
    h = {c: num[c] / jnp.maximum(jnp.abs(den[c]), jnp.exp(-m_s[c])) for c in combos}

    r_col = {c: b_last[c] - b_col[c] + ig_col[c] for c in combos}
    m_new = {c: jnp.maximum(b_last[c] + m_old[c], jnp.max(r_col[c], axis=0, keepdims=True)) for c in combos}
    decay = {c: jnp.exp(b_last[c] + m_old[c] - m_new[c]) for c in combos}
    w_r = {c: jnp.exp(r_col[c] - m_new[c]) for c in combos}

    outs = []
    for d, (q, k, v, g, C, n, m) in enumerate(dirs):
        w_lanes = jnp.zeros((T, M_HEADS * M_QK_DIM), F32)
        dec_lanes = jnp.zeros((1, M_HEADS * M_QK_DIM), F32)
        dec_rows = jnp.zeros((M_HEADS * M_QK_DIM, 1), F32)
        m_row = jnp.zeros((1, LANES), F32)
        for hd in range(M_HEADS):
            w_lanes = jnp.where(hmask[hd], w_r[(d, hd)], w_lanes)
            dec_lanes = jnp.where(hmask[hd], decay[(d, hd)], dec_lanes)
            dec_rows = jnp.where(row_c == hd, decay[(d, hd)], dec_rows)
            m_row = jnp.where(lane_m == hd, m_new[(d, hd)], m_row)
        kw = k.astype(F32) * (w_lanes * kscale)
        kwt = kw.T.astype(BF16)
        upd = jnp.concatenate(
            [_dot(kwt[hd * M_QK_DIM:(hd + 1) * M_QK_DIM, :], vh[(d, hd)]) for hd in range(M_HEADS)], axis=0)
        c_new = dec_rows * C + upd
        n_new = dec_lanes * n + jnp.sum(kw, axis=0, keepdims=True)
        h_all = jnp.concatenate([h[(d, hd)] for hd in range(M_HEADS)], axis=1)
        outs.append((h_all, c_new, n_new, m_row))
    return outs


def _mlstm_kernel(gb_ref, qf_ref, kf_ref, vf_ref, gf_ref, qb_ref, kb_ref, vb_ref, gbk_ref,
                  c0_ref, n0_ref, m0_ref, hf_ref, hb_ref, cf_ref, nf_ref, mf_ref,
                  c_s, n_s, m_s):
    ci = pl.program_id(1)
    T = MLSTM_TILE
    n_sub = qf_ref.shape[1] // T

    @pl.when(ci == 0)
    def _():
        c_s[...] = c0_ref[0]
        n_s[...] = n0_ref[0]
        m_s[...] = m0_ref[0]

    gb = gb_ref[...]
    state = [(c_s[d], n_s[d], m_s[d]) for d in range(2)]
    refs = ((qf_ref, kf_ref, vf_ref, gf_ref, hf_ref), (qb_ref, kb_ref, vb_ref, gbk_ref, hb_ref))
    for s in range(n_sub):
        lo = (s * T, (n_sub - 1 - s) * T)
        dirs = [(q_ref[0, lo[d]:lo[d] + T, :], k_ref[0, lo[d]:lo[d] + T, :], v_ref[0, lo[d]:lo[d] + T, :],
                 g_ref[0, lo[d]:lo[d] + T, :] + gb) + state[d]
                for d, (q_ref, k_ref, v_ref, g_ref, _) in enumerate(refs)]
        outs = _mlstm_step(dirs, T)
        for d in range(2):
            refs[d][4][0, lo[d]:lo[d] + T, :] = outs[d][0]
        state = [outs[d][1:] for d in range(2)]
    for d in range(2):
        c_s[d], n_s[d], m_s[d] = state[d]

    @pl.when(ci == pl.num_programs(1) - 1)
    def _():
        cf_ref[0] = c_s[...]
        nf_ref[0] = n_s[...]
        mf_ref[0] = m_s[...]


MLSTM_TILE = 128
MLSTM_STEP_CHUNKS = 4


def _mlstm(qm, km, vm, gm, gate_b, state):
    B, S, _ = qm.shape
    T = MLSTM_TILE * min(MLSTM_STEP_CHUNKS, S // MLSTM_TILE)
    nc = S // T
    nq = M_HEADS * M_QK_DIM
    fwd = lambda n: pl.BlockSpec((1, T, n), lambda b, c: (b, c, 0))
    bwd = lambda n: pl.BlockSpec((1, T, n), lambda b, c: (b, nc - 1 - c, 0))
    st_specs = [pl.BlockSpec((1, 2, nq, M_V_DIM), lambda b, c: (b, 0, 0, 0)),
                pl.BlockSpec((1, 2, 1, nq), lambda b, c: (b, 0, 0, 0)),
                pl.BlockSpec((1, 2, 1, LANES), lambda b, c: (b, 0, 0, 0))]
    st_shapes = [jax.ShapeDtypeStruct((B, 2, nq, M_V_DIM), F32),
                 jax.ShapeDtypeStruct((B, 2, 1, nq), F32),
                 jax.ShapeDtypeStruct((B, 2, 1, LANES), F32)]
    gb_row = jnp.pad(gate_b.reshape(1, -1).astype(F32), ((0, 0), (0, LANES - gate_b.size)))
    outs = pl.pallas_call(
        _mlstm_kernel,
        grid=(B, nc),
        in_specs=[pl.BlockSpec((1, LANES), lambda b, c: (0, 0)),
                  fwd(256), fwd(256), fwd(512), fwd(LANES),
                  bwd(256), bwd(256), bwd(512), bwd(LANES)] + st_specs,
        out_specs=[fwd(M_OUT), bwd(M_OUT)] + st_specs,
        out_shape=[jax.ShapeDtypeStruct((B, S, M_OUT), F32)] * 2 + st_shapes,
        scratch_shapes=[pltpu.VMEM((2, nq, M_V_DIM), F32),
                        pltpu.VMEM((2, 1, nq), F32),
                        pltpu.VMEM((2, 1, LANES), F32)],
        compiler_params=_params(("parallel", "arbitrary")),
        name="mlstm_scan",
    )(gb_row, qm, km, vm, gm, qm, km, vm, gm, *state)
    return outs[0], outs[1], tuple(outs[2:])


def _route(scores, sel):
    tm = scores.shape[1]
    gi8 = lax.broadcasted_iota(jnp.int32, (GROUP_SIZE, tm), 0)

    def stack_rows(rows):
        out = jnp.broadcast_to(rows[0], (len(rows), tm))
        for r, v in enumerate(rows[1:], start=1):
            out = jnp.where(gi8 == r, v, out)
        return out

    gs = []
    for g in range(N_GROUPS):
        blk = sel[g * GROUP_SIZE:(g + 1) * GROUP_SIZE, :]
        m1 = jnp.max(blk, axis=0, keepdims=True)
        first = jnp.min(jnp.where(blk == m1, gi8, GROUP_SIZE), axis=0, keepdims=True)
        m2 = jnp.max(jnp.where(gi8 == first, -jnp.inf, blk), axis=0, keepdims=True)
        gs.append(m1 + m2)
    gsc = stack_rows(gs)
    gsel = jnp.zeros((N_GROUPS, tm), F32)
    for _ in range(TOPK_GROUPS):
        mx = jnp.max(gsc, axis=0, keepdims=True)
        first = jnp.min(jnp.where(gsc == mx, gi8, N_GROUPS), axis=0, keepdims=True)
        pick = gi8 == first
        gsel = jnp.where(pick, 1.0, gsel)
        gsc = jnp.where(pick, -jnp.inf, gsc)
    cur = jnp.concatenate(
        [jnp.where(gsel[g:g + 1, :] > 0.0, sel[g * GROUP_SIZE:(g + 1) * GROUP_SIZE, :], -jnp.inf)
         for g in range(N_GROUPS)], axis=0)
    ei = lax.broadcasted_iota(jnp.int32, (N_EXPERTS, tm), 0)
    idx, wts = [], []
    for _ in range(TOP_K):
        mx = jnp.max(cur, axis=0, keepdims=True)
        first = jnp.min(jnp.where(cur == mx, ei, N_EXPERTS), axis=0, keepdims=True)
        pick = ei == first
        idx.append(first)
        wts.append(jnp.sum(jnp.where(pick, scores, 0.0), axis=0, keepdims=True))
        cur = jnp.where(pick, -jnp.inf, cur)
    tot = wts[0]
    for w in wts[1:]:
        tot = tot + w
    wts = [w / tot * ROUTED_SCALE for w in wts]
    return stack_rows(idx), stack_rows(wts)


def _merge_kernel(x_ref, mod_ref, oat_ref, hf_ref, hb_ref, om_ref, ng_ref, u_ref, up_ref, un_ref,
                  bc_ref, cw_ref, gt_ref, bgb_ref, wa_ref, wm_ref, wc_ref, wo_ref, n2_ref,
                  rw_ref, rb_ref, sgu_ref, sd_ref,
                  base_ref, h2_ref, idx_ref, wt_ref, cnt_ref):
    i = pl.program_id(1)
    tm = x_ref.shape[1]
    x = x_ref[0]
    g1 = mod_ref[0, 2:3, :]
    sh2 = mod_ref[0, 3:4, :]
    sc2 = mod_ref[0, 4:5, :]
    g2 = mod_ref[0, 5:6, :]

    ya = jnp.concatenate(
        [jnp.concatenate([oat_ref[0, qb, hh * ATT_BLOCK:(hh + 1) * ATT_BLOCK, :] for hh in range(ATT_GROUP)], axis=1)
         for qb in range(tm // ATT_BLOCK)], axis=0)

    hsum = hf_ref[0] + hb_ref[0]
    parts = []
    for hd in range(M_HEADS):
        hh_ = hsum[:, hd * M_V_DIM:(hd + 1) * M_V_DIM]
        parts.append(hh_ * lax.rsqrt(jnp.mean(hh_ * hh_, axis=-1, keepdims=True) + EPS))
    hn = jnp.concatenate(parts, axis=1) * ng_ref[...]
    ym = (_sigmoid_tanh(om_ref[0].astype(F32)) * hn).astype(BF16)

    u = u_ref[0].astype(F32)
    row = lax.broadcasted_iota(jnp.int32, (tm, 1), 0)
    has_prev = (i > 0).astype(F32)
    has_next = (i < pl.num_programs(1) - 1).astype(F32)
    prev_row = up_ref[0, BF16_SUBLANES - 1:BF16_SUBLANES, :].astype(F32) * has_prev
    next_row = un_ref[0, 0:1, :].astype(F32) * has_next
    u_m1 = jnp.where(row == 0, prev_row, pltpu.roll(u, 1, axis=0))
    u_p1 = jnp.where(row == tm - 1, next_row, pltpu.roll(u, tm - 1, axis=0))
    conv = cw_ref[0:1, :] * u_m1 + cw_ref[1:2, :] * u + cw_ref[2:3, :] * u_p1
    yc = (bc_ref[0].astype(F32) * conv).astype(BF16)

    gg = _sigmoid_tanh(gt_ref[0].astype(F32) + bgb_ref[...])
    ymix = (gg[:, 0:D_MODEL] * _dot(ya, wa_ref[...])
            + gg[:, D_MODEL:2 * D_MODEL] * _dot(ym, wm_ref[...])
            + gg[:, 2 * D_MODEL:3 * D_MODEL] * _dot(yc, wc_ref[...]))
    y = _dot(ymix.astype(BF16), wo_ref[...])
    xm = x + g1 * y

    h2f = _rms_mod(xm, n2_ref[...], sh2, sc2)
    h2 = h2f.astype(BF16)
    h2_ref[0] = h2

    h2_lo = (h2f - h2.astype(F32)).astype(BF16)
    rw = rw_ref[...]
    part = _dot_nt(rw, h2)
    logits_t = part[0:N_EXPERTS, :] + part[N_EXPERTS:2 * N_EXPERTS, :] + _dot_nt(rw[0:N_EXPERTS, :], h2_lo)
    scores = _sigmoid(logits_t)
    idx, wts = _route(scores, scores + rb_ref[...])
    idx_ref[0] = idx
    wt_ref[0] = wts
    ei = lax.broadcasted_iota(jnp.int32, (N_EXPERTS, tm), 0)
    pick = jnp.zeros((N_EXPERTS, tm), F32)
    for kk in range(TOP_K):
        pick = jnp.where(ei == idx[kk:kk + 1, :], 1.0, pick)
    for sub in range(tm // MOE_TILE):
        n_e = jnp.sum(pick[:, sub * MOE_TILE:(sub + 1) * MOE_TILE], axis=1, keepdims=True)
        cnt_ref[sub] = jnp.broadcast_to(n_e, (N_EXPERTS, LANES)).astype(jnp.int32)

    a = _dot(h2, sgu_ref[...])
    act = (_silu(a[:, 0:SHARED_FF]) * a[:, SHARED_FF:2 * SHARED_FF]).astype(BF16)
    base_ref[0] = xm + g2 * _dot(act, sd_ref[...])


def _merge(x, mod, mod_row, oat, hf, hb, om, u, bc, gt, lw, tm):
    B, S, D = x.shape
    nt = S // tm
    hal = BF16_SUBLANES
    last_h = S // hal - 1
    tspec = lambda n: pl.BlockSpec((1, tm, n), lambda b, i: (b, i, 0))
    full = lambda a: pl.BlockSpec(a.shape, lambda b, i: (0,) * a.ndim, pipeline_mode=pl.Buffered(1))
    weights = [lw['mlstm_norm_g'], lw['conv_w'], lw['branch_gate_b'], lw['w_br_attn'], lw['w_br_mlstm'],
               lw['w_br_conv'], lw['w_out'], lw['norm2_g'], lw['router_wt'], lw['router_bias'],
               lw['sh_gu'], lw['sh_d']]
    in_specs = [tspec(D),
                pl.BlockSpec((1, 6, D), lambda b, i: (mod_row(b), 0, 0)),
                pl.BlockSpec((1, tm // ATT_BLOCK, ATT_GROUP * ATT_BLOCK, LANES), lambda b, i: (b, i, 0, 0)),
                tspec(M_OUT), tspec(M_OUT), tspec(M_OUT), full(weights[0]),
                tspec(CONV_WIDTH),
                pl.BlockSpec((1, hal, CONV_WIDTH), lambda b, i: (b, jnp.maximum(i * (tm // hal) - 1, 0), 0)),
                pl.BlockSpec((1, hal, CONV_WIDTH), lambda b, i: (b, jnp.minimum((i + 1) * (tm // hal), last_h), 0)),
                tspec(CONV_WIDTH), full(weights[1]), tspec(N_BRANCH * D), full(weights[2])]
    in_specs += [full(w) for w in weights[3:]]
    tr = lambda n, dt: (jax.ShapeDtypeStruct((B, n, S), dt), pl.BlockSpec((1, n, tm), lambda b, i: (b, 0, i)))
    outs = [(jax.ShapeDtypeStruct((B, S, D), F32), tspec(D)),
            (jax.ShapeDtypeStruct((B, S, D), BF16), tspec(D)),
            tr(TOP_K, jnp.int32), tr(TOP_K, F32),
            (jax.ShapeDtypeStruct((B * S // MOE_TILE, N_EXPERTS, LANES), jnp.int32),
             pl.BlockSpec((tm // MOE_TILE, N_EXPERTS, LANES), lambda b, i: (b * nt + i, 0, 0)))]
    return pl.pallas_call(
        _merge_kernel,
        grid=(B, nt),
        in_specs=in_specs,
        out_specs=[o[1] for o in outs],
        out_shape=[o[0] for o in outs],
        compiler_params=_params(("parallel", "parallel")),
        name="merge_route",
    )(x, mod, oat, hf, hb, om, weights[0], u, u, u, bc, weights[1], gt, weights[2], *weights[3:])


MOE_TILE = 256
CHUNK = BF16_SUBLANES
GROUP_CHUNKS = 64
GROUP_ROWS = GROUP_CHUNKS * CHUNK
TILE_ROWS = MOE_TILE * TOP_K + N_EXPERTS * CHUNK
TILE_CHUNKS = TILE_ROWS // CHUNK
N_PAD_CHUNKS = N_EXPERTS * (GROUP_CHUNKS - 1)
N_SPARE_CHUNKS = 2 * TILE_CHUNKS


def _dispatch_tile(h, idx, wts):
    tm = h.shape[0]
    ei = lax.broadcasted_iota(jnp.int32, (N_EXPERTS, tm), 0)
    pick = jnp.zeros((N_EXPERTS, tm), F32)
    wmat = jnp.zeros((N_EXPERTS, tm), F32)
    for kk in range(TOP_K):
        chosen = ei == idx[kk:kk + 1, :]
        pick = jnp.where(chosen, 1.0, pick)
        wmat = jnp.where(chosen, wts[kk:kk + 1, :], wmat)
    t0 = lax.broadcasted_iota(jnp.int32, (tm, tm), 0)
    t1 = lax.broadcasted_iota(jnp.int32, (tm, tm), 1)
    rank = _dot(pick.astype(BF16), jnp.where(t0 < t1, 1.0, 0.0).astype(BF16))
    n_e = jnp.sum(pick, axis=1, keepdims=True)
    n_pad = jnp.floor((n_e + (CHUNK - 1)) * (1.0 / CHUNK)) * CHUNK
    e0 = lax.broadcasted_iota(jnp.int32, (N_EXPERTS, N_EXPERTS), 0)
    e1 = lax.broadcasted_iota(jnp.int32, (N_EXPERTS, N_EXPERTS), 1)
    seg = _dot(jnp.where(e1 < e0, 1.0, 0.0).astype(BF16),
               jnp.broadcast_to(n_pad, (N_EXPERTS, tm)).astype(BF16))
    posmat = seg + rank
    chunk_of = jnp.floor(posmat * (1.0 / CHUNK))
    offs_of = posmat - chunk_of * CHUNK
    chunk_row = jnp.where(pick > 0.0, chunk_of * CHUNK, -float(CHUNK))
    eye = e0 == e1
    to_row = lambda col: jnp.sum(jnp.where(eye, col, 0.0), axis=0, keepdims=True)
    seg_row = to_row(seg[:, 0:1])
    end_row = to_row(seg[:, 0:1] + n_pad)
    seg_row2 = jnp.concatenate([seg_row, seg_row], axis=1)
    end_row2 = jnp.concatenate([end_row, end_row], axis=1)
    r128 = lax.broadcasted_iota(jnp.int32, (TILE_ROWS, 2 * N_EXPERTS), 0).astype(F32)
    own2 = jnp.where(r128 >= seg_row2, jnp.where(r128 < end_row2, 1.0, 0.0), 0.0).astype(BF16)
    row_of = _dot(own2, jnp.concatenate([chunk_row, offs_of], axis=0).astype(BF16))
    w_of = _dot(own2[:, 0:N_EXPERTS], wmat.astype(BF16))
    riota = lax.broadcasted_iota(jnp.int32, (TILE_ROWS, tm), 0).astype(F32)
    hit = row_of == riota
    xg = _dot(jnp.where(hit, 1.0, 0.0).astype(BF16), h).astype(BF16)
    return xg, jnp.where(hit, w_of, 0.0).astype(BF16)


def _dispatch_kernel(pos_ref, pad_ref, h_ref, idx_ref, wt_ref, xs_ref, pw_ref, buf0, buf1, zero_buf, sems, pad_sem,
                     *, n_tiles):
    j = pl.program_id(0)
    bufs = (buf0, buf1)

    def compute(p):
        xg, pw = _dispatch_tile(h_ref[0], idx_ref[0], wt_ref[0])
        bufs[p][...] = xg.reshape(bufs[p].shape)
        pw_ref[...] = pw

    def issue(tile, p):
        for c in range(TILE_CHUNKS):
            pltpu.make_async_copy(bufs[p].at[c], xs_ref.at[pos_ref[tile * TILE_CHUNKS + c]], sems.at[p]).start()

    def wait(p):
        pltpu.make_async_copy(bufs[p], xs_ref.at[pl.ds(0, TILE_CHUNKS)], sems.at[p]).wait()

    for p in range(2):
        @pl.when((j >= 2) & (j % 2 == p))
        def _(p=p):
            wait(p)

        @pl.when((j >= 1) & (j < n_tiles) & (j % 2 == p))
        def _(p=p):
            issue(j - 1, 1 - p)
            compute(p)

    @pl.when(j == 0)
    def _():
        compute(0)

    @pl.when(j == n_tiles)
    def _():
        last = (n_tiles - 1) % 2
        issue(n_tiles - 1, last)
        zero_buf[...] = jnp.zeros_like(zero_buf)

        def pad_expert(e, carry):
            def pad_issue(c, inner):
                pltpu.make_async_copy(zero_buf, xs_ref.at[pad_ref[e] + c], pad_sem).start()
                return inner
            lax.fori_loop(0, pad_ref[N_EXPERTS + e], pad_issue, 0)

            def pad_wait(c, inner):
                pltpu.make_async_copy(zero_buf, xs_ref.at[pad_ref[e] + c], pad_sem).wait()
                return inner
            lax.fori_loop(0, pad_ref[N_EXPERTS + e], pad_wait, 0)
            return carry
        lax.fori_loop(0, N_EXPERTS, pad_expert, 0)
        wait(last)


def _dispatch(h2, idx, wts, pos, pad_pos, n_slots):
    B, S, D = h2.shape
    tm = MOE_TILE
    nt = S // tm
    n_tiles = B * nt
    tile = lambda j: jnp.minimum(j, n_tiles - 1)
    grid_spec = pltpu.PrefetchScalarGridSpec(
        num_scalar_prefetch=2,
        grid=(n_tiles + 1,),
        in_specs=[pl.BlockSpec((1, tm, D), lambda j, pos, pad: (tile(j) // nt, tile(j) % nt, 0)),
                  pl.BlockSpec((1, TOP_K, tm), lambda j, pos, pad: (tile(j) // nt, 0, tile(j) % nt)),
                  pl.BlockSpec((1, TOP_K, tm), lambda j, pos, pad: (tile(j) // nt, 0, tile(j) % nt))],
        out_specs=[pl.BlockSpec(memory_space=pl.ANY),
                   pl.BlockSpec((TILE_ROWS, tm), lambda j, pos, pad: (tile(j), 0))],
        scratch_shapes=[pltpu.VMEM((TILE_CHUNKS, CHUNK, D), BF16),
                        pltpu.VMEM((TILE_CHUNKS, CHUNK, D), BF16),
                        pltpu.VMEM((CHUNK, D), BF16),
                        pltpu.SemaphoreType.DMA((2,)),
                        pltpu.SemaphoreType.DMA(())],
    )
    return pl.pallas_call(
        functools.partial(_dispatch_kernel, n_tiles=n_tiles),
        grid_spec=grid_spec,
        out_shape=[jax.ShapeDtypeStruct((n_slots + N_SPARE_CHUNKS, CHUNK, D), BF16),
                   jax.ShapeDtypeStruct((n_tiles * TILE_ROWS, tm), BF16)],
        compiler_params=_params(("arbitrary",)),
        name="moe_dispatch",
    )(pos, pad_pos, h2, idx, wts)


def _moe_tables(cnt, g_max):
    nt = cnt.shape[0]
    cc = (cnt + (CHUNK - 1)) // CHUNK
    segblk = jnp.cumsum(cc, axis=1) - cc
    tile_chunks = jnp.sum(cc, axis=1)
    prior = jnp.cumsum(cc, axis=0) - cc
    ge_cnt = (jnp.sum(cc, axis=0) + (GROUP_CHUNKS - 1)) // GROUP_CHUNKS
    gbase = jnp.cumsum(ge_cnt) - ge_cnt
    n_groups = jnp.sum(ge_cnt)
    c = jnp.arange(TILE_CHUNKS, dtype=jnp.int32)
    e_of = jnp.sum(((segblk + cc)[:, None, :] <= c[None, :, None]).astype(jnp.int32), axis=-1)
    e_of = jnp.minimum(e_of, N_EXPERTS - 1)
    seg_base = gbase[None, :] * GROUP_CHUNKS + prior - segblk
    onehot = e_of[:, :, None] == jnp.arange(N_EXPERTS, dtype=jnp.int32)[None, None, :]
    pos = jnp.sum(jnp.where(onehot, seg_base[:, None, :], 0), axis=-1) + c[None, :]
    valid = c[None, :] < tile_chunks[:, None]
    n_slots = g_max * GROUP_CHUNKS
    parity = (jnp.arange(nt, dtype=jnp.int32) % 2)[:, None]
    pos_write = jnp.where(valid, pos, n_slots + parity * TILE_CHUNKS + c[None, :]).astype(jnp.int32)
    ce = jnp.sum(cc, axis=0)
    pad_pos = jnp.concatenate([gbase * GROUP_CHUNKS + ce, ge_cnt * GROUP_CHUNKS - ce]).astype(jnp.int32)
    g = jnp.arange(g_max, dtype=jnp.int32)
    grp_e = jnp.minimum(jnp.sum(((gbase + ge_cnt)[None, :] <= g[:, None]).astype(jnp.int32), axis=1),
                        N_EXPERTS - 1).astype(jnp.int32)
    hi = lax.Precision.HIGHEST
    experts = jnp.arange(N_EXPERTS, dtype=jnp.int32)
    slot = jnp.arange(n_slots, dtype=jnp.int32)
    e_slot = jnp.minimum(jnp.sum((((gbase + ge_cnt) * GROUP_CHUNKS)[None, :] <= slot[:, None]).astype(jnp.int32),
                                 axis=1), N_EXPERTS - 1)
    oh_e = (e_slot[:, None] == experts[None, :]).astype(F32)
    q = slot - jnp.sum(oh_e * (gbase * GROUP_CHUNKS).astype(F32)[None, :], axis=1).astype(jnp.int32)
    slot_valid = q < jnp.sum(oh_e * ce.astype(F32)[None, :], axis=1).astype(jnp.int32)
    cum_end = jnp.dot(oh_e, (prior + cc).astype(F32).T, precision=hi)
    t_slot = jnp.minimum(jnp.sum((cum_end <= q[:, None].astype(F32)).astype(jnp.int32), axis=1), nt - 1)
    oh_t = (t_slot[:, None] == jnp.arange(nt, dtype=jnp.int32)[None, :]).astype(F32)
    shift = jnp.sum(jnp.dot(oh_t, (segblk - prior).astype(F32), precision=hi) * oh_e, axis=1).astype(jnp.int32)
    spare = nt * TILE_CHUNKS + ((slot // GROUP_CHUNKS) % 2) * GROUP_CHUNKS + slot % GROUP_CHUNKS
    inv = jnp.where(slot_valid, t_slot * TILE_CHUNKS + q + shift, spare).astype(jnp.int32)
    tiles = jnp.arange(nt, dtype=jnp.int32)
    fill = jnp.concatenate([tiles * TILE_CHUNKS + tile_chunks, TILE_CHUNKS - tile_chunks]).astype(jnp.int32)
    return (pos_write.reshape(-1), inv, pad_pos.reshape(-1), fill, grp_e,
            n_groups.reshape(1).astype(jnp.int32))


def _ffn_kernel(ge_ref, na_ref, inv_ref, fill_ref, x_ref, wg_ref, wu_ref, wd_ref, yt_ref,
                buf0, buf1, zero_buf, wg_s, wu_s, wd_s, sems, fill_sem, *, n_tiles):
    g = pl.program_id(0)
    na = na_ref[0]
    bufs = (buf0, buf1)

    g_live = jnp.minimum(g, jnp.maximum(na - 1, 0))
    @pl.when((g < na) & ((g == 0) | (ge_ref[g_live] != ge_ref[jnp.maximum(g_live - 1, 0)])))
    def _():
        wg_s[...] = wg_ref[0, 0].astype(BF16)
        wu_s[...] = wu_ref[0, 0].astype(BF16)
        wd_s[...] = wd_ref[0, 0].astype(BF16)

    def compute(p):
        x = x_ref[...].reshape(GROUP_ROWS, x_ref.shape[2])
        gate = _dot(x, wg_s[...])
        up = _dot(x, wu_s[...])
        act = (_silu(gate) * up).astype(BF16)
        bufs[p][...] = _dot(act, wd_s[...]).astype(BF16).reshape(bufs[p].shape)

    def issue(grp, p):
        for c in range(GROUP_CHUNKS):
            pltpu.make_async_copy(bufs[p].at[c], yt_ref.at[inv_ref[grp * GROUP_CHUNKS + c]], sems.at[p]).start()

    def wait(p):
        pltpu.make_async_copy(bufs[p], yt_ref.at[pl.ds(0, GROUP_CHUNKS)], sems.at[p]).wait()

    @pl.when(g == 0)
    def _():
        zero_buf[...] = jnp.zeros_like(zero_buf)

        def fill_tile(t, carry):
            def fill_issue(c, inner):
                pltpu.make_async_copy(zero_buf, yt_ref.at[fill_ref[t] + c], fill_sem).start()
                return inner
            lax.fori_loop(0, fill_ref[n_tiles + t], fill_issue, 0)

            def fill_wait(c, inner):
                pltpu.make_async_copy(zero_buf, yt_ref.at[fill_ref[t] + c], fill_sem).wait()
                return inner
            lax.fori_loop(0, fill_ref[n_tiles + t], fill_wait, 0)
            return carry
        lax.fori_loop(0, n_tiles, fill_tile, 0)

    for p in range(2):
        @pl.when((g >= 2) & (g - 2 < na) & (g % 2 == p))
        def _(p=p):
            wait(p)

        @pl.when((g >= 1) & (g < na) & (g % 2 == p))
        def _(p=p):
            issue(g - 1, 1 - p)
            compute(p)

        @pl.when((g >= 1) & (g == na) & (g % 2 == p))
        def _(p=p):
            issue(g - 1, 1 - p)

    @pl.when((g == 0) & (na > 0))
    def _():
        compute(0)


def _ffn_grouped(xs, inv, fill, grp_e, n_groups, layer, w_gate, w_up, w_down, g_max, n_tiles):
    D = xs.shape[2]
    live = lambda g, na: jnp.minimum(g, jnp.maximum(na[0] - 1, 0))
    wspec = lambda shape: pl.BlockSpec((1, 1) + shape, lambda g, ge, na, inv, fill: (layer, ge[live(g, na)], 0, 0))
    grid_spec = pltpu.PrefetchScalarGridSpec(
        num_scalar_prefetch=4,
        grid=(g_max + 2,),
        in_specs=[pl.BlockSpec((GROUP_CHUNKS, CHUNK, D), lambda g, ge, na, inv, fill: (live(g, na), 0, 0)),
                  wspec((D, EXPERT_FF)), wspec((D, EXPERT_FF)), wspec((EXPERT_FF, D))],
        out_specs=pl.BlockSpec(memory_space=pl.ANY),
        scratch_shapes=[pltpu.VMEM((GROUP_CHUNKS, CHUNK, D), BF16),
                        pltpu.VMEM((GROUP_CHUNKS, CHUNK, D), BF16),
                        pltpu.VMEM((CHUNK, D), BF16),
                        pltpu.VMEM((D, EXPERT_FF), BF16),
                        pltpu.VMEM((D, EXPERT_FF), BF16),
                        pltpu.VMEM((EXPERT_FF, D), BF16),
                        pltpu.SemaphoreType.DMA((2,)),
                        pltpu.SemaphoreType.DMA(())],
    )
    return pl.pallas_call(
        functools.partial(_ffn_kernel, n_tiles=n_tiles),
        grid_spec=grid_spec,
        out_shape=jax.ShapeDtypeStruct((n_tiles * TILE_CHUNKS + 2 * GROUP_CHUNKS, CHUNK, D), BF16),
        compiler_params=_params(("arbitrary",)),
        name="moe_ffn",
    )(grp_e, n_groups, inv, fill, xs, w_gate, w_up, w_down)


def _combine_kernel(yt_ref, pw_ref, base_ref, mod_ref, fg_ref, o_ref, *, final):
    yt = yt_ref[...].reshape(TILE_ROWS, yt_ref.shape[2])
    routed = _dot_tn(pw_ref[...], yt)
    out = base_ref[...] + mod_ref[0, 5:6, :] * routed
    if final:
        out = out * lax.rsqrt(jnp.mean(out * out, axis=-1, keepdims=True) + EPS) * fg_ref[...]
    o_ref[...] = out


def _combine(yt, pw, base, mod, mod_row, final_g):
    B, S, D = base.shape
    tm = MOE_TILE
    nt = S // tm
    final = final_g is not None
    fg = (final_g if final else jnp.ones((D,), F32)).reshape(1, D)
    out = pl.pallas_call(
        functools.partial(_combine_kernel, final=final),
        grid=(B * nt,),
        in_specs=[pl.BlockSpec((TILE_CHUNKS, CHUNK, D), lambda i: (i, 0, 0)),
                  pl.BlockSpec((TILE_ROWS, tm), lambda i: (i, 0)),
                  pl.BlockSpec((tm, D), lambda i: (i, 0)),
                  pl.BlockSpec((1, 6, D), lambda i: (mod_row(i // nt), 0, 0)),
                  pl.BlockSpec((1, D), lambda i: (0, 0))],
        out_specs=pl.BlockSpec((tm, D), lambda i: (i, 0)),
        out_shape=jax.ShapeDtypeStruct((B * S, D), F32),
        compiler_params=_params(("parallel",)),
        name="moe_combine",
    )(yt, pw, base.reshape(B * S, D), mod, fg)
    return out.reshape(B, S, D)


def _moe_sparse(h2, idx, wts, cnt, base, mod, mod_row, layer, w_gate, w_up, w_down, final_g=None):
    B, S, D = h2.shape
    n_tiles = B * (S // MOE_TILE)
    g_max = (n_tiles * TILE_CHUNKS + N_PAD_CHUNKS + GROUP_CHUNKS - 1) // GROUP_CHUNKS
    pos_write, inv, pad_pos, fill, grp_e, n_groups = _moe_tables(cnt[:, :, 0], g_max)
    xs, pw = _dispatch(h2, idx, wts, pos_write, pad_pos, g_max * GROUP_CHUNKS)
    yt = _ffn_grouped(xs, inv, fill, grp_e, n_groups, layer, w_gate, w_up, w_down, g_max, n_tiles)
    return _combine(yt, pw, base, mod, mod_row, final_g)


def _zero_state(batch):
    nq = M_HEADS * M_QK_DIM
    return (jnp.zeros((batch, 2, nq, M_V_DIM), F32),
            jnp.zeros((batch, 2, 1, nq), F32),
            jnp.zeros((batch, 2, 1, LANES), F32))


def kernel(x, c, ctx, c_ctx, ada_w, ada_b, norm1_g, norm2_g, w_in, attn_sink, mlstm_gate_b, mlstm_norm_g, conv_w, w_br_attn, w_br_mlstm, w_br_conv, branch_gate_b, w_out, router_w, router_bias, exp_w_gate, exp_w_up, exp_w_down, sh_w_gate, sh_w_up, sh_w_down, final_g):
    B, S, D = x.shape
    L = ctx.shape[1]
    depth = ada_w.shape[0]
    ctx_row = B

    pad_rows = (-(B + 1)) % 8
    cc = jnp.concatenate([c, c_ctx[None, :], jnp.zeros((pad_rows, D), F32)], axis=0)
    mod_all = _ada(cc, ada_w, ada_b).reshape(depth, B + 1 + pad_rows, 6, D)

    cos_t, sin_t = _rope_tables(S)
    cos_c = jnp.ones((L, LANES), F32)
    sin_c = jnp.zeros((L, LANES), F32)
    lat_row = lambda b: b
    ctx_mod = lambda b: ctx_row

    xc = ctx
    for l in range(depth):
        need_ctx = l < depth - 1
        mod = mod_all[l]
        w_p = _permute_w_in(w_in[l])
        lw = {
            'mlstm_norm_g': mlstm_norm_g[l].reshape(1, M_OUT),
            'conv_w': conv_w[l],
            'branch_gate_b': branch_gate_b[l].reshape(1, N_BRANCH * D),
            'w_br_attn': w_br_attn[l].reshape(ATT_KV_HEADS, ATT_GROUP, ATT_HEAD_DIM, D)
                         .transpose(1, 0, 2, 3).reshape(ATT_OUT, D).astype(BF16),
            'w_br_mlstm': w_br_mlstm[l].astype(BF16),
            'w_br_conv': w_br_conv[l].astype(BF16),
            'w_out': w_out[l].astype(BF16),
            'norm2_g': norm2_g[l].reshape(1, D),
            'router_wt': _split_hi_lo(router_w[l].T),
            'router_bias': router_bias[l].reshape(N_EXPERTS, 1),
            'sh_gu': jnp.concatenate([sh_w_gate[l], sh_w_up[l]], axis=1).astype(BF16),
            'sh_d': sh_w_down[l].astype(BF16),
        }
        experts = (l, exp_w_gate, exp_w_up, exp_w_down)

        pc = _in_proj(xc, mod, ctx_mod, norm1_g[l], w_p, cos_c, sin_c, tm=256)
        p = _in_proj(x, mod, lat_row, norm1_g[l], w_p, cos_t, sin_t, tm=min(512, S))
        qs_c, k_c, v_c, qm_c, km_c, vm_c, om_c, gm_c, bc_c, u_c, gt_c = pc
        qs, k, v, qm, km, vm, om, gm, bc, u, gt = p

        oat = _attention(qs, k, v, k_c, v_c, attn_sink[l], band=True)
        hf_c, hb_c, st = _mlstm(qm_c, km_c, vm_c, gm_c, mlstm_gate_b[l], _zero_state(B))
        hf, hb, _ = _mlstm(qm, km, vm, gm, mlstm_gate_b[l], st)

        base, h2, idx, wts, cnt = _merge(x, mod, lat_row, oat, hf, hb, om, u, bc, gt, lw, tm=min(2 * MOE_TILE, S))
        x_new = _moe_sparse(h2, idx, wts, cnt, base, mod, lat_row, *experts,
                            final_g=final_g if l == depth - 1 else None)

        if need_ctx:
            oat_c = _attention(qs_c, None, None, k_c, v_c, attn_sink[l], band=False)
            base_c, h2_c, idx_c, wts_c, cnt_c = _merge(xc, mod, ctx_mod, oat_c, hf_c, hb_c, om_c, u_c, bc_c, gt_c, lw,
                                                       tm=MOE_TILE)
            xc = _moe_sparse(h2_c, idx_c, wts_c, cnt_c, base_c, mod, ctx_mod, *experts)
        x = x_new
    return x
```

```python
import functools

import numpy as np
import jax
import jax.numpy as jnp
from jax import lax
from jax.experimental import pallas as pl
from jax.experimental.pallas import tpu as pltpu

F32 = jnp.float32
BF16 = jnp.bfloat16

D_MODEL = 1024
GRID_W = 64
EPS = 1e-6
ATT_HEADS = 8
ATT_KV_HEADS = 2
ATT_HEAD_DIM = 64
ATT_GROUP = ATT_HEADS // ATT_KV_HEADS
ATT_BLOCK = 128
ATT_OUT = ATT_HEADS * ATT_HEAD_DIM
ROPE_BASE = 10000.0
M_HEADS = 4
M_QK_DIM = 64
M_V_DIM = 128
M_CHUNK = 64
M_OUT = M_HEADS * M_V_DIM
CONV_WIDTH = 512
N_BRANCH = 3
N_EXPERTS = 64
N_GROUPS = 8
GROUP_SIZE = N_EXPERTS // N_GROUPS
TOPK_GROUPS = 4
TOP_K = 8
EXPERT_FF = 256
SHARED_FF = 256
ROUTED_SCALE = 2.5

LOG2E = 1.4426950408889634
LANES = 128
BF16_SUBLANES = 16
VMEM_LIMIT = 56 * 1024 * 1024

_SEGS = (('q', 512), ('k', 128), ('v', 128), ('qm', 256), ('km', 256), ('vm', 512), ('om', 512),
         ('bc', 512), ('cc', 512), ('xc', 512), ('gt', 3072), ('gm', 128))
_OFF = {}
_o = 0
for _n, _s in _SEGS:
    _OFF[_n] = (_o, _o + _s)
    _o += _s
N_PROJ = _o
D_IN = 6928


def _permute_w_in(w):
    d = w.shape[0]
    half = ATT_HEAD_DIM // 2
    q = w[:, 0:512].reshape(d, ATT_KV_HEADS, ATT_GROUP, half, 2).transpose(0, 2, 1, 4, 3).reshape(d, 512)
    k = w[:, 512:640].reshape(d, ATT_KV_HEADS, half, 2).transpose(0, 1, 3, 2).reshape(d, 128)
    gm = jnp.pad(w[:, 2304:2320], ((0, 0), (0, LANES - 16)))
    out = jnp.concatenate([q, k, w[:, 640:2304], w[:, 2320:D_IN], gm], axis=1)
    assert out.shape[1] == N_PROJ
    return out.astype(BF16)


def _rope_tables(seq):
    rows = seq // GRID_W
    row = jnp.repeat(jnp.arange(rows, dtype=F32), GRID_W)
    col = jnp.tile(jnp.arange(GRID_W, dtype=F32), rows)
    n_pairs = ATT_HEAD_DIM // 4
    inv_freq = ROPE_BASE ** (-jnp.arange(n_pairs, dtype=F32) / n_pairs)
    ang = jnp.concatenate([row[:, None] * inv_freq, col[:, None] * inv_freq], axis=-1)
    c, s = jnp.cos(ang), jnp.sin(ang)
    cos_t = jnp.concatenate([c, c, c, c], axis=-1)
    sin_t = jnp.concatenate([-s, s, -s, s], axis=-1)
    return cos_t, sin_t


def _dot(a, b):
    return jnp.dot(a, b, preferred_element_type=F32)


def _dot_nt(a, b):
    return lax.dot_general(a, b, (((1,), (1,)), ((), ())), preferred_element_type=F32)


def _dot_tn(a, b):
    return lax.dot_general(a, b, (((0,), (0,)), ((), ())), preferred_element_type=F32)


def _split_hi_lo(w):
    hi = w.astype(BF16)
    lo = (w - hi.astype(F32)).astype(BF16)
    return jnp.concatenate([hi, lo], axis=0)


def _sigmoid(x):
    return 1.0 / (1.0 + jnp.exp(-x))


def _sigmoid_tanh(x):
    return 0.5 * jnp.tanh(0.5 * x) + 0.5


def _silu(x):
    return x * _sigmoid(x)


def _log_sigmoid(x):
    return jnp.minimum(x, 0.0) - jnp.log(1.0 + jnp.exp(-jnp.abs(x)))


def _rms_mod(x, g, shift, scale):
    y = x * lax.rsqrt(jnp.mean(x * x, axis=-1, keepdims=True) + EPS) * g
    return y * (1.0 + scale) + shift


def _params(sem):
    return pltpu.CompilerParams(dimension_semantics=sem, vmem_limit_bytes=VMEM_LIMIT)


def _ada_kernel(c_ref, w_ref, b_ref, o_ref):
    s = _silu(c_ref[...])
    o_ref[0] = jnp.dot(s, w_ref[0], preferred_element_type=F32,
                       precision=lax.Precision.HIGHEST) + b_ref[0]


def _ada(cc, ada_w, ada_b):
    depth, d, n = ada_w.shape
    rows = cc.shape[0]
    tn = 1536
    return pl.pallas_call(
        _ada_kernel,
        grid=(depth, n // tn),
        in_specs=[pl.BlockSpec((rows, d), lambda l, j: (0, 0)),
                  pl.BlockSpec((1, d, tn), lambda l, j: (l, 0, j)),
                  pl.BlockSpec((1, 1, tn), lambda l, j: (l, 0, j))],
        out_specs=pl.BlockSpec((1, rows, tn), lambda l, j: (l, 0, j)),
        out_shape=jax.ShapeDtypeStruct((depth, rows, n), F32),
        compiler_params=_params(("parallel", "parallel")),
        name="ada_mod",
    )(cc, ada_w, ada_b.reshape(depth, 1, n))


def _swap_halves(x):
    lane = lax.broadcasted_iota(jnp.int32, x.shape, 1)
    first = (lane % ATT_HEAD_DIM) < (ATT_HEAD_DIM // 2)
    return jnp.where(first, pltpu.roll(x, LANES - 32, axis=1), pltpu.roll(x, 32, axis=1))


def _in_kernel(x_ref, mod_ref, g_ref, w_ref, cos_ref, sin_ref,
               qs_ref, k_ref, v_ref, qm_ref, km_ref, vm_ref, om_ref, gm_ref, bc_ref, u_ref, gt_ref):
    tm = x_ref.shape[1]
    h = _rms_mod(x_ref[0], g_ref[...], mod_ref[0, 0:1, :], mod_ref[0, 1:2, :]).astype(BF16)

    def proj(name):
        lo, hi = _OFF[name]
        return _dot(h, w_ref[:, lo:hi])

    cos_t = cos_ref[...]
    sin_t = sin_ref[...]

    def rope(t):
        return t * cos_t + _swap_halves(t) * sin_t

    q = proj('q')
    scale = ATT_HEAD_DIM ** -0.5 * LOG2E
    for hh in range(ATT_GROUP):
        r = (rope(q[:, hh * LANES:(hh + 1) * LANES]) * scale).astype(BF16)
        for qb in range(tm // ATT_BLOCK):
            qs_ref[0, qb, hh * ATT_BLOCK:(hh + 1) * ATT_BLOCK, :] = r[qb * ATT_BLOCK:(qb + 1) * ATT_BLOCK, :]
    k_ref[0] = rope(proj('k')).astype(BF16)
    v_ref[0] = proj('v').astype(BF16)
    qm_ref[0] = proj('qm').astype(BF16)
    km_ref[0] = proj('km').astype(BF16)
    vm_ref[0] = proj('vm').astype(BF16)
    om_ref[0] = proj('om').astype(BF16)
    gm_ref[0] = proj('gm')
    bc_ref[0] = proj('bc').astype(BF16)
    u_ref[0] = (proj('cc') * proj('xc')).astype(BF16)
    gt_ref[0] = proj('gt').astype(BF16)


def _in_proj(x, mod, mod_row, norm_g, w_p, cos_t, sin_t, tm):
    B, S, D = x.shape
    nb = S // ATT_BLOCK
    tok = lambda n, dt: jax.ShapeDtypeStruct((B, S, n), dt)
    tspec = lambda n: pl.BlockSpec((1, tm, n), lambda b, i: (b, i, 0))
    out_shape = (jax.ShapeDtypeStruct((B, nb, ATT_GROUP * ATT_BLOCK, LANES), BF16),
                 tok(128, BF16), tok(128, BF16), tok(256, BF16), tok(256, BF16), tok(512, BF16),
                 tok(512, BF16), tok(128, F32), tok(512, BF16), tok(512, BF16), tok(3072, BF16))
    out_specs = (pl.BlockSpec((1, tm // ATT_BLOCK, ATT_GROUP * ATT_BLOCK, LANES), lambda b, i: (b, i, 0, 0)),
                 tspec(128), tspec(128), tspec(256), tspec(256), tspec(512), tspec(512), tspec(128),
                 tspec(512), tspec(512), tspec(3072))
    return pl.pallas_call(
        _in_kernel,
        grid=(B, S // tm),
        in_specs=[pl.BlockSpec((1, tm, D), lambda b, i: (b, i, 0)),
                  pl.BlockSpec((1, 6, D), lambda b, i: (mod_row(b), 0, 0)),
                  pl.BlockSpec((1, D), lambda b, i: (0, 0)),
                  pl.BlockSpec((D, N_PROJ), lambda b, i: (0, 0), pipeline_mode=pl.Buffered(1)),
                  pl.BlockSpec((tm, LANES), lambda b, i: (i, 0)),
                  pl.BlockSpec((tm, LANES), lambda b, i: (i, 0))],
        out_specs=out_specs,
        out_shape=out_shape,
        compiler_params=_params(("parallel", "parallel")),
        name="in_proj",
    )(x, mod, norm_g.reshape(1, D), w_p, cos_t, sin_t)


def _attn_block(q, sink_ref, kcat, vcat, masks):
    rows = q.shape[0]
    lane = lax.broadcasted_iota(jnp.int32, (1, LANES), 1)
    hh = lax.broadcasted_iota(jnp.int32, (rows, 1), 0) // ATT_BLOCK
    out = jnp.zeros((rows, LANES), F32)
    for g in range(ATT_KV_HEADS):
        lm = (lane < ATT_HEAD_DIM) if g == 0 else (lane >= ATT_HEAD_DIM)
        kz = jnp.where(lm, kcat, jnp.zeros_like(kcat))
        ones_lane = ATT_HEAD_DIM if g == 0 else 0
        vz = jnp.where(lm, vcat, jnp.where(lane == ones_lane, 1.0, 0.0).astype(BF16))
        s = _dot_nt(q, kz)
        if any(mk is not None for mk in masks):
            s = jnp.concatenate(
                [s[:, n * ATT_BLOCK:(n + 1) * ATT_BLOCK] if mk is None
                 else jnp.where(mk, s[:, n * ATT_BLOCK:(n + 1) * ATT_BLOCK], -jnp.inf)
                 for n, mk in enumerate(masks)], axis=1)
        sink = jnp.zeros((rows, 1), F32)
        for a in range(ATT_GROUP):
            sink = jnp.where(hh == a, sink_ref[g * ATT_GROUP + a] * LOG2E, sink)
        m = jnp.maximum(jnp.max(s, axis=-1, keepdims=True), sink)
        p = jnp.exp2((s - m).astype(BF16))
        pv = _dot(p, vz)
        l = pv[:, ones_lane:ones_lane + 1] + jnp.exp2(sink - m)
        out = out + jnp.where(lm, pv, 0.0) / l
    return out.astype(BF16)


ATT_STEP_BLOCKS = 8


def _attn_kernel(sink_ref, qs_ref, kc_ref, vc_ref, *rest, band):
    nsb = qs_ref.shape[1]
    if not band:
        (o_ref,) = rest
        n_ctx = kc_ref.shape[1] // ATT_BLOCK
        for sb in range(nsb):
            o_ref[0, sb] = _attn_block(qs_ref[0, sb], sink_ref, kc_ref[0], vc_ref[0], [None] * n_ctx)
        return
    kp_ref, kcur_ref, kn_ref, vp_ref, vcur_ref, vn_ref, o_ref = rest
    j = pl.program_id(1)
    nstep = pl.num_programs(1)
    n_ctx = kc_ref.shape[1] // ATT_BLOCK
    rows = qs_ref.shape[2]
    t = lax.broadcasted_iota(jnp.int32, (rows, 1), 0) % ATT_BLOCK
    i = lax.broadcasted_iota(jnp.int32, (1, ATT_BLOCK), 1)
    below = i >= t
    above = i <= t
    first = i >= t + jnp.where(j > 0, 0, 2 * ATT_BLOCK)
    last = i <= t - jnp.where(j < nstep - 1, 0, 2 * ATT_BLOCK)
    kblk = [kp_ref[0]] + [kcur_ref[0, n * ATT_BLOCK:(n + 1) * ATT_BLOCK, :] for n in range(nsb)] + [kn_ref[0]]
    vblk = [vp_ref[0]] + [vcur_ref[0, n * ATT_BLOCK:(n + 1) * ATT_BLOCK, :] for n in range(nsb)] + [vn_ref[0]]
    for sb in range(nsb):
        kcat = jnp.concatenate([kc_ref[0]] + kblk[sb:sb + 3], axis=0)
        vcat = jnp.concatenate([vc_ref[0]] + vblk[sb:sb + 3], axis=0)
        masks = [first if sb == 0 else below, None, last if sb == nsb - 1 else above]
        o_ref[0, sb] = _attn_block(qs_ref[0, sb], sink_ref, kcat, vcat, [None] * n_ctx + masks)


def _attention(qs, k, v, kc, vc, sink, band):
    B, nb = qs.shape[:2]
    lc = kc.shape[1]
    blocks = min(ATT_STEP_BLOCKS, nb)
    nstep = nb // blocks
    last = nb - 1
    qspec = pl.BlockSpec((1, blocks, ATT_GROUP * ATT_BLOCK, LANES), lambda b, j: (b, j, 0, 0))
    cspec = pl.BlockSpec((1, lc, LANES), lambda b, j: (b, 0, 0))
    in_specs = [pl.BlockSpec(memory_space=pltpu.SMEM), qspec, cspec, cspec]
    args = [sink.astype(F32), qs, kc, vc]
    if band:
        prev = pl.BlockSpec((1, ATT_BLOCK, LANES), lambda b, j: (b, jnp.maximum(blocks * j - 1, 0), 0))
        cur = pl.BlockSpec((1, blocks * ATT_BLOCK, LANES), lambda b, j: (b, j, 0))
        nxt = pl.BlockSpec((1, ATT_BLOCK, LANES), lambda b, j: (b, jnp.minimum(blocks * (j + 1), last), 0))
        in_specs += [prev, cur, nxt, prev, cur, nxt]
        args += [k, k, k, v, v, v]
    return pl.pallas_call(
        functools.partial(_attn_kernel, band=band),
        grid=(B, nstep),
        in_specs=in_specs,
        out_specs=qspec,
        out_shape=jax.ShapeDtypeStruct(qs.shape, BF16),
        compiler_params=_params(("parallel", "parallel")),
        name="attention_band" if band else "attention_ctx",
    )(*args)


def _mlstm_step(dirs, T):
    kscale = M_QK_DIM ** -0.5
    si = lax.broadcasted_iota(jnp.int32, (T, T), 0)
    ri = lax.broadcasted_iota(jnp.int32, (T, T), 1)
    lane_qk = lax.broadcasted_iota(jnp.int32, (1, M_HEADS * M_QK_DIM), 1) // M_QK_DIM
    lane_m = lax.broadcasted_iota(jnp.int32, (1, LANES), 1)
    row_c = lax.broadcasted_iota(jnp.int32, (M_HEADS * M_QK_DIM, 1), 0) // M_QK_DIM
    combos = [(d, hd) for d in range(2) for hd in range(M_HEADS)]

    tri, bcol, gt, bt, blast = [], [], [], [], []
    for d, (q, k, v, g, C, n, m) in enumerate(dirs):
        t = (ri <= si) if d == 0 else (ri >= si)
        tri.append(t)
        lf = _log_sigmoid(g)
        bc = jnp.dot(t.astype(F32), lf, preferred_element_type=F32, precision=lax.Precision.HIGHEST)
        bcol.append(bc)
        gt.append(g.T)
        bt.append(bc.T)
        blast.append(bc[T - 1:T, :] if d == 0 else bc[0:1, :])

    def lanes(d, hd):
        return (2 * d) * M_HEADS + hd, (2 * d + 1) * M_HEADS + hd

    b_col = {c: bcol[c[0]][:, lanes(*c)[1]:lanes(*c)[1] + 1] for c in combos}
    ig_col = {c: dirs[c[0]][3][:, lanes(*c)[0]:lanes(*c)[0] + 1] for c in combos}
    alpha = {c: gt[c[0]][lanes(*c)[0]:lanes(*c)[0] + 1, :] - bt[c[0]][lanes(*c)[1]:lanes(*c)[1] + 1, :]
             for c in combos}
    m_old = {c: dirs[c[0]][6][:, c[1]:c[1] + 1] for c in combos}
    b_last = {c: blast[c[0]][:, lanes(*c)[1]:lanes(*c)[1] + 1] for c in combos}
    hmask = {hd: lane_qk == hd for hd in range(M_HEADS)}

    a_mat = {c: jnp.where(tri[c[0]], alpha[c], -jnp.inf) for c in combos}
    a_max = {c: jnp.max(a_mat[c], axis=1, keepdims=True) for c in combos}
    a_int = {c: b_col[c] + m_old[c] for c in combos}
    m_s = {c: jnp.maximum(a_int[c], b_col[c] + a_max[c]) for c in combos}
    w_int = {c: jnp.exp(a_int[c] - m_s[c]) for c in combos}
    w_mat = {c: jnp.exp(a_mat[c] + (b_col[c] - m_s[c])) for c in combos}
    qmask = {c: jnp.where(hmask[c[1]], dirs[c[0]][0], jnp.zeros_like(dirs[c[0]][0])) for c in combos}
    s_qk = {c: w_mat[c] * (_dot_nt(qmask[c], dirs[c[0]][1]) * kscale) for c in combos}
    vh = {c: dirs[c[0]][2][:, c[1] * M_V_DIM:(c[1] + 1) * M_V_DIM] for c in combos}
    c_bf = [dirs[d][4].astype(BF16) for d in range(2)]
    num = {c: _dot(s_qk[c].astype(BF16), vh[c]) + w_int[c] * _dot(qmask[c], c_bf[c[0]]) for c in combos}
    qn_all = [dirs[d][0].astype(F32) * dirs[d][5] for d in range(2)]
    qn = {c: jnp.sum(jnp.where(hmask[c[1]], qn_all[c[0]], 0.0), axis=1, keepdims=True) for c in combos}
    den = {c: jnp.sum(s_qk[c], axis=1, keepdims=True) + w_int[c] * qn[c] for c in combos}
    h = {c: num[c] / jnp.maximum(jnp.abs(den[c]), jnp.exp(-m_s[c])) for c in combos}

    r_col = {c: b_last[c] - b_col[c] + ig_col[c] for c in combos}
    m_new = {c: jnp.maximum(b_last[c] + m_old[c], jnp.max(r_col[c], axis=0, keepdims=True)) for c in combos}
    decay = {c: jnp.exp(b_last[c] + m_old[c] - m_new[c]) for c in combos}
    w_r = {c: jnp.exp(r_col[c] - m_new[c]) for c in combos}

    outs = []
    for d, (q, k, v, g, C, n, m) in enumerate(dirs):
        w_lanes = jnp.zeros((T, M_HEADS * M_QK_DIM), F32)
        dec_lanes = jnp.zeros((1, M_HEADS * M_QK_DIM), F32)
        dec_rows = jnp.zeros((M_HEADS * M_QK_DIM, 1), F32)
        m_row = jnp.zeros((1, LANES), F32)
        for hd in range(M_HEADS):
            w_lanes = jnp.where(hmask[hd], w_r[(d, hd)], w_lanes)
            dec_lanes = jnp.where(hmask[hd], decay[(d, hd)], dec_lanes)
            dec_rows = jnp.where(row_c == hd, decay[(d, hd)], dec_rows)
            m_row = jnp.where(lane_m == hd, m_new[(d, hd)], m_row)
        kw = k.astype(F32) * (w_lanes * kscale)
        kwt = kw.T.astype(BF16)
        upd = jnp.concatenate(
            [_dot(kwt[hd * M_QK_DIM:(hd + 1) * M_QK_DIM, :], vh[(d, hd)]) for hd in range(M_HEADS)], axis=0)
        c_new = dec_rows * C + upd
        n_new = dec_lanes * n + jnp.sum(kw, axis=0, keepdims=True)
        h_all = jnp.concatenate([h[(d, hd)] for hd in range(M_HEADS)], axis=1)
        outs.append((h_all, c_new, n_new, m_row))
    return outs


def _mlstm_kernel(gb_ref, qf_ref, kf_ref, vf_ref, gf_ref, qb_ref, kb_ref, vb_ref, gbk_ref,
                  c0_ref, n0_ref, m0_ref, hf_ref, hb_ref, cf_ref, nf_ref, mf_ref,
                  c_s, n_s, m_s):
    ci = pl.program_id(1)
    T = MLSTM_TILE
    n_sub = qf_ref.shape[1] // T

    @pl.when(ci == 0)
    def _():
        c_s[...] = c0_ref[0]
        n_s[...] = n0_ref[0]
        m_s[...] = m0_ref[0]

    gb = gb_ref[...]
    state = [(c_s[d], n_s[d], m_s[d]) for d in range(2)]
    refs = ((qf_ref, kf_ref, vf_ref, gf_ref, hf_ref), (qb_ref, kb_ref, vb_ref, gbk_ref, hb_ref))
    for s in range(n_sub):
        lo = (s * T, (n_sub - 1 - s) * T)
        dirs = [(q_ref[0, lo[d]:lo[d] + T, :], k_ref[0, lo[d]:lo[d] + T, :], v_ref[0, lo[d]:lo[d] + T, :],
                 g_ref[0, lo[d]:lo[d] + T, :] + gb) + state[d]
                for d, (q_ref, k_ref, v_ref, g_ref, _) in enumerate(refs)]
        outs = _mlstm_step(dirs, T)
        for d in range(2):
            refs[d][4][0, lo[d]:lo[d] + T, :] = outs[d][0]
        state = [outs[d][1:] for d in range(2)]
    for d in range(2):
        c_s[d], n_s[d], m_s[d] = state[d]

    @pl.when(ci == pl.num_programs(1) - 1)
    def _():
        cf_ref[0] = c_s[...]
        nf_ref[0] = n_s[...]
        mf_ref[0] = m_s[...]


MLSTM_TILE = 128
MLSTM_STEP_CHUNKS = 4


def _mlstm(qm, km, vm, gm, gate_b, state):
    B, S, _ = qm.shape
    T = MLSTM_TILE * min(MLSTM_STEP_CHUNKS, S // MLSTM_TILE)
    nc = S // T
    nq = M_HEADS * M_QK_DIM
    fwd = lambda n: pl.BlockSpec((1, T, n), lambda b, c: (b, c, 0))
    bwd = lambda n: pl.BlockSpec((1, T, n), lambda b, c: (b, nc - 1 - c, 0))
    st_specs = [pl.BlockSpec((1, 2, nq, M_V_DIM), lambda b, c: (b, 0, 0, 0)),
                pl.BlockSpec((1, 2, 1, nq), lambda b, c: (b, 0, 0, 0)),
                pl.BlockSpec((1, 2, 1, LANES), lambda b, c: (b, 0, 0, 0))]
    st_shapes = [jax.ShapeDtypeStruct((B, 2, nq, M_V_DIM), F32),
                 jax.ShapeDtypeStruct((B, 2, 1, nq), F32),
                 jax.ShapeDtypeStruct((B, 2, 1, LANES), F32)]
    gb_row = jnp.pad(gate_b.reshape(1, -1).astype(F32), ((0, 0), (0, LANES - gate_b.size)))
    outs = pl.pallas_call(
        _mlstm_kernel,
        grid=(B, nc),
        in_specs=[pl.BlockSpec((1, LANES), lambda b, c: (0, 0)),
                  fwd(256), fwd(256), fwd(512), fwd(LANES),
                  bwd(256), bwd(256), bwd(512), bwd(LANES)] + st_specs,
        out_specs=[fwd(M_OUT), bwd(M_OUT)] + st_specs,
        out_shape=[jax.ShapeDtypeStruct((B, S, M_OUT), F32)] * 2 + st_shapes,
        scratch_shapes=[pltpu.VMEM((2, nq, M_V_DIM), F32),
                        pltpu.VMEM((2, 1, nq), F32),
                        pltpu.VMEM((2, 1, LANES), F32)],
        compiler_params=_params(("parallel", "arbitrary")),
        name="mlstm_scan",
    )(gb_row, qm, km, vm, gm, qm, km, vm, gm, *state)
    return outs[0], outs[1], tuple(outs[2:])


def _route(scores, sel):
    tm = scores.shape[1]
    gi8 = lax.broadcasted_iota(jnp.int32, (GROUP_SIZE, tm), 0)

    def stack_rows(rows):
        out = jnp.broadcast_to(rows[0], (len(rows), tm))
        for r, v in enumerate(rows[1:], start=1):
            out = jnp.where(gi8 == r, v, out)
        return out

    gs = []
    for g in range(N_GROUPS):
        blk = sel[g * GROUP_SIZE:(g + 1) * GROUP_SIZE, :]
        m1 = jnp.max(blk, axis=0, keepdims=True)
        first = jnp.min(jnp.where(blk == m1, gi8, GROUP_SIZE), axis=0, keepdims=True)
        m2 = jnp.max(jnp.where(gi8 == first, -jnp.inf, blk), axis=0, keepdims=True)
        gs.append(m1 + m2)
    gsc = stack_rows(gs)
    gsel = jnp.zeros((N_GROUPS, tm), F32)
    for _ in range(TOPK_GROUPS):
        mx = jnp.max(gsc, axis=0, keepdims=True)
        first = jnp.min(jnp.where(gsc == mx, gi8, N_GROUPS), axis=0, keepdims=True)
        pick = gi8 == first
        gsel = jnp.where(pick, 1.0, gsel)
        gsc = jnp.where(pick, -jnp.inf, gsc)
    cur = jnp.concatenate(
        [jnp.where(gsel[g:g + 1, :] > 0.0, sel[g * GROUP_SIZE:(g + 1) * GROUP_SIZE, :], -jnp.inf)
         for g in range(N_GROUPS)], axis=0)
    ei = lax.broadcasted_iota(jnp.int32, (N_EXPERTS, tm), 0)
    idx, wts = [], []
    for _ in range(TOP_K):
        mx = jnp.max(cur, axis=0, keepdims=True)
        first = jnp.min(jnp.where(cur == mx, ei, N_EXPERTS), axis=0, keepdims=True)
        pick = ei == first
        idx.append(first)
        wts.append(jnp.sum(jnp.where(pick, scores, 0.0), axis=0, keepdims=True))
        cur = jnp.where(pick, -jnp.inf, cur)
    tot = wts[0]
    for w in wts[1:]:
        tot = tot + w
    wts = [w / tot * ROUTED_SCALE for w in wts]
    return stack_rows(idx), stack_rows(wts)


def _merge_kernel(x_ref, mod_ref, oat_ref, hf_ref, hb_ref, om_ref, ng_ref, u_ref, up_ref, un_ref,
                  bc_ref, cw_ref, gt_ref, bgb_ref, wa_ref, wm_ref, wc_ref, wo_ref, n2_ref,
                  rw_ref, rb_ref, sgu_ref, sd_ref,
                  base_ref, h2_ref, idx_ref, wt_ref, cnt_ref):
    i = pl.program_id(1)
    tm = x_ref.shape[1]
    x = x_ref[0]
    g1 = mod_ref[0, 2:3, :]
    sh2 = mod_ref[0, 3:4, :]
    sc2 = mod_ref[0, 4:5, :]
    g2 = mod_ref[0, 5:6, :]

    ya = jnp.concatenate(
        [jnp.concatenate([oat_ref[0, qb, hh * ATT_BLOCK:(hh + 1) * ATT_BLOCK, :] for hh in range(ATT_GROUP)], axis=1)
         for qb in range(tm // ATT_BLOCK)], axis=0)

    hsum = hf_ref[0] + hb_ref[0]
    parts = []
    for hd in range(M_HEADS):
        hh_ = hsum[:, hd * M_V_DIM:(hd + 1) * M_V_DIM]
        parts.append(hh_ * lax.rsqrt(jnp.mean(hh_ * hh_, axis=-1, keepdims=True) + EPS))
    hn = jnp.concatenate(parts, axis=1) * ng_ref[...]
    ym = (_sigmoid_tanh(om_ref[0].astype(F32)) * hn).astype(BF16)

    u = u_ref[0].astype(F32)
    row = lax.broadcasted_iota(jnp.int32, (tm, 1), 0)
    has_prev = (i > 0).astype(F32)
    has_next = (i < pl.num_programs(1) - 1).astype(F32)
    prev_row = up_ref[0, BF16_SUBLANES - 1:BF16_SUBLANES, :].astype(F32) * has_prev
    next_row = un_ref[0, 0:1, :].astype(F32) * has_next
    u_m1 = jnp.where(row == 0, prev_row, pltpu.roll(u, 1, axis=0))
    u_p1 = jnp.where(row == tm - 1, next_row, pltpu.roll(u, tm - 1, axis=0))
    conv = cw_ref[0:1, :] * u_m1 + cw_ref[1:2, :] * u + cw_ref[2:3, :] * u_p1
    yc = (bc_ref[0].astype(F32) * conv).astype(BF16)

    gg = _sigmoid_tanh(gt_ref[0].astype(F32) + bgb_ref[...])
    ymix = (gg[:, 0:D_MODEL] * _dot(ya, wa_ref[...])
            + gg[:, D_MODEL:2 * D_MODEL] * _dot(ym, wm_ref[...])
            + gg[:, 2 * D_MODEL:3 * D_MODEL] * _dot(yc, wc_ref[...]))
    y = _dot(ymix.astype(BF16), wo_ref[...])
    xm = x + g1 * y

    h2f = _rms_mod(xm, n2_ref[...], sh2, sc2)
    h2 = h2f.astype(BF16)
    h2_ref[0] = h2

    h2_lo = (h2f - h2.astype(F32)).astype(BF16)
    rw = rw_ref[...]
    part = _dot_nt(rw, h2)
    logits_t = part[0:N_EXPERTS, :] + part[N_EXPERTS:2 * N_EXPERTS, :] + _dot_nt(rw[0:N_EXPERTS, :], h2_lo)
    scores = _sigmoid(logits_t)
    idx, wts = _route(scores, scores + rb_ref[...])
    idx_ref[0] = idx
    wt_ref[0] = wts
    ei = lax.broadcasted_iota(jnp.int32, (N_EXPERTS, tm), 0)
    pick = jnp.zeros((N_EXPERTS, tm), F32)
    for kk in range(TOP_K):
        pick = jnp.where(ei == idx[kk:kk + 1, :], 1.0, pick)
    for sub in range(tm // MOE_TILE):
        n_e = jnp.sum(pick[:, sub * MOE_TILE:(sub + 1) * MOE_TILE], axis=1, keepdims=True)
        cnt_ref[sub] = jnp.broadcast_to(n_e, (N_EXPERTS, LANES)).astype(jnp.int32)

    a = _dot(h2, sgu_ref[...])
    act = (_silu(a[:, 0:SHARED_FF]) * a[:, SHARED_FF:2 * SHARED_FF]).astype(BF16)
    base_ref[0] = xm + g2 * _dot(act, sd_ref[...])


def _merge(x, mod, mod_row, oat, hf, hb, om, u, bc, gt, lw, tm):
    B, S, D = x.shape
    nt = S // tm
    hal = BF16_SUBLANES
    last_h = S // hal - 1
    tspec = lambda n: pl.BlockSpec((1, tm, n), lambda b, i: (b, i, 0))
    full = lambda a: pl.BlockSpec(a.shape, lambda b, i: (0,) * a.ndim, pipeline_mode=pl.Buffered(1))
    weights = [lw['mlstm_norm_g'], lw['conv_w'], lw['branch_gate_b'], lw['w_br_attn'], lw['w_br_mlstm'],
               lw['w_br_conv'], lw['w_out'], lw['norm2_g'], lw['router_wt'], lw['router_bias'],
               lw['sh_gu'], lw['sh_d']]
    in_specs = [tspec(D),
                pl.BlockSpec((1, 6, D), lambda b, i: (mod_row(b), 0, 0)),
                pl.BlockSpec((1, tm // ATT_BLOCK, ATT_GROUP * ATT_BLOCK, LANES), lambda b, i: (b, i, 0, 0)),
                tspec(M_OUT), tspec(M_OUT), tspec(M_OUT), full(weights[0]),
                tspec(CONV_WIDTH),
                pl.BlockSpec((1, hal, CONV_WIDTH), lambda b, i: (b, jnp.maximum(i * (tm // hal) - 1, 0), 0)),
                pl.BlockSpec((1, hal, CONV_WIDTH), lambda b, i: (b, jnp.minimum((i + 1) * (tm // hal), last_h), 0)),
                tspec(CONV_WIDTH), full(weights[1]), tspec(N_BRANCH * D), full(weights[2])]
    in_specs += [full(w) for w in weights[3:]]
    tr = lambda n, dt: (jax.ShapeDtypeStruct((B, n, S), dt), pl.BlockSpec((1, n, tm), lambda b, i: (b, 0, i)))
    outs = [(jax.ShapeDtypeStruct((B, S, D), F32), tspec(D)),
            (jax.ShapeDtypeStruct((B, S, D), BF16), tspec(D)),
            tr(TOP_K, jnp.int32), tr(TOP_K, F32),
            (jax.ShapeDtypeStruct((B * S // MOE_TILE, N_EXPERTS, LANES), jnp.int32),
             pl.BlockSpec((tm // MOE_TILE, N_EXPERTS, LANES), lambda b, i: (b * nt + i, 0, 0)))]
    return pl.pallas_call(
        _merge_kernel,
        grid=(B, nt),
        in_specs=in_specs,
        out_specs=[o[1] for o in outs],
        out_shape=[o[0] for o in outs],
        compiler_params=_params(("parallel", "parallel")),
        name="merge_route",
    )(x, mod, oat, hf, hb, om, weights[0], u, u, u, bc, weights[1], gt, weights[2], *weights[3:])


MOE_TILE = 256
CHUNK = BF16_SUBLANES
GROUP_CHUNKS = 64
GROUP_ROWS = GROUP_CHUNKS * CHUNK
TILE_ROWS = MOE_TILE * TOP_K + N_EXPERTS * CHUNK
TILE_CHUNKS = TILE_ROWS // CHUNK
N_PAD_CHUNKS = N_EXPERTS * (GROUP_CHUNKS - 1)
MIN_TILE_CHUNKS = MOE_TILE * TOP_K // CHUNK
FILL_SIZES = (32, 32, 16, 8, 4, 2, 1)
assert sum(FILL_SIZES) >= TILE_CHUNKS - MIN_TILE_CHUNKS


def _dispatch_tile(h, idx, wts):
    tm = h.shape[0]
    ei = lax.broadcasted_iota(jnp.int32, (N_EXPERTS, tm), 0)
    pick = jnp.zeros((N_EXPERTS, tm), F32)
    wmat = jnp.zeros((N_EXPERTS, tm), F32)
    for kk in range(TOP_K):
        chosen = ei == idx[kk:kk + 1, :]
        pick = jnp.where(chosen, 1.0, pick)
        wmat = jnp.where(chosen, wts[kk:kk + 1, :], wmat)
    t0 = lax.broadcasted_iota(jnp.int32, (tm, tm), 0)
    t1 = lax.broadcasted_iota(jnp.int32, (tm, tm), 1)
    rank = _dot(pick.astype(BF16), jnp.where(t0 < t1, 1.0, 0.0).astype(BF16))
    n_e = jnp.sum(pick, axis=1, keepdims=True)
    n_pad = jnp.floor((n_e + (CHUNK - 1)) * (1.0 / CHUNK)) * CHUNK
    e0 = lax.broadcasted_iota(jnp.int32, (N_EXPERTS, N_EXPERTS), 0)
    e1 = lax.broadcasted_iota(jnp.int32, (N_EXPERTS, N_EXPERTS), 1)
    seg = _dot(jnp.where(e1 < e0, 1.0, 0.0).astype(BF16),
               jnp.broadcast_to(n_pad, (N_EXPERTS, tm)).astype(BF16))
    posmat = seg + rank
    chunk_of = jnp.floor(posmat * (1.0 / CHUNK))
    offs_of = posmat - chunk_of * CHUNK
    chunk_row = jnp.where(pick > 0.0, chunk_of * CHUNK, -float(CHUNK))
    eye = e0 == e1
    to_row = lambda col: jnp.sum(jnp.where(eye, col, 0.0), axis=0, keepdims=True)
    seg_row = to_row(seg[:, 0:1])
    end_row = to_row(seg[:, 0:1] + n_pad)
    seg_row2 = jnp.concatenate([seg_row, seg_row], axis=1)
    end_row2 = jnp.concatenate([end_row, end_row], axis=1)
    r128 = lax.broadcasted_iota(jnp.int32, (TILE_ROWS, 2 * N_EXPERTS), 0).astype(F32)
    own2 = jnp.where(r128 >= seg_row2, jnp.where(r128 < end_row2, 1.0, 0.0), 0.0).astype(BF16)
    row_of = _dot(own2, jnp.concatenate([chunk_row, offs_of], axis=0).astype(BF16))
    w_of = _dot(own2[:, 0:N_EXPERTS], wmat.astype(BF16))
    riota = lax.broadcasted_iota(jnp.int32, (TILE_ROWS, tm), 0).astype(F32)
    hit = row_of == riota
    xg = _dot(jnp.where(hit, 1.0, 0.0).astype(BF16), h).astype(BF16)
    return xg, jnp.where(hit, w_of, 0.0).astype(BF16)


def _dispatch_kernel(pos_ref, pad_ref, nch_ref, h_ref, idx_ref, wt_ref, xs_ref, pw_ref, buf0, buf1, zero_buf,
                     sems, pad_sem, *, n_tiles):
    j = pl.program_id(0)
    bufs = (buf0, buf1)

    def compute(p):
        xg, pw = _dispatch_tile(h_ref[0], idx_ref[0], wt_ref[0])
        bufs[p][...] = xg.reshape(bufs[p].shape)
        pw_ref[...] = pw

    def chunk_copy(tile, p, c):
        return pltpu.make_async_copy(bufs[p].at[c], xs_ref.at[pos_ref[tile * TILE_CHUNKS + c]], sems.at[p])

    def issue(tile, p):
        for c in range(MIN_TILE_CHUNKS):
            chunk_copy(tile, p, c).start()

        def rest(c, carry):
            chunk_copy(tile, p, c).start()
            return carry
        lax.fori_loop(MIN_TILE_CHUNKS, nch_ref[tile], rest, 0)

    def wait(tile, p):
        pltpu.make_async_copy(bufs[p].at[pl.ds(0, MIN_TILE_CHUNKS)], xs_ref.at[pl.ds(0, MIN_TILE_CHUNKS)],
                              sems.at[p]).wait()

        def rest(c, carry):
            chunk_copy(tile, p, c).wait()
            return carry
        lax.fori_loop(MIN_TILE_CHUNKS, nch_ref[tile], rest, 0)

    for p in range(2):
        @pl.when((j >= 2) & (j % 2 == p))
        def _(p=p):
            wait(j - 2, p)

        @pl.when((j >= 1) & (j < n_tiles) & (j % 2 == p))
        def _(p=p):
            issue(j - 1, 1 - p)
            compute(p)

    @pl.when(j == 0)
    def _():
        compute(0)

    @pl.when(j == n_tiles)
    def _():
        last = (n_tiles - 1) % 2
        issue(n_tiles - 1, last)
        zero_buf[...] = jnp.zeros_like(zero_buf)

        def pad_expert(e, carry):
            def pad_issue(c, inner):
                pltpu.make_async_copy(zero_buf, xs_ref.at[pad_ref[e] + c], pad_sem).start()
                return inner
            lax.fori_loop(0, pad_ref[N_EXPERTS + e], pad_issue, 0)

            def pad_wait(c, inner):
                pltpu.make_async_copy(zero_buf, xs_ref.at[pad_ref[e] + c], pad_sem).wait()
                return inner
            lax.fori_loop(0, pad_ref[N_EXPERTS + e], pad_wait, 0)
            return carry
        lax.fori_loop(0, N_EXPERTS, pad_expert, 0)
        wait(n_tiles - 1, last)


def _dispatch(h2, idx, wts, pos, pad_pos, tile_chunks, n_slots):
    B, S, D = h2.shape
    tm = MOE_TILE
    nt = S // tm
    n_tiles = B * nt
    tile = lambda j: jnp.minimum(j, n_tiles - 1)
    grid_spec = pltpu.PrefetchScalarGridSpec(
        num_scalar_prefetch=3,
        grid=(n_tiles + 1,),
        in_specs=[pl.BlockSpec((1, tm, D), lambda j, pos, pad, nch: (tile(j) // nt, tile(j) % nt, 0)),
                  pl.BlockSpec((1, TOP_K, tm), lambda j, pos, pad, nch: (tile(j) // nt, 0, tile(j) % nt)),
                  pl.BlockSpec((1, TOP_K, tm), lambda j, pos, pad, nch: (tile(j) // nt, 0, tile(j) % nt))],
        out_specs=[pl.BlockSpec(memory_space=pl.ANY),
                   pl.BlockSpec((TILE_ROWS, tm), lambda j, pos, pad, nch: (tile(j), 0))],
        scratch_shapes=[pltpu.VMEM((TILE_CHUNKS, CHUNK, D), BF16),
                        pltpu.VMEM((TILE_CHUNKS, CHUNK, D), BF16),
                        pltpu.VMEM((CHUNK, D), BF16),
                        pltpu.SemaphoreType.DMA((2,)),
                        pltpu.SemaphoreType.DMA(())],
    )
    return pl.pallas_call(
        functools.partial(_dispatch_kernel, n_tiles=n_tiles),
        grid_spec=grid_spec,
        out_shape=[jax.ShapeDtypeStruct((n_slots, CHUNK, D), BF16),
                   jax.ShapeDtypeStruct((n_tiles * TILE_ROWS, tm), BF16)],
        compiler_params=_params(("arbitrary",)),
        name="moe_dispatch",
    )(pos, pad_pos, tile_chunks, h2, idx, wts)


def _moe_tables(cnt, g_max):
    nt = cnt.shape[0]
    cc = (cnt + (CHUNK - 1)) // CHUNK
    segblk = jnp.cumsum(cc, axis=1) - cc
    tile_chunks = jnp.sum(cc, axis=1)
    prior = jnp.cumsum(cc, axis=0) - cc
    ge_cnt = (jnp.sum(cc, axis=0) + (GROUP_CHUNKS - 1)) // GROUP_CHUNKS
    gbase = jnp.cumsum(ge_cnt) - ge_cnt
    n_groups = jnp.sum(ge_cnt)
    c = jnp.arange(TILE_CHUNKS, dtype=jnp.int32)
    e_of = jnp.sum(((segblk + cc)[:, None, :] <= c[None, :, None]).astype(jnp.int32), axis=-1)
    e_of = jnp.minimum(e_of, N_EXPERTS - 1)
    seg_base = gbase[None, :] * GROUP_CHUNKS + prior - segblk
    onehot = e_of[:, :, None] == jnp.arange(N_EXPERTS, dtype=jnp.int32)[None, None, :]
    pos = jnp.sum(jnp.where(onehot, seg_base[:, None, :], 0), axis=-1) + c[None, :]
    valid = c[None, :] < tile_chunks[:, None]
    n_slots = g_max * GROUP_CHUNKS
    pos_write = jnp.where(valid, pos, 0).astype(jnp.int32)
    ce = jnp.sum(cc, axis=0)
    pad_pos = jnp.concatenate([gbase * GROUP_CHUNKS + ce, ge_cnt * GROUP_CHUNKS - ce]).astype(jnp.int32)
    g = jnp.arange(g_max, dtype=jnp.int32)
    grp_e = jnp.minimum(jnp.sum(((gbase + ge_cnt)[None, :] <= g[:, None]).astype(jnp.int32), axis=1),
                        N_EXPERTS - 1).astype(jnp.int32)
    hi = lax.Precision.HIGHEST
    experts = jnp.arange(N_EXPERTS, dtype=jnp.int32)
    slot = jnp.arange(n_slots, dtype=jnp.int32)
    e_slot = jnp.minimum(jnp.sum((((gbase + ge_cnt) * GROUP_CHUNKS)[None, :] <= slot[:, None]).astype(jnp.int32),
                                 axis=1), N_EXPERTS - 1)
    oh_e = (e_slot[:, None] == experts[None, :]).astype(F32)
    q = slot - jnp.sum(oh_e * (gbase * GROUP_CHUNKS).astype(F32)[None, :], axis=1).astype(jnp.int32)
    slot_valid = q < jnp.sum(oh_e * ce.astype(F32)[None, :], axis=1).astype(jnp.int32)
    cum_end = jnp.dot(oh_e, (prior + cc).astype(F32).T, precision=hi)
    t_slot = jnp.minimum(jnp.sum((cum_end <= q[:, None].astype(F32)).astype(jnp.int32), axis=1), nt - 1)
    oh_t = (t_slot[:, None] == jnp.arange(nt, dtype=jnp.int32)[None, :]).astype(F32)
    shift = jnp.sum(jnp.dot(oh_t, (segblk - prior).astype(F32), precision=hi) * oh_e, axis=1).astype(jnp.int32)
    spare = nt * TILE_CHUNKS + ((slot // GROUP_CHUNKS) % 2) * GROUP_CHUNKS + slot % GROUP_CHUNKS
    inv = jnp.where(slot_valid, t_slot * TILE_CHUNKS + q + shift, spare).astype(jnp.int32)
    tiles = jnp.arange(nt, dtype=jnp.int32)
    fill = jnp.concatenate([tiles * TILE_CHUNKS + tile_chunks, TILE_CHUNKS - tile_chunks]).astype(jnp.int32)
    return (pos_write.reshape(-1), tile_chunks.astype(jnp.int32), inv, pad_pos.reshape(-1), fill, grp_e,
            n_groups.reshape(1).astype(jnp.int32))


def _ffn_kernel(ge_ref, na_ref, inv_ref, fill_ref, x_ref, wg_ref, wu_ref, wd_ref, yt_ref,
                buf0, buf1, zero_buf, wg_s, wu_s, wd_s, sems, fill_sem, *, n_tiles):
    g = pl.program_id(0)
    na = na_ref[0]
    bufs = (buf0, buf1)

    g_live = jnp.minimum(g, jnp.maximum(na - 1, 0))
    @pl.when((g < na) & ((g == 0) | (ge_ref[g_live] != ge_ref[jnp.maximum(g_live - 1, 0)])))
    def _():
        wg_s[...] = wg_ref[0, 0].astype(BF16)
        wu_s[...] = wu_ref[0, 0].astype(BF16)
        wd_s[...] = wd_ref[0, 0].astype(BF16)

    def compute(p):
        x = x_ref[...].reshape(GROUP_ROWS, x_ref.shape[2])
        gate = _dot(x, wg_s[...])
        up = _dot(x, wu_s[...])
        act = (_silu(gate) * up).astype(BF16)
        bufs[p][...] = _dot(act, wd_s[...]).astype(BF16).reshape(bufs[p].shape)

    def issue(grp, p):
        for c in range(GROUP_CHUNKS):
            pltpu.make_async_copy(bufs[p].at[c], yt_ref.at[inv_ref[grp * GROUP_CHUNKS + c]], sems.at[p]).start()

    def wait(p):
        pltpu.make_async_copy(bufs[p], yt_ref.at[pl.ds(0, GROUP_CHUNKS)], sems.at[p]).wait()

    @pl.when(g == 0)
    def _():
        zero_buf[...] = jnp.zeros_like(zero_buf)

        def fill_tile(t, carry):
            for action in ("start", "wait"):
                off = fill_ref[t]
                left = fill_ref[n_tiles + t]
                for size in FILL_SIZES:
                    take = left >= size

                    @pl.when(take)
                    def _(off=off, size=size):
                        cp = pltpu.make_async_copy(zero_buf.at[pl.ds(0, size)], yt_ref.at[pl.ds(off, size)], fill_sem)
                        cp.start() if action == "start" else cp.wait()
                    off = off + jnp.where(take, size, 0)
                    left = left - jnp.where(take, size, 0)
            return carry
        lax.fori_loop(0, n_tiles, fill_tile, 0)

    for p in range(2):
        @pl.when((g >= 2) & (g - 2 < na) & (g % 2 == p))
        def _(p=p):
            wait(p)

        @pl.when((g >= 1) & (g < na) & (g % 2 == p))
        def _(p=p):
            issue(g - 1, 1 - p)
            compute(p)

        @pl.when((g >= 1) & (g == na) & (g % 2 == p))
        def _(p=p):
            issue(g - 1, 1 - p)

    @pl.when((g == 0) & (na > 0))
    def _():
        compute(0)


def _ffn_grouped(xs, inv, fill, grp_e, n_groups, layer, w_gate, w_up, w_down, g_max, n_tiles):
    D = xs.shape[2]
    live = lambda g, na: jnp.minimum(g, jnp.maximum(na[0] - 1, 0))
    wspec = lambda shape: pl.BlockSpec((1, 1) + shape, lambda g, ge, na, inv, fill: (layer, ge[live(g, na)], 0, 0))
    grid_spec = pltpu.PrefetchScalarGridSpec(
        num_scalar_prefetch=4,
        grid=(g_max + 2,),
        in_specs=[pl.BlockSpec((GROUP_CHUNKS, CHUNK, D), lambda g, ge, na, inv, fill: (live(g, na), 0, 0)),
                  wspec((D, EXPERT_FF)), wspec((D, EXPERT_FF)), wspec((EXPERT_FF, D))],
        out_specs=pl.BlockSpec(memory_space=pl.ANY),
        scratch_shapes=[pltpu.VMEM((GROUP_CHUNKS, CHUNK, D), BF16),
                        pltpu.VMEM((GROUP_CHUNKS, CHUNK, D), BF16),
                        pltpu.VMEM((FILL_SIZES[0], CHUNK, D), BF16),
                        pltpu.VMEM((D, EXPERT_FF), BF16),
                        pltpu.VMEM((D, EXPERT_FF), BF16),
                        pltpu.VMEM((EXPERT_FF, D), BF16),
                        pltpu.SemaphoreType.DMA((2,)),
                        pltpu.SemaphoreType.DMA(())],
    )
    return pl.pallas_call(
        functools.partial(_ffn_kernel, n_tiles=n_tiles),
        grid_spec=grid_spec,
        out_shape=jax.ShapeDtypeStruct((n_tiles * TILE_CHUNKS + 2 * GROUP_CHUNKS, CHUNK, D), BF16),
        compiler_params=_params(("arbitrary",)),
        name="moe_ffn",
    )(grp_e, n_groups, inv, fill, xs, w_gate, w_up, w_down)


def _combine_kernel(yt_ref, pw_ref, base_ref, mod_ref, fg_ref, o_ref, *, final):
    yt = yt_ref[...].reshape(TILE_ROWS, yt_ref.shape[2])
    routed = _dot_tn(pw_ref[...], yt)
    out = base_ref[...] + mod_ref[0, 5:6, :] * routed
    if final:
        out = out * lax.rsqrt(jnp.mean(out * out, axis=-1, keepdims=True) + EPS) * fg_ref[...]
    o_ref[...] = out


def _combine(yt, pw, base, mod, mod_row, final_g):
    B, S, D = base.shape
    tm = MOE_TILE
    nt = S // tm
    final = final_g is not None
    fg = (final_g if final else jnp.ones((D,), F32)).reshape(1, D)
    out = pl.pallas_call(
        functools.partial(_combine_kernel, final=final),
        grid=(B * nt,),
        in_specs=[pl.BlockSpec((TILE_CHUNKS, CHUNK, D), lambda i: (i, 0, 0)),
                  pl.BlockSpec((TILE_ROWS, tm), lambda i: (i, 0)),
                  pl.BlockSpec((tm, D), lambda i: (i, 0)),
                  pl.BlockSpec((1, 6, D), lambda i: (mod_row(i // nt), 0, 0)),
                  pl.BlockSpec((1, D), lambda i: (0, 0))],
        out_specs=pl.BlockSpec((tm, D), lambda i: (i, 0)),
        out_shape=jax.ShapeDtypeStruct((B * S, D), F32),
        compiler_params=_params(("parallel",)),
        name="moe_combine",
    )(yt, pw, base.reshape(B * S, D), mod, fg)
    return out.reshape(B, S, D)


def _moe_sparse(h2, idx, wts, cnt, base, mod, mod_row, layer, w_gate, w_up, w_down, final_g=None):
    B, S, D = h2.shape
    n_tiles = B * (S // MOE_TILE)
    g_max = (n_tiles * TILE_CHUNKS + N_PAD_CHUNKS + GROUP_CHUNKS - 1) // GROUP_CHUNKS
    pos_write, tile_chunks, inv, pad_pos, fill, grp_e, n_groups = _moe_tables(cnt[:, :, 0], g_max)
    xs, pw = _dispatch(h2, idx, wts, pos_write, pad_pos, tile_chunks, g_max * GROUP_CHUNKS)
    yt = _ffn_grouped(xs, inv, fill, grp_e, n_groups, layer, w_gate, w_up, w_down, g_max, n_tiles)
    return _combine(yt, pw, base, mod, mod_row, final_g)


def _zero_state(batch):
    nq = M_HEADS * M_QK_DIM
    return (jnp.zeros((batch, 2, nq, M_V_DIM), F32),
            jnp.zeros((batch, 2, 1, nq), F32),
            jnp.zeros((batch, 2, 1, LANES), F32))


def kernel(x, c, ctx, c_ctx, ada_w, ada_b, norm1_g, norm2_g, w_in, attn_sink, mlstm_gate_b, mlstm_norm_g, conv_w, w_br_attn, w_br_mlstm, w_br_conv, branch_gate_b, w_out, router_w, router_bias, exp_w_gate, exp_w_up, exp_w_down, sh_w_gate, sh_w_up, sh_w_down, final_g):
    B, S, D = x.shape
    L = ctx.shape[1]
    depth = ada_w.shape[0]
    ctx_row = B

    pad_rows = (-(B + 1)) % 8
    cc = jnp.concatenate([c, c_ctx[None, :], jnp.zeros((pad_rows, D), F32)], axis=0)
    mod_all = _ada(cc, ada_w, ada_b).reshape(depth, B + 1 + pad_rows, 6, D)

    cos_t, sin_t = _rope_tables(S)
    cos_c = jnp.ones((L, LANES), F32)
    sin_c = jnp.zeros((L, LANES), F32)
    lat_row = lambda b: b
    ctx_mod = lambda b: ctx_row

    xc = ctx
    for l in range(depth):
        need_ctx = l < depth - 1
        mod = mod_all[l]
        w_p = _permute_w_in(w_in[l])
        lw = {
            'mlstm_norm_g': mlstm_norm_g[l].reshape(1, M_OUT),
            'conv_w': conv_w[l],
            'branch_gate_b': branch_gate_b[l].reshape(1, N_BRANCH * D),
            'w_br_attn': w_br_attn[l].reshape(ATT_KV_HEADS, ATT_GROUP, ATT_HEAD_DIM, D)
                         .transpose(1, 0, 2, 3).reshape(ATT_OUT, D).astype(BF16),
            'w_br_mlstm': w_br_mlstm[l].astype(BF16),
            'w_br_conv': w_br_conv[l].astype(BF16),
            'w_out': w_out[l].astype(BF16),
            'norm2_g': norm2_g[l].reshape(1, D),
            'router_wt': _split_hi_lo(router_w[l].T),
            'router_bias': router_bias[l].reshape(N_EXPERTS, 1),
            'sh_gu': jnp.concatenate([sh_w_gate[l], sh_w_up[l]], axis=1).astype(BF16),
            'sh_d': sh_w_down[l].astype(BF16),
        }
        experts = (l, exp_w_gate, exp_w_up, exp_w_down)

        pc = _in_proj(xc, mod, ctx_mod, norm1_g[l], w_p, cos_c, sin_c, tm=256)
        p = _in_proj(x, mod, lat_row, norm1_g[l], w_p, cos_t, sin_t, tm=min(512, S))
        qs_c, k_c, v_c, qm_c, km_c, vm_c, om_c, gm_c, bc_c, u_c, gt_c = pc
        qs, k, v, qm, km, vm, om, gm, bc, u, gt = p

        oat = _attention(qs, k, v, k_c, v_c, attn_sink[l], band=True)
        hf_c, hb_c, st = _mlstm(qm_c, km_c, vm_c, gm_c, mlstm_gate_b[l], _zero_state(B))
        hf, hb, _ = _mlstm(qm, km, vm, gm, mlstm_gate_b[l], st)

        base, h2, idx, wts, cnt = _merge(x, mod, lat_row, oat, hf, hb, om, u, bc, gt, lw, tm=min(2 * MOE_TILE, S))
        x_new = _moe_sparse(h2, idx, wts, cnt, base, mod, lat_row, *experts,
                            final_g=final_g if l == depth - 1 else None)

        if need_ctx:
            oat_c = _attention(qs_c, None, None, k_c, v_c, attn_sink[l], band=False)
            base_c, h2_c, idx_c, wts_c, cnt_c = _merge(xc, mod, ctx_mod, oat_c, hf_c, hb_c, om_c, u_c, bc_c, gt_c, lw,
                                                       tm=MOE_TILE)
            xc = _moe_sparse(h2_c, idx_c, wts_c, cnt_c, base_c, mod, ctx_mod, *experts)
        x = x_new
    return x
```

```python
import functools

import numpy as np
import jax
import jax.numpy as jnp
from jax import lax
from jax.experimental import pallas as pl
from jax.experimental.pallas import tpu as pltpu

F32 = jnp.float32
BF16 = jnp.bfloat16

D_MODEL = 1024
GRID_W = 64
EPS = 1e-6
ATT_HEADS = 8
ATT_KV_HEADS = 2
ATT_HEAD_DIM = 64
ATT_GROUP = ATT_HEADS // ATT_KV_HEADS
ATT_BLOCK = 128
ATT_OUT = ATT_HEADS * ATT_HEAD_DIM
ROPE_BASE = 10000.0
M_HEADS = 4
M_QK_DIM = 64
M_V_DIM = 128
M_CHUNK = 64
M_OUT = M_HEADS * M_V_DIM
CONV_WIDTH = 512
N_BRANCH = 3
N_EXPERTS = 64
N_GROUPS = 8
GROUP_SIZE = N_EXPERTS // N_GROUPS
TOPK_GROUPS = 4
TOP_K = 8
EXPERT_FF = 256
SHARED_FF = 256
ROUTED_SCALE = 2.5

LOG2E = 1.4426950408889634
LANES = 128
BF16_SUBLANES = 16
VMEM_LIMIT = 56 * 1024 * 1024

_SEGS = (('q', 512), ('k', 128), ('v', 128), ('qm', 256), ('km', 256), ('vm', 512), ('om', 512),
         ('bc', 512), ('cc', 512), ('xc', 512), ('gt', 3072), ('gm', 128))
_OFF = {}
_o = 0
for _n, _s in _SEGS:
    _OFF[_n] = (_o, _o + _s)
    _o += _s
N_PROJ = _o
D_IN = 6928


def _permute_w_in(w):
    d = w.shape[0]
    half = ATT_HEAD_DIM // 2
    q = w[:, 0:512].reshape(d, ATT_KV_HEADS, ATT_GROUP, half, 2).transpose(0, 2, 1, 4, 3).reshape(d, 512)
    k = w[:, 512:640].reshape(d, ATT_KV_HEADS, half, 2).transpose(0, 1, 3, 2).reshape(d, 128)
    gm = jnp.pad(w[:, 2304:2320], ((0, 0), (0, LANES - 16)))
    out = jnp.concatenate([q, k, w[:, 640:2304], w[:, 2320:D_IN], gm], axis=1)
    assert out.shape[1] == N_PROJ
    return out.astype(BF16)


def _rope_tables(seq):
    rows = seq // GRID_W
    row = jnp.repeat(jnp.arange(rows, dtype=F32), GRID_W)
    col = jnp.tile(jnp.arange(GRID_W, dtype=F32), rows)
    n_pairs = ATT_HEAD_DIM // 4
    inv_freq = ROPE_BASE ** (-jnp.arange(n_pairs, dtype=F32) / n_pairs)
    ang = jnp.concatenate([row[:, None] * inv_freq, col[:, None] * inv_freq], axis=-1)
    c, s = jnp.cos(ang), jnp.sin(ang)
    cos_t = jnp.concatenate([c, c, c, c], axis=-1)
    sin_t = jnp.concatenate([-s, s, -s, s], axis=-1)
    return cos_t, sin_t


def _dot(a, b):
    return jnp.dot(a, b, preferred_element_type=F32)


def _dot_nt(a, b):
    return lax.dot_general(a, b, (((1,), (1,)), ((), ())), preferred_element_type=F32)


def _dot_tn(a, b):
    return lax.dot_general(a, b, (((0,), (0,)), ((), ())), preferred_element_type=F32)


def _split_hi_lo(w):
    hi = w.astype(BF16)
    lo = (w - hi.astype(F32)).astype(BF16)
    return jnp.concatenate([hi, lo], axis=0)


def _sigmoid(x):
    return 1.0 / (1.0 + jnp.exp(-x))


def _sigmoid_tanh(x):
    return 0.5 * jnp.tanh(0.5 * x) + 0.5


def _silu(x):
    return x * _sigmoid(x)


def _log_sigmoid(x):
    return jnp.minimum(x, 0.0) - jnp.log(1.0 + jnp.exp(-jnp.abs(x)))


def _rms_mod(x, g, shift, scale):
    y = x * lax.rsqrt(jnp.mean(x * x, axis=-1, keepdims=True) + EPS) * g
    return y * (1.0 + scale) + shift


def _params(sem):
    return pltpu.CompilerParams(dimension_semantics=sem, vmem_limit_bytes=VMEM_LIMIT)


def _ada_kernel(c_ref, w_ref, b_ref, o_ref):
    s = _silu(c_ref[...])
    o_ref[0] = jnp.dot(s, w_ref[0], preferred_element_type=F32,
                       precision=lax.Precision.HIGHEST) + b_ref[0]


def _ada(cc, ada_w, ada_b):
    depth, d, n = ada_w.shape
    rows = cc.shape[0]
    tn = 1536
    return pl.pallas_call(
        _ada_kernel,
        grid=(depth, n // tn),
        in_specs=[pl.BlockSpec((rows, d), lambda l, j: (0, 0)),
                  pl.BlockSpec((1, d, tn), lambda l, j: (l, 0, j)),
                  pl.BlockSpec((1, 1, tn), lambda l, j: (l, 0, j))],
        out_specs=pl.BlockSpec((1, rows, tn), lambda l, j: (l, 0, j)),
        out_shape=jax.ShapeDtypeStruct((depth, rows, n), F32),
        compiler_params=_params(("parallel", "parallel")),
        name="ada_mod",
    )(cc, ada_w, ada_b.reshape(depth, 1, n))


def _swap_halves(x):
    lane = lax.broadcasted_iota(jnp.int32, x.shape, 1)
    first = (lane % ATT_HEAD_DIM) < (ATT_HEAD_DIM // 2)
    return jnp.where(first, pltpu.roll(x, LANES - 32, axis=1), pltpu.roll(x, 32, axis=1))


def _in_kernel(x_ref, mod_ref, g_ref, w_ref, cos_ref, sin_ref,
               qs_ref, k_ref, v_ref, qm_ref, km_ref, vm_ref, om_ref, gm_ref, bc_ref, u_ref, gt_ref):
    tm = x_ref.shape[1]
    h = _rms_mod(x_ref[0], g_ref[...], mod_ref[0, 0:1, :], mod_ref[0, 1:2, :]).astype(BF16)

    def proj(name):
        lo, hi = _OFF[name]
        return _dot(h, w_ref[:, lo:hi])

    cos_t = cos_ref[...]
    sin_t = sin_ref[...]

    def rope(t):
        return t * cos_t + _swap_halves(t) * sin_t

    q = proj('q')
    scale = ATT_HEAD_DIM ** -0.5 * LOG2E
    for hh in range(ATT_GROUP):
        r = (rope(q[:, hh * LANES:(hh + 1) * LANES]) * scale).astype(BF16)
        for qb in range(tm // ATT_BLOCK):
            qs_ref[0, qb, hh * ATT_BLOCK:(hh + 1) * ATT_BLOCK, :] = r[qb * ATT_BLOCK:(qb + 1) * ATT_BLOCK, :]
    k_ref[0] = rope(proj('k')).astype(BF16)
    v_ref[0] = proj('v').astype(BF16)
    qm_ref[0] = proj('qm').astype(BF16)
    km_ref[0] = proj('km').astype(BF16)
    vm_ref[0] = proj('vm').astype(BF16)
    om_ref[0] = proj('om').astype(BF16)
    gm_ref[0] = proj('gm')
    bc_ref[0] = proj('bc').astype(BF16)
    u_ref[0] = (proj('cc') * proj('xc')).astype(BF16)
    gt_ref[0] = proj('gt').astype(BF16)


def _in_proj(x, mod, mod_row, norm_g, w_p, cos_t, sin_t, tm):
    B, S, D = x.shape
    nb = S // ATT_BLOCK
    tok = lambda n, dt: jax.ShapeDtypeStruct((B, S, n), dt)
    tspec = lambda n: pl.BlockSpec((1, tm, n), lambda b, i: (b, i, 0))
    out_shape = (jax.ShapeDtypeStruct((B, nb, ATT_GROUP * ATT_BLOCK, LANES), BF16),
                 tok(128, BF16), tok(128, BF16), tok(256, BF16), tok(256, BF16), tok(512, BF16),
                 tok(512, BF16), tok(128, F32), tok(512, BF16), tok(512, BF16), tok(3072, BF16))
    out_specs = (pl.BlockSpec((1, tm // ATT_BLOCK, ATT_GROUP * ATT_BLOCK, LANES), lambda b, i: (b, i, 0, 0)),
                 tspec(128), tspec(128), tspec(256), tspec(256), tspec(512), tspec(512), tspec(128),
                 tspec(512), tspec(512), tspec(3072))
    return pl.pallas_call(
        _in_kernel,
        grid=(B, S // tm),
        in_specs=[pl.BlockSpec((1, tm, D), lambda b, i: (b, i, 0)),
                  pl.BlockSpec((1, 6, D), lambda b, i: (mod_row(b), 0, 0)),
                  pl.BlockSpec((1, D), lambda b, i: (0, 0)),
                  pl.BlockSpec((D, N_PROJ), lambda b, i: (0, 0), pipeline_mode=pl.Buffered(1)),
                  pl.BlockSpec((tm, LANES), lambda b, i: (i, 0)),
                  pl.BlockSpec((tm, LANES), lambda b, i: (i, 0))],
        out_specs=out_specs,
        out_shape=out_shape,
        compiler_params=_params(("parallel", "parallel")),
        name="in_proj",
    )(x, mod, norm_g.reshape(1, D), w_p, cos_t, sin_t)


def _attn_block(q, sink_ref, kcat, vcat, masks):
    rows = q.shape[0]
    lane = lax.broadcasted_iota(jnp.int32, (1, LANES), 1)
    hh = lax.broadcasted_iota(jnp.int32, (rows, 1), 0) // ATT_BLOCK
    out = jnp.zeros((rows, LANES), F32)
    for g in range(ATT_KV_HEADS):
        lm = (lane < ATT_HEAD_DIM) if g == 0 else (lane >= ATT_HEAD_DIM)
        kz = jnp.where(lm, kcat, jnp.zeros_like(kcat))
        ones_lane = ATT_HEAD_DIM if g == 0 else 0
        vz = jnp.where(lm, vcat, jnp.where(lane == ones_lane, 1.0, 0.0).astype(BF16))
        s = _dot_nt(q, kz)
        if any(mk is not None for mk in masks):
            s = jnp.concatenate(
                [s[:, n * ATT_BLOCK:(n + 1) * ATT_BLOCK] if mk is None
                 else jnp.where(mk, s[:, n * ATT_BLOCK:(n + 1) * ATT_BLOCK], -jnp.inf)
                 for n, mk in enumerate(masks)], axis=1)
        sink = jnp.zeros((rows, 1), F32)
        for a in range(ATT_GROUP):
            sink = jnp.where(hh == a, sink_ref[g * ATT_GROUP + a] * LOG2E, sink)
        m = jnp.maximum(jnp.max(s, axis=-1, keepdims=True), sink)
        p = jnp.exp2((s - m).astype(BF16))
        pv = _dot(p, vz)
        l = pv[:, ones_lane:ones_lane + 1] + jnp.exp2(sink - m)
        out = out + jnp.where(lm, pv, 0.0) / l
    return out.astype(BF16)


ATT_STEP_BLOCKS = 8


def _attn_kernel(sink_ref, qs_ref, kc_ref, vc_ref, *rest, band):
    nsb = qs_ref.shape[1]
    if not band:
        (o_ref,) = rest
        n_ctx = kc_ref.shape[1] // ATT_BLOCK
        for sb in range(nsb):
            o_ref[0, sb] = _attn_block(qs_ref[0, sb], sink_ref, kc_ref[0], vc_ref[0], [None] * n_ctx)
        return
    kp_ref, kcur_ref, kn_ref, vp_ref, vcur_ref, vn_ref, o_ref = rest
    j = pl.program_id(1)
    nstep = pl.num_programs(1)
    n_ctx = kc_ref.shape[1] // ATT_BLOCK
    rows = qs_ref.shape[2]
    t = lax.broadcasted_iota(jnp.int32, (rows, 1), 0) % ATT_BLOCK
    i = lax.broadcasted_iota(jnp.int32, (1, ATT_BLOCK), 1)
    below = i >= t
    above = i <= t
    first = i >= t + jnp.where(j > 0, 0, 2 * ATT_BLOCK)
    last = i <= t - jnp.where(j < nstep - 1, 0, 2 * ATT_BLOCK)
    kblk = [kp_ref[0]] + [kcur_ref[0, n * ATT_BLOCK:(n + 1) * ATT_BLOCK, :] for n in range(nsb)] + [kn_ref[0]]
    vblk = [vp_ref[0]] + [vcur_ref[0, n * ATT_BLOCK:(n + 1) * ATT_BLOCK, :] for n in range(nsb)] + [vn_ref[0]]
    for sb in range(nsb):
        kcat = jnp.concatenate([kc_ref[0]] + kblk[sb:sb + 3], axis=0)
        vcat = jnp.concatenate([vc_ref[0]] + vblk[sb:sb + 3], axis=0)
        masks = [first if sb == 0 else below, None, last if sb == nsb - 1 else above]
        o_ref[0, sb] = _attn_block(qs_ref[0, sb], sink_ref, kcat, vcat, [None] * n_ctx + masks)


def _attention(qs, k, v, kc, vc, sink, band):
    B, nb = qs.shape[:2]
    lc = kc.shape[1]
    blocks = min(ATT_STEP_BLOCKS, nb)
    nstep = nb // blocks
    last = nb - 1
    qspec = pl.BlockSpec((1, blocks, ATT_GROUP * ATT_BLOCK, LANES), lambda b, j: (b, j, 0, 0))
    cspec = pl.BlockSpec((1, lc, LANES), lambda b, j: (b, 0, 0))
    in_specs = [pl.BlockSpec(memory_space=pltpu.SMEM), qspec, cspec, cspec]
    args = [sink.astype(F32), qs, kc, vc]
    if band:
        prev = pl.BlockSpec((1, ATT_BLOCK, LANES), lambda b, j: (b, jnp.maximum(blocks * j - 1, 0), 0))
        cur = pl.BlockSpec((1, blocks * ATT_BLOCK, LANES), lambda b, j: (b, j, 0))
        nxt = pl.BlockSpec((1, ATT_BLOCK, LANES), lambda b, j: (b, jnp.minimum(blocks * (j + 1), last), 0))
        in_specs += [prev, cur, nxt, prev, cur, nxt]
        args += [k, k, k, v, v, v]
    return pl.pallas_call(
        functools.partial(_attn_kernel, band=band),
        grid=(B, nstep),
        in_specs=in_specs,
        out_specs=qspec,
        out_shape=jax.ShapeDtypeStruct(qs.shape, BF16),
        compiler_params=_params(("parallel", "parallel")),
        name="attention_band" if band else "attention_ctx",
    )(*args)


def _mlstm_step(dirs, T):
    kscale = M_QK_DIM ** -0.5
    si = lax.broadcasted_iota(jnp.int32, (T, T), 0)
    ri = lax.broadcasted_iota(jnp.int32, (T, T), 1)
    lane_qk = lax.broadcasted_iota(jnp.int32, (1, M_HEADS * M_QK_DIM), 1) // M_QK_DIM
    lane_m = lax.broadcasted_iota(jnp.int32, (1, LANES), 1)
    row_c = lax.broadcasted_iota(jnp.int32, (M_HEADS * M_QK_DIM, 1), 0) // M_QK_DIM
    combos = [(d, hd) for d in range(2) for hd in range(M_HEADS)]

    tri, bcol, gt, bt, blast = [], [], [], [], []
    for d, (q, k, v, g, C, n, m) in enumerate(dirs):
        t = (ri <= si) if d == 0 else (ri >= si)
        tri.append(t)
        lf = _log_sigmoid(g)
        bc = jnp.dot(t.astype(F32), lf, preferred_element_type=F32, precision=lax.Precision.HIGHEST)
        bcol.append(bc)
        gt.append(g.T)
        bt.append(bc.T)
        blast.append(bc[T - 1:T, :] if d == 0 else bc[0:1, :])

    def lanes(d, hd):
        return (2 * d) * M_HEADS + hd, (2 * d + 1) * M_HEADS + hd

    b_col = {c: bcol[c[0]][:, lanes(*c)[1]:lanes(*c)[1] + 1] for c in combos}
    ig_col = {c: dirs[c[0]][3][:, lanes(*c)[0]:lanes(*c)[0] + 1] for c in combos}
    alpha = {c: gt[c[0]][lanes(*c)[0]:lanes(*c)[0] + 1, :] - bt[c[0]][lanes(*c)[1]:lanes(*c)[1] + 1, :]
             for c in combos}
    m_old = {c: dirs[c[0]][6][:, c[1]:c[1] + 1] for c in combos}
    b_last = {c: blast[c[0]][:, lanes(*c)[1]:lanes(*c)[1] + 1] for c in combos}
    hmask = {hd: lane_qk == hd for hd in range(M_HEADS)}

    a_mat = {c: jnp.where(tri[c[0]], alpha[c], -jnp.inf) for c in combos}
    a_max = {c: jnp.max(a_mat[c], axis=1, keepdims=True) for c in combos}
    a_int = {c: b_col[c] + m_old[c] for c in combos}
    m_s = {c: jnp.maximum(a_int[c], b_col[c] + a_max[c]) for c in combos}
    w_int = {c: jnp.exp(a_int[c] - m_s[c]) for c in combos}
    w_mat = {c: jnp.exp(a_mat[c] + (b_col[c] - m_s[c])) for c in combos}
    qmask = {c: jnp.where(hmask[c[1]], dirs[c[0]][0], jnp.zeros_like(dirs[c[0]][0])) for c in combos}
    s_qk = {c: w_mat[c] * (_dot_nt(qmask[c], dirs[c[0]][1]) * kscale) for c in combos}
    vh = {c: dirs[c[0]][2][:, c[1] * M_V_DIM:(c[1] + 1) * M_V_DIM] for c in combos}
    c_bf = [dirs[d][4].astype(BF16) for d in range(2)]
    num = {c: _dot(s_qk[c].astype(BF16), vh[c]) + w_int[c] * _dot(qmask[c], c_bf[c[0]]) for c in combos}
    qn_all = [dirs[d][0].astype(F32) * dirs[d][5] for d in range(2)]
    qn = {c: jnp.sum(jnp.where(hmask[c[1]], qn_all[c[0]], 0.0), axis=1, keepdims=True) for c in combos}
    den = {c: jnp.sum(s_qk[c], axis=1, keepdims=True) + w_int[c] * qn[c] for c in combos}
    h = {c: num[c] / jnp.maximum(jnp.abs(den[c]), jnp.exp(-m_s[c])) for c in combos}

    r_col = {c: b_last[c] - b_col[c] + ig_col[c] for c in combos}
    m_new = {c: jnp.maximum(b_last[c] + m_old[c], jnp.max(r_col[c], axis=0, keepdims=True)) for c in combos}
    decay = {c: jnp.exp(b_last[c] + m_old[c] - m_new[c]) for c in combos}
    w_r = {c: jnp.exp(r_col[c] - m_new[c]) for c in combos}

    outs = []
    for d, (q, k, v, g, C, n, m) in enumerate(dirs):
        w_lanes = jnp.zeros((T, M_HEADS * M_QK_DIM), F32)
        dec_lanes = jnp.zeros((1, M_HEADS * M_QK_DIM), F32)
        dec_rows = jnp.zeros((M_HEADS * M_QK_DIM, 1), F32)
        m_row = jnp.zeros((1, LANES), F32)
        for hd in range(M_HEADS):
            w_lanes = jnp.where(hmask[hd], w_r[(d, hd)], w_lanes)
            dec_lanes = jnp.where(hmask[hd], decay[(d, hd)], dec_lanes)
            dec_rows = jnp.where(row_c == hd, decay[(d, hd)], dec_rows)
            m_row = jnp.where(lane_m == hd, m_new[(d, hd)], m_row)
        kw = k.astype(F32) * (w_lanes * kscale)
        kwt = kw.T.astype(BF16)
        upd = jnp.concatenate(
            [_dot(kwt[hd * M_QK_DIM:(hd + 1) * M_QK_DIM, :], vh[(d, hd)]) for hd in range(M_HEADS)], axis=0)
        c_new = dec_rows * C + upd
        n_new = dec_lanes * n + jnp.sum(kw, axis=0, keepdims=True)
        h_all = jnp.concatenate([h[(d, hd)] for hd in range(M_HEADS)], axis=1)
        outs.append((h_all, c_new, n_new, m_row))
    return outs


def _mlstm_kernel(gb_ref, qf_ref, kf_ref, vf_ref, gf_ref, qb_ref, kb_ref, vb_ref, gbk_ref,
                  c0_ref, n0_ref, m0_ref, hf_ref, hb_ref, cf_ref, nf_ref, mf_ref,
                  c_s, n_s, m_s):
    ci = pl.program_id(1)
    T = MLSTM_TILE
    n_sub = qf_ref.shape[1] // T

    @pl.when(ci == 0)
    def _():
        c_s[...] = c0_ref[0]
        n_s[...] = n0_ref[0]
        m_s[...] = m0_ref[0]

    gb = gb_ref[...]
    state = [(c_s[d], n_s[d], m_s[d]) for d in range(2)]
    refs = ((qf_ref, kf_ref, vf_ref, gf_ref, hf_ref), (qb_ref, kb_ref, vb_ref, gbk_ref, hb_ref))
    for s in range(n_sub):
        lo = (s * T, (n_sub - 1 - s) * T)
        dirs = [(q_ref[0, lo[d]:lo[d] + T, :], k_ref[0, lo[d]:lo[d] + T, :], v_ref[0, lo[d]:lo[d] + T, :],
                 g_ref[0, lo[d]:lo[d] + T, :] + gb) + state[d]
                for d, (q_ref, k_ref, v_ref, g_ref, _) in enumerate(refs)]
        outs = _mlstm_step(dirs, T)
        for d in range(2):
            refs[d][4][0, lo[d]:lo[d] + T, :] = outs[d][0]
        state = [outs[d][1:] for d in range(2)]
    for d in range(2):
        c_s[d], n_s[d], m_s[d] = state[d]

    @pl.when(ci == pl.num_programs(1) - 1)
    def _():
        cf_ref[0] = c_s[...]
        nf_ref[0] = n_s[...]
        mf_ref[0] = m_s[...]


MLSTM_TILE = 128
MLSTM_STEP_CHUNKS = 4


def _mlstm(qm, km, vm, gm, gate_b, state):
    B, S, _ = qm.shape
    T = MLSTM_TILE * min(MLSTM_STEP_CHUNKS, S // MLSTM_TILE)
    nc = S // T
    nq = M_HEADS * M_QK_DIM
    fwd = lambda n: pl.BlockSpec((1, T, n), lambda b, c: (b, c, 0))
    bwd = lambda n: pl.BlockSpec((1, T, n), lambda b, c: (b, nc - 1 - c, 0))
    st_specs = [pl.BlockSpec((1, 2, nq, M_V_DIM), lambda b, c: (b, 0, 0, 0)),
                pl.BlockSpec((1, 2, 1, nq), lambda b, c: (b, 0, 0, 0)),
                pl.BlockSpec((1, 2, 1, LANES), lambda b, c: (b, 0, 0, 0))]
    st_shapes = [jax.ShapeDtypeStruct((B, 2, nq, M_V_DIM), F32),
                 jax.ShapeDtypeStruct((B, 2, 1, nq), F32),
                 jax.ShapeDtypeStruct((B, 2, 1, LANES), F32)]
    gb_row = jnp.pad(gate_b.reshape(1, -1).astype(F32), ((0, 0), (0, LANES - gate_b.size)))
    outs = pl.pallas_call(
        _mlstm_kernel,
        grid=(B, nc),
        in_specs=[pl.BlockSpec((1, LANES), lambda b, c: (0, 0)),
                  fwd(256), fwd(256), fwd(512), fwd(LANES),
                  bwd(256), bwd(256), bwd(512), bwd(LANES)] + st_specs,
        out_specs=[fwd(M_OUT), bwd(M_OUT)] + st_specs,
        out_shape=[jax.ShapeDtypeStruct((B, S, M_OUT), F32)] * 2 + st_shapes,
        scratch_shapes=[pltpu.VMEM((2, nq, M_V_DIM), F32),
                        pltpu.VMEM((2, 1, nq), F32),
                        pltpu.VMEM((2, 1, LANES), F32)],
        compiler_params=_params(("parallel", "arbitrary")),
        name="mlstm_scan",
    )(gb_row, qm, km, vm, gm, qm, km, vm, gm, *state)
    return outs[0], outs[1], tuple(outs[2:])


def _route(scores, sel):
    tm = scores.shape[1]
    gi8 = lax.broadcasted_iota(jnp.int32, (GROUP_SIZE, tm), 0)

    def stack_rows(rows):
        out = jnp.broadcast_to(rows[0], (len(rows), tm))
        for r, v in enumerate(rows[1:], start=1):
            out = jnp.where(gi8 == r, v, out)
        return out

    gs = []
    for g in range(N_GROUPS):
        blk = sel[g * GROUP_SIZE:(g + 1) * GROUP_SIZE, :]
        m1 = jnp.max(blk, axis=0, keepdims=True)
        first = jnp.min(jnp.where(blk == m1, gi8, GROUP_SIZE), axis=0, keepdims=True)
        m2 = jnp.max(jnp.where(gi8 == first, -jnp.inf, blk), axis=0, keepdims=True)
        gs.append(m1 + m2)
    gsc = stack_rows(gs)
    gsel = jnp.zeros((N_GROUPS, tm), F32)
    for _ in range(TOPK_GROUPS):
        mx = jnp.max(gsc, axis=0, keepdims=True)
        first = jnp.min(jnp.where(gsc == mx, gi8, N_GROUPS), axis=0, keepdims=True)
        pick = gi8 == first
        gsel = jnp.where(pick, 1.0, gsel)
        gsc = jnp.where(pick, -jnp.inf, gsc)
    cur = jnp.concatenate(
        [jnp.where(gsel[g:g + 1, :] > 0.0, sel[g * GROUP_SIZE:(g + 1) * GROUP_SIZE, :], -jnp.inf)
         for g in range(N_GROUPS)], axis=0)
    ei = lax.broadcasted_iota(jnp.int32, (N_EXPERTS, tm), 0)
    idx, wts = [], []
    for _ in range(TOP_K):
        mx = jnp.max(cur, axis=0, keepdims=True)
        first = jnp.min(jnp.where(cur == mx, ei, N_EXPERTS), axis=0, keepdims=True)
        pick = ei == first
        idx.append(first)
        wts.append(jnp.sum(jnp.where(pick, scores, 0.0), axis=0, keepdims=True))
        cur = jnp.where(pick, -jnp.inf, cur)
    tot = wts[0]
    for w in wts[1:]:
        tot = tot + w
    wts = [w / tot * ROUTED_SCALE for w in wts]
    return stack_rows(idx), stack_rows(wts)


def _merge_kernel(x_ref, mod_ref, oat_ref, hf_ref, hb_ref, om_ref, ng_ref, u_ref, up_ref, un_ref,
                  bc_ref, cw_ref, gt_ref, bgb_ref, wa_ref, wm_ref, wc_ref, wo_ref, n2_ref,
                  rw_ref, rb_ref, sgu_ref, sd_ref,
                  base_ref, h2_ref, idx_ref, wt_ref, cnt_ref):
    i = pl.program_id(1)
    tm = x_ref.shape[1]
    x = x_ref[0]
    g1 = mod_ref[0, 2:3, :]
    sh2 = mod_ref[0, 3:4, :]
    sc2 = mod_ref[0, 4:5, :]
    g2 = mod_ref[0, 5:6, :]

    ya = jnp.concatenate(
        [jnp.concatenate([oat_ref[0, qb, hh * ATT_BLOCK:(hh + 1) * ATT_BLOCK, :] for hh in range(ATT_GROUP)], axis=1)
         for qb in range(tm // ATT_BLOCK)], axis=0)

    hsum = hf_ref[0] + hb_ref[0]
    parts = []
    for hd in range(M_HEADS):
        hh_ = hsum[:, hd * M_V_DIM:(hd + 1) * M_V_DIM]
        parts.append(hh_ * lax.rsqrt(jnp.mean(hh_ * hh_, axis=-1, keepdims=True) + EPS))
    hn = jnp.concatenate(parts, axis=1) * ng_ref[...]
    ym = (_sigmoid_tanh(om_ref[0].astype(F32)) * hn).astype(BF16)

    u = u_ref[0].astype(F32)
    row = lax.broadcasted_iota(jnp.int32, (tm, 1), 0)
    has_prev = (i > 0).astype(F32)
    has_next = (i < pl.num_programs(1) - 1).astype(F32)
    prev_row = up_ref[0, BF16_SUBLANES - 1:BF16_SUBLANES, :].astype(F32) * has_prev
    next_row = un_ref[0, 0:1, :].astype(F32) * has_next
    u_m1 = jnp.where(row == 0, prev_row, pltpu.roll(u, 1, axis=0))
    u_p1 = jnp.where(row == tm - 1, next_row, pltpu.roll(u, tm - 1, axis=0))
    conv = cw_ref[0:1, :] * u_m1 + cw_ref[1:2, :] * u + cw_ref[2:3, :] * u_p1
    yc = (bc_ref[0].astype(F32) * conv).astype(BF16)

    gg = _sigmoid_tanh(gt_ref[0].astype(F32) + bgb_ref[...])
    ymix = (gg[:, 0:D_MODEL] * _dot(ya, wa_ref[...])
            + gg[:, D_MODEL:2 * D_MODEL] * _dot(ym, wm_ref[...])
            + gg[:, 2 * D_MODEL:3 * D_MODEL] * _dot(yc, wc_ref[...]))
    y = _dot(ymix.astype(BF16), wo_ref[...])
    xm = x + g1 * y

    h2f = _rms_mod(xm, n2_ref[...], sh2, sc2)
    h2 = h2f.astype(BF16)
    h2_ref[0] = h2

    h2_lo = (h2f - h2.astype(F32)).astype(BF16)
    rw = rw_ref[...]
    part = _dot_nt(rw, h2)
    logits_t = part[0:N_EXPERTS, :] + part[N_EXPERTS:2 * N_EXPERTS, :] + _dot_nt(rw[0:N_EXPERTS, :], h2_lo)
    scores = _sigmoid(logits_t)
    idx, wts = _route(scores, scores + rb_ref[...])
    idx_ref[0] = idx
    wt_ref[0] = wts
    ei = lax.broadcasted_iota(jnp.int32, (N_EXPERTS, tm), 0)
    pick = jnp.zeros((N_EXPERTS, tm), F32)
    for kk in range(TOP_K):
        pick = jnp.where(ei == idx[kk:kk + 1, :], 1.0, pick)
    for sub in range(tm // MOE_TILE):
        n_e = jnp.sum(pick[:, sub * MOE_TILE:(sub + 1) * MOE_TILE], axis=1, keepdims=True)
        cnt_ref[sub] = jnp.broadcast_to(n_e, (N_EXPERTS, LANES)).astype(jnp.int32)

    a = _dot(h2, sgu_ref[...])
    act = (_silu(a[:, 0:SHARED_FF]) * a[:, SHARED_FF:2 * SHARED_FF]).astype(BF16)
    base_ref[0] = xm + g2 * _dot(act, sd_ref[...])


def _merge(x, mod, mod_row, oat, hf, hb, om, u, bc, gt, lw, tm):
    B, S, D = x.shape
    nt = S // tm
    hal = BF16_SUBLANES
    last_h = S // hal - 1
    tspec = lambda n: pl.BlockSpec((1, tm, n), lambda b, i: (b, i, 0))
    full = lambda a: pl.BlockSpec(a.shape, lambda b, i: (0,) * a.ndim, pipeline_mode=pl.Buffered(1))
    weights = [lw['mlstm_norm_g'], lw['conv_w'], lw['branch_gate_b'], lw['w_br_attn'], lw['w_br_mlstm'],
               lw['w_br_conv'], lw['w_out'], lw['norm2_g'], lw['router_wt'], lw['router_bias'],
               lw['sh_gu'], lw['sh_d']]
    in_specs = [tspec(D),
                pl.BlockSpec((1, 6, D), lambda b, i: (mod_row(b), 0, 0)),
                pl.BlockSpec((1, tm // ATT_BLOCK, ATT_GROUP * ATT_BLOCK, LANES), lambda b, i: (b, i, 0, 0)),
                tspec(M_OUT), tspec(M_OUT), tspec(M_OUT), full(weights[0]),
                tspec(CONV_WIDTH),
                pl.BlockSpec((1, hal, CONV_WIDTH), lambda b, i: (b, jnp.maximum(i * (tm // hal) - 1, 0), 0)),
                pl.BlockSpec((1, hal, CONV_WIDTH), lambda b, i: (b, jnp.minimum((i + 1) * (tm // hal), last_h), 0)),
                tspec(CONV_WIDTH), full(weights[1]), tspec(N_BRANCH * D), full(weights[2])]
    in_specs += [full(w) for w in weights[3:]]
    tr = lambda n, dt: (jax.ShapeDtypeStruct((B, n, S), dt), pl.BlockSpec((1, n, tm), lambda b, i: (b, 0, i)))
    outs = [(jax.ShapeDtypeStruct((B, S, D), F32), tspec(D)),
            (jax.ShapeDtypeStruct((B, S, D), BF16), tspec(D)),
            tr(TOP_K, jnp.int32), tr(TOP_K, F32),
            (jax.ShapeDtypeStruct((B * S // MOE_TILE, N_EXPERTS, LANES), jnp.int32),
             pl.BlockSpec((tm // MOE_TILE, N_EXPERTS, LANES), lambda b, i: (b * nt + i, 0, 0)))]
    return pl.pallas_call(
        _merge_kernel,
        grid=(B, nt),
        in_specs=in_specs,
        out_specs=[o[1] for o in outs],
        out_shape=[o[0] for o in outs],
        compiler_params=_params(("parallel", "parallel")),
        name="merge_route",
    )(x, mod, oat, hf, hb, om, weights[0], u, u, u, bc, weights[1], gt, weights[2], *weights[3:])


MOE_TILE = 256
CHUNK = BF16_SUBLANES
GROUP_CHUNKS = 64
GROUP_ROWS = GROUP_CHUNKS * CHUNK
TILE_ROWS = MOE_TILE * TOP_K + N_EXPERTS * CHUNK
TILE_CHUNKS = TILE_ROWS // CHUNK
N_PAD_CHUNKS = N_EXPERTS * (GROUP_CHUNKS - 1)
N_SPARE_CHUNKS = 2 * TILE_CHUNKS


def _dispatch_tile(h, idx, wts):
    tm = h.shape[0]
    ei = lax.broadcasted_iota(jnp.int32, (N_EXPERTS, tm), 0)
    pick = jnp.zeros((N_EXPERTS, tm), F32)
    wmat = jnp.zeros((N_EXPERTS, tm), F32)
    for kk in range(TOP_K):
        chosen = ei == idx[kk:kk + 1, :]
        pick = jnp.where(chosen, 1.0, pick)
        wmat = jnp.where(chosen, wts[kk:kk + 1, :], wmat)
    t0 = lax.broadcasted_iota(jnp.int32, (tm, tm), 0)
    t1 = lax.broadcasted_iota(jnp.int32, (tm, tm), 1)
    rank = _dot(pick.astype(BF16), jnp.where(t0 < t1, 1.0, 0.0).astype(BF16))
    n_e = jnp.sum(pick, axis=1, keepdims=True)
    n_pad = jnp.floor((n_e + (CHUNK - 1)) * (1.0 / CHUNK)) * CHUNK
    e0 = lax.broadcasted_iota(jnp.int32, (N_EXPERTS, N_EXPERTS), 0)
    e1 = lax.broadcasted_iota(jnp.int32, (N_EXPERTS, N_EXPERTS), 1)
    seg = _dot(jnp.where(e1 < e0, 1.0, 0.0).astype(BF16),
               jnp.broadcast_to(n_pad, (N_EXPERTS, tm)).astype(BF16))
    posmat = seg + rank
    chunk_of = jnp.floor(posmat * (1.0 / CHUNK))
    offs_of = posmat - chunk_of * CHUNK
    chunk_row = jnp.where(pick > 0.0, chunk_of * CHUNK, -float(CHUNK))
    eye = e0 == e1
    to_row = lambda col: jnp.sum(jnp.where(eye, col, 0.0), axis=0, keepdims=True)
    seg_row = to_row(seg[:, 0:1])
    end_row = to_row(seg[:, 0:1] + n_pad)
    seg_row2 = jnp.concatenate([seg_row, seg_row], axis=1)
    end_row2 = jnp.concatenate([end_row, end_row], axis=1)
    r128 = lax.broadcasted_iota(jnp.int32, (TILE_ROWS, 2 * N_EXPERTS), 0).astype(F32)
    own2 = jnp.where(r128 >= seg_row2, jnp.where(r128 < end_row2, 1.0, 0.0), 0.0).astype(BF16)
    row_of = _dot(own2, jnp.concatenate([chunk_row, offs_of], axis=0).astype(BF16))
    w_of = _dot(own2[:, 0:N_EXPERTS], wmat.astype(BF16))
    riota = lax.broadcasted_iota(jnp.int32, (TILE_ROWS, tm), 0).astype(F32)
    hit = row_of == riota
    xg = _dot(jnp.where(hit, 1.0, 0.0).astype(BF16), h).astype(BF16)
    return xg, jnp.where(hit, w_of, 0.0).astype(BF16)


def _dispatch_kernel(pos_ref, pad_ref, h_ref, idx_ref, wt_ref, xs_ref, pw_ref, buf0, buf1, zero_buf, sems, pad_sem,
                     *, n_tiles):
    j = pl.program_id(0)
    bufs = (buf0, buf1)

    def compute(p):
        xg, pw = _dispatch_tile(h_ref[0], idx_ref[0], wt_ref[0])
        bufs[p][...] = xg.reshape(bufs[p].shape)
        pw_ref[...] = pw

    def issue(tile, p):
        for c in range(TILE_CHUNKS):
            pltpu.make_async_copy(bufs[p].at[c], xs_ref.at[pos_ref[tile * TILE_CHUNKS + c]], sems.at[p]).start()

    def wait(p):
        pltpu.make_async_copy(bufs[p], xs_ref.at[pl.ds(0, TILE_CHUNKS)], sems.at[p]).wait()

    for p in range(2):
        @pl.when((j >= 2) & (j % 2 == p))
        def _(p=p):
            wait(p)

        @pl.when((j >= 1) & (j < n_tiles) & (j % 2 == p))
        def _(p=p):
            issue(j - 1, 1 - p)
            compute(p)

    @pl.when(j == 0)
    def _():
        compute(0)

    @pl.when(j == n_tiles)
    def _():
        last = (n_tiles - 1) % 2
        issue(n_tiles - 1, last)
        zero_buf[...] = jnp.zeros_like(zero_buf)

        def pad_expert(e, carry):
            def pad_issue(c, inner):
                pltpu.make_async_copy(zero_buf, xs_ref.at[pad_ref[e] + c], pad_sem).start()
                return inner
            lax.fori_loop(0, pad_ref[N_EXPERTS + e], pad_issue, 0)

            def pad_wait(c, inner):
                pltpu.make_async_copy(zero_buf, xs_ref.at[pad_ref[e] + c], pad_sem).wait()
                return inner
            lax.fori_loop(0, pad_ref[N_EXPERTS + e], pad_wait, 0)
            return carry
        lax.fori_loop(0, N_EXPERTS, pad_expert, 0)
        wait(last)


def _dispatch(h2, idx, wts, pos, pad_pos, n_slots):
    B, S, D = h2.shape
    tm = MOE_TILE
    nt = S // tm
    n_tiles = B * nt
    tile = lambda j: jnp.minimum(j, n_tiles - 1)
    grid_spec = pltpu.PrefetchScalarGridSpec(
        num_scalar_prefetch=2,
        grid=(n_tiles + 1,),
        in_specs=[pl.BlockSpec((1, tm, D), lambda j, pos, pad: (tile(j) // nt, tile(j) % nt, 0)),
                  pl.BlockSpec((1, TOP_K, tm), lambda j, pos, pad: (tile(j) // nt, 0, tile(j) % nt)),
                  pl.BlockSpec((1, TOP_K, tm), lambda j, pos, pad: (tile(j) // nt, 0, tile(j) % nt))],
        out_specs=[pl.BlockSpec(memory_space=pl.ANY),
                   pl.BlockSpec((TILE_ROWS, tm), lambda j, pos, pad: (tile(j), 0))],
        scratch_shapes=[pltpu.VMEM((TILE_CHUNKS, CHUNK, D), BF16),
                        pltpu.VMEM((TILE_CHUNKS, CHUNK, D), BF16),
                        pltpu.VMEM((CHUNK, D), BF16),
                        pltpu.SemaphoreType.DMA((2,)),
                        pltpu.SemaphoreType.DMA(())],
    )
    return pl.pallas_call(
        functools.partial(_dispatch_kernel, n_tiles=n_tiles),
        grid_spec=grid_spec,
        out_shape=[jax.ShapeDtypeStruct((n_slots + N_SPARE_CHUNKS, CHUNK, D), BF16),
                   jax.ShapeDtypeStruct((n_tiles * TILE_ROWS, tm), BF16)],
        compiler_params=_params(("arbitrary",)),
        name="moe_dispatch",
    )(pos, pad_pos, h2, idx, wts)


def _moe_tables(cnt, g_max):
    nt = cnt.shape[0]
    cc = (cnt + (CHUNK - 1)) // CHUNK
    segblk = jnp.cumsum(cc, axis=1) - cc
    tile_chunks = jnp.sum(cc, axis=1)
    prior = jnp.cumsum(cc, axis=0) - cc
    ge_cnt = (jnp.sum(cc, axis=0) + (GROUP_CHUNKS - 1)) // GROUP_CHUNKS
    gbase = jnp.cumsum(ge_cnt) - ge_cnt
    n_groups = jnp.sum(ge_cnt)
    c = jnp.arange(TILE_CHUNKS, dtype=jnp.int32)
    e_of = jnp.sum(((segblk + cc)[:, None, :] <= c[None, :, None]).astype(jnp.int32), axis=-1)
    e_of = jnp.minimum(e_of, N_EXPERTS - 1)
    seg_base = gbase[None, :] * GROUP_CHUNKS + prior - segblk
    onehot = e_of[:, :, None] == jnp.arange(N_EXPERTS, dtype=jnp.int32)[None, None, :]
    pos = jnp.sum(jnp.where(onehot, seg_base[:, None, :], 0), axis=-1) + c[None, :]
    valid = c[None, :] < tile_chunks[:, None]
    n_slots = g_max * GROUP_CHUNKS
    parity = (jnp.arange(nt, dtype=jnp.int32) % 2)[:, None]
    pos_write = jnp.where(valid, pos, n_slots + parity * TILE_CHUNKS + c[None, :]).astype(jnp.int32)
    ce = jnp.sum(cc, axis=0)
    pad_pos = jnp.concatenate([gbase * GROUP_CHUNKS + ce, ge_cnt * GROUP_CHUNKS - ce]).astype(jnp.int32)
    g = jnp.arange(g_max, dtype=jnp.int32)
    grp_e = jnp.minimum(jnp.sum(((gbase + ge_cnt)[None, :] <= g[:, None]).astype(jnp.int32), axis=1),
                        N_EXPERTS - 1).astype(jnp.int32)
    hi = lax.Precision.HIGHEST
    experts = jnp.arange(N_EXPERTS, dtype=jnp.int32)
    slot = jnp.arange(n_slots, dtype=jnp.int32)
    e_slot = jnp.minimum(jnp.sum((((gbase + ge_cnt) * GROUP_CHUNKS)[None, :] <= slot[:, None]).astype(jnp.int32),
                                 axis=1), N_EXPERTS - 1)
    oh_e = (e_slot[:, None] == experts[None, :]).astype(F32)
    q = slot - jnp.sum(oh_e * (gbase * GROUP_CHUNKS).astype(F32)[None, :], axis=1).astype(jnp.int32)
    slot_valid = q < jnp.sum(oh_e * ce.astype(F32)[None, :], axis=1).astype(jnp.int32)
    cum_end = jnp.dot(oh_e, (prior + cc).astype(F32).T, precision=hi)
    t_slot = jnp.minimum(jnp.sum((cum_end <= q[:, None].astype(F32)).astype(jnp.int32), axis=1), nt - 1)
    oh_t = (t_slot[:, None] == jnp.arange(nt, dtype=jnp.int32)[None, :]).astype(F32)
    shift = jnp.sum(jnp.dot(oh_t, (segblk - prior).astype(F32), precision=hi) * oh_e, axis=1).astype(jnp.int32)
    spare = nt * TILE_CHUNKS + ((slot // GROUP_CHUNKS) % 2) * GROUP_CHUNKS + slot % GROUP_CHUNKS
    inv = jnp.where(slot_valid, t_slot * TILE_CHUNKS + q + shift, spare).astype(jnp.int32)
    tiles = jnp.arange(nt, dtype=jnp.int32)
    fill = jnp.concatenate([tiles * TILE_CHUNKS + tile_chunks, TILE_CHUNKS - tile_chunks]).astype(jnp.int32)
    return (pos_write.reshape(-1), inv, pad_pos.reshape(-1), fill, grp_e,
            n_groups.reshape(1).astype(jnp.int32))


def _ffn_kernel(ge_ref, na_ref, inv_ref, fill_ref, x_ref, wg_ref, wu_ref, wd_ref, yt_ref,
                buf0, buf1, zero_buf, wg_s, wu_s, wd_s, sems, fill_sem, *, n_tiles):
    g = pl.program_id(0)
    na = na_ref[0]
    bufs = (buf0, buf1)

    g_live = jnp.minimum(g, jnp.maximum(na - 1, 0))
    @pl.when((g < na) & ((g == 0) | (ge_ref[g_live] != ge_ref[jnp.maximum(g_live - 1, 0)])))
    def _():
        wg_s[...] = wg_ref[0, 0].astype(BF16)
        wu_s[...] = wu_ref[0, 0].astype(BF16)
        wd_s[...] = wd_ref[0, 0].astype(BF16)

    def compute(p):
        x = x_ref[...].reshape(GROUP_ROWS, x_ref.shape[2])
        gate = _dot(x, wg_s[...])
        up = _dot(x, wu_s[...])
        act = (_silu(gate) * up).astype(BF16)
        bufs[p][...] = _dot(act, wd_s[...]).astype(BF16).reshape(bufs[p].shape)

    def issue(grp, p):
        for c in range(GROUP_CHUNKS):
            pltpu.make_async_copy(bufs[p].at[c], yt_ref.at[inv_ref[grp * GROUP_CHUNKS + c]], sems.at[p]).start()

    def wait(p):
        pltpu.make_async_copy(bufs[p], yt_ref.at[pl.ds(0, GROUP_CHUNKS)], sems.at[p]).wait()

    @pl.when(g == 0)
    def _():
        zero_buf[...] = jnp.zeros_like(zero_buf)

        def fill_tile(t, carry):
            def fill_issue(c, inner):
                pltpu.make_async_copy(zero_buf, yt_ref.at[fill_ref[t] + c], fill_sem).start()
                return inner
            lax.fori_loop(0, fill_ref[n_tiles + t], fill_issue, 0)

            def fill_wait(c, inner):
                pltpu.make_async_copy(zero_buf, yt_ref.at[fill_ref[t] + c], fill_sem).wait()
                return inner
            lax.fori_loop(0, fill_ref[n_tiles + t], fill_wait, 0)
            return carry
        lax.fori_loop(0, n_tiles, fill_tile, 0)

    for p in range(2):
        @pl.when((g >= 2) & (g - 2 < na) & (g % 2 == p))
        def _(p=p):
            wait(p)

        @pl.when((g >= 1) & (g < na) & (g % 2 == p))
        def _(p=p):
            issue(g - 1, 1 - p)
            compute(p)

        @pl.when((g >= 1) & (g == na) & (g % 2 == p))
        def _(p=p):
            issue(g - 1, 1 - p)

    @pl.when((g == 0) & (na > 0))
    def _():
        compute(0)


def _ffn_grouped(xs, inv, fill, grp_e, n_groups, layer, w_gate, w_up, w_down, g_max, n_tiles):
    D = xs.shape[2]
    live = lambda g, na: jnp.minimum(g, jnp.maximum(na[0] - 1, 0))
    wspec = lambda shape: pl.BlockSpec((1, 1) + shape, lambda g, ge, na, inv, fill: (layer, ge[live(g, na)], 0, 0))
    grid_spec = pltpu.PrefetchScalarGridSpec(
        num_scalar_prefetch=4,
        grid=(g_max + 2,),
        in_specs=[pl.BlockSpec((GROUP_CHUNKS, CHUNK, D), lambda g, ge, na, inv, fill: (live(g, na), 0, 0)),
                  wspec((D, EXPERT_FF)), wspec((D, EXPERT_FF)), wspec((EXPERT_FF, D))],
        out_specs=pl.BlockSpec(memory_space=pl.ANY),
        scratch_shapes=[pltpu.VMEM((GROUP_CHUNKS, CHUNK, D), BF16),
                        pltpu.VMEM((GROUP_CHUNKS, CHUNK, D), BF16),
                        pltpu.VMEM((CHUNK, D), BF16),
                        pltpu.VMEM((D, EXPERT_FF), BF16),
                        pltpu.VMEM((D, EXPERT_FF), BF16),
                        pltpu.VMEM((EXPERT_FF, D), BF16),
                        pltpu.SemaphoreType.DMA((2,)),
                        pltpu.SemaphoreType.DMA(())],
    )
    return pl.pallas_call(
        functools.partial(_ffn_kernel, n_tiles=n_tiles),
        grid_spec=grid_spec,
        out_shape=jax.ShapeDtypeStruct((n_tiles * TILE_CHUNKS + 2 * GROUP_CHUNKS, CHUNK, D), BF16),
        compiler_params=_params(("arbitrary",)),
        name="moe_ffn",
    )(grp_e, n_groups, inv, fill, xs, w_gate, w_up, w_down)


def _combine_kernel(yt_ref, pw_ref, base_ref, mod_ref, fg_ref, o_ref, *, final):
    yt = yt_ref[...].reshape(TILE_ROWS, yt_ref.shape[2])
    routed = _dot_tn(pw_ref[...], yt)
    out = base_ref[...] + mod_ref[0, 5:6, :] * routed
    if final:
        out = out * lax.rsqrt(jnp.mean(out * out, axis=-1, keepdims=True) + EPS) * fg_ref[...]
    o_ref[...] = out


def _combine(yt, pw, base, mod, mod_row, final_g):
    B, S, D = base.shape
    tm = MOE_TILE
    nt = S // tm
    final = final_g is not None
    fg = (final_g if final else jnp.ones((D,), F32)).reshape(1, D)
    out = pl.pallas_call(
        functools.partial(_combine_kernel, final=final),
        grid=(B * nt,),
        in_specs=[pl.BlockSpec((TILE_CHUNKS, CHUNK, D), lambda i: (i, 0, 0)),
                  pl.BlockSpec((TILE_ROWS, tm), lambda i: (i, 0)),
                  pl.BlockSpec((tm, D), lambda i: (i, 0)),
                  pl.BlockSpec((1, 6, D), lambda i: (mod_row(i // nt), 0, 0)),
                  pl.BlockSpec((1, D), lambda i: (0, 0))],
        out_specs=pl.BlockSpec((tm, D), lambda i: (i, 0)),
        out_shape=jax.ShapeDtypeStruct((B * S, D), F32),
        compiler_params=_params(("parallel",)),
        name="moe_combine",
    )(yt, pw, base.reshape(B * S, D), mod, fg)
    return out.reshape(B, S, D)


def _moe_sparse(h2, idx, wts, cnt, base, mod, mod_row, layer, w_gate, w_up, w_down, final_g=None):
    B, S, D = h2.shape
    n_tiles = B * (S // MOE_TILE)
    g_max = (n_tiles * TILE_CHUNKS + N_PAD_CHUNKS + GROUP_CHUNKS - 1) // GROUP_CHUNKS
    pos_write, inv, pad_pos, fill, grp_e, n_groups = _moe_tables(cnt[:, :, 0], g_max)
    xs, pw = _dispatch(h2, idx, wts, pos_write, pad_pos, g_max * GROUP_CHUNKS)
    yt = _ffn_grouped(xs, inv, fill, grp_e, n_groups, layer, w_gate, w_up, w_down, g_max, n_tiles)
    return _combine(yt, pw, base, mod, mod_row, final_g)


def _zero_state(batch):
    nq = M_HEADS * M_QK_DIM
    return (jnp.zeros((batch, 2, nq, M_V_DIM), F32),
            jnp.zeros((batch, 2, 1, nq), F32),
            jnp.zeros((batch, 2, 1, LANES), F32))


def kernel(x, c, ctx, c_ctx, ada_w, ada_b, norm1_g, norm2_g, w_in, attn_sink, mlstm_gate_b, mlstm_norm_g, conv_w, w_br_attn, w_br_mlstm, w_br_conv, branch_gate_b, w_out, router_w, router_bias, exp_w_gate, exp_w_up, exp_w_down, sh_w_gate, sh_w_up, sh_w_down, final_g):
    B, S, D = x.shape
    L = ctx.shape[1]
    depth = ada_w.shape[0]
    ctx_row = B

    pad_rows = (-(B + 1)) % 8
    cc = jnp.concatenate([c, c_ctx[None, :], jnp.zeros((pad_rows, D), F32)], axis=0)
    mod_all = _ada(cc, ada_w, ada_b).reshape(depth, B + 1 + pad_rows, 6, D)

    cos_t, sin_t = _rope_tables(S)
    cos_c = jnp.ones((L, LANES), F32)
    sin_c = jnp.zeros((L, LANES), F32)
    lat_row = lambda b: b
    ctx_mod = lambda b: ctx_row

    xc = ctx
    for l in range(depth):
        need_ctx = l < depth - 1
        mod = mod_all[l]
        w_p = _permute_w_in(w_in[l])
        lw = {
            'mlstm_norm_g': mlstm_norm_g[l].reshape(1, M_OUT),
            'conv_w': conv_w[l],
            'branch_gate_b': branch_gate_b[l].reshape(1, N_BRANCH * D),
            'w_br_attn': w_br_attn[l].reshape(ATT_KV_HEADS, ATT_GROUP, ATT_HEAD_DIM, D)
                         .transpose(1, 0, 2, 3).reshape(ATT_OUT, D).astype(BF16),
            'w_br_mlstm': w_br_mlstm[l].astype(BF16),
            'w_br_conv': w_br_conv[l].astype(BF16),
            'w_out': w_out[l].astype(BF16),
            'norm2_g': norm2_g[l].reshape(1, D),
            'router_wt': _split_hi_lo(router_w[l].T),
            'router_bias': router_bias[l].reshape(N_EXPERTS, 1),
            'sh_gu': jnp.concatenate([sh_w_gate[l], sh_w_up[l]], axis=1).astype(BF16),
            'sh_d': sh_w_down[l].astype(BF16),
        }
        experts = (l, exp_w_gate, exp_w_up, exp_w_down)

        pc = _in_proj(xc, mod, ctx_mod, norm1_g[l], w_p, cos_c, sin_c, tm=256)
        p = _in_proj(x, mod, lat_row, norm1_g[l], w_p, cos_t, sin_t, tm=min(512, S))
        qs_c, k_c, v_c, qm_c, km_c, vm_c, om_c, gm_c, bc_c, u_c, gt_c = pc
        qs, k, v, qm, km, vm, om, gm, bc, u, gt = p

        oat = _attention(qs, k, v, k_c, v_c, attn_sink[l], band=True)
        hf_c, hb_c, st = _mlstm(qm_c, km_c, vm_c, gm_c, mlstm_gate_b[l], _zero_state(B))
        hf, hb, _ = _mlstm(qm, km, vm, gm, mlstm_gate_b[l], st)

        base, h2, idx, wts, cnt = _merge(x, mod, lat_row, oat, hf, hb, om, u, bc, gt, lw, tm=min(2 * MOE_TILE, S))
        x_new = _moe_sparse(h2, idx, wts, cnt, base, mod, lat_row, *experts,
                            final_g=final_g if l == depth - 1 else None)

        if need_ctx:
            oat_c = _attention(qs_c, None, None, k_c, v_c, attn_sink[l], band=False)
            base_c, h2_c, idx_c, wts_c, cnt_c = _merge(xc, mod, ctx_mod, oat_c, hf_c, hb_c, om_c, u_c, bc_c, gt_c, lw,
                                                       tm=MOE_TILE)
            xc = _moe_sparse(h2_c, idx_c, wts_c, cnt_c, base_c, mod, ctx_mod, *experts)
        x = x_new
    return x
```

```python
import functools

import numpy as np
import jax
import jax.numpy as jnp
from jax import lax
from jax.experimental import pallas as pl
from jax.experimental.pallas import tpu as pltpu

F32 = jnp.float32
BF16 = jnp.bfloat16

D_MODEL = 1024
GRID_W = 64
EPS = 1e-6
ATT_HEADS = 8
ATT_KV_HEADS = 2
ATT_HEAD_DIM = 64
ATT_GROUP = ATT_HEADS // ATT_KV_HEADS
ATT_BLOCK = 128
ATT_OUT = ATT_HEADS * ATT_HEAD_DIM
ROPE_BASE = 10000.0
M_HEADS = 4
M_QK_DIM = 64
M_V_DIM = 128
M_CHUNK = 64
M_OUT = M_HEADS * M_V_DIM
CONV_WIDTH = 512
N_BRANCH = 3
N_EXPERTS = 64
N_GROUPS = 8
GROUP_SIZE = N_EXPERTS // N_GROUPS
TOPK_GROUPS = 4
TOP_K = 8
EXPERT_FF = 256
SHARED_FF = 256
ROUTED_SCALE = 2.5

LOG2E = 1.4426950408889634
LANES = 128
BF16_SUBLANES = 16
VMEM_LIMIT = 56 * 1024 * 1024

_SEGS = (('q', 512), ('k', 128), ('v', 128), ('qm', 256), ('km', 256), ('vm', 512), ('om', 512),
         ('bc', 512), ('cc', 512), ('xc', 512), ('gt', 3072), ('gm', 128))
_OFF = {}
_o = 0
for _n, _s in _SEGS:
    _OFF[_n] = (_o, _o + _s)
    _o += _s
N_PROJ = _o
D_IN = 6928


def _permute_w_in(w):
    d = w.shape[0]
    half = ATT_HEAD_DIM // 2
    q = w[:, 0:512].reshape(d, ATT_KV_HEADS, ATT_GROUP, half, 2).transpose(0, 2, 1, 4, 3).reshape(d, 512)
    k = w[:, 512:640].reshape(d, ATT_KV_HEADS, half, 2).transpose(0, 1, 3, 2).reshape(d, 128)
    gm = jnp.pad(w[:, 2304:2320], ((0, 0), (0, LANES - 16)))
    out = jnp.concatenate([q, k, w[:, 640:2304], w[:, 2320:D_IN], gm], axis=1)
    assert out.shape[1] == N_PROJ
    return out.astype(BF16)


def _rope_tables(seq):
    rows = seq // GRID_W
    row = jnp.repeat(jnp.arange(rows, dtype=F32), GRID_W)
    col = jnp.tile(jnp.arange(GRID_W, dtype=F32), rows)
    n_pairs = ATT_HEAD_DIM // 4
    inv_freq = ROPE_BASE ** (-jnp.arange(n_pairs, dtype=F32) / n_pairs)
    ang = jnp.concatenate([row[:, None] * inv_freq, col[:, None] * inv_freq], axis=-1)
    c, s = jnp.cos(ang), jnp.sin(ang)
    cos_t = jnp.concatenate([c, c, c, c], axis=-1)
    sin_t = jnp.concatenate([-s, s, -s, s], axis=-1)
    return cos_t, sin_t


def _dot(a, b):
    return jnp.dot(a, b, preferred_element_type=F32)


def _dot_nt(a, b):
    return lax.dot_general(a, b, (((1,), (1,)), ((), ())), preferred_element_type=F32)


def _dot_tn(a, b):
    return lax.dot_general(a, b, (((0,), (0,)), ((), ())), preferred_element_type=F32)


def _split_hi_lo(w):
    hi = w.astype(BF16)
    lo = (w - hi.astype(F32)).astype(BF16)
    return jnp.concatenate([hi, lo], axis=0)


def _sigmoid(x):
    return 1.0 / (1.0 + jnp.exp(-x))


def _sigmoid_tanh(x):
    return 0.5 * jnp.tanh(0.5 * x) + 0.5


def _silu(x):
    return x * _sigmoid(x)


def _log_sigmoid(x):
    return jnp.minimum(x, 0.0) - jnp.log(1.0 + jnp.exp(-jnp.abs(x)))


def _rms_mod(x, g, shift, scale):
    y = x * lax.rsqrt(jnp.mean(x * x, axis=-1, keepdims=True) + EPS) * g
    return y * (1.0 + scale) + shift


def _params(sem):
    return pltpu.CompilerParams(dimension_semantics=sem, vmem_limit_bytes=VMEM_LIMIT)


def _ada_kernel(c_ref, w_ref, b_ref, o_ref):
    s = _silu(c_ref[...])
    o_ref[0] = jnp.dot(s, w_ref[0], preferred_element_type=F32,
                       precision=lax.Precision.HIGHEST) + b_ref[0]


def _ada(cc, ada_w, ada_b):
    depth, d, n = ada_w.shape
    rows = cc.shape[0]
    tn = 1536
    return pl.pallas_call(
        _ada_kernel,
        grid=(depth, n // tn),
        in_specs=[pl.BlockSpec((rows, d), lambda l, j: (0, 0)),
                  pl.BlockSpec((1, d, tn), lambda l, j: (l, 0, j)),
                  pl.BlockSpec((1, 1, tn), lambda l, j: (l, 0, j))],
        out_specs=pl.BlockSpec((1, rows, tn), lambda l, j: (l, 0, j)),
        out_shape=jax.ShapeDtypeStruct((depth, rows, n), F32),
        compiler_params=_params(("parallel", "parallel")),
        name="ada_mod",
    )(cc, ada_w, ada_b.reshape(depth, 1, n))


def _swap_halves(x):
    lane = lax.broadcasted_iota(jnp.int32, x.shape, 1)
    first = (lane % ATT_HEAD_DIM) < (ATT_HEAD_DIM // 2)
    return jnp.where(first, pltpu.roll(x, LANES - 32, axis=1), pltpu.roll(x, 32, axis=1))


_PROJ_OUTPUTS = ('qs', 'k', 'v', 'qm', 'km', 'vm', 'om', 'gm', 'bc', 'u', 'gt')
_STATE_OUTPUTS = ('k', 'v', 'qm', 'km', 'vm', 'gm')


def _in_kernel(x_ref, mod_ref, g_ref, w_ref, cos_ref, sin_ref, *out_refs, names):
    out = dict(zip(names, out_refs))
    tm = x_ref.shape[1]
    h = _rms_mod(x_ref[0], g_ref[...], mod_ref[0, 0:1, :], mod_ref[0, 1:2, :]).astype(BF16)

    def proj(name):
        lo, hi = _OFF[name]
        return _dot(h, w_ref[:, lo:hi])

    cos_t = cos_ref[...]
    sin_t = sin_ref[...]

    def rope(t):
        return t * cos_t + _swap_halves(t) * sin_t

    if 'qs' in out:
        q = proj('q')
        scale = ATT_HEAD_DIM ** -0.5 * LOG2E
        for hh in range(ATT_GROUP):
            r = (rope(q[:, hh * LANES:(hh + 1) * LANES]) * scale).astype(BF16)
            for qb in range(tm // ATT_BLOCK):
                out['qs'][0, qb, hh * ATT_BLOCK:(hh + 1) * ATT_BLOCK, :] = r[qb * ATT_BLOCK:(qb + 1) * ATT_BLOCK, :]
    for name in names:
        if name == 'qs':
            continue
        if name == 'k':
            val = rope(proj('k'))
        elif name == 'u':
            val = proj('cc') * proj('xc')
        else:
            val = proj(name)
        out[name][0] = val.astype(out[name].dtype)


def _in_proj(x, mod, mod_row, norm_g, w_p, cos_t, sin_t, tm, names=_PROJ_OUTPUTS):
    B, S, D = x.shape
    nb = S // ATT_BLOCK
    tok = lambda n, dt: (jax.ShapeDtypeStruct((B, S, n), dt), pl.BlockSpec((1, tm, n), lambda b, i: (b, i, 0)))
    outs = {'qs': (jax.ShapeDtypeStruct((B, nb, ATT_GROUP * ATT_BLOCK, LANES), BF16),
                   pl.BlockSpec((1, tm // ATT_BLOCK, ATT_GROUP * ATT_BLOCK, LANES), lambda b, i: (b, i, 0, 0))),
            'k': tok(128, BF16), 'v': tok(128, BF16), 'qm': tok(256, BF16), 'km': tok(256, BF16),
            'vm': tok(512, BF16), 'om': tok(512, BF16), 'gm': tok(128, F32), 'bc': tok(512, BF16),
            'u': tok(512, BF16), 'gt': tok(3072, BF16)}
    out_shape = tuple(outs[n][0] for n in names)
    out_specs = tuple(outs[n][1] for n in names)
    res = pl.pallas_call(
        functools.partial(_in_kernel, names=names),
        grid=(B, S // tm),
        in_specs=[pl.BlockSpec((1, tm, D), lambda b, i: (b, i, 0)),
                  pl.BlockSpec((1, 6, D), lambda b, i: (mod_row(b), 0, 0)),
                  pl.BlockSpec((1, D), lambda b, i: (0, 0)),
                  pl.BlockSpec((D, N_PROJ), lambda b, i: (0, 0), pipeline_mode=pl.Buffered(1)),
                  pl.BlockSpec((tm, LANES), lambda b, i: (i, 0)),
                  pl.BlockSpec((tm, LANES), lambda b, i: (i, 0))],
        out_specs=out_specs,
        out_shape=out_shape,
        compiler_params=_params(("parallel", "parallel")),
        name="in_proj",
    )(x, mod, norm_g.reshape(1, D), w_p, cos_t, sin_t)
    return dict(zip(names, res))


def _attn_block(q, sink_ref, kcat, vcat, masks):
    rows = q.shape[0]
    lane = lax.broadcasted_iota(jnp.int32, (1, LANES), 1)
    hh = lax.broadcasted_iota(jnp.int32, (rows, 1), 0) // ATT_BLOCK
    out = jnp.zeros((rows, LANES), F32)
    for g in range(ATT_KV_HEADS):
        lm = (lane < ATT_HEAD_DIM) if g == 0 else (lane >= ATT_HEAD_DIM)
        kz = jnp.where(lm, kcat, jnp.zeros_like(kcat))
        ones_lane = ATT_HEAD_DIM if g == 0 else 0
        vz = jnp.where(lm, vcat, jnp.where(lane == ones_lane, 1.0, 0.0).astype(BF16))
        s = _dot_nt(q, kz)
        if any(mk is not None for mk in masks):
            s = jnp.concatenate(
                [s[:, n * ATT_BLOCK:(n + 1) * ATT_BLOCK] if mk is None
                 else jnp.where(mk, s[:, n * ATT_BLOCK:(n + 1) * ATT_BLOCK], -jnp.inf)
                 for n, mk in enumerate(masks)], axis=1)
        sink = jnp.zeros((rows, 1), F32)
        for a in range(ATT_GROUP):
            sink = jnp.where(hh == a, sink_ref[g * ATT_GROUP + a] * LOG2E, sink)
        m = jnp.maximum(jnp.max(s, axis=-1, keepdims=True), sink)
        p = jnp.exp2((s - m).astype(BF16))
        pv = _dot(p, vz)
        l = pv[:, ones_lane:ones_lane + 1] + jnp.exp2(sink - m)
        out = out + jnp.where(lm, pv, 0.0) / l
    return out.astype(BF16)


ATT_STEP_BLOCKS = 8


def _attn_kernel(sink_ref, qs_ref, kc_ref, vc_ref, *rest, band):
    nsb = qs_ref.shape[1]
    if not band:
        (o_ref,) = rest
        n_ctx = kc_ref.shape[1] // ATT_BLOCK
        for sb in range(nsb):
            o_ref[0, sb] = _attn_block(qs_ref[0, sb], sink_ref, kc_ref[0], vc_ref[0], [None] * n_ctx)
        return
    kp_ref, kcur_ref, kn_ref, vp_ref, vcur_ref, vn_ref, o_ref = rest
    j = pl.program_id(1)
    nstep = pl.num_programs(1)
    n_ctx = kc_ref.shape[1] // ATT_BLOCK
    rows = qs_ref.shape[2]
    t = lax.broadcasted_iota(jnp.int32, (rows, 1), 0) % ATT_BLOCK
    i = lax.broadcasted_iota(jnp.int32, (1, ATT_BLOCK), 1)
    below = i >= t
    above = i <= t
    first = i >= t + jnp.where(j > 0, 0, 2 * ATT_BLOCK)
    last = i <= t - jnp.where(j < nstep - 1, 0, 2 * ATT_BLOCK)
    kblk = [kp_ref[0]] + [kcur_ref[0, n * ATT_BLOCK:(n + 1) * ATT_BLOCK, :] for n in range(nsb)] + [kn_ref[0]]
    vblk = [vp_ref[0]] + [vcur_ref[0, n * ATT_BLOCK:(n + 1) * ATT_BLOCK, :] for n in range(nsb)] + [vn_ref[0]]
    for sb in range(nsb):
        kcat = jnp.concatenate([kc_ref[0]] + kblk[sb:sb + 3], axis=0)
        vcat = jnp.concatenate([vc_ref[0]] + vblk[sb:sb + 3], axis=0)
        masks = [first if sb == 0 else below, None, last if sb == nsb - 1 else above]
        o_ref[0, sb] = _attn_block(qs_ref[0, sb], sink_ref, kcat, vcat, [None] * n_ctx + masks)


def _attention(qs, k, v, kc, vc, sink, band):
    B, nb = qs.shape[:2]
    lc = kc.shape[1]
    blocks = min(ATT_STEP_BLOCKS, nb)
    nstep = nb // blocks
    last = nb - 1
    qspec = pl.BlockSpec((1, blocks, ATT_GROUP * ATT_BLOCK, LANES), lambda b, j: (b, j, 0, 0))
    cspec = pl.BlockSpec((1, lc, LANES), lambda b, j: (b, 0, 0))
    in_specs = [pl.BlockSpec(memory_space=pltpu.SMEM), qspec, cspec, cspec]
    args = [sink.astype(F32), qs, kc, vc]
    if band:
        prev = pl.BlockSpec((1, ATT_BLOCK, LANES), lambda b, j: (b, jnp.maximum(blocks * j - 1, 0), 0))
        cur = pl.BlockSpec((1, blocks * ATT_BLOCK, LANES), lambda b, j: (b, j, 0))
        nxt = pl.BlockSpec((1, ATT_BLOCK, LANES), lambda b, j: (b, jnp.minimum(blocks * (j + 1), last), 0))
        in_specs += [prev, cur, nxt, prev, cur, nxt]
        args += [k, k, k, v, v, v]
    return pl.pallas_call(
        functools.partial(_attn_kernel, band=band),
        grid=(B, nstep),
        in_specs=in_specs,
        out_specs=qspec,
        out_shape=jax.ShapeDtypeStruct(qs.shape, BF16),
        compiler_params=_params(("parallel", "parallel")),
        name="attention_band" if band else "attention_ctx",
    )(*args)


def _mlstm_step(dirs, T):
    kscale = M_QK_DIM ** -0.5
    si = lax.broadcasted_iota(jnp.int32, (T, T), 0)
    ri = lax.broadcasted_iota(jnp.int32, (T, T), 1)
    lane_qk = lax.broadcasted_iota(jnp.int32, (1, M_HEADS * M_QK_DIM), 1) // M_QK_DIM
    lane_m = lax.broadcasted_iota(jnp.int32, (1, LANES), 1)
    row_c = lax.broadcasted_iota(jnp.int32, (M_HEADS * M_QK_DIM, 1), 0) // M_QK_DIM
    combos = [(d, hd) for d in range(2) for hd in range(M_HEADS)]

    tri, bcol, gt, bt, blast = [], [], [], [], []
    for d, (q, k, v, g, C, n, m) in enumerate(dirs):
        t = (ri <= si) if d == 0 else (ri >= si)
        tri.append(t)
        lf = _log_sigmoid(g)
        lf1 = lf.astype(BF16)
        rem = lf - lf1.astype(F32)
        lf2 = rem.astype(BF16)
        lf3 = (rem - lf2.astype(F32)).astype(BF16)
        parts = _dot(jnp.where(t, 1.0, 0.0).astype(BF16), jnp.concatenate([lf1, lf2, lf3], axis=1))
        bc = parts[:, 0:LANES] + parts[:, LANES:2 * LANES] + parts[:, 2 * LANES:3 * LANES]
        bcol.append(bc)
        gt.append(g.T)
        bt.append(bc.T)
        blast.append(bc[T - 1:T, :] if d == 0 else bc[0:1, :])

    def lanes(d, hd):
        return (2 * d) * M_HEADS + hd, (2 * d + 1) * M_HEADS + hd

    b_col = {c: bcol[c[0]][:, lanes(*c)[1]:lanes(*c)[1] + 1] for c in combos}
    ig_col = {c: dirs[c[0]][3][:, lanes(*c)[0]:lanes(*c)[0] + 1] for c in combos}
    alpha = {c: gt[c[0]][lanes(*c)[0]:lanes(*c)[0] + 1, :] - bt[c[0]][lanes(*c)[1]:lanes(*c)[1] + 1, :]
             for c in combos}
    m_old = {c: dirs[c[0]][6][:, c[1]:c[1] + 1] for c in combos}
    b_last = {c: blast[c[0]][:, lanes(*c)[1]:lanes(*c)[1] + 1] for c in combos}
    hmask = {hd: lane_qk == hd for hd in range(M_HEADS)}

    a_mat = {c: jnp.where(tri[c[0]], alpha[c], -jnp.inf) for c in combos}
    a_max = {c: jnp.max(a_mat[c], axis=1, keepdims=True) for c in combos}
    a_int = {c: b_col[c] + m_old[c] for c in combos}
    m_s = {c: jnp.maximum(a_int[c], b_col[c] + a_max[c]) for c in combos}
    w_int = {c: jnp.exp(a_int[c] - m_s[c]) for c in combos}
    w_mat = {c: jnp.exp(a_mat[c] + (b_col[c] - m_s[c])) for c in combos}
    qmask = {c: jnp.where(hmask[c[1]], dirs[c[0]][0], jnp.zeros_like(dirs[c[0]][0])) for c in combos}
    s_qk = {c: w_mat[c] * (_dot_nt(qmask[c], dirs[c[0]][1]) * kscale) for c in combos}
    vh = {c: dirs[c[0]][2][:, c[1] * M_V_DIM:(c[1] + 1) * M_V_DIM] for c in combos}
    c_bf = [dirs[d][4].astype(BF16) for d in range(2)]
    num = {c: _dot(s_qk[c].astype(BF16), vh[c]) + w_int[c] * _dot(qmask[c], c_bf[c[0]]) for c in combos}
    qn_all = [dirs[d][0].astype(F32) * dirs[d][5] for d in range(2)]
    qn = {c: jnp.sum(jnp.where(hmask[c[1]], qn_all[c[0]], 0.0), axis=1, keepdims=True) for c in combos}
    den = {c: jnp.sum(s_qk[c], axis=1, keepdims=True) + w_int[c] * qn[c] for c in combos}
    h = {c: num[c] / jnp.maximum(jnp.abs(den[c]), jnp.exp(-m_s[c])) for c in combos}

    r_col = {c: b_last[c] - b_col[c] + ig_col[c] for c in combos}
    m_new = {c: jnp.maximum(b_last[c] + m_old[c], jnp.max(r_col[c], axis=0, keepdims=True)) for c in combos}
    decay = {c: jnp.exp(b_last[c] + m_old[c] - m_new[c]) for c in combos}
    w_r = {c: jnp.exp(r_col[c] - m_new[c]) for c in combos}

    outs = []
    for d, (q, k, v, g, C, n, m) in enumerate(dirs):
        w_lanes = jnp.zeros((T, M_HEADS * M_QK_DIM), F32)
        dec_lanes = jnp.zeros((1, M_HEADS * M_QK_DIM), F32)
        dec_rows = jnp.zeros((M_HEADS * M_QK_DIM, 1), F32)
        m_row = jnp.zeros((1, LANES), F32)
        for hd in range(M_HEADS):
            w_lanes = jnp.where(hmask[hd], w_r[(d, hd)], w_lanes)
            dec_lanes = jnp.where(hmask[hd], decay[(d, hd)], dec_lanes)
            dec_rows = jnp.where(row_c == hd, decay[(d, hd)], dec_rows)
            m_row = jnp.where(lane_m == hd, m_new[(d, hd)], m_row)
        kw = k.astype(F32) * (w_lanes * kscale)
        kwt = kw.T.astype(BF16)
        upd = jnp.concatenate(
            [_dot(kwt[hd * M_QK_DIM:(hd + 1) * M_QK_DIM, :], vh[(d, hd)]) for hd in range(M_HEADS)], axis=0)
        c_new = dec_rows * C + upd
        n_new = dec_lanes * n + jnp.sum(kw, axis=0, keepdims=True)
        h_all = jnp.concatenate([h[(d, hd)] for hd in range(M_HEADS)], axis=1)
        outs.append((h_all, c_new, n_new, m_row))
    return outs


def _mlstm_kernel(gb_ref, qf_ref, kf_ref, vf_ref, gf_ref, qb_ref, kb_ref, vb_ref, gbk_ref,
                  c0_ref, n0_ref, m0_ref, hf_ref, hb_ref, cf_ref, nf_ref, mf_ref,
                  c_s, n_s, m_s):
    ci = pl.program_id(1)
    T = MLSTM_TILE
    n_sub = qf_ref.shape[1] // T

    @pl.when(ci == 0)
    def _():
        c_s[...] = c0_ref[0]
        n_s[...] = n0_ref[0]
        m_s[...] = m0_ref[0]

    gb = gb_ref[...]
    state = [(c_s[d], n_s[d], m_s[d]) for d in range(2)]
    refs = ((qf_ref, kf_ref, vf_ref, gf_ref, hf_ref), (qb_ref, kb_ref, vb_ref, gbk_ref, hb_ref))
    for s in range(n_sub):
        lo = (s * T, (n_sub - 1 - s) * T)
        dirs = [(q_ref[0, lo[d]:lo[d] + T, :], k_ref[0, lo[d]:lo[d] + T, :], v_ref[0, lo[d]:lo[d] + T, :],
                 g_ref[0, lo[d]:lo[d] + T, :] + gb) + state[d]
                for d, (q_ref, k_ref, v_ref, g_ref, _) in enumerate(refs)]
        outs = _mlstm_step(dirs, T)
        for d in range(2):
            refs[d][4][0, lo[d]:lo[d] + T, :] = outs[d][0]
        state = [outs[d][1:] for d in range(2)]
    for d in range(2):
        c_s[d], n_s[d], m_s[d] = state[d]

    @pl.when(ci == pl.num_programs(1) - 1)
    def _():
        cf_ref[0] = c_s[...]
        nf_ref[0] = n_s[...]
        mf_ref[0] = m_s[...]


MLSTM_TILE = 128
MLSTM_STEP_CHUNKS = 4


def _mlstm(qm, km, vm, gm, gate_b, state):
    B, S, _ = qm.shape
    T = MLSTM_TILE * min(MLSTM_STEP_CHUNKS, S // MLSTM_TILE)
    nc = S // T
    nq = M_HEADS * M_QK_DIM
    fwd = lambda n: pl.BlockSpec((1, T, n), lambda b, c: (b, c, 0))
    bwd = lambda n: pl.BlockSpec((1, T, n), lambda b, c: (b, nc - 1 - c, 0))
    st_specs = [pl.BlockSpec((1, 2, nq, M_V_DIM), lambda b, c: (b, 0, 0, 0)),
                pl.BlockSpec((1, 2, 1, nq), lambda b, c: (b, 0, 0, 0)),
                pl.BlockSpec((1, 2, 1, LANES), lambda b, c: (b, 0, 0, 0))]
    st_shapes = [jax.ShapeDtypeStruct((B, 2, nq, M_V_DIM), F32),
                 jax.ShapeDtypeStruct((B, 2, 1, nq), F32),
                 jax.ShapeDtypeStruct((B, 2, 1, LANES), F32)]
    gb_row = jnp.pad(gate_b.reshape(1, -1).astype(F32), ((0, 0), (0, LANES - gate_b.size)))
    outs = pl.pallas_call(
        _mlstm_kernel,
        grid=(B, nc),
        in_specs=[pl.BlockSpec((1, LANES), lambda b, c: (0, 0)),
                  fwd(256), fwd(256), fwd(512), fwd(LANES),
                  bwd(256), bwd(256), bwd(512), bwd(LANES)] + st_specs,
        out_specs=[fwd(M_OUT), bwd(M_OUT)] + st_specs,
        out_shape=[jax.ShapeDtypeStruct((B, S, M_OUT), F32)] * 2 + st_shapes,
        scratch_shapes=[pltpu.VMEM((2, nq, M_V_DIM), F32),
                        pltpu.VMEM((2, 1, nq), F32),
                        pltpu.VMEM((2, 1, LANES), F32)],
        compiler_params=_params(("parallel", "arbitrary")),
        name="mlstm_scan",
    )(gb_row, qm, km, vm, gm, qm, km, vm, gm, *state)
    return outs[0], outs[1], tuple(outs[2:])


def _route(scores, sel):
    tm = scores.shape[1]
    gi8 = lax.broadcasted_iota(jnp.int32, (GROUP_SIZE, tm), 0)

    def stack_rows(rows):
        out = jnp.broadcast_to(rows[0], (len(rows), tm))
        for r, v in enumerate(rows[1:], start=1):
            out = jnp.where(gi8 == r, v, out)
        return out

    gs = []
    for g in range(N_GROUPS):
        blk = sel[g * GROUP_SIZE:(g + 1) * GROUP_SIZE, :]
        m1 = jnp.max(blk, axis=0, keepdims=True)
        first = jnp.min(jnp.where(blk == m1, gi8, GROUP_SIZE), axis=0, keepdims=True)
        m2 = jnp.max(jnp.where(gi8 == first, -jnp.inf, blk), axis=0, keepdims=True)
        gs.append(m1 + m2)
    gsc = stack_rows(gs)
    gsel = jnp.zeros((N_GROUPS, tm), F32)
    for _ in range(TOPK_GROUPS):
        mx = jnp.max(gsc, axis=0, keepdims=True)
        first = jnp.min(jnp.where(gsc == mx, gi8, N_GROUPS), axis=0, keepdims=True)
        pick = gi8 == first
        gsel = jnp.where(pick, 1.0, gsel)
        gsc = jnp.where(pick, -jnp.inf, gsc)
    cur = jnp.concatenate(
        [jnp.where(gsel[g:g + 1, :] > 0.0, sel[g * GROUP_SIZE:(g + 1) * GROUP_SIZE, :], -jnp.inf)
         for g in range(N_GROUPS)], axis=0)
    ei = lax.broadcasted_iota(jnp.int32, (N_EXPERTS, tm), 0)
    idx, wts = [], []
    for _ in range(TOP_K):
        mx = jnp.max(cur, axis=0, keepdims=True)
        first = jnp.min(jnp.where(cur == mx, ei, N_EXPERTS), axis=0, keepdims=True)
        pick = ei == first
        idx.append(first)
        wts.append(jnp.sum(jnp.where(pick, scores, 0.0), axis=0, keepdims=True))
        cur = jnp.where(pick, -jnp.inf, cur)
    tot = wts[0]
    for w in wts[1:]:
        tot = tot + w
    wts = [w / tot * ROUTED_SCALE for w in wts]
    return stack_rows(idx), stack_rows(wts)


def _merge_kernel(x_ref, mod_ref, oat_ref, hf_ref, hb_ref, om_ref, ng_ref, u_ref, up_ref, un_ref,
                  bc_ref, cw_ref, gt_ref, bgb_ref, wa_ref, wm_ref, wc_ref, wo_ref, n2_ref,
                  rw_ref, rb_ref, sgu_ref, sd_ref,
                  base_ref, h2_ref, idx_ref, wt_ref, cnt_ref):
    i = pl.program_id(1)
    tm = x_ref.shape[1]
    x = x_ref[0]
    g1 = mod_ref[0, 2:3, :]
    sh2 = mod_ref[0, 3:4, :]
    sc2 = mod_ref[0, 4:5, :]
    g2 = mod_ref[0, 5:6, :]

    ya = jnp.concatenate(
        [jnp.concatenate([oat_ref[0, qb, hh * ATT_BLOCK:(hh + 1) * ATT_BLOCK, :] for hh in range(ATT_GROUP)], axis=1)
         for qb in range(tm // ATT_BLOCK)], axis=0)

    hsum = hf_ref[0] + hb_ref[0]
    parts = []
    for hd in range(M_HEADS):
        hh_ = hsum[:, hd * M_V_DIM:(hd + 1) * M_V_DIM]
        parts.append(hh_ * lax.rsqrt(jnp.mean(hh_ * hh_, axis=-1, keepdims=True) + EPS))
    hn = jnp.concatenate(parts, axis=1) * ng_ref[...]
    ym = (_sigmoid_tanh(om_ref[0].astype(F32)) * hn).astype(BF16)

    u = u_ref[0].astype(F32)
    row = lax.broadcasted_iota(jnp.int32, (tm, 1), 0)
    has_prev = (i > 0).astype(F32)
    has_next = (i < pl.num_programs(1) - 1).astype(F32)
    prev_row = up_ref[0, BF16_SUBLANES - 1:BF16_SUBLANES, :].astype(F32) * has_prev
    next_row = un_ref[0, 0:1, :].astype(F32) * has_next
    u_m1 = jnp.where(row == 0, prev_row, pltpu.roll(u, 1, axis=0))
    u_p1 = jnp.where(row == tm - 1, next_row, pltpu.roll(u, tm - 1, axis=0))
    conv = cw_ref[0:1, :] * u_m1 + cw_ref[1:2, :] * u + cw_ref[2:3, :] * u_p1
    yc = (bc_ref[0].astype(F32) * conv).astype(BF16)

    gg = _sigmoid_tanh(gt_ref[0].astype(F32) + bgb_ref[...])
    ymix = (gg[:, 0:D_MODEL] * _dot(ya, wa_ref[...])
            + gg[:, D_MODEL:2 * D_MODEL] * _dot(ym, wm_ref[...])
            + gg[:, 2 * D_MODEL:3 * D_MODEL] * _dot(yc, wc_ref[...]))
    y = _dot(ymix.astype(BF16), wo_ref[...])
    xm = x + g1 * y

    h2f = _rms_mod(xm, n2_ref[...], sh2, sc2)
    h2 = h2f.astype(BF16)
    h2_ref[0] = h2

    h2_lo = (h2f - h2.astype(F32)).astype(BF16)
    rw = rw_ref[...]
    part = _dot_nt(rw, h2)
    logits_t = part[0:N_EXPERTS, :] + part[N_EXPERTS:2 * N_EXPERTS, :] + _dot_nt(rw[0:N_EXPERTS, :], h2_lo)
    scores = _sigmoid(logits_t)
    idx, wts = _route(scores, scores + rb_ref[...])
    idx_ref[0] = idx
    wt_ref[0] = wts
    ei = lax.broadcasted_iota(jnp.int32, (N_EXPERTS, tm), 0)
    pick = jnp.zeros((N_EXPERTS, tm), F32)
    for kk in range(TOP_K):
        pick = jnp.where(ei == idx[kk:kk + 1, :], 1.0, pick)
    for sub in range(tm // MOE_TILE):
        n_e = jnp.sum(pick[:, sub * MOE_TILE:(sub + 1) * MOE_TILE], axis=1, keepdims=True)
        cnt_ref[sub] = jnp.broadcast_to(n_e, (N_EXPERTS, LANES)).astype(jnp.int32)

    a = _dot(h2, sgu_ref[...])
    act = (_silu(a[:, 0:SHARED_FF]) * a[:, SHARED_FF:2 * SHARED_FF]).astype(BF16)
    base_ref[0] = xm + g2 * _dot(act, sd_ref[...])


def _merge(x, mod, mod_row, oat, hf, hb, om, u, bc, gt, lw, tm):
    B, S, D = x.shape
    nt = S // tm
    hal = BF16_SUBLANES
    last_h = S // hal - 1
    tspec = lambda n: pl.BlockSpec((1, tm, n), lambda b, i: (b, i, 0))
    full = lambda a: pl.BlockSpec(a.shape, lambda b, i: (0,) * a.ndim, pipeline_mode=pl.Buffered(1))
    weights = [lw['mlstm_norm_g'], lw['conv_w'], lw['branch_gate_b'], lw['w_br_attn'], lw['w_br_mlstm'],
               lw['w_br_conv'], lw['w_out'], lw['norm2_g'], lw['router_wt'], lw['router_bias'],
               lw['sh_gu'], lw['sh_d']]
    in_specs = [tspec(D),
                pl.BlockSpec((1, 6, D), lambda b, i: (mod_row(b), 0, 0)),
                pl.BlockSpec((1, tm // ATT_BLOCK, ATT_GROUP * ATT_BLOCK, LANES), lambda b, i: (b, i, 0, 0)),
                tspec(M_OUT), tspec(M_OUT), tspec(M_OUT), full(weights[0]),
                tspec(CONV_WIDTH),
                pl.BlockSpec((1, hal, CONV_WIDTH), lambda b, i: (b, jnp.maximum(i * (tm // hal) - 1, 0), 0)),
                pl.BlockSpec((1, hal, CONV_WIDTH), lambda b, i: (b, jnp.minimum((i + 1) * (tm // hal), last_h), 0)),
                tspec(CONV_WIDTH), full(weights[1]), tspec(N_BRANCH * D), full(weights[2])]
    in_specs += [full(w) for w in weights[3:]]
    tr = lambda n, dt: (jax.ShapeDtypeStruct((B, n, S), dt), pl.BlockSpec((1, n, tm), lambda b, i: (b, 0, i)))
    outs = [(jax.ShapeDtypeStruct((B, S, D), F32), tspec(D)),
            (jax.ShapeDtypeStruct((B, S, D), BF16), tspec(D)),
            tr(TOP_K, jnp.int32), tr(TOP_K, F32),
            (jax.ShapeDtypeStruct((B * S // MOE_TILE, N_EXPERTS, LANES), jnp.int32),
             pl.BlockSpec((tm // MOE_TILE, N_EXPERTS, LANES), lambda b, i: (b * nt + i, 0, 0)))]
    return pl.pallas_call(
        _merge_kernel,
        grid=(B, nt),
        in_specs=in_specs,
        out_specs=[o[1] for o in outs],
        out_shape=[o[0] for o in outs],
        compiler_params=_params(("parallel", "parallel")),
        name="merge_route",
    )(x, mod, oat, hf, hb, om, weights[0], u, u, u, bc, weights[1], gt, weights[2], *weights[3:])


MOE_TILE = 256
CHUNK = BF16_SUBLANES
GROUP_CHUNKS = 64
GROUP_ROWS = GROUP_CHUNKS * CHUNK
TILE_ROWS = MOE_TILE * TOP_K + N_EXPERTS * CHUNK
TILE_CHUNKS = TILE_ROWS // CHUNK
N_PAD_CHUNKS = N_EXPERTS * (GROUP_CHUNKS - 1)
N_SPARE_CHUNKS = 2 * TILE_CHUNKS


def _dispatch_tile(h, idx, wts):
    tm = h.shape[0]
    ei = lax.broadcasted_iota(jnp.int32, (N_EXPERTS, tm), 0)
    pick = jnp.zeros((N_EXPERTS, tm), F32)
    wmat = jnp.zeros((N_EXPERTS, tm), F32)
    for kk in range(TOP_K):
        chosen = ei == idx[kk:kk + 1, :]
        pick = jnp.where(chosen, 1.0, pick)
        wmat = jnp.where(chosen, wts[kk:kk + 1, :], wmat)
    t0 = lax.broadcasted_iota(jnp.int32, (tm, tm), 0)
    t1 = lax.broadcasted_iota(jnp.int32, (tm, tm), 1)
    rank = _dot(pick.astype(BF16), jnp.where(t0 < t1, 1.0, 0.0).astype(BF16))
    n_e = jnp.sum(pick, axis=1, keepdims=True)
    n_pad = jnp.floor((n_e + (CHUNK - 1)) * (1.0 / CHUNK)) * CHUNK
    e0 = lax.broadcasted_iota(jnp.int32, (N_EXPERTS, N_EXPERTS), 0)
    e1 = lax.broadcasted_iota(jnp.int32, (N_EXPERTS, N_EXPERTS), 1)
    seg = _dot(jnp.where(e1 < e0, 1.0, 0.0).astype(BF16),
               jnp.broadcast_to(n_pad, (N_EXPERTS, tm)).astype(BF16))
    posmat = seg + rank
    chunk_of = jnp.floor(posmat * (1.0 / CHUNK))
    offs_of = posmat - chunk_of * CHUNK
    chunk_row = jnp.where(pick > 0.0, chunk_of * CHUNK, -float(CHUNK))
    eye = e0 == e1
    to_row = lambda col: jnp.sum(jnp.where(eye, col, 0.0), axis=0, keepdims=True)
    seg_row = to_row(seg[:, 0:1])
    end_row = to_row(seg[:, 0:1] + n_pad)
    seg_row2 = jnp.concatenate([seg_row, seg_row], axis=1)
    end_row2 = jnp.concatenate([end_row, end_row], axis=1)
    r128 = lax.broadcasted_iota(jnp.int32, (TILE_ROWS, 2 * N_EXPERTS), 0).astype(F32)
    own2 = jnp.where(r128 >= seg_row2, jnp.where(r128 < end_row2, 1.0, 0.0), 0.0).astype(BF16)
    row_of = _dot(own2, jnp.concatenate([chunk_row, offs_of], axis=0).astype(BF16))
    w_of = _dot(own2[:, 0:N_EXPERTS], wmat.astype(BF16))
    riota = lax.broadcasted_iota(jnp.int32, (TILE_ROWS, tm), 0).astype(F32)
    hit = row_of == riota
    xg = _dot(jnp.where(hit, 1.0, 0.0).astype(BF16), h).astype(BF16)
    return xg, jnp.where(hit, w_of, 0.0).astype(BF16)


def _dispatch_kernel(pos_ref, pad_ref, h_ref, idx_ref, wt_ref, xs_ref, pw_ref, buf0, buf1, zero_buf, sems, pad_sem,
                     *, n_tiles):
    j = pl.program_id(0)
    bufs = (buf0, buf1)

    def compute(p):
        xg, pw = _dispatch_tile(h_ref[0], idx_ref[0], wt_ref[0])
        bufs[p][...] = xg.reshape(bufs[p].shape)
        pw_ref[...] = pw

    def issue(tile, p):
        for c in range(TILE_CHUNKS):
            pltpu.make_async_copy(bufs[p].at[c], xs_ref.at[pos_ref[tile * TILE_CHUNKS + c]], sems.at[p]).start()

    def wait(p):
        pltpu.make_async_copy(bufs[p], xs_ref.at[pl.ds(0, TILE_CHUNKS)], sems.at[p]).wait()

    for p in range(2):
        @pl.when((j >= 2) & (j % 2 == p))
        def _(p=p):
            wait(p)

        @pl.when((j >= 1) & (j < n_tiles) & (j % 2 == p))
        def _(p=p):
            issue(j - 1, 1 - p)
            compute(p)

    @pl.when(j == 0)
    def _():
        compute(0)

    @pl.when(j == n_tiles)
    def _():
        last = (n_tiles - 1) % 2
        issue(n_tiles - 1, last)
        zero_buf[...] = jnp.zeros_like(zero_buf)

        def pad_expert(e, carry):
            def pad_issue(c, inner):
                pltpu.make_async_copy(zero_buf, xs_ref.at[pad_ref[e] + c], pad_sem).start()
                return inner
            lax.fori_loop(0, pad_ref[N_EXPERTS + e], pad_issue, 0)

            def pad_wait(c, inner):
                pltpu.make_async_copy(zero_buf, xs_ref.at[pad_ref[e] + c], pad_sem).wait()
                return inner
            lax.fori_loop(0, pad_ref[N_EXPERTS + e], pad_wait, 0)
            return carry
        lax.fori_loop(0, N_EXPERTS, pad_expert, 0)
        wait(last)


def _dispatch(h2, idx, wts, pos, pad_pos, n_slots):
    B, S, D = h2.shape
    tm = MOE_TILE
    nt = S // tm
    n_tiles = B * nt
    tile = lambda j: jnp.minimum(j, n_tiles - 1)
    grid_spec = pltpu.PrefetchScalarGridSpec(
        num_scalar_prefetch=2,
        grid=(n_tiles + 1,),
        in_specs=[pl.BlockSpec((1, tm, D), lambda j, pos, pad: (tile(j) // nt, tile(j) % nt, 0)),
                  pl.BlockSpec((1, TOP_K, tm), lambda j, pos, pad: (tile(j) // nt, 0, tile(j) % nt)),
                  pl.BlockSpec((1, TOP_K, tm), lambda j, pos, pad: (tile(j) // nt, 0, tile(j) % nt))],
        out_specs=[pl.BlockSpec(memory_space=pl.ANY),
                   pl.BlockSpec((TILE_ROWS, tm), lambda j, pos, pad: (tile(j), 0))],
        scratch_shapes=[pltpu.VMEM((TILE_CHUNKS, CHUNK, D), BF16),
                        pltpu.VMEM((TILE_CHUNKS, CHUNK, D), BF16),
                        pltpu.VMEM((CHUNK, D), BF16),
                        pltpu.SemaphoreType.DMA((2,)),
                        pltpu.SemaphoreType.DMA(())],
    )
    return pl.pallas_call(
        functools.partial(_dispatch_kernel, n_tiles=n_tiles),
        grid_spec=grid_spec,
        out_shape=[jax.ShapeDtypeStruct((n_slots + N_SPARE_CHUNKS, CHUNK, D), BF16),
                   jax.ShapeDtypeStruct((n_tiles * TILE_ROWS, tm), BF16)],
        compiler_params=_params(("arbitrary",)),
        name="moe_dispatch",
    )(pos, pad_pos, h2, idx, wts)


def _moe_tables(cnt, g_max):
    nt = cnt.shape[0]
    cc = (cnt + (CHUNK - 1)) // CHUNK
    segblk = jnp.cumsum(cc, axis=1) - cc
    tile_chunks = jnp.sum(cc, axis=1)
    prior = jnp.cumsum(cc, axis=0) - cc
    ge_cnt = (jnp.sum(cc, axis=0) + (GROUP_CHUNKS - 1)) // GROUP_CHUNKS
    gbase = jnp.cumsum(ge_cnt) - ge_cnt
    n_groups = jnp.sum(ge_cnt)
    c = jnp.arange(TILE_CHUNKS, dtype=jnp.int32)
    e_of = jnp.sum(((segblk + cc)[:, None, :] <= c[None, :, None]).astype(jnp.int32), axis=-1)
    e_of = jnp.minimum(e_of, N_EXPERTS - 1)
    seg_base = gbase[None, :] * GROUP_CHUNKS + prior - segblk
    onehot = e_of[:, :, None] == jnp.arange(N_EXPERTS, dtype=jnp.int32)[None, None, :]
    pos = jnp.sum(jnp.where(onehot, seg_base[:, None, :], 0), axis=-1) + c[None, :]
    valid = c[None, :] < tile_chunks[:, None]
    n_slots = g_max * GROUP_CHUNKS
    parity = (jnp.arange(nt, dtype=jnp.int32) % 2)[:, None]
    pos_write = jnp.where(valid, pos, n_slots + parity * TILE_CHUNKS + c[None, :]).astype(jnp.int32)
    ce = jnp.sum(cc, axis=0)
    pad_pos = jnp.concatenate([gbase * GROUP_CHUNKS + ce, ge_cnt * GROUP_CHUNKS - ce]).astype(jnp.int32)
    g = jnp.arange(g_max, dtype=jnp.int32)
    grp_e = jnp.minimum(jnp.sum(((gbase + ge_cnt)[None, :] <= g[:, None]).astype(jnp.int32), axis=1),
                        N_EXPERTS - 1).astype(jnp.int32)
    hi = lax.Precision.HIGHEST
    experts = jnp.arange(N_EXPERTS, dtype=jnp.int32)
    slot = jnp.arange(n_slots, dtype=jnp.int32)
    e_slot = jnp.minimum(jnp.sum((((gbase + ge_cnt) * GROUP_CHUNKS)[None, :] <= slot[:, None]).astype(jnp.int32),
                                 axis=1), N_EXPERTS - 1)
    oh_e = (e_slot[:, None] == experts[None, :]).astype(F32)
    q = slot - jnp.sum(oh_e * (gbase * GROUP_CHUNKS).astype(F32)[None, :], axis=1).astype(jnp.int32)
    slot_valid = q < jnp.sum(oh_e * ce.astype(F32)[None, :], axis=1).astype(jnp.int32)
    cum_end = jnp.dot(oh_e, (prior + cc).astype(F32).T, precision=hi)
    t_slot = jnp.minimum(jnp.sum((cum_end <= q[:, None].astype(F32)).astype(jnp.int32), axis=1), nt - 1)
    oh_t = (t_slot[:, None] == jnp.arange(nt, dtype=jnp.int32)[None, :]).astype(F32)
    shift = jnp.sum(jnp.dot(oh_t, (segblk - prior).astype(F32), precision=hi) * oh_e, axis=1).astype(jnp.int32)
    spare = nt * TILE_CHUNKS + ((slot // GROUP_CHUNKS) % 2) * GROUP_CHUNKS + slot % GROUP_CHUNKS
    inv = jnp.where(slot_valid, t_slot * TILE_CHUNKS + q + shift, spare).astype(jnp.int32)
    tiles = jnp.arange(nt, dtype=jnp.int32)
    fill = jnp.concatenate([tiles * TILE_CHUNKS + tile_chunks, TILE_CHUNKS - tile_chunks]).astype(jnp.int32)
    return (pos_write.reshape(-1), inv, pad_pos.reshape(-1), fill, grp_e,
            n_groups.reshape(1).astype(jnp.int32))


def _ffn_kernel(ge_ref, na_ref, inv_ref, fill_ref, x_ref, wg_ref, wu_ref, wd_ref, yt_ref,
                buf0, buf1, zero_buf, wg_s, wu_s, wd_s, sems, fill_sem, *, n_tiles):
    g = pl.program_id(0)
    na = na_ref[0]
    bufs = (buf0, buf1)

    g_live = jnp.minimum(g, jnp.maximum(na - 1, 0))
    @pl.when((g < na) & ((g == 0) | (ge_ref[g_live] != ge_ref[jnp.maximum(g_live - 1, 0)])))
    def _():
        wg_s[...] = wg_ref[0, 0].astype(BF16)
        wu_s[...] = wu_ref[0, 0].astype(BF16)
        wd_s[...] = wd_ref[0, 0].astype(BF16)

    def compute(p):
        x = x_ref[...].reshape(GROUP_ROWS, x_ref.shape[2])
        gate = _dot(x, wg_s[...])
        up = _dot(x, wu_s[...])
        act = (_silu(gate) * up).astype(BF16)
        bufs[p][...] = _dot(act, wd_s[...]).astype(BF16).reshape(bufs[p].shape)

    def issue(grp, p):
        for c in range(GROUP_CHUNKS):
            pltpu.make_async_copy(bufs[p].at[c], yt_ref.at[inv_ref[grp * GROUP_CHUNKS + c]], sems.at[p]).start()

    def wait(p):
        pltpu.make_async_copy(bufs[p], yt_ref.at[pl.ds(0, GROUP_CHUNKS)], sems.at[p]).wait()

    @pl.when(g == 0)
    def _():
        zero_buf[...] = jnp.zeros_like(zero_buf)

        def fill_tile(t, carry):
            def fill_issue(c, inner):
                pltpu.make_async_copy(zero_buf, yt_ref.at[fill_ref[t] + c], fill_sem).start()
                return inner
            lax.fori_loop(0, fill_ref[n_tiles + t], fill_issue, 0)

            def fill_wait(c, inner):
                pltpu.make_async_copy(zero_buf, yt_ref.at[fill_ref[t] + c], fill_sem).wait()
                return inner
            lax.fori_loop(0, fill_ref[n_tiles + t], fill_wait, 0)
            return carry
        lax.fori_loop(0, n_tiles, fill_tile, 0)

    for p in range(2):
        @pl.when((g >= 2) & (g - 2 < na) & (g % 2 == p))
        def _(p=p):
            wait(p)

        @pl.when((g >= 1) & (g < na) & (g % 2 == p))
        def _(p=p):
            issue(g - 1, 1 - p)
            compute(p)

        @pl.when((g >= 1) & (g == na) & (g % 2 == p))
        def _(p=p):
            issue(g - 1, 1 - p)

    @pl.when((g == 0) & (na > 0))
    def _():
        compute(0)


def _ffn_grouped(xs, inv, fill, grp_e, n_groups, layer, w_gate, w_up, w_down, g_max, n_tiles):
    D = xs.shape[2]
    live = lambda g, na: jnp.minimum(g, jnp.maximum(na[0] - 1, 0))
    wspec = lambda shape: pl.BlockSpec((1, 1) + shape, lambda g, ge, na, inv, fill: (layer, ge[live(g, na)], 0, 0))
    grid_spec = pltpu.PrefetchScalarGridSpec(
        num_scalar_prefetch=4,
        grid=(g_max + 2,),
        in_specs=[pl.BlockSpec((GROUP_CHUNKS, CHUNK, D), lambda g, ge, na, inv, fill: (live(g, na), 0, 0)),
                  wspec((D, EXPERT_FF)), wspec((D, EXPERT_FF)), wspec((EXPERT_FF, D))],
        out_specs=pl.BlockSpec(memory_space=pl.ANY),
        scratch_shapes=[pltpu.VMEM((GROUP_CHUNKS, CHUNK, D), BF16),
                        pltpu.VMEM((GROUP_CHUNKS, CHUNK, D), BF16),
                        pltpu.VMEM((CHUNK, D), BF16),
                        pltpu.VMEM((D, EXPERT_FF), BF16),
                        pltpu.VMEM((D, EXPERT_FF), BF16),
                        pltpu.VMEM((EXPERT_FF, D), BF16),
                        pltpu.SemaphoreType.DMA((2,)),
                        pltpu.SemaphoreType.DMA(())],
    )
    return pl.pallas_call(
        functools.partial(_ffn_kernel, n_tiles=n_tiles),
        grid_spec=grid_spec,
        out_shape=jax.ShapeDtypeStruct((n_tiles * TILE_CHUNKS + 2 * GROUP_CHUNKS, CHUNK, D), BF16),
        compiler_params=_params(("arbitrary",)),
        name="moe_ffn",
    )(grp_e, n_groups, inv, fill, xs, w_gate, w_up, w_down)


def _combine_kernel(yt_ref, pw_ref, base_ref, mod_ref, fg_ref, o_ref, *, final):
    yt = yt_ref[...].reshape(TILE_ROWS, yt_ref.shape[2])
    routed = _dot_tn(pw_ref[...], yt)
    out = base_ref[...] + mod_ref[0, 5:6, :] * routed
    if final:
        out = out * lax.rsqrt(jnp.mean(out * out, axis=-1, keepdims=True) + EPS) * fg_ref[...]
    o_ref[...] = out


def _combine(yt, pw, base, mod, mod_row, final_g):
    B, S, D = base.shape
    tm = MOE_TILE
    nt = S // tm
    final = final_g is not None
    fg = (final_g if final else jnp.ones((D,), F32)).reshape(1, D)
    out = pl.pallas_call(
        functools.partial(_combine_kernel, final=final),
        grid=(B * nt,),
        in_specs=[pl.BlockSpec((TILE_CHUNKS, CHUNK, D), lambda i: (i, 0, 0)),
                  pl.BlockSpec((TILE_ROWS, tm), lambda i: (i, 0)),
                  pl.BlockSpec((tm, D), lambda i: (i, 0)),
                  pl.BlockSpec((1, 6, D), lambda i: (mod_row(i // nt), 0, 0)),
                  pl.BlockSpec((1, D), lambda i: (0, 0))],
        out_specs=pl.BlockSpec((tm, D), lambda i: (i, 0)),
        out_shape=jax.ShapeDtypeStruct((B * S, D), F32),
        compiler_params=_params(("parallel",)),
        name="moe_combine",
    )(yt, pw, base.reshape(B * S, D), mod, fg)
    return out.reshape(B, S, D)


def _moe_sparse(h2, idx, wts, cnt, base, mod, mod_row, layer, w_gate, w_up, w_down, final_g=None):
    B, S, D = h2.shape
    n_tiles = B * (S // MOE_TILE)
    g_max = (n_tiles * TILE_CHUNKS + N_PAD_CHUNKS + GROUP_CHUNKS - 1) // GROUP_CHUNKS
    pos_write, inv, pad_pos, fill, grp_e, n_groups = _moe_tables(cnt[:, :, 0], g_max)
    xs, pw = _dispatch(h2, idx, wts, pos_write, pad_pos, g_max * GROUP_CHUNKS)
    yt = _ffn_grouped(xs, inv, fill, grp_e, n_groups, layer, w_gate, w_up, w_down, g_max, n_tiles)
    return _combine(yt, pw, base, mod, mod_row, final_g)


def _zero_state(batch):
    nq = M_HEADS * M_QK_DIM
    return (jnp.zeros((batch, 2, nq, M_V_DIM), F32),
            jnp.zeros((batch, 2, 1, nq), F32),
            jnp.zeros((batch, 2, 1, LANES), F32))


def kernel(x, c, ctx, c_ctx, ada_w, ada_b, norm1_g, norm2_g, w_in, attn_sink, mlstm_gate_b, mlstm_norm_g, conv_w, w_br_attn, w_br_mlstm, w_br_conv, branch_gate_b, w_out, router_w, router_bias, exp_w_gate, exp_w_up, exp_w_down, sh_w_gate, sh_w_up, sh_w_down, final_g):
    B, S, D = x.shape
    L = ctx.shape[1]
    depth = ada_w.shape[0]
    ctx_row = B

    pad_rows = (-(B + 1)) % 8
    cc = jnp.concatenate([c, c_ctx[None, :], jnp.zeros((pad_rows, D), F32)], axis=0)
    mod_all = _ada(cc, ada_w, ada_b).reshape(depth, B + 1 + pad_rows, 6, D)

    cos_t, sin_t = _rope_tables(S)
    cos_c = jnp.ones((L, LANES), F32)
    sin_c = jnp.zeros((L, LANES), F32)
    lat_row = lambda b: b
    ctx_mod = lambda b: ctx_row

    xc = ctx
    for l in range(depth):
        need_ctx = l < depth - 1
        mod = mod_all[l]
        w_p = _permute_w_in(w_in[l])
        lw = {
            'mlstm_norm_g': mlstm_norm_g[l].reshape(1, M_OUT),
            'conv_w': conv_w[l],
            'branch_gate_b': branch_gate_b[l].reshape(1, N_BRANCH * D),
            'w_br_attn': w_br_attn[l].reshape(ATT_KV_HEADS, ATT_GROUP, ATT_HEAD_DIM, D)
                         .transpose(1, 0, 2, 3).reshape(ATT_OUT, D).astype(BF16),
            'w_br_mlstm': w_br_mlstm[l].astype(BF16),
            'w_br_conv': w_br_conv[l].astype(BF16),
            'w_out': w_out[l].astype(BF16),
            'norm2_g': norm2_g[l].reshape(1, D),
            'router_wt': _split_hi_lo(router_w[l].T),
            'router_bias': router_bias[l].reshape(N_EXPERTS, 1),
            'sh_gu': jnp.concatenate([sh_w_gate[l], sh_w_up[l]], axis=1).astype(BF16),
            'sh_d': sh_w_down[l].astype(BF16),
        }
        experts = (l, exp_w_gate, exp_w_up, exp_w_down)

        pc = _in_proj(xc, mod, ctx_mod, norm1_g[l], w_p, cos_c, sin_c, tm=256,
                      names=_PROJ_OUTPUTS if need_ctx else _STATE_OUTPUTS)
        p = _in_proj(x, mod, lat_row, norm1_g[l], w_p, cos_t, sin_t, tm=min(512, S))

        oat = _attention(p['qs'], p['k'], p['v'], pc['k'], pc['v'], attn_sink[l], band=True)
        hf_c, hb_c, st = _mlstm(pc['qm'], pc['km'], pc['vm'], pc['gm'], mlstm_gate_b[l], _zero_state(B))
        hf, hb, _ = _mlstm(p['qm'], p['km'], p['vm'], p['gm'], mlstm_gate_b[l], st)

        base, h2, idx, wts, cnt = _merge(x, mod, lat_row, oat, hf, hb, p['om'], p['u'], p['bc'], p['gt'], lw,
                                         tm=min(2 * MOE_TILE, S))
        x_new = _moe_sparse(h2, idx, wts, cnt, base, mod, lat_row, *experts,
                            final_g=final_g if l == depth - 1 else None)

        if need_ctx:
            oat_c = _attention(pc['qs'], None, None, pc['k'], pc['v'], attn_sink[l], band=False)
            base_c, h2_c, idx_c, wts_c, cnt_c = _merge(xc, mod, ctx_mod, oat_c, hf_c, hb_c, pc['om'], pc['u'],
                                                       pc['bc'], pc['gt'], lw, tm=MOE_TILE)
            xc = _moe_sparse(h2_c, idx_c, wts_c, cnt_c, base_c, mod, ctx_mod, *experts)
        x = x_new
    return x
```

```python
import functools

import numpy as np
import jax
import jax.numpy as jnp
from jax import lax
from jax.experimental import pallas as pl
from jax.experimental.pallas import tpu as pltpu

F32 = jnp.float32
BF16 = jnp.bfloat16

D_MODEL = 1024
GRID_W = 64
EPS = 1e-6
ATT_HEADS = 8
ATT_KV_HEADS = 2
ATT_HEAD_DIM = 64
ATT_GROUP = ATT_HEADS // ATT_KV_HEADS
ATT_BLOCK = 128
ATT_OUT = ATT_HEADS * ATT_HEAD_DIM
ROPE_BASE = 10000.0
M_HEADS = 4
M_QK_DIM = 64
M_V_DIM = 128
M_CHUNK = 64
M_OUT = M_HEADS * M_V_DIM
CONV_WIDTH = 512
N_BRANCH = 3
N_EXPERTS = 64
N_GROUPS = 8
GROUP_SIZE = N_EXPERTS // N_GROUPS
TOPK_GROUPS = 4
TOP_K = 8
EXPERT_FF = 256
SHARED_FF = 256
ROUTED_SCALE = 2.5

LOG2E = 1.4426950408889634
LANES = 128
BF16_SUBLANES = 16
VMEM_LIMIT = 56 * 1024 * 1024

_SEGS = (('q', 512), ('k', 128), ('v', 128), ('qm', 256), ('km', 256), ('vm', 512), ('om', 512),
         ('bc', 512), ('cc', 512), ('xc', 512), ('gt', 3072), ('gm', 128))
_OFF = {}
_o = 0
for _n, _s in _SEGS:
    _OFF[_n] = (_o, _o + _s)
    _o += _s
N_PROJ = _o
D_IN = 6928


def _permute_w_in(w):
    d = w.shape[0]
    half = ATT_HEAD_DIM // 2
    q = w[:, 0:512].reshape(d, ATT_KV_HEADS, ATT_GROUP, half, 2).transpose(0, 2, 1, 4, 3).reshape(d, 512)
    k = w[:, 512:640].reshape(d, ATT_KV_HEADS, half, 2).transpose(0, 1, 3, 2).reshape(d, 128)
    gm = jnp.pad(w[:, 2304:2320], ((0, 0), (0, LANES - 16)))
    out = jnp.concatenate([q, k, w[:, 640:2304], w[:, 2320:D_IN], gm], axis=1)
    assert out.shape[1] == N_PROJ
    return out.astype(BF16)


def _rope_tables(seq):
    rows = seq // GRID_W
    row = jnp.repeat(jnp.arange(rows, dtype=F32), GRID_W)
    col = jnp.tile(jnp.arange(GRID_W, dtype=F32), rows)
    n_pairs = ATT_HEAD_DIM // 4
    inv_freq = ROPE_BASE ** (-jnp.arange(n_pairs, dtype=F32) / n_pairs)
    ang = jnp.concatenate([row[:, None] * inv_freq, col[:, None] * inv_freq], axis=-1)
    c, s = jnp.cos(ang), jnp.sin(ang)
    cos_t = jnp.concatenate([c, c, c, c], axis=-1)
    sin_t = jnp.concatenate([-s, s, -s, s], axis=-1)
    return cos_t, sin_t


def _dot(a, b):
    return jnp.dot(a, b, preferred_element_type=F32)


def _dot_nt(a, b):
    return lax.dot_general(a, b, (((1,), (1,)), ((), ())), preferred_element_type=F32)


def _dot_tn(a, b):
    return lax.dot_general(a, b, (((0,), (0,)), ((), ())), preferred_element_type=F32)


def _split_hi_lo(w):
    hi = w.astype(BF16)
    lo = (w - hi.astype(F32)).astype(BF16)
    return jnp.concatenate([hi, lo], axis=0)


def _sigmoid(x):
    return 1.0 / (1.0 + jnp.exp(-x))


def _sigmoid_tanh(x):
    return 0.5 * jnp.tanh(0.5 * x) + 0.5


def _silu(x):
    return x * _sigmoid(x)


def _log_sigmoid(x):
    return jnp.minimum(x, 0.0) - jnp.log(1.0 + jnp.exp(-jnp.abs(x)))


def _rms_mod(x, g, shift, scale):
    y = x * lax.rsqrt(jnp.mean(x * x, axis=-1, keepdims=True) + EPS) * g
    return y * (1.0 + scale) + shift


def _params(sem):
    return pltpu.CompilerParams(dimension_semantics=sem, vmem_limit_bytes=VMEM_LIMIT)


def _ada_kernel(c_ref, w_ref, b_ref, o_ref):
    s = _silu(c_ref[...])
    o_ref[0] = jnp.dot(s, w_ref[0], preferred_element_type=F32,
                       precision=lax.Precision.HIGHEST) + b_ref[0]


def _ada(cc, ada_w, ada_b):
    depth, d, n = ada_w.shape
    rows = cc.shape[0]
    tn = 1536
    return pl.pallas_call(
        _ada_kernel,
        grid=(depth, n // tn),
        in_specs=[pl.BlockSpec((rows, d), lambda l, j: (0, 0)),
                  pl.BlockSpec((1, d, tn), lambda l, j: (l, 0, j)),
                  pl.BlockSpec((1, 1, tn), lambda l, j: (l, 0, j))],
        out_specs=pl.BlockSpec((1, rows, tn), lambda l, j: (l, 0, j)),
        out_shape=jax.ShapeDtypeStruct((depth, rows, n), F32),
        compiler_params=_params(("parallel", "parallel")),
        name="ada_mod",
    )(cc, ada_w, ada_b.reshape(depth, 1, n))


def _swap_halves(x):
    lane = lax.broadcasted_iota(jnp.int32, x.shape, 1)
    first = (lane % ATT_HEAD_DIM) < (ATT_HEAD_DIM // 2)
    return jnp.where(first, pltpu.roll(x, LANES - 32, axis=1), pltpu.roll(x, 32, axis=1))


_PROJ_OUTPUTS = ('qs', 'k', 'v', 'qm', 'km', 'vm', 'om', 'gm', 'bc', 'u', 'gt')
_STATE_OUTPUTS = ('k', 'v', 'qm', 'km', 'vm', 'gm')


def _in_kernel(x_ref, mod_ref, g_ref, w_ref, cos_ref, sin_ref, *out_refs, names):
    out = dict(zip(names, out_refs))
    tm = x_ref.shape[1]
    h = _rms_mod(x_ref[0], g_ref[...], mod_ref[0, 0:1, :], mod_ref[0, 1:2, :]).astype(BF16)

    def proj(name):
        lo, hi = _OFF[name]
        return _dot(h, w_ref[:, lo:hi])

    cos_t = cos_ref[...]
    sin_t = sin_ref[...]

    def rope(t):
        return t * cos_t + _swap_halves(t) * sin_t

    if 'qs' in out:
        q = proj('q')
        scale = ATT_HEAD_DIM ** -0.5 * LOG2E
        for hh in range(ATT_GROUP):
            r = (rope(q[:, hh * LANES:(hh + 1) * LANES]) * scale).astype(BF16)
            for qb in range(tm // ATT_BLOCK):
                out['qs'][0, qb, hh * ATT_BLOCK:(hh + 1) * ATT_BLOCK, :] = r[qb * ATT_BLOCK:(qb + 1) * ATT_BLOCK, :]
    for name in names:
        if name == 'qs':
            continue
        if name == 'k':
            val = rope(proj('k'))
        elif name == 'u':
            val = proj('cc') * proj('xc')
        else:
            val = proj(name)
        out[name][0] = val.astype(out[name].dtype)


def _in_proj(x, mod, mod_row, norm_g, w_p, cos_t, sin_t, tm, names=_PROJ_OUTPUTS):
    B, S, D = x.shape
    nb = S // ATT_BLOCK
    tok = lambda n, dt: (jax.ShapeDtypeStruct((B, S, n), dt), pl.BlockSpec((1, tm, n), lambda b, i: (b, i, 0)))
    outs = {'qs': (jax.ShapeDtypeStruct((B, nb, ATT_GROUP * ATT_BLOCK, LANES), BF16),
                   pl.BlockSpec((1, tm // ATT_BLOCK, ATT_GROUP * ATT_BLOCK, LANES), lambda b, i: (b, i, 0, 0))),
            'k': tok(128, BF16), 'v': tok(128, BF16), 'qm': tok(256, BF16), 'km': tok(256, BF16),
            'vm': tok(512, BF16), 'om': tok(512, BF16), 'gm': tok(128, F32), 'bc': tok(512, BF16),
            'u': tok(512, BF16), 'gt': tok(3072, BF16)}
    out_shape = tuple(outs[n][0] for n in names)
    out_specs = tuple(outs[n][1] for n in names)
    res = pl.pallas_call(
        functools.partial(_in_kernel, names=names),
        grid=(B, S // tm),
        in_specs=[pl.BlockSpec((1, tm, D), lambda b, i: (b, i, 0)),
                  pl.BlockSpec((1, 6, D), lambda b, i: (mod_row(b), 0, 0)),
                  pl.BlockSpec((1, D), lambda b, i: (0, 0)),
                  pl.BlockSpec((D, N_PROJ), lambda b, i: (0, 0), pipeline_mode=pl.Buffered(1)),
                  pl.BlockSpec((tm, LANES), lambda b, i: (i, 0)),
                  pl.BlockSpec((tm, LANES), lambda b, i: (i, 0))],
        out_specs=out_specs,
        out_shape=out_shape,
        compiler_params=_params(("parallel", "parallel")),
        name="in_proj",
    )(x, mod, norm_g.reshape(1, D), w_p, cos_t, sin_t)
    return dict(zip(names, res))


def _attn_block(q, sink_ref, kcat, vcat, masks):
    rows = q.shape[0]
    lane = lax.broadcasted_iota(jnp.int32, (1, LANES), 1)
    hh = lax.broadcasted_iota(jnp.int32, (rows, 1), 0) // ATT_BLOCK
    out = jnp.zeros((rows, LANES), F32)
    for g in range(ATT_KV_HEADS):
        lm = (lane < ATT_HEAD_DIM) if g == 0 else (lane >= ATT_HEAD_DIM)
        kz = jnp.where(lm, kcat, jnp.zeros_like(kcat))
        ones_lane = ATT_HEAD_DIM if g == 0 else 0
        vz = jnp.where(lm, vcat, jnp.where(lane == ones_lane, 1.0, 0.0).astype(BF16))
        s = _dot_nt(q, kz)
        if any(mk is not None for mk in masks):
            s = jnp.concatenate(
                [s[:, n * ATT_BLOCK:(n + 1) * ATT_BLOCK] if mk is None
                 else jnp.where(mk, s[:, n * ATT_BLOCK:(n + 1) * ATT_BLOCK], -jnp.inf)
                 for n, mk in enumerate(masks)], axis=1)
        sink = jnp.zeros((rows, 1), F32)
        for a in range(ATT_GROUP):
            sink = jnp.where(hh == a, sink_ref[g * ATT_GROUP + a] * LOG2E, sink)
        m = jnp.maximum(jnp.max(s, axis=-1, keepdims=True), sink)
        p = jnp.exp2((s - m).astype(BF16))
        pv = _dot(p, vz)
        l = pv[:, ones_lane:ones_lane + 1] + jnp.exp2(sink - m)
        out = out + jnp.where(lm, pv, 0.0) / l
    return out.astype(BF16)


ATT_STEP_BLOCKS = 8


def _attn_kernel(sink_ref, qs_ref, kc_ref, vc_ref, *rest, band):
    nsb = qs_ref.shape[1]
    if not band:
        (o_ref,) = rest
        n_ctx = kc_ref.shape[1] // ATT_BLOCK
        for sb in range(nsb):
            o_ref[0, sb] = _attn_block(qs_ref[0, sb], sink_ref, kc_ref[0], vc_ref[0], [None] * n_ctx)
        return
    kp_ref, kcur_ref, kn_ref, vp_ref, vcur_ref, vn_ref, o_ref = rest
    j = pl.program_id(1)
    nstep = pl.num_programs(1)
    n_ctx = kc_ref.shape[1] // ATT_BLOCK
    rows = qs_ref.shape[2]
    t = lax.broadcasted_iota(jnp.int32, (rows, 1), 0) % ATT_BLOCK
    i = lax.broadcasted_iota(jnp.int32, (1, ATT_BLOCK), 1)
    below = i >= t
    above = i <= t
    first = i >= t + jnp.where(j > 0, 0, 2 * ATT_BLOCK)
    last = i <= t - jnp.where(j < nstep - 1, 0, 2 * ATT_BLOCK)
    kblk = [kp_ref[0]] + [kcur_ref[0, n * ATT_BLOCK:(n + 1) * ATT_BLOCK, :] for n in range(nsb)] + [kn_ref[0]]
    vblk = [vp_ref[0]] + [vcur_ref[0, n * ATT_BLOCK:(n + 1) * ATT_BLOCK, :] for n in range(nsb)] + [vn_ref[0]]
    for sb in range(nsb):
        kcat = jnp.concatenate([kc_ref[0]] + kblk[sb:sb + 3], axis=0)
        vcat = jnp.concatenate([vc_ref[0]] + vblk[sb:sb + 3], axis=0)
        masks = [first if sb == 0 else below, None, last if sb == nsb - 1 else above]
        o_ref[0, sb] = _attn_block(qs_ref[0, sb], sink_ref, kcat, vcat, [None] * n_ctx + masks)


def _attention(qs, k, v, kc, vc, sink, band):
    B, nb = qs.shape[:2]
    lc = kc.shape[1]
    blocks = min(ATT_STEP_BLOCKS, nb)
    nstep = nb // blocks
    last = nb - 1
    qspec = pl.BlockSpec((1, blocks, ATT_GROUP * ATT_BLOCK, LANES), lambda b, j: (b, j, 0, 0))
    cspec = pl.BlockSpec((1, lc, LANES), lambda b, j: (b, 0, 0))
    in_specs = [pl.BlockSpec(memory_space=pltpu.SMEM), qspec, cspec, cspec]
    args = [sink.astype(F32), qs, kc, vc]
    if band:
        prev = pl.BlockSpec((1, ATT_BLOCK, LANES), lambda b, j: (b, jnp.maximum(blocks * j - 1, 0), 0))
        cur = pl.BlockSpec((1, blocks * ATT_BLOCK, LANES), lambda b, j: (b, j, 0))
        nxt = pl.BlockSpec((1, ATT_BLOCK, LANES), lambda b, j: (b, jnp.minimum(blocks * (j + 1), last), 0))
        in_specs += [prev, cur, nxt, prev, cur, nxt]
        args += [k, k, k, v, v, v]
    return pl.pallas_call(
        functools.partial(_attn_kernel, band=band),
        grid=(B, nstep),
        in_specs=in_specs,
        out_specs=qspec,
        out_shape=jax.ShapeDtypeStruct(qs.shape, BF16),
        compiler_params=_params(("parallel", "parallel")),
        name="attention_band" if band else "attention_ctx",
    )(*args)


def _mlstm_step(dirs, T):
    kscale = M_QK_DIM ** -0.5
    si = lax.broadcasted_iota(jnp.int32, (T, T), 0)
    ri = lax.broadcasted_iota(jnp.int32, (T, T), 1)
    lane_qk = lax.broadcasted_iota(jnp.int32, (1, M_HEADS * M_QK_DIM), 1) // M_QK_DIM
    lane_m = lax.broadcasted_iota(jnp.int32, (1, LANES), 1)
    row_c = lax.broadcasted_iota(jnp.int32, (M_HEADS * M_QK_DIM, 1), 0) // M_QK_DIM
    combos = [(d, hd) for d in range(2) for hd in range(M_HEADS)]

    tri, bcol, gt, bt, blast = [], [], [], [], []
    for d, (q, k, v, g, C, n, m) in enumerate(dirs):
        t = (ri <= si) if d == 0 else (ri >= si)
        tri.append(t)
        lf = _log_sigmoid(g)
        lf1 = lf.astype(BF16)
        rem = lf - lf1.astype(F32)
        lf2 = rem.astype(BF16)
        lf3 = (rem - lf2.astype(F32)).astype(BF16)
        parts = _dot(jnp.where(t, 1.0, 0.0).astype(BF16), jnp.concatenate([lf1, lf2, lf3], axis=1))
        bc = parts[:, 0:LANES] + parts[:, LANES:2 * LANES] + parts[:, 2 * LANES:3 * LANES]
        bcol.append(bc)
        gt.append(g.T)
        bt.append(bc.T)
        blast.append(bc[T - 1:T, :] if d == 0 else bc[0:1, :])

    def lanes(d, hd):
        return (2 * d) * M_HEADS + hd, (2 * d + 1) * M_HEADS + hd

    b_col = {c: bcol[c[0]][:, lanes(*c)[1]:lanes(*c)[1] + 1] for c in combos}
    ig_col = {c: dirs[c[0]][3][:, lanes(*c)[0]:lanes(*c)[0] + 1] for c in combos}
    alpha = {c: gt[c[0]][lanes(*c)[0]:lanes(*c)[0] + 1, :] - bt[c[0]][lanes(*c)[1]:lanes(*c)[1] + 1, :]
             for c in combos}
    m_old = {c: dirs[c[0]][6][:, c[1]:c[1] + 1] for c in combos}
    b_last = {c: blast[c[0]][:, lanes(*c)[1]:lanes(*c)[1] + 1] for c in combos}
    hmask = {hd: lane_qk == hd for hd in range(M_HEADS)}

    a_mat = {c: jnp.where(tri[c[0]], alpha[c], -jnp.inf) for c in combos}
    a_max = {c: jnp.max(a_mat[c], axis=1, keepdims=True) for c in combos}
    a_int = {c: b_col[c] + m_old[c] for c in combos}
    m_s = {c: jnp.maximum(a_int[c], b_col[c] + a_max[c]) for c in combos}
    w_int = {c: jnp.exp(a_int[c] - m_s[c]) for c in combos}
    w_mat = {c: jnp.exp(a_mat[c] + (b_col[c] - m_s[c])) for c in combos}
    qmask = {c: jnp.where(hmask[c[1]], dirs[c[0]][0], jnp.zeros_like(dirs[c[0]][0])) for c in combos}
    s_qk = {c: w_mat[c] * (_dot_nt(qmask[c], dirs[c[0]][1]) * kscale) for c in combos}
    vh = {c: dirs[c[0]][2][:, c[1] * M_V_DIM:(c[1] + 1) * M_V_DIM] for c in combos}
    c_bf = [dirs[d][4].astype(BF16) for d in range(2)]
    num = {c: _dot(s_qk[c].astype(BF16), vh[c]) + w_int[c] * _dot(qmask[c], c_bf[c[0]]) for c in combos}
    qn_all = [dirs[d][0].astype(F32) * dirs[d][5] for d in range(2)]
    qn = {c: jnp.sum(jnp.where(hmask[c[1]], qn_all[c[0]], 0.0), axis=1, keepdims=True) for c in combos}
    den = {c: jnp.sum(s_qk[c], axis=1, keepdims=True) + w_int[c] * qn[c] for c in combos}
    h = {c: num[c] / jnp.maximum(jnp.abs(den[c]), jnp.exp(-m_s[c])) for c in combos}

    r_col = {c: b_last[c] - b_col[c] + ig_col[c] for c in combos}
    m_new = {c: jnp.maximum(b_last[c] + m_old[c], jnp.max(r_col[c], axis=0, keepdims=True)) for c in combos}
    decay = {c: jnp.exp(b_last[c] + m_old[c] - m_new[c]) for c in combos}
    w_r = {c: jnp.exp(r_col[c] - m_new[c]) for c in combos}

    outs = []
    for d, (q, k, v, g, C, n, m) in enumerate(dirs):
        w_lanes = jnp.zeros((T, M_HEADS * M_QK_DIM), F32)
        dec_lanes = jnp.zeros((1, M_HEADS * M_QK_DIM), F32)
        dec_rows = jnp.zeros((M_HEADS * M_QK_DIM, 1), F32)
        m_row = jnp.zeros((1, LANES), F32)
        for hd in range(M_HEADS):
            w_lanes = jnp.where(hmask[hd], w_r[(d, hd)], w_lanes)
            dec_lanes = jnp.where(hmask[hd], decay[(d, hd)], dec_lanes)
            dec_rows = jnp.where(row_c == hd, decay[(d, hd)], dec_rows)
            m_row = jnp.where(lane_m == hd, m_new[(d, hd)], m_row)
        kw = k.astype(F32) * (w_lanes * kscale)
        kwt = kw.T.astype(BF16)
        upd = jnp.concatenate(
            [_dot(kwt[hd * M_QK_DIM:(hd + 1) * M_QK_DIM, :], vh[(d, hd)]) for hd in range(M_HEADS)], axis=0)
        c_new = dec_rows * C + upd
        n_new = dec_lanes * n + jnp.sum(kw, axis=0, keepdims=True)
        h_all = jnp.concatenate([h[(d, hd)] for hd in range(M_HEADS)], axis=1)
        outs.append((h_all, c_new, n_new, m_row))
    return outs


def _mlstm_kernel(gb_ref, qf_ref, kf_ref, vf_ref, gf_ref, qb_ref, kb_ref, vb_ref, gbk_ref,
                  c0_ref, n0_ref, m0_ref, hf_ref, hb_ref, cf_ref, nf_ref, mf_ref,
                  c_s, n_s, m_s):
    ci = pl.program_id(1)
    T = MLSTM_TILE
    n_sub = qf_ref.shape[1] // T

    @pl.when(ci == 0)
    def _():
        c_s[...] = c0_ref[0]
        n_s[...] = n0_ref[0]
        m_s[...] = m0_ref[0]

    gb = gb_ref[...]
    state = [(c_s[d], n_s[d], m_s[d]) for d in range(2)]
    refs = ((qf_ref, kf_ref, vf_ref, gf_ref, hf_ref), (qb_ref, kb_ref, vb_ref, gbk_ref, hb_ref))
    for s in range(n_sub):
        lo = (s * T, (n_sub - 1 - s) * T)
        dirs = [(q_ref[0, lo[d]:lo[d] + T, :], k_ref[0, lo[d]:lo[d] + T, :], v_ref[0, lo[d]:lo[d] + T, :],
                 g_ref[0, lo[d]:lo[d] + T, :] + gb) + state[d]
                for d, (q_ref, k_ref, v_ref, g_ref, _) in enumerate(refs)]
        outs = _mlstm_step(dirs, T)
        for d in range(2):
            refs[d][4][0, lo[d]:lo[d] + T, :] = outs[d][0]
        state = [outs[d][1:] for d in range(2)]
    for d in range(2):
        c_s[d], n_s[d], m_s[d] = state[d]

    @pl.when(ci == pl.num_programs(1) - 1)
    def _():
        cf_ref[0] = c_s[...]
        nf_ref[0] = n_s[...]
        mf_ref[0] = m_s[...]


MLSTM_TILE = 128
MLSTM_STEP_CHUNKS = 8


def _mlstm(qm, km, vm, gm, gate_b, state):
    B, S, _ = qm.shape
    T = MLSTM_TILE * min(MLSTM_STEP_CHUNKS, S // MLSTM_TILE)
    nc = S // T
    nq = M_HEADS * M_QK_DIM
    fwd = lambda n: pl.BlockSpec((1, T, n), lambda b, c: (b, c, 0))
    bwd = lambda n: pl.BlockSpec((1, T, n), lambda b, c: (b, nc - 1 - c, 0))
    st_specs = [pl.BlockSpec((1, 2, nq, M_V_DIM), lambda b, c: (b, 0, 0, 0)),
                pl.BlockSpec((1, 2, 1, nq), lambda b, c: (b, 0, 0, 0)),
                pl.BlockSpec((1, 2, 1, LANES), lambda b, c: (b, 0, 0, 0))]
    st_shapes = [jax.ShapeDtypeStruct((B, 2, nq, M_V_DIM), F32),
                 jax.ShapeDtypeStruct((B, 2, 1, nq), F32),
                 jax.ShapeDtypeStruct((B, 2, 1, LANES), F32)]
    gb_row = jnp.pad(gate_b.reshape(1, -1).astype(F32), ((0, 0), (0, LANES - gate_b.size)))
    outs = pl.pallas_call(
        _mlstm_kernel,
        grid=(B, nc),
        in_specs=[pl.BlockSpec((1, LANES), lambda b, c: (0, 0)),
                  fwd(256), fwd(256), fwd(512), fwd(LANES),
                  bwd(256), bwd(256), bwd(512), bwd(LANES)] + st_specs,
        out_specs=[fwd(M_OUT), bwd(M_OUT)] + st_specs,
        out_shape=[jax.ShapeDtypeStruct((B, S, M_OUT), F32)] * 2 + st_shapes,
        scratch_shapes=[pltpu.VMEM((2, nq, M_V_DIM), F32),
                        pltpu.VMEM((2, 1, nq), F32),
                        pltpu.VMEM((2, 1, LANES), F32)],
        compiler_params=_params(("parallel", "arbitrary")),
        name="mlstm_scan",
    )(gb_row, qm, km, vm, gm, qm, km, vm, gm, *state)
    return outs[0], outs[1], tuple(outs[2:])


def _route(scores, sel):
    tm = scores.shape[1]
    gi8 = lax.broadcasted_iota(jnp.int32, (GROUP_SIZE, tm), 0)

    def stack_rows(rows):
        out = jnp.broadcast_to(rows[0], (len(rows), tm))
        for r, v in enumerate(rows[1:], start=1):
            out = jnp.where(gi8 == r, v, out)
        return out

    gs = []
    for g in range(N_GROUPS):
        blk = sel[g * GROUP_SIZE:(g + 1) * GROUP_SIZE, :]
        m1 = jnp.max(blk, axis=0, keepdims=True)
        first = jnp.min(jnp.where(blk == m1, gi8, GROUP_SIZE), axis=0, keepdims=True)
        m2 = jnp.max(jnp.where(gi8 == first, -jnp.inf, blk), axis=0, keepdims=True)
        gs.append(m1 + m2)
    gsc = stack_rows(gs)
    gsel = jnp.zeros((N_GROUPS, tm), F32)
    for _ in range(TOPK_GROUPS):
        mx = jnp.max(gsc, axis=0, keepdims=True)
        first = jnp.min(jnp.where(gsc == mx, gi8, N_GROUPS), axis=0, keepdims=True)
        pick = gi8 == first
        gsel = jnp.where(pick, 1.0, gsel)
        gsc = jnp.where(pick, -jnp.inf, gsc)
    cur = jnp.concatenate(
        [jnp.where(gsel[g:g + 1, :] > 0.0, sel[g * GROUP_SIZE:(g + 1) * GROUP_SIZE, :], -jnp.inf)
         for g in range(N_GROUPS)], axis=0)
    ei = lax.broadcasted_iota(jnp.int32, (N_EXPERTS, tm), 0)
    idx, wts = [], []
    for _ in range(TOP_K):
        mx = jnp.max(cur, axis=0, keepdims=True)
        first = jnp.min(jnp.where(cur == mx, ei, N_EXPERTS), axis=0, keepdims=True)
        pick = ei == first
        idx.append(first)
        wts.append(jnp.sum(jnp.where(pick, scores, 0.0), axis=0, keepdims=True))
        cur = jnp.where(pick, -jnp.inf, cur)
    tot = wts[0]
    for w in wts[1:]:
        tot = tot + w
    wts = [w / tot * ROUTED_SCALE for w in wts]
    return stack_rows(idx), stack_rows(wts)


def _merge_kernel(x_ref, mod_ref, oat_ref, hf_ref, hb_ref, om_ref, ng_ref, u_ref, up_ref, un_ref,
                  bc_ref, cw_ref, gt_ref, bgb_ref, wa_ref, wm_ref, wc_ref, wo_ref, n2_ref,
                  rw_ref, rb_ref, sgu_ref, sd_ref,
                  base_ref, h2_ref, idx_ref, wt_ref, cnt_ref):
    i = pl.program_id(1)
    tm = x_ref.shape[1]
    x = x_ref[0]
    g1 = mod_ref[0, 2:3, :]
    sh2 = mod_ref[0, 3:4, :]
    sc2 = mod_ref[0, 4:5, :]
    g2 = mod_ref[0, 5:6, :]

    ya = jnp.concatenate(
        [jnp.concatenate([oat_ref[0, qb, hh * ATT_BLOCK:(hh + 1) * ATT_BLOCK, :] for hh in range(ATT_GROUP)], axis=1)
         for qb in range(tm // ATT_BLOCK)], axis=0)

    hsum = hf_ref[0] + hb_ref[0]
    parts = []
    for hd in range(M_HEADS):
        hh_ = hsum[:, hd * M_V_DIM:(hd + 1) * M_V_DIM]
        parts.append(hh_ * lax.rsqrt(jnp.mean(hh_ * hh_, axis=-1, keepdims=True) + EPS))
    hn = jnp.concatenate(parts, axis=1) * ng_ref[...]
    ym = (_sigmoid_tanh(om_ref[0].astype(F32)) * hn).astype(BF16)

    u = u_ref[0].astype(F32)
    row = lax.broadcasted_iota(jnp.int32, (tm, 1), 0)
    has_prev = (i > 0).astype(F32)
    has_next = (i < pl.num_programs(1) - 1).astype(F32)
    prev_row = up_ref[0, BF16_SUBLANES - 1:BF16_SUBLANES, :].astype(F32) * has_prev
    next_row = un_ref[0, 0:1, :].astype(F32) * has_next
    u_m1 = jnp.where(row == 0, prev_row, pltpu.roll(u, 1, axis=0))
    u_p1 = jnp.where(row == tm - 1, next_row, pltpu.roll(u, tm - 1, axis=0))
    conv = cw_ref[0:1, :] * u_m1 + cw_ref[1:2, :] * u + cw_ref[2:3, :] * u_p1
    yc = (bc_ref[0].astype(F32) * conv).astype(BF16)

    gg = _sigmoid_tanh(gt_ref[0].astype(F32) + bgb_ref[...])
    ymix = (gg[:, 0:D_MODEL] * _dot(ya, wa_ref[...])
            + gg[:, D_MODEL:2 * D_MODEL] * _dot(ym, wm_ref[...])
            + gg[:, 2 * D_MODEL:3 * D_MODEL] * _dot(yc, wc_ref[...]))
    y = _dot(ymix.astype(BF16), wo_ref[...])
    xm = x + g1 * y

    h2f = _rms_mod(xm, n2_ref[...], sh2, sc2)
    h2 = h2f.astype(BF16)
    h2_ref[0] = h2

    h2_lo = (h2f - h2.astype(F32)).astype(BF16)
    rw = rw_ref[...]
    part = _dot_nt(rw, h2)
    logits_t = part[0:N_EXPERTS, :] + part[N_EXPERTS:2 * N_EXPERTS, :] + _dot_nt(rw[0:N_EXPERTS, :], h2_lo)
    scores = _sigmoid(logits_t)
    idx, wts = _route(scores, scores + rb_ref[...])
    idx_ref[0] = idx
    wt_ref[0] = wts
    ei = lax.broadcasted_iota(jnp.int32, (N_EXPERTS, tm), 0)
    pick = jnp.zeros((N_EXPERTS, tm), F32)
    for kk in range(TOP_K):
        pick = jnp.where(ei == idx[kk:kk + 1, :], 1.0, pick)
    for sub in range(tm // MOE_TILE):
        n_e = jnp.sum(pick[:, sub * MOE_TILE:(sub + 1) * MOE_TILE], axis=1, keepdims=True)
        cnt_ref[sub] = jnp.broadcast_to(n_e, (N_EXPERTS, LANES)).astype(jnp.int32)

    a = _dot(h2, sgu_ref[...])
    act = (_silu(a[:, 0:SHARED_FF]) * a[:, SHARED_FF:2 * SHARED_FF]).astype(BF16)
    base_ref[0] = xm + g2 * _dot(act, sd_ref[...])


def _merge(x, mod, mod_row, oat, hf, hb, om, u, bc, gt, lw, tm):
    B, S, D = x.shape
    nt = S // tm
    hal = BF16_SUBLANES
    last_h = S // hal - 1
    tspec = lambda n: pl.BlockSpec((1, tm, n), lambda b, i: (b, i, 0))
    full = lambda a: pl.BlockSpec(a.shape, lambda b, i: (0,) * a.ndim, pipeline_mode=pl.Buffered(1))
    weights = [lw['mlstm_norm_g'], lw['conv_w'], lw['branch_gate_b'], lw['w_br_attn'], lw['w_br_mlstm'],
               lw['w_br_conv'], lw['w_out'], lw['norm2_g'], lw['router_wt'], lw['router_bias'],
               lw['sh_gu'], lw['sh_d']]
    in_specs = [tspec(D),
                pl.BlockSpec((1, 6, D), lambda b, i: (mod_row(b), 0, 0)),
                pl.BlockSpec((1, tm // ATT_BLOCK, ATT_GROUP * ATT_BLOCK, LANES), lambda b, i: (b, i, 0, 0)),
                tspec(M_OUT), tspec(M_OUT), tspec(M_OUT), full(weights[0]),
                tspec(CONV_WIDTH),
                pl.BlockSpec((1, hal, CONV_WIDTH), lambda b, i: (b, jnp.maximum(i * (tm // hal) - 1, 0), 0)),
                pl.BlockSpec((1, hal, CONV_WIDTH), lambda b, i: (b, jnp.minimum((i + 1) * (tm // hal), last_h), 0)),
                tspec(CONV_WIDTH), full(weights[1]), tspec(N_BRANCH * D), full(weights[2])]
    in_specs += [full(w) for w in weights[3:]]
    tr = lambda n, dt: (jax.ShapeDtypeStruct((B, n, S), dt), pl.BlockSpec((1, n, tm), lambda b, i: (b, 0, i)))
    outs = [(jax.ShapeDtypeStruct((B, S, D), F32), tspec(D)),
            (jax.ShapeDtypeStruct((B, S, D), BF16), tspec(D)),
            tr(TOP_K, jnp.int32), tr(TOP_K, F32),
            (jax.ShapeDtypeStruct((B * S // MOE_TILE, N_EXPERTS, LANES), jnp.int32),
             pl.BlockSpec((tm // MOE_TILE, N_EXPERTS, LANES), lambda b, i: (b * nt + i, 0, 0)))]
    return pl.pallas_call(
        _merge_kernel,
        grid=(B, nt),
        in_specs=in_specs,
        out_specs=[o[1] for o in outs],
        out_shape=[o[0] for o in outs],
        compiler_params=_params(("parallel", "parallel")),
        name="merge_route",
    )(x, mod, oat, hf, hb, om, weights[0], u, u, u, bc, weights[1], gt, weights[2], *weights[3:])


MOE_TILE = 256
CHUNK = BF16_SUBLANES
GROUP_CHUNKS = 64
GROUP_ROWS = GROUP_CHUNKS * CHUNK
TILE_ROWS = MOE_TILE * TOP_K + N_EXPERTS * CHUNK
TILE_CHUNKS = TILE_ROWS // CHUNK
N_PAD_CHUNKS = N_EXPERTS * (GROUP_CHUNKS - 1)
N_SPARE_CHUNKS = 2 * TILE_CHUNKS


def _dispatch_tile(h, idx, wts):
    tm = h.shape[0]
    ei = lax.broadcasted_iota(jnp.int32, (N_EXPERTS, tm), 0)
    pick = jnp.zeros((N_EXPERTS, tm), F32)
    wmat = jnp.zeros((N_EXPERTS, tm), F32)
    for kk in range(TOP_K):
        chosen = ei == idx[kk:kk + 1, :]
        pick = jnp.where(chosen, 1.0, pick)
        wmat = jnp.where(chosen, wts[kk:kk + 1, :], wmat)
    t0 = lax.broadcasted_iota(jnp.int32, (tm, tm), 0)
    t1 = lax.broadcasted_iota(jnp.int32, (tm, tm), 1)
    rank = _dot(pick.astype(BF16), jnp.where(t0 < t1, 1.0, 0.0).astype(BF16))
    n_e = jnp.sum(pick, axis=1, keepdims=True)
    n_pad = jnp.floor((n_e + (CHUNK - 1)) * (1.0 / CHUNK)) * CHUNK
    e0 = lax.broadcasted_iota(jnp.int32, (N_EXPERTS, N_EXPERTS), 0)
    e1 = lax.broadcasted_iota(jnp.int32, (N_EXPERTS, N_EXPERTS), 1)
    seg = _dot(jnp.where(e1 < e0, 1.0, 0.0).astype(BF16),
               jnp.broadcast_to(n_pad, (N_EXPERTS, tm)).astype(BF16))
    posmat = seg + rank
    chunk_of = jnp.floor(posmat * (1.0 / CHUNK))
    offs_of = posmat - chunk_of * CHUNK
    chunk_row = jnp.where(pick > 0.0, chunk_of * CHUNK, -float(CHUNK))
    eye = e0 == e1
    to_row = lambda col: jnp.sum(jnp.where(eye, col, 0.0), axis=0, keepdims=True)
    seg_row = to_row(seg[:, 0:1])
    end_row = to_row(seg[:, 0:1] + n_pad)
    seg_row2 = jnp.concatenate([seg_row, seg_row], axis=1)
    end_row2 = jnp.concatenate([end_row, end_row], axis=1)
    r128 = lax.broadcasted_iota(jnp.int32, (TILE_ROWS, 2 * N_EXPERTS), 0).astype(F32)
    own2 = jnp.where(r128 >= seg_row2, jnp.where(r128 < end_row2, 1.0, 0.0), 0.0).astype(BF16)
    row_of = _dot(own2, jnp.concatenate([chunk_row, offs_of], axis=0).astype(BF16))
    w_of = _dot(own2[:, 0:N_EXPERTS], wmat.astype(BF16))
    riota = lax.broadcasted_iota(jnp.int32, (TILE_ROWS, tm), 0).astype(F32)
    hit = row_of == riota
    xg = _dot(jnp.where(hit, 1.0, 0.0).astype(BF16), h).astype(BF16)
    return xg, jnp.where(hit, w_of, 0.0).astype(BF16)


def _dispatch_kernel(pos_ref, pad_ref, h_ref, idx_ref, wt_ref, xs_ref, pw_ref, buf0, buf1, zero_buf, sems, pad_sem,
                     *, n_tiles):
    j = pl.program_id(0)
    bufs = (buf0, buf1)

    def compute(p):
        xg, pw = _dispatch_tile(h_ref[0], idx_ref[0], wt_ref[0])
        bufs[p][...] = xg.reshape(bufs[p].shape)
        pw_ref[...] = pw

    def issue(tile, p):
        for c in range(TILE_CHUNKS):
            pltpu.make_async_copy(bufs[p].at[c], xs_ref.at[pos_ref[tile * TILE_CHUNKS + c]], sems.at[p]).start()

    def wait(p):
        pltpu.make_async_copy(bufs[p], xs_ref.at[pl.ds(0, TILE_CHUNKS)], sems.at[p]).wait()

    for p in range(2):
        @pl.when((j >= 2) & (j % 2 == p))
        def _(p=p):
            wait(p)

        @pl.when((j >= 1) & (j < n_tiles) & (j % 2 == p))
        def _(p=p):
            issue(j - 1, 1 - p)
            compute(p)

    @pl.when(j == 0)
    def _():
        compute(0)

    @pl.when(j == n_tiles)
    def _():
        last = (n_tiles - 1) % 2
        issue(n_tiles - 1, last)
        zero_buf[...] = jnp.zeros_like(zero_buf)

        def pad_expert(e, carry):
            def pad_issue(c, inner):
                pltpu.make_async_copy(zero_buf, xs_ref.at[pad_ref[e] + c], pad_sem).start()
                return inner
            lax.fori_loop(0, pad_ref[N_EXPERTS + e], pad_issue, 0)

            def pad_wait(c, inner):
                pltpu.make_async_copy(zero_buf, xs_ref.at[pad_ref[e] + c], pad_sem).wait()
                return inner
            lax.fori_loop(0, pad_ref[N_EXPERTS + e], pad_wait, 0)
            return carry
        lax.fori_loop(0, N_EXPERTS, pad_expert, 0)
        wait(last)


def _dispatch(h2, idx, wts, pos, pad_pos, n_slots):
    B, S, D = h2.shape
    tm = MOE_TILE
    nt = S // tm
    n_tiles = B * nt
    tile = lambda j: jnp.minimum(j, n_tiles - 1)
    grid_spec = pltpu.PrefetchScalarGridSpec(
        num_scalar_prefetch=2,
        grid=(n_tiles + 1,),
        in_specs=[pl.BlockSpec((1, tm, D), lambda j, pos, pad: (tile(j) // nt, tile(j) % nt, 0)),
                  pl.BlockSpec((1, TOP_K, tm), lambda j, pos, pad: (tile(j) // nt, 0, tile(j) % nt)),
                  pl.BlockSpec((1, TOP_K, tm), lambda j, pos, pad: (tile(j) // nt, 0, tile(j) % nt))],
        out_specs=[pl.BlockSpec(memory_space=pl.ANY),
                   pl.BlockSpec((TILE_ROWS, tm), lambda j, pos, pad: (tile(j), 0))],
        scratch_shapes=[pltpu.VMEM((TILE_CHUNKS, CHUNK, D), BF16),
                        pltpu.VMEM((TILE_CHUNKS, CHUNK, D), BF16),
                        pltpu.VMEM((CHUNK, D), BF16),
                        pltpu.SemaphoreType.DMA((2,)),
                        pltpu.SemaphoreType.DMA(())],
    )
    return pl.pallas_call(
        functools.partial(_dispatch_kernel, n_tiles=n_tiles),
        grid_spec=grid_spec,
        out_shape=[jax.ShapeDtypeStruct((n_slots + N_SPARE_CHUNKS, CHUNK, D), BF16),
                   jax.ShapeDtypeStruct((n_tiles * TILE_ROWS, tm), BF16)],
        compiler_params=_params(("arbitrary",)),
        name="moe_dispatch",
    )(pos, pad_pos, h2, idx, wts)


def _moe_tables(cnt, g_max):
    nt = cnt.shape[0]
    cc = (cnt + (CHUNK - 1)) // CHUNK
    segblk = jnp.cumsum(cc, axis=1) - cc
    tile_chunks = jnp.sum(cc, axis=1)
    prior = jnp.cumsum(cc, axis=0) - cc
    ge_cnt = (jnp.sum(cc, axis=0) + (GROUP_CHUNKS - 1)) // GROUP_CHUNKS
    gbase = jnp.cumsum(ge_cnt) - ge_cnt
    n_groups = jnp.sum(ge_cnt)
    c = jnp.arange(TILE_CHUNKS, dtype=jnp.int32)
    e_of = jnp.sum(((segblk + cc)[:, None, :] <= c[None, :, None]).astype(jnp.int32), axis=-1)
    e_of = jnp.minimum(e_of, N_EXPERTS - 1)
    seg_base = gbase[None, :] * GROUP_CHUNKS + prior - segblk
    onehot = e_of[:, :, None] == jnp.arange(N_EXPERTS, dtype=jnp.int32)[None, None, :]
    pos = jnp.sum(jnp.where(onehot, seg_base[:, None, :], 0), axis=-1) + c[None, :]
    valid = c[None, :] < tile_chunks[:, None]
    n_slots = g_max * GROUP_CHUNKS
    parity = (jnp.arange(nt, dtype=jnp.int32) % 2)[:, None]
    pos_write = jnp.where(valid, pos, n_slots + parity * TILE_CHUNKS + c[None, :]).astype(jnp.int32)
    ce = jnp.sum(cc, axis=0)
    pad_pos = jnp.concatenate([gbase * GROUP_CHUNKS + ce, ge_cnt * GROUP_CHUNKS - ce]).astype(jnp.int32)
    g = jnp.arange(g_max, dtype=jnp.int32)
    grp_e = jnp.minimum(jnp.sum(((gbase + ge_cnt)[None, :] <= g[:, None]).astype(jnp.int32), axis=1),
                        N_EXPERTS - 1).astype(jnp.int32)
    hi = lax.Precision.HIGHEST
    experts = jnp.arange(N_EXPERTS, dtype=jnp.int32)
    slot = jnp.arange(n_slots, dtype=jnp.int32)
    e_slot = jnp.minimum(jnp.sum((((gbase + ge_cnt) * GROUP_CHUNKS)[None, :] <= slot[:, None]).astype(jnp.int32),
                                 axis=1), N_EXPERTS - 1)
    oh_e = (e_slot[:, None] == experts[None, :]).astype(F32)
    q = slot - jnp.sum(oh_e * (gbase * GROUP_CHUNKS).astype(F32)[None, :], axis=1).astype(jnp.int32)
    slot_valid = q < jnp.sum(oh_e * ce.astype(F32)[None, :], axis=1).astype(jnp.int32)
    cum_end = jnp.dot(oh_e, (prior + cc).astype(F32).T, precision=hi)
    t_slot = jnp.minimum(jnp.sum((cum_end <= q[:, None].astype(F32)).astype(jnp.int32), axis=1), nt - 1)
    oh_t = (t_slot[:, None] == jnp.arange(nt, dtype=jnp.int32)[None, :]).astype(F32)
    shift = jnp.sum(jnp.dot(oh_t, (segblk - prior).astype(F32), precision=hi) * oh_e, axis=1).astype(jnp.int32)
    spare = nt * TILE_CHUNKS + ((slot // GROUP_CHUNKS) % 2) * GROUP_CHUNKS + slot % GROUP_CHUNKS
    inv = jnp.where(slot_valid, t_slot * TILE_CHUNKS + q + shift, spare).astype(jnp.int32)
    tiles = jnp.arange(nt, dtype=jnp.int32)
    fill = jnp.concatenate([tiles * TILE_CHUNKS + tile_chunks, TILE_CHUNKS - tile_chunks]).astype(jnp.int32)
    return (pos_write.reshape(-1), inv, pad_pos.reshape(-1), fill, grp_e,
            n_groups.reshape(1).astype(jnp.int32))


def _ffn_kernel(ge_ref, na_ref, inv_ref, fill_ref, x_ref, wg_ref, wu_ref, wd_ref, yt_ref,
                buf0, buf1, zero_buf, wg_s, wu_s, wd_s, sems, fill_sem, *, n_tiles):
    g = pl.program_id(0)
    na = na_ref[0]
    bufs = (buf0, buf1)

    g_live = jnp.minimum(g, jnp.maximum(na - 1, 0))
    @pl.when((g < na) & ((g == 0) | (ge_ref[g_live] != ge_ref[jnp.maximum(g_live - 1, 0)])))
    def _():
        wg_s[...] = wg_ref[0, 0].astype(BF16)
        wu_s[...] = wu_ref[0, 0].astype(BF16)
        wd_s[...] = wd_ref[0, 0].astype(BF16)

    def compute(p):
        x = x_ref[...].reshape(GROUP_ROWS, x_ref.shape[2])
        gate = _dot(x, wg_s[...])
        up = _dot(x, wu_s[...])
        act = (_silu(gate) * up).astype(BF16)
        bufs[p][...] = _dot(act, wd_s[...]).astype(BF16).reshape(bufs[p].shape)

    def issue(grp, p):
        for c in range(GROUP_CHUNKS):
            pltpu.make_async_copy(bufs[p].at[c], yt_ref.at[inv_ref[grp * GROUP_CHUNKS + c]], sems.at[p]).start()

    def wait(p):
        pltpu.make_async_copy(bufs[p], yt_ref.at[pl.ds(0, GROUP_CHUNKS)], sems.at[p]).wait()

    @pl.when(g == 0)
    def _():
        zero_buf[...] = jnp.zeros_like(zero_buf)

        def fill_tile(t, carry):
            def fill_issue(c, inner):
                pltpu.make_async_copy(zero_buf, yt_ref.at[fill_ref[t] + c], fill_sem).start()
                return inner
            lax.fori_loop(0, fill_ref[n_tiles + t], fill_issue, 0)

            def fill_wait(c, inner):
                pltpu.make_async_copy(zero_buf, yt_ref.at[fill_ref[t] + c], fill_sem).wait()
                return inner
            lax.fori_loop(0, fill_ref[n_tiles + t], fill_wait, 0)
            return carry
        lax.fori_loop(0, n_tiles, fill_tile, 0)

    for p in range(2):
        @pl.when((g >= 2) & (g - 2 < na) & (g % 2 == p))
        def _(p=p):
            wait(p)

        @pl.when((g >= 1) & (g < na) & (g % 2 == p))
        def _(p=p):
            issue(g - 1, 1 - p)
            compute(p)

        @pl.when((g >= 1) & (g == na) & (g % 2 == p))
        def _(p=p):
            issue(g - 1, 1 - p)

    @pl.when((g == 0) & (na > 0))
    def _():
        compute(0)


def _ffn_grouped(xs, inv, fill, grp_e, n_groups, layer, w_gate, w_up, w_down, g_max, n_tiles):
    D = xs.shape[2]
    live = lambda g, na: jnp.minimum(g, jnp.maximum(na[0] - 1, 0))
    wspec = lambda shape: pl.BlockSpec((1, 1) + shape, lambda g, ge, na, inv, fill: (layer, ge[live(g, na)], 0, 0))
    grid_spec = pltpu.PrefetchScalarGridSpec(
        num_scalar_prefetch=4,
        grid=(g_max + 2,),
        in_specs=[pl.BlockSpec((GROUP_CHUNKS, CHUNK, D), lambda g, ge, na, inv, fill: (live(g, na), 0, 0)),
                  wspec((D, EXPERT_FF)), wspec((D, EXPERT_FF)), wspec((EXPERT_FF, D))],
        out_specs=pl.BlockSpec(memory_space=pl.ANY),
        scratch_shapes=[pltpu.VMEM((GROUP_CHUNKS, CHUNK, D), BF16),
                        pltpu.VMEM((GROUP_CHUNKS, CHUNK, D), BF16),
                        pltpu.VMEM((CHUNK, D), BF16),
                        pltpu.VMEM((D, EXPERT_FF), BF16),
                        pltpu.VMEM((D, EXPERT_FF), BF16),
                        pltpu.VMEM((EXPERT_FF, D), BF16),
                        pltpu.SemaphoreType.DMA((2,)),
                        pltpu.SemaphoreType.DMA(())],
    )
    return pl.pallas_call(
        functools.partial(_ffn_kernel, n_tiles=n_tiles),
        grid_spec=grid_spec,
        out_shape=jax.ShapeDtypeStruct((n_tiles * TILE_CHUNKS + 2 * GROUP_CHUNKS, CHUNK, D), BF16),
        compiler_params=_params(("arbitrary",)),
        name="moe_ffn",
    )(grp_e, n_groups, inv, fill, xs, w_gate, w_up, w_down)


def _combine_kernel(yt_ref, pw_ref, base_ref, mod_ref, fg_ref, o_ref, *, final):
    yt = yt_ref[...].reshape(TILE_ROWS, yt_ref.shape[2])
    routed = _dot_tn(pw_ref[...], yt)
    out = base_ref[...] + mod_ref[0, 5:6, :] * routed
    if final:
        out = out * lax.rsqrt(jnp.mean(out * out, axis=-1, keepdims=True) + EPS) * fg_ref[...]
    o_ref[...] = out


def _combine(yt, pw, base, mod, mod_row, final_g):
    B, S, D = base.shape
    tm = MOE_TILE
    nt = S // tm
    final = final_g is not None
    fg = (final_g if final else jnp.ones((D,), F32)).reshape(1, D)
    out = pl.pallas_call(
        functools.partial(_combine_kernel, final=final),
        grid=(B * nt,),
        in_specs=[pl.BlockSpec((TILE_CHUNKS, CHUNK, D), lambda i: (i, 0, 0)),
                  pl.BlockSpec((TILE_ROWS, tm), lambda i: (i, 0)),
                  pl.BlockSpec((tm, D), lambda i: (i, 0)),
                  pl.BlockSpec((1, 6, D), lambda i: (mod_row(i // nt), 0, 0)),
                  pl.BlockSpec((1, D), lambda i: (0, 0))],
        out_specs=pl.BlockSpec((tm, D), lambda i: (i, 0)),
        out_shape=jax.ShapeDtypeStruct((B * S, D), F32),
        compiler_params=_params(("parallel",)),
        name="moe_combine",
    )(yt, pw, base.reshape(B * S, D), mod, fg)
    return out.reshape(B, S, D)


def _moe_sparse(h2, idx, wts, cnt, base, mod, mod_row, layer, w_gate, w_up, w_down, final_g=None):
    B, S, D = h2.shape
    n_tiles = B * (S // MOE_TILE)
    g_max = (n_tiles * TILE_CHUNKS + N_PAD_CHUNKS + GROUP_CHUNKS - 1) // GROUP_CHUNKS
    pos_write, inv, pad_pos, fill, grp_e, n_groups = _moe_tables(cnt[:, :, 0], g_max)
    xs, pw = _dispatch(h2, idx, wts, pos_write, pad_pos, g_max * GROUP_CHUNKS)
    yt = _ffn_grouped(xs, inv, fill, grp_e, n_groups, layer, w_gate, w_up, w_down, g_max, n_tiles)
    return _combine(yt, pw, base, mod, mod_row, final_g)


def _zero_state(batch):
    nq = M_HEADS * M_QK_DIM
    return (jnp.zeros((batch, 2, nq, M_V_DIM), F32),
            jnp.zeros((batch, 2, 1, nq), F32),
            jnp.zeros((batch, 2, 1, LANES), F32))


def kernel(x, c, ctx, c_ctx, ada_w, ada_b, norm1_g, norm2_g, w_in, attn_sink, mlstm_gate_b, mlstm_norm_g, conv_w, w_br_attn, w_br_mlstm, w_br_conv, branch_gate_b, w_out, router_w, router_bias, exp_w_gate, exp_w_up, exp_w_down, sh_w_gate, sh_w_up, sh_w_down, final_g):
    B, S, D = x.shape
    L = ctx.shape[1]
    depth = ada_w.shape[0]
    ctx_row = B

    pad_rows = (-(B + 1)) % 8
    cc = jnp.concatenate([c, c_ctx[None, :], jnp.zeros((pad_rows, D), F32)], axis=0)
    mod_all = _ada(cc, ada_w, ada_b).reshape(depth, B + 1 + pad_rows, 6, D)

    cos_t, sin_t = _rope_tables(S)
    cos_c = jnp.ones((L, LANES), F32)
    sin_c = jnp.zeros((L, LANES), F32)
    lat_row = lambda b: b
    ctx_mod = lambda b: ctx_row

    xc = ctx
    for l in range(depth):
        need_ctx = l < depth - 1
        mod = mod_all[l]
        w_p = _permute_w_in(w_in[l])
        lw = {
            'mlstm_norm_g': mlstm_norm_g[l].reshape(1, M_OUT),
            'conv_w': conv_w[l],
            'branch_gate_b': branch_gate_b[l].reshape(1, N_BRANCH * D),
            'w_br_attn': w_br_attn[l].reshape(ATT_KV_HEADS, ATT_GROUP, ATT_HEAD_DIM, D)
                         .transpose(1, 0, 2, 3).reshape(ATT_OUT, D).astype(BF16),
            'w_br_mlstm': w_br_mlstm[l].astype(BF16),
            'w_br_conv': w_br_conv[l].astype(BF16),
            'w_out': w_out[l].astype(BF16),
            'norm2_g': norm2_g[l].reshape(1, D),
            'router_wt': _split_hi_lo(router_w[l].T),
            'router_bias': router_bias[l].reshape(N_EXPERTS, 1),
            'sh_gu': jnp.concatenate([sh_w_gate[l], sh_w_up[l]], axis=1).astype(BF16),
            'sh_d': sh_w_down[l].astype(BF16),
        }
        experts = (l, exp_w_gate, exp_w_up, exp_w_down)

        pc = _in_proj(xc, mod, ctx_mod, norm1_g[l], w_p, cos_c, sin_c, tm=256,
                      names=_PROJ_OUTPUTS if need_ctx else _STATE_OUTPUTS)
        p = _in_proj(x, mod, lat_row, norm1_g[l], w_p, cos_t, sin_t, tm=min(512, S))

        oat = _attention(p['qs'], p['k'], p['v'], pc['k'], pc['v'], attn_sink[l], band=True)
        hf_c, hb_c, st = _mlstm(pc['qm'], pc['km'], pc['vm'], pc['gm'], mlstm_gate_b[l], _zero_state(B))
        hf, hb, _ = _mlstm(p['qm'], p['km'], p['vm'], p['gm'], mlstm_gate_b[l], st)

        base, h2, idx, wts, cnt = _merge(x, mod, lat_row, oat, hf, hb, p['om'], p['u'], p['bc'], p['gt'], lw,
                                         tm=min(2 * MOE_TILE, S))
        x_new = _moe_sparse(h2, idx, wts, cnt, base, mod, lat_row, *experts,
                            final_g=final_g if l == depth - 1 else None)

        if need_ctx:
            oat_c = _attention(pc['qs'], None, None, pc['k'], pc['v'], attn_sink[l], band=False)
            base_c, h2_c, idx_c, wts_c, cnt_c = _merge(xc, mod, ctx_mod, oat_c, hf_c, hb_c, pc['om'], pc['u'],
                                                       pc['bc'], pc['gt'], lw, tm=MOE_TILE)
            xc = _moe_sparse(h2_c, idx_c, wts_c, cnt_c, base_c, mod, ctx_mod, *experts)
        x = x_new
    return x
```
